```python
import jax
import jax.numpy as jnp
from jax import lax
import numpy as np

D_MODEL = 1024
BATCH = 8
SEQ = 4096
DEPTH = 4

GRID_W = 64
CTX_LEN = 256
EPS = 1e-6
DEEPNORM_ALPHA = (2 * DEPTH) ** 0.25
DEEPNORM_BETA = (8 * DEPTH) ** -0.25

RET_HEADS = 8
RET_DK = 64
RET_DV = 128
RET_CHUNK = 128
RET_W = RET_HEADS * RET_DV

ATT_HEADS = 16
ATT_KV_HEADS = 4
ATT_DH = 64
ATT_W = ATT_HEADS * ATT_DH
Q_BLOCK = 128
ROPE_THETA = 10000.0

CONV_CH = 1024
CONV_K = 31

N_EXPERTS = 64
TOP_K = 8
N_GROUPS = 8
TOPK_GROUPS = 4
D_EXPERT = 256
D_SHARED = 256
ROUTED_SCALE = 2.5
MOE_BLOCK = 128

N_BRANCHES = 3
SPLIT_SIZES = (RET_HEADS * RET_DK, RET_HEADS * RET_DK, RET_W, RET_W,
               ATT_W, ATT_KV_HEADS * ATT_DH, ATT_KV_HEADS * ATT_DH,
               2 * CONV_CH, N_BRANCHES * D_MODEL)
D_IN = sum(SPLIT_SIZES)

kernel_name = 'hybrid_dit_retention_gqa_conformer_moe'


def layer_norm(x, g, b):
    xf = x.astype(jnp.float32)
    mu = jnp.mean(xf, axis=-1, keepdims=True)
    var = jnp.mean(jnp.square(xf - mu), axis=-1, keepdims=True)
    return ((xf - mu) * lax.rsqrt(var + EPS) * g + b).astype(x.dtype)


def rms_norm(x, g):
    xf = x.astype(jnp.float32)
    return (xf * lax.rsqrt(jnp.mean(xf * xf, axis=-1, keepdims=True) + EPS) * g).astype(x.dtype)


def split_heads(a, n_heads):
    return a.reshape(a.shape[0], a.shape[1], n_heads, -1)


def split_projection(z):
    idx = np.cumsum(SPLIT_SIZES)[:-1].tolist()
    return jnp.split(z, idx, axis=-1)


def axial_rope_tables(row, col, head_dim):
    n_freq = head_dim // 4
    inv_freq = ROPE_THETA ** (-jnp.arange(n_freq, dtype=jnp.float32) / n_freq)
    ang = jnp.concatenate([row[:, None] * inv_freq, col[:, None] * inv_freq], axis=-1)
    return jnp.cos(ang), jnp.sin(ang)


def apply_rope(x, cos, sin):
    half = x.shape[-1] // 2
    x1, x2 = x[..., :half], x[..., half:]
    cs, sn = cos[None, :, None, :], sin[None, :, None, :]
    return jnp.concatenate([x1 * cs - x2 * sn, x1 * sn + x2 * cs], axis=-1).astype(x.dtype)


def retention_scan(q, k, v, log_gamma, s0):
    b, t, h, dk = q.shape
    dv = v.shape[-1]
    nc = t // RET_CHUNK
    qc = q.astype(jnp.float32).reshape(b, nc, RET_CHUNK, h, dk)
    kc = k.astype(jnp.float32).reshape(b, nc, RET_CHUNK, h, dk)
    vc = v.astype(jnp.float32).reshape(b, nc, RET_CHUNK, h, dv)
    pos = jnp.arange(RET_CHUNK, dtype=jnp.float32)
    rel = pos[:, None] - pos[None, :]
    decay_in = jnp.where(rel >= 0, jnp.exp(log_gamma[:, None, None] * jnp.maximum(rel, 0.0)), 0.0)
    scores = jnp.einsum('bnihd,bnjhd->bnhij', qc, kc) * decay_in
    y_inner = jnp.einsum('bnhij,bnjhe->bnihe', scores, vc)
    k_w = jnp.exp(log_gamma[None, :] * (RET_CHUNK - 1 - pos)[:, None])
    kv = jnp.einsum('bnjhd,jh,bnjhe->bnhde', kc, k_w, vc)
    g_chunk = jnp.exp(log_gamma * RET_CHUNK)[None, :, None, None]

    def step(s, kv_c):
        return g_chunk * s + kv_c, s

    s_final, s_prev = lax.scan(step, s0, jnp.moveaxis(kv, 1, 0))
    s_prev = jnp.moveaxis(s_prev, 0, 1)
    q_w = jnp.exp(log_gamma[None, :] * (pos + 1.0)[:, None])
    y_cross = jnp.einsum('bnihd,ih,bnhde->bnihe', qc, q_w, s_prev)
    return (y_inner + y_cross).reshape(b, t, h, dv), s_final


def bidirectional_retention(q_l, k_l, v_l, q_c, k_c, v_c, log_gamma):
    b = q_l.shape[0]
    s0 = jnp.zeros((b, RET_HEADS, RET_DK, RET_DV), jnp.float32)
    rev = lambda a: jnp.flip(a, axis=1)
    yc_f, sc_f = retention_scan(q_c, k_c, v_c, log_gamma[0], s0)
    yl_f, _ = retention_scan(q_l, k_l, v_l, log_gamma[0], sc_f)
    yc_b, sc_b = retention_scan(rev(q_c), rev(k_c), rev(v_c), log_gamma[1], s0)
    yl_b, _ = retention_scan(rev(q_l), rev(k_l), rev(v_l), log_gamma[1], sc_b)
    return yl_f + rev(yl_b), yc_f + rev(yc_b)


def retention_output(y, g, w_o):
    mu = jnp.mean(y, axis=-1, keepdims=True)
    var = jnp.mean(jnp.square(y - mu), axis=-1, keepdims=True)
    yn = ((y - mu) * lax.rsqrt(var + EPS)).reshape(y.shape[0], y.shape[1], RET_W).astype(g.dtype)
    return (jax.nn.silu(g) * yn) @ w_o


def block_attention(q, k, v):
    b, t, h, d = q.shape
    kvh = k.shape[2]
    grp = h // kvh
    nq = t // Q_BLOCK
    qb = jnp.moveaxis(q.reshape(b, nq, Q_BLOCK, kvh, grp, d), 1, 0)
    scale = d ** -0.5

    def one_block(qblk):
        s = jnp.einsum('bqkgd,bskd->bkgqs', qblk, k, preferred_element_type=jnp.float32) * scale
        p = jax.nn.softmax(s, axis=-1).astype(v.dtype)
        return jnp.einsum('bkgqs,bskd->bqkgd', p, v)

    o = lax.map(one_block, qb)
    return jnp.moveaxis(o, 0, 1).reshape(b, t, h * d)


def conformer_conv(u, w_dw, b_dw, ln_g, ln_b):
    a, gte = jnp.split(u, 2, axis=-1)
    glu = a * jax.nn.sigmoid(gte)
    y = lax.conv_general_dilated(glu, w_dw[:, None, :].astype(glu.dtype), window_strides=(1,),
                                 padding=[(CONV_K // 2, CONV_K // 2)],
                                 dimension_numbers=('NWC', 'WIO', 'NWC'),
                                 feature_group_count=CONV_CH) + b_dw
    return jax.nn.silu(layer_norm(y, ln_g, ln_b))


def merge_branches(gates, br_ret, br_att, br_conv):
    g_ret, g_att, g_conv = jnp.split(gates, N_BRANCHES, axis=-1)
    return jax.nn.sigmoid(g_ret) * br_ret + jax.nn.sigmoid(g_att) * br_att + jax.nn.sigmoid(g_conv) * br_conv


def token_mixers(h_l, h_c, cos, sin, need_ctx, w_in, decay_logit, q_norm, k_norm,
                 conv_dw, conv_db, conv_ln_g, conv_ln_b, w_ret_o, w_att_o, w_conv_o, w_out):
    rq_l, rk_l, rv_l, rg_l, aq_l, ak_l, av_l, cu_l, gt_l = split_projection(h_l @ w_in)
    rq_c, rk_c, rv_c, rg_c, aq_c, ak_c, av_c, cu_c, gt_c = split_projection(h_c @ w_in)

    rscale = RET_DK ** -0.5
    rq_l = apply_rope(split_heads(rq_l, RET_HEADS), cos, sin)
    rk_l = apply_rope(split_heads(rk_l, RET_HEADS), cos, sin) * rscale
    rq_c = split_heads(rq_c, RET_HEADS)
    rk_c = split_heads(rk_c, RET_HEADS) * rscale
    log_gamma = jax.nn.log_sigmoid(decay_logit.astype(jnp.float32))
    ret_l, ret_c = bidirectional_retention(rq_l, rk_l, split_heads(rv_l, RET_HEADS),
                                           rq_c, rk_c, split_heads(rv_c, RET_HEADS), log_gamma)
    br_ret_l = retention_output(ret_l, rg_l, w_ret_o)

    aq_l = apply_rope(rms_norm(split_heads(aq_l, ATT_HEADS), q_norm), cos, sin)
    ak_l = apply_rope(rms_norm(split_heads(ak_l, ATT_KV_HEADS), k_norm), cos, sin)
    av_l = split_heads(av_l, ATT_KV_HEADS)
    ak_c = rms_norm(split_heads(ak_c, ATT_KV_HEADS), k_norm)
    av_c = split_heads(av_c, ATT_KV_HEADS)
    br_att_l = block_attention(aq_l, jnp.concatenate([ak_c, ak_l], axis=1),
                               jnp.concatenate([av_c, av_l], axis=1)) @ w_att_o

    br_conv_l = conformer_conv(cu_l, conv_dw, conv_db, conv_ln_g, conv_ln_b) @ w_conv_o

    out_l = merge_branches(gt_l, br_ret_l, br_att_l, br_conv_l) @ w_out
    if not need_ctx:
        return out_l, None

    br_ret_c = retention_output(ret_c, rg_c, w_ret_o)
    aq_c = rms_norm(split_heads(aq_c, ATT_HEADS), q_norm)
    br_att_c = block_attention(aq_c, ak_c, av_c) @ w_att_o
    br_conv_c = conformer_conv(cu_c, conv_dw, conv_db, conv_ln_g, conv_ln_b) @ w_conv_o
    out_c = merge_branches(gt_c, br_ret_c, br_att_c, br_conv_c) @ w_out
    return out_l, out_c


def moe_ffn(h, w_router, router_bias, w_exp_gate, w_exp_up, w_exp_down, w_sh_gate, w_sh_up, w_sh_down):
    n, d = h.shape
    scores = jax.nn.sigmoid(jnp.dot(h, w_router, preferred_element_type=jnp.float32))
    sel = scores + router_bias.astype(jnp.float32)
    per_group = N_EXPERTS // N_GROUPS
    grp_score = lax.top_k(sel.reshape(n, N_GROUPS, per_group), 2)[0].sum(-1)
    _, top_grp = lax.top_k(grp_score, TOPK_GROUPS)
    grp_keep = jnp.any(top_grp[:, :, None] == jnp.arange(N_GROUPS)[None, None, :], axis=1)
    sel = jnp.where(jnp.repeat(grp_keep, per_group, axis=1), sel, -jnp.inf)
    _, top_e = lax.top_k(sel, TOP_K)
    gate_w = jnp.take_along_axis(scores, top_e, axis=1)
    gate_w = ROUTED_SCALE * gate_w / jnp.sum(gate_w, axis=-1, keepdims=True)

    n_assign = n * TOP_K
    n_blocks = (n_assign + N_EXPERTS * (MOE_BLOCK - 1) + MOE_BLOCK - 1) // MOE_BLOCK
    flat_e = top_e.reshape(-1).astype(jnp.int32)
    flat_tok = jnp.repeat(jnp.arange(n, dtype=jnp.int32), TOP_K)
    flat_w = gate_w.reshape(-1)
    order = jnp.argsort(flat_e, stable=True)
    se = flat_e[order]
    counts = jnp.bincount(flat_e, length=N_EXPERTS).astype(jnp.int32)
    padded = (counts + MOE_BLOCK - 1) // MOE_BLOCK * MOE_BLOCK
    start_unp = jnp.cumsum(counts) - counts
    end_pad = jnp.cumsum(padded)
    start_pad = end_pad - padded
    dest = start_pad[se] + (jnp.arange(n_assign, dtype=jnp.int32) - start_unp[se])
    buf_tok = jnp.full((n_blocks * MOE_BLOCK,), n, jnp.int32).at[dest].set(flat_tok[order])
    buf_w = jnp.zeros((n_blocks * MOE_BLOCK,), jnp.float32).at[dest].set(flat_w[order])
    block_start = jnp.arange(n_blocks, dtype=jnp.int32) * MOE_BLOCK
    block_e = jnp.minimum(jnp.searchsorted(end_pad, block_start, side='right'), N_EXPERTS - 1)
    h_pad = jnp.concatenate([h, jnp.zeros((1, d), h.dtype)], axis=0)

    def expert_block(acc, blk):
        tok, wt, e = blk
        xb = h_pad[tok]
        hid = jax.nn.silu(xb @ w_exp_gate[e]) * (xb @ w_exp_up[e])
        yb = (hid @ w_exp_down[e]) * wt[:, None].astype(h.dtype)
        return acc.at[tok].add(yb), None

    routed, _ = lax.scan(expert_block, jnp.zeros((n + 1, d), h.dtype),
                         (buf_tok.reshape(n_blocks, MOE_BLOCK), buf_w.reshape(n_blocks, MOE_BLOCK), block_e))
    shared = (jax.nn.silu(h @ w_sh_gate) * (h @ w_sh_up)) @ w_sh_down
    return routed[:n] + shared


def setup_inputs(seed: int = 0) -> dict:
    key = jax.random.key(seed)
    ks = jax.random.split(key, 32)
    f32 = jnp.float32
    nrm = lambda k, shape, s: jax.random.normal(k, shape, f32) * s
    L = DEPTH
    base_logit = jnp.log(2.0 ** (5.0 + jnp.arange(RET_HEADS, dtype=f32)) - 1.0)
    beta = DEEPNORM_BETA
    return {
        'x': nrm(ks[0], (BATCH, SEQ, D_MODEL), 1.0),
        'c': nrm(ks[1], (BATCH, D_MODEL), 1.0),
        'ctx': nrm(ks[2], (BATCH, CTX_LEN, D_MODEL), 1.0),
        'c_ctx': nrm(ks[3], (D_MODEL,), 1.0),
        'w_ada': nrm(ks[4], (L, D_MODEL, 6 * D_MODEL), 0.5 * D_MODEL ** -0.5),
        'b_ada': nrm(ks[5], (L, 6 * D_MODEL), 0.02),
        'w_in': nrm(ks[6], (L, D_MODEL, D_IN), D_MODEL ** -0.5),
        'ret_decay_logit': base_logit[None, None, :] + nrm(ks[7], (L, 2, RET_HEADS), 0.1),
        'att_q_norm': 1.0 + nrm(ks[8], (L, ATT_DH), 0.02),
        'att_k_norm': 1.0 + nrm(ks[9], (L, ATT_DH), 0.02),
        'conv_dw': nrm(ks[10], (L, CONV_K, CONV_CH), CONV_K ** -0.5),
        'conv_db': nrm(ks[11], (L, CONV_CH), 0.02),
        'conv_ln_g': 1.0 + nrm(ks[12], (L, CONV_CH), 0.02),
        'conv_ln_b': nrm(ks[13], (L, CONV_CH), 0.02),
        'w_ret_o': nrm(ks[14], (L, RET_W, D_MODEL), beta * RET_W ** -0.5),
        'w_att_o': nrm(ks[15], (L, ATT_W, D_MODEL), beta * ATT_W ** -0.5),
        'w_conv_o': nrm(ks[16], (L, CONV_CH, D_MODEL), beta * CONV_CH ** -0.5),
        'w_out': nrm(ks[17], (L, D_MODEL, D_MODEL), beta * D_MODEL ** -0.5),
        'ln1_g': 1.0 + nrm(ks[18], (L, D_MODEL), 0.02),
        'ln1_b': nrm(ks[19], (L, D_MODEL), 0.02),
        'w_router': nrm(ks[20], (L, D_MODEL, N_EXPERTS), D_MODEL ** -0.5),
        'router_bias': nrm(ks[21], (L, N_EXPERTS), 0.01),
        'w_exp_gate': nrm(ks[22], (L, N_EXPERTS, D_MODEL, D_EXPERT), D_MODEL ** -0.5),
        'w_exp_up': nrm(ks[23], (L, N_EXPERTS, D_MODEL, D_EXPERT), D_MODEL ** -0.5),
        'w_exp_down': nrm(ks[24], (L, N_EXPERTS, D_EXPERT, D_MODEL), beta * D_EXPERT ** -0.5),
        'w_sh_gate': nrm(ks[25], (L, D_MODEL, D_SHARED), D_MODEL ** -0.5),
        'w_sh_up': nrm(ks[26], (L, D_MODEL, D_SHARED), D_MODEL ** -0.5),
        'w_sh_down': nrm(ks[27], (L, D_SHARED, D_MODEL), beta * D_SHARED ** -0.5),
        'ln2_g': 1.0 + nrm(ks[28], (L, D_MODEL), 0.02),
        'ln2_b': nrm(ks[29], (L, D_MODEL), 0.02),
    }


def reference(x, c, ctx, c_ctx, w_ada, b_ada, w_in, ret_decay_logit, att_q_norm, att_k_norm,
              conv_dw, conv_db, conv_ln_g, conv_ln_b, w_ret_o, w_att_o, w_conv_o, w_out, ln1_g, ln1_b,
              w_router, router_bias, w_exp_gate, w_exp_up, w_exp_down, w_sh_gate, w_sh_up, w_sh_down,
              ln2_g, ln2_b):
    b, n_lat, _ = x.shape
    n_ctx = ctx.shape[1]
    ROWS = n_lat // GRID_W
    row = jnp.repeat(jnp.arange(ROWS, dtype=jnp.float32), GRID_W)
    col = jnp.tile(jnp.arange(GRID_W, dtype=jnp.float32), ROWS)
    cos, sin = axial_rope_tables(row, col, ATT_DH)
    s_lat = jax.nn.silu(c)
    s_ctx = jax.nn.silu(c_ctx)[None]
    x_l, x_c = x, ctx
    for l in range(DEPTH):
        need_ctx = l < DEPTH - 1
        mod_l = jnp.split((s_lat @ w_ada[l] + b_ada[l])[:, None, :], 6, axis=-1)
        mod_c = jnp.split((s_ctx @ w_ada[l] + b_ada[l])[:, None, :], 6, axis=-1)
        h_l = x_l * (1.0 + mod_l[1]) + mod_l[0]
        h_c = x_c * (1.0 + mod_c[1]) + mod_c[0]
        y_l, y_c = token_mixers(h_l, h_c, cos, sin, need_ctx, w_in[l], ret_decay_logit[l],
                                att_q_norm[l], att_k_norm[l], conv_dw[l], conv_db[l], conv_ln_g[l],
                                conv_ln_b[l], w_ret_o[l], w_att_o[l], w_conv_o[l], w_out[l])
        x_l = layer_norm(DEEPNORM_ALPHA * x_l + mod_l[2] * y_l, ln1_g[l], ln1_b[l])
        h2_l = x_l * (1.0 + mod_l[4]) + mod_l[3]
        if need_ctx:
            x_c = layer_norm(DEEPNORM_ALPHA * x_c + mod_c[2] * y_c, ln1_g[l], ln1_b[l])
            h2_c = x_c * (1.0 + mod_c[4]) + mod_c[3]
            tokens = jnp.concatenate([h2_l.reshape(-1, D_MODEL), h2_c.reshape(-1, D_MODEL)], axis=0)
            f = moe_ffn(tokens, w_router[l], router_bias[l], w_exp_gate[l], w_exp_up[l], w_exp_down[l],
                        w_sh_gate[l], w_sh_up[l], w_sh_down[l])
            f_l = f[: b * n_lat].reshape(b, n_lat, D_MODEL)
            f_c = f[b * n_lat:].reshape(b, n_ctx, D_MODEL)
            x_c = layer_norm(DEEPNORM_ALPHA * x_c + mod_c[5] * f_c, ln2_g[l], ln2_b[l])
        else:
            f_l = moe_ffn(h2_l.reshape(-1, D_MODEL), w_router[l], router_bias[l], w_exp_gate[l], w_exp_up[l],
                          w_exp_down[l], w_sh_gate[l], w_sh_up[l], w_sh_down[l]).reshape(b, n_lat, D_MODEL)
        x_l = layer_norm(DEEPNORM_ALPHA * x_l + mod_l[5] * f_l, ln2_g[l], ln2_b[l])
    return x_l
```

```python
import functools

import numpy as np
import jax
import jax.numpy as jnp
from jax import lax
from jax.experimental import pallas as pl
from jax.experimental.pallas import tpu as pltpu

F32 = jnp.float32
BF16 = jnp.bfloat16
HIGHEST = lax.Precision.HIGHEST

D_MODEL = 1024
GRID_W = 64
EPS = 1e-6

RET_HEADS = 8
RET_DK = 64
RET_DV = 128
RET_CHUNK = 128
RET_W = RET_HEADS * RET_DV

ATT_HEADS = 16
ATT_KV_HEADS = 4
ATT_DH = 64
ATT_GROUP = ATT_HEADS // ATT_KV_HEADS
ATT_W = ATT_HEADS * ATT_DH
ROPE_THETA = 10000.0
ATT_KEY_BLOCK = 512

CONV_CH = 1024
CONV_K = 31
CONV_HALO = 16

N_EXPERTS = 64
TOP_K = 8
N_GROUPS = 8
TOPK_GROUPS = 4
D_EXPERT = 256
D_SHARED = 256
ROUTED_SCALE = 2.5
MOE_BLOCK = 256

_ORIG = dict(rq=0, rk=512, rv=1024, rg=2048, aq=3072, ak=4096, av=4352, cu=4608, gt=6656)
D_IN = 9728
COL_CU = 0
COL_GT = 2048
COL_RG = 5120
COL_RV = 6144
COL_RQ = 7168
COL_RK = 7680
COL_ATT = 8192
ATT_SECTION = ATT_GROUP * ATT_DH + 2 * ATT_DH


def _column_permutation():
    cols = []
    cols += list(range(_ORIG["cu"], _ORIG["cu"] + 2 * CONV_CH))
    cols += list(range(_ORIG["gt"], _ORIG["gt"] + 3 * D_MODEL))
    cols += list(range(_ORIG["rg"], _ORIG["rg"] + RET_W))
    cols += list(range(_ORIG["rv"], _ORIG["rv"] + RET_W))
    cols += list(range(_ORIG["rq"], _ORIG["rq"] + RET_HEADS * RET_DK))
    cols += list(range(_ORIG["rk"], _ORIG["rk"] + RET_HEADS * RET_DK))
    for g in range(ATT_KV_HEADS):
        cols += list(range(_ORIG["aq"] + g * ATT_GROUP * ATT_DH, _ORIG["aq"] + (g + 1) * ATT_GROUP * ATT_DH))
        cols += list(range(_ORIG["ak"] + g * ATT_DH, _ORIG["ak"] + (g + 1) * ATT_DH))
        cols += list(range(_ORIG["av"] + g * ATT_DH, _ORIG["av"] + (g + 1) * ATT_DH))
    assert len(cols) == D_IN and sorted(cols) == list(range(D_IN))
    return np.asarray(cols, np.int32)


def _params(n_axes, vmem_mib):
    return pltpu.CompilerParams(dimension_semantics=("arbitrary",) * n_axes,
                                vmem_limit_bytes=vmem_mib * 1024 * 1024)


def _silu(v):
    return v * jax.nn.sigmoid(v)


def _layer_norm(v, g, b):
    mu = jnp.mean(v, axis=-1, keepdims=True)
    d = v - mu
    var = jnp.mean(d * d, axis=-1, keepdims=True)
    return d * lax.rsqrt(var + EPS) * g + b


def _mods_kernel(c_ref, w_ref, b_ref, o_ref):
    s = _silu(c_ref[...])
    o_ref[0] = jnp.dot(s, w_ref[0], preferred_element_type=F32, precision=HIGHEST) + b_ref[0]


def _mods_call(cvecs, w_ada, b_ada):
    n_layers = w_ada.shape[0]
    rows, d = cvecs.shape
    return pl.pallas_call(
        _mods_kernel,
        grid=(n_layers, 6),
        in_specs=[pl.BlockSpec((rows, d), lambda l, j: (0, 0)),
                  pl.BlockSpec((1, d, d), lambda l, j: (l, 0, j)),
                  pl.BlockSpec((1, 1, d), lambda l, j: (l, 0, j))],
        out_specs=pl.BlockSpec((1, rows, d), lambda l, j: (l, 0, j)),
        out_shape=jax.ShapeDtypeStruct((n_layers, rows, 6 * d), F32),
        compiler_params=_params(2, 32),
        name="adaln_mods",
    )(cvecs, w_ada, b_ada.reshape(n_layers, 1, 6 * d))


class _Geom:
    def __init__(self, b, t, c):
        assert t % c == 0 and c % RET_CHUNK == 0 and c % CONV_HALO == 0 and t % ATT_KEY_BLOCK == 0
        self.b, self.t, self.c = b, t, c
        self.nl, self.nc = b * t, b * c
        self.nt = self.nl + self.nc
        self.lat_blocks = t // c
        self.nlb = self.nl // c
        self.p = t + c

    def row_block(self, bi, r):
        return jnp.where(r < self.lat_blocks, bi * self.lat_blocks + r, self.nlb + bi)

    def mod_row(self, i, tm):
        return jnp.where(i * tm < self.nl, (i * tm) // self.t, self.b)


def _mod_spec(geom, tm, which, grid_pos=0):
    d = D_MODEL
    if grid_pos == 0:
        return pl.BlockSpec((1, 1, d), lambda i, *_: (geom.mod_row(i, tm) * 6 + which, 0, 0))
    return pl.BlockSpec((1, 1, d), lambda j, i: (geom.mod_row(i, tm) * 6 + which, 0, 0))


def _inproj_kernel(x_ref, sh_ref, sc_ref, w_ref, o_ref):
    h = x_ref[...] * (1.0 + sc_ref[0]) + sh_ref[0]
    o_ref[...] = jnp.dot(h.astype(BF16), w_ref[...], preferred_element_type=F32)


def _inproj_call(geom, x, mods, w_in_bf16):
    tm = 512 if geom.nc % 512 == 0 and geom.t % 512 == 0 else geom.c
    tn = D_IN // 4
    return pl.pallas_call(
        _inproj_kernel,
        grid=(D_IN // tn, geom.nt // tm),
        in_specs=[pl.BlockSpec((tm, D_MODEL), lambda j, i: (i, 0)),
                  _mod_spec(geom, tm, 0, grid_pos=1),
                  _mod_spec(geom, tm, 1, grid_pos=1),
                  pl.BlockSpec((D_MODEL, tn), lambda j, i: (0, j))],
        out_specs=pl.BlockSpec((tm, tn), lambda j, i: (i, j)),
        out_shape=jax.ShapeDtypeStruct((geom.nt, D_IN), F32),
        compiler_params=_params(2, 48),
        name="in_proj",
    )(x, mods, mods, w_in_bf16)


def _rot_half_128(v):
    lane = lax.broadcasted_iota(jnp.int32, v.shape, 1)
    return jnp.where((lane % 64) < 32, pltpu.roll(v, 96, 1), pltpu.roll(v, 32, 1))


def _ret_kernel(lg_ref, ql_ref, qc_ref, kl_ref, kc_ref, vl_ref, vc_ref, g_ref, cos_ref, sin_ref, o_ref,
                qs, kts, yf, yb, st, dm, qwb, kwb, gcs, *, t, c):
    ch = RET_CHUNK
    hp = pl.program_id(1)
    r = pl.program_id(2)
    lat_blocks = t // c
    n_lat, n_ctx = t // ch, c // ch

    @pl.when(r == 0)
    def _scan():
        ri = lax.broadcasted_iota(jnp.int32, (ch, ch), 0).astype(F32)
        ci = lax.broadcasted_iota(jnp.int32, (ch, ch), 1).astype(F32)
        for d in range(2):
            for h in range(2):
                u = 2 * d + h
                lg = lg_ref[d, 2 * hp + h]
                rel = (ri - ci) if d == 0 else (ci - ri)
                dm[u] = jnp.where(rel >= 0.0, jnp.exp(lg * jnp.maximum(rel, 0.0)), 0.0)
                qwb[u] = jnp.exp(lg * ((ri + 1.0) if d == 0 else (float(ch) - ri)))
                kwb[u] = jnp.exp(lg * ((float(ch) - 1.0 - ri) if d == 0 else ri))
                gcs[u] = jnp.exp(jnp.full((RET_DK, RET_DV), lg * float(ch), F32))
                st[u] = jnp.zeros((RET_DK, RET_DV), F32)

        def stage(q, k, seq_rows):
            qs[0, seq_rows, :] = q[:, :RET_DK].astype(BF16)
            qs[1, seq_rows, :] = q[:, RET_DK:].astype(BF16)
            kt = k.T
            kts[0, :, seq_rows] = kt[:RET_DK].astype(BF16)
            kts[1, :, seq_rows] = kt[RET_DK:].astype(BF16)

        kscale = RET_DK ** -0.5
        for cc in range(n_ctx):
            rows = pl.ds(cc * ch, ch)
            stage(qc_ref[rows, :], kc_ref[rows, :] * kscale, rows)

        def stage_lat(cc, carry):
            rows = pl.ds(pl.multiple_of(cc * ch, ch), ch)
            cs, sn = cos_ref[rows, :], sin_ref[rows, :]
            q = ql_ref[rows, :]
            k = kl_ref[rows, :]
            q = q * cs + _rot_half_128(q) * sn
            k = (k * cs + _rot_half_128(k) * sn) * kscale
            stage(q, k, pl.ds(pl.multiple_of(c + cc * ch, ch), ch))
            return carry

        lax.fori_loop(0, n_lat, stage_lat, 0)

        def run_segment(v_ref, seq_off, n):
            def body(i, carry):
                for d, cc in ((0, i), (1, n - 1 - i)):
                    vrows = pl.ds(pl.multiple_of(cc * ch, ch), ch)
                    srows = pl.ds(pl.multiple_of(seq_off + cc * ch, ch), ch)
                    for h in range(2):
                        u = 2 * d + h
                        q = qs[h, srows, :]
                        kt = kts[h, :, srows]
                        v = v_ref[vrows, h * RET_DV:(h + 1) * RET_DV]
                        s = jnp.dot(q, kt, preferred_element_type=F32)
                        y = jnp.dot((s * dm[u]).astype(BF16), v.astype(BF16), preferred_element_type=F32)
                        state = st[u]
                        y = y + jnp.dot(q, state.astype(BF16), preferred_element_type=F32) * qwb[u]
                        dst = yf if d == 0 else yb
                        dst[srows, h * RET_DV:(h + 1) * RET_DV] = y
                        kv = jnp.dot(kt, (v * kwb[u]).astype(BF16), preferred_element_type=F32)
                        st[u] = gcs[u] * state + kv
                return carry

            lax.fori_loop(0, n, body, 0)

        run_segment(vc_ref, 0, n_ctx)
        run_segment(vl_ref, c, n_lat)

    def finish(srows):
        y = yf[srows, :] + yb[srows, :]
        for h in range(2):
            cols = slice(h * RET_DV, (h + 1) * RET_DV)
            yh = y[:, cols]
            mu = jnp.mean(yh, axis=-1, keepdims=True)
            dlt = yh - mu
            var = jnp.mean(dlt * dlt, axis=-1, keepdims=True)
            o_ref[:, cols] = _silu(g_ref[:, cols]) * (dlt * lax.rsqrt(var + EPS))

    @pl.when(r < lat_blocks)
    def _fin_lat():
        finish(pl.ds(pl.multiple_of(c + r * c, c), c))

    @pl.when(r == lat_blocks)
    def _fin_ctx():
        finish(pl.ds(0, c))


def _retention_call(geom, z, log_gamma, cos128, sin128):
    t, c, p = geom.t, geom.c, geom.p
    hpairs = RET_HEADS // 2
    qb, kb = COL_RQ // 128, COL_RK // 128
    vb, gb = COL_RV // 256, COL_RG // 256
    rb = geom.row_block
    in_specs = [
        pl.BlockSpec(memory_space=pltpu.SMEM),
        pl.BlockSpec((t, 128), lambda b, h, r: (b, qb + h)),
        pl.BlockSpec((c, 128), lambda b, h, r: (geom.nlb + b, qb + h)),
        pl.BlockSpec((t, 128), lambda b, h, r: (b, kb + h)),
        pl.BlockSpec((c, 128), lambda b, h, r: (geom.nlb + b, kb + h)),
        pl.BlockSpec((t, 256), lambda b, h, r: (b, vb + h)),
        pl.BlockSpec((c, 256), lambda b, h, r: (geom.nlb + b, vb + h)),
        pl.BlockSpec((c, 256), lambda b, h, r: (rb(b, r), gb + h)),
        pl.BlockSpec((t, 128), lambda b, h, r: (0, 0)),
        pl.BlockSpec((t, 128), lambda b, h, r: (0, 0)),
    ]
    scratch = [
        pltpu.VMEM((2, p, RET_DK), BF16),
        pltpu.VMEM((2, RET_DK, p), BF16),
        pltpu.VMEM((p, 2 * RET_DV), F32),
        pltpu.VMEM((p, 2 * RET_DV), F32),
        pltpu.VMEM((4, RET_DK, RET_DV), F32),
        pltpu.VMEM((4, RET_CHUNK, RET_CHUNK), F32),
        pltpu.VMEM((4, RET_CHUNK, RET_CHUNK), F32),
        pltpu.VMEM((4, RET_CHUNK, RET_CHUNK), F32),
        pltpu.VMEM((4, RET_DK, RET_DV), F32),
    ]
    return pl.pallas_call(
        functools.partial(_ret_kernel, t=t, c=c),
        grid=(geom.b, hpairs, geom.lat_blocks + 1),
        in_specs=in_specs,
        out_specs=pl.BlockSpec((c, 256), lambda b, h, r: (rb(b, r), h)),
        out_shape=jax.ShapeDtypeStruct((geom.nt, RET_W), F32),
        scratch_shapes=scratch,
        compiler_params=_params(3, 56),
        name="retention",
    )(log_gamma, z, z, z, z, z, z, z, cos128, sin128)


def _rms_head(v, g):
    return v * lax.rsqrt(jnp.mean(v * v, axis=-1, keepdims=True) + EPS) * g


def _rope_head(v, cs, sn):
    half = ATT_DH // 2
    return v * cs + jnp.concatenate([v[:, half:], v[:, :half]], axis=1) * sn


def _att_kernel(qa_ref, qb_ref, kvl_ref, kvc_ref, qn_ref, kn_ref, cos_ref, sin_ref, o_ref,
                ks, vs, m_s, l_s, acc_s, *, t, c):
    r = pl.program_id(2)
    lat_blocks = t // c
    dh = ATT_DH
    tk = ATT_KEY_BLOCK

    @pl.when(r == 0)
    def _stage_kv():
        kn = kn_ref[...]
        kv = kvc_ref[...]
        ks[0:c, :] = _rms_head(kv[:, :dh], kn).astype(BF16)
        vs[0:c, :] = kv[:, dh:].astype(BF16)

        def stage(i, carry):
            rows = pl.ds(pl.multiple_of(i * c, c), c)
            dst = pl.ds(pl.multiple_of(c + i * c, c), c)
            kv = kvl_ref[rows, :]
            k = _rope_head(_rms_head(kv[:, :dh], kn), cos_ref[rows, :], sin_ref[rows, :])
            ks[dst, :] = k.astype(BF16)
            vs[dst, :] = kv[:, dh:].astype(BF16)
            return carry

        lax.fori_loop(0, lat_blocks, stage, 0)

    is_ctx = r == lat_blocks
    rows = pl.ds(pl.multiple_of(jnp.minimum(r, lat_blocks - 1) * c, c), c)
    cs, sn = cos_ref[rows, :], sin_ref[rows, :]
    qn = qn_ref[...]
    heads = []
    for src in (qa_ref, qb_ref):
        x = src[...]
        for half in range(2):
            xn = _rms_head(x[:, half * dh:(half + 1) * dh], qn)
            xr = jnp.where(is_ctx, xn, _rope_head(xn, cs, sn))
            heads.append((xr * (dh ** -0.5)).astype(BF16))
    q = jnp.concatenate(heads, axis=0)

    m_s[...] = jnp.full(m_s.shape, -jnp.inf, F32)
    l_s[...] = jnp.zeros(l_s.shape, F32)
    acc_s[...] = jnp.zeros(acc_s.shape, F32)

    def flash_step(k, v):
        n = k.shape[0]
        s = lax.dot_general(q, k, (((1,), (1,)), ((), ())), preferred_element_type=F32)
        m_prev = m_s[...]
        m_next = jnp.maximum(m_prev, jnp.max(s, axis=1, keepdims=True))
        prob = jnp.exp(s - jnp.concatenate([m_next] * (n // 128), axis=1))
        alpha = jnp.exp(m_prev - m_next)
        l_s[...] = alpha * l_s[...] + jnp.sum(prob, axis=1, keepdims=True)
        m_s[...] = m_next
        acc_s[...] = acc_s[...] * alpha[:, :dh] + jnp.dot(prob.astype(BF16), v, preferred_element_type=F32)

    flash_step(ks[0:c, :], vs[0:c, :])

    def lat_step(j, carry):
        krows = pl.ds(pl.multiple_of(c + j * tk, 128), tk)
        flash_step(ks[krows, :], vs[krows, :])
        return carry

    lax.fori_loop(0, jnp.where(is_ctx, 0, t // tk), lat_step, 0)

    out = acc_s[...] / l_s[...][:, :dh]
    for h in range(ATT_GROUP):
        o_ref[:, h * dh:(h + 1) * dh] = out[h * c:(h + 1) * c, :]


def _attention_call(geom, z, q_norm, k_norm, cos64, sin64):
    t, c, p = geom.t, geom.c, geom.p
    ab = COL_ATT // 128
    sec = ATT_SECTION // 128
    rb = geom.row_block
    in_specs = [
        pl.BlockSpec((c, 128), lambda b, g, r: (rb(b, r), ab + sec * g)),
        pl.BlockSpec((c, 128), lambda b, g, r: (rb(b, r), ab + sec * g + 1)),
        pl.BlockSpec((t, 128), lambda b, g, r: (b, ab + sec * g + 2)),
        pl.BlockSpec((c, 128), lambda b, g, r: (geom.nlb + b, ab + sec * g + 2)),
        pl.BlockSpec((1, ATT_DH), lambda b, g, r: (0, 0)),
        pl.BlockSpec((1, ATT_DH), lambda b, g, r: (0, 0)),
        pl.BlockSpec((t, ATT_DH), lambda b, g, r: (0, 0)),
        pl.BlockSpec((t, ATT_DH), lambda b, g, r: (0, 0)),
    ]
    scratch = [
        pltpu.VMEM((p, ATT_DH), BF16),
        pltpu.VMEM((p, ATT_DH), BF16),
        pltpu.VMEM((ATT_GROUP * c, 128), F32),
        pltpu.VMEM((ATT_GROUP * c, 128), F32),
        pltpu.VMEM((ATT_GROUP * c, ATT_DH), F32),
    ]
    return pl.pallas_call(
        functools.partial(_att_kernel, t=t, c=c),
        grid=(geom.b, ATT_KV_HEADS, geom.lat_blocks + 1),
        in_specs=in_specs,
        out_specs=pl.BlockSpec((c, ATT_GROUP * ATT_DH), lambda b, g, r: (rb(b, r), g)),
        out_shape=jax.ShapeDtypeStruct((geom.nt, ATT_W), F32),
        scratch_shapes=scratch,
        compiler_params=_params(3, 48),
        name="attention",
    )(z, z, z, z, q_norm.reshape(1, ATT_DH), k_norm.reshape(1, ATT_DH), cos64, sin64)


def _conv_kernel(a_ref, g_ref, ap_ref, gp_ref, an_ref, gn_ref, w_ref, b_ref, lng_ref, lnb_ref, o_ref,
                 ext, ys, *, t, c):
    r = pl.program_id(1)
    lat_blocks = t // c
    halo = CONV_HALO
    has_prev = jnp.logical_and(r != 0, r != lat_blocks)
    has_next = jnp.logical_and(r != lat_blocks - 1, r != lat_blocks)
    ext[halo:halo + c, :] = a_ref[...] * jax.nn.sigmoid(g_ref[...])
    ext[0:halo, :] = jnp.where(has_prev, ap_ref[...] * jax.nn.sigmoid(gp_ref[...]), 0.0)
    ext[halo + c:, :] = jnp.where(has_next, an_ref[...] * jax.nn.sigmoid(gn_ref[...]), 0.0)

    rt = 64
    first = halo - CONV_K // 2

    def lane_block(cb, carry):
        lanes = pl.ds(pl.multiple_of(cb * 128, 128), 128)
        for ti in range(c // rt):
            acc = jnp.zeros((rt, 128), F32)
            for j in range(CONV_K):
                acc = acc + w_ref[pl.ds(j, 1), lanes] * ext[pl.ds(ti * rt + first + j, rt), lanes]
            ys[pl.ds(ti * rt, rt), lanes] = acc
        return carry

    lax.fori_loop(0, CONV_CH // 128, lane_block, 0)
    y = ys[...] + b_ref[...]
    o_ref[...] = _silu(_layer_norm(y, lng_ref[...], lnb_ref[...]))


def _conv_call(geom, z, conv_dw, conv_db, ln_g, ln_b):
    t, c = geom.t, geom.c
    rb = geom.row_block
    hb = c // CONV_HALO
    last = geom.nt // CONV_HALO - 1
    prev = lambda b, r: jnp.maximum(rb(b, r) * hb - 1, 0)
    nxt = lambda b, r: jnp.minimum((rb(b, r) + 1) * hb, last)
    w = jnp.zeros((32, CONV_CH), F32).at[:CONV_K].set(conv_dw)
    vec = lambda v: v.reshape(1, CONV_CH)
    cst = pl.BlockSpec((1, CONV_CH), lambda b, r: (0, 0))
    in_specs = [
        pl.BlockSpec((c, CONV_CH), lambda b, r: (rb(b, r), 0)),
        pl.BlockSpec((c, CONV_CH), lambda b, r: (rb(b, r), 1)),
        pl.BlockSpec((CONV_HALO, CONV_CH), lambda b, r: (prev(b, r), 0)),
        pl.BlockSpec((CONV_HALO, CONV_CH), lambda b, r: (prev(b, r), 1)),
        pl.BlockSpec((CONV_HALO, CONV_CH), lambda b, r: (nxt(b, r), 0)),
        pl.BlockSpec((CONV_HALO, CONV_CH), lambda b, r: (nxt(b, r), 1)),
        pl.BlockSpec((32, CONV_CH), lambda b, r: (0, 0)),
        cst, cst, cst,
    ]
    return pl.pallas_call(
        functools.partial(_conv_kernel, t=t, c=c),
        grid=(geom.b, geom.lat_blocks + 1),
        in_specs=in_specs,
        out_specs=pl.BlockSpec((c, CONV_CH), lambda b, r: (rb(b, r), 0)),
        out_shape=jax.ShapeDtypeStruct((geom.nt, CONV_CH), F32),
        scratch_shapes=[pltpu.VMEM((c + 2 * CONV_HALO, CONV_CH), F32), pltpu.VMEM((c, CONV_CH), F32)],
        compiler_params=_params(2, 32),
        name="conformer_conv",
    )(z, z, z, z, z, z, w, vec(conv_db), vec(ln_g), vec(ln_b))


def _mix_kernel(ret_ref, att_ref, cv_ref, gr_ref, ga_ref, gc_ref, x_ref, g1_ref, sh2_ref, sc2_ref,
                wr_ref, wa_ref, wc_ref, wo_ref, lng_ref, lnb_ref, x1_ref, h2_ref, *, alpha):
    def proj(v_ref, w_ref):
        return jnp.dot(v_ref[...].astype(BF16), w_ref[...], preferred_element_type=F32)

    merged = (jax.nn.sigmoid(gr_ref[...]) * proj(ret_ref, wr_ref)
              + jax.nn.sigmoid(ga_ref[...]) * proj(att_ref, wa_ref)
              + jax.nn.sigmoid(gc_ref[...]) * proj(cv_ref, wc_ref))
    y = jnp.dot(merged.astype(BF16), wo_ref[...], preferred_element_type=F32)
    x1 = _layer_norm(alpha * x_ref[...] + g1_ref[0] * y, lng_ref[...], lnb_ref[...])
    x1_ref[...] = x1
    h2_ref[...] = x1 * (1.0 + sc2_ref[0]) + sh2_ref[0]


def _mix_call(geom, alpha, ret, att, cv, z, x, mods, w_ret_o, w_att_o, w_conv_o, w_out, ln_g, ln_b):
    tm = geom.c
    d = D_MODEL
    tile = pl.BlockSpec((tm, d), lambda i: (i, 0))
    gate = lambda k: pl.BlockSpec((tm, d), lambda i: (i, COL_GT // d + k))
    wsp = pl.BlockSpec((d, d), lambda i: (0, 0))
    vsp = pl.BlockSpec((1, d), lambda i: (0, 0))
    return pl.pallas_call(
        functools.partial(_mix_kernel, alpha=alpha),
        grid=(geom.nt // tm,),
        in_specs=[tile, tile, tile, gate(0), gate(1), gate(2), tile,
                  _mod_spec(geom, tm, 2), _mod_spec(geom, tm, 3), _mod_spec(geom, tm, 4),
                  wsp, wsp, wsp, wsp, vsp, vsp],
        out_specs=[tile, tile],
        out_shape=[jax.ShapeDtypeStruct((geom.nt, d), F32)] * 2,
        compiler_params=_params(1, 48),
        name="merge_ln1",
    )(ret, att, cv, z, z, z, x, mods, mods, mods, w_ret_o, w_att_o, w_conv_o, w_out,
      ln_g.reshape(1, d), ln_b.reshape(1, d))


def _router_kernel(h_ref, wr_ref, bias_ref, e_ref, w_ref, pos_ref, cnt_ref, cnt):
    i = pl.program_id(0)
    tm = h_ref.shape[0]
    ne, per = N_EXPERTS, N_EXPERTS // N_GROUPS
    neg = -jnp.inf

    @pl.when(i == 0)
    def _init():
        cnt[...] = jnp.zeros(cnt.shape, F32)

    logits = jnp.dot(h_ref[...], wr_ref[...], preferred_element_type=F32, precision=HIGHEST)
    scores = jax.nn.sigmoid(logits.T[:ne])
    sel = scores + bias_ref[...]

    member = lax.broadcasted_iota(jnp.int32, (per, tm), 0)
    grp_rows = []
    for g in range(N_GROUPS):
        blk = sel[g * per:(g + 1) * per]
        m1 = jnp.max(blk, axis=0, keepdims=True)
        first = jnp.min(jnp.where(blk == m1, member, per), axis=0, keepdims=True)
        m2 = jnp.max(jnp.where(member == first, neg, blk), axis=0, keepdims=True)
        grp_rows.append(m1 + m2)
    gs = jnp.concatenate(grp_rows, axis=0)

    gidx = lax.broadcasted_iota(jnp.int32, (N_GROUPS, tm), 0)
    rank = jnp.zeros((N_GROUPS, tm), jnp.int32)
    for g in range(N_GROUPS):
        row = gs[g:g + 1]
        ahead = jnp.logical_or(row > gs, jnp.logical_and(row == gs, g < gidx))
        rank = rank + ahead.astype(jnp.int32)
    keep = (rank < TOPK_GROUPS).astype(F32)
    keep_e = jnp.concatenate([jnp.broadcast_to(keep[g:g + 1], (per, tm)) for g in range(N_GROUPS)], axis=0)
    cand = jnp.where(keep_e > 0.5, sel, neg)

    eidx = lax.broadcasted_iota(jnp.int32, (ne, tm), 0)
    picks, gates, hots = [], [], []
    chosen = jnp.zeros((ne, tm), F32)
    for _ in range(TOP_K):
        m = jnp.max(cand, axis=0, keepdims=True)
        idx = jnp.min(jnp.where(cand == m, eidx, ne), axis=0, keepdims=True)
        hot = eidx == idx
        picks.append(idx)
        gates.append(jnp.sum(jnp.where(hot, scores, 0.0), axis=0, keepdims=True))
        hots.append(hot)
        chosen = jnp.where(hot, 1.0, chosen)
        cand = jnp.where(hot, neg, cand)
    total = gates[0]
    for gk in gates[1:]:
        total = total + gk

    ti = lax.broadcasted_iota(jnp.int32, (tm, tm), 0)
    tj = lax.broadcasted_iota(jnp.int32, (tm, tm), 1)
    before = jnp.where(ti < tj, 1.0, 0.0).astype(BF16)
    prior = jnp.dot(chosen.astype(BF16), before, preferred_element_type=F32) + cnt[...][:, :1]
    pos = [jnp.sum(jnp.where(hot, prior, 0.0), axis=0, keepdims=True) for hot in hots]

    e_ref[...] = jnp.concatenate(picks, axis=0)
    w_ref[...] = jnp.concatenate([ROUTED_SCALE * gk / total for gk in gates], axis=0)
    pos_ref[...] = jnp.concatenate(pos, axis=0).astype(jnp.int32)
    cnt[...] = cnt[...] + jnp.sum(chosen, axis=1, keepdims=True)
    cnt_ref[...] = cnt[...]


def _router_call(geom, h2, w_router, router_bias):
    tm = geom.c
    wr = jnp.zeros((D_MODEL, 128), F32).at[:, :N_EXPERTS].set(w_router)
    tok = pl.BlockSpec((TOP_K, tm), lambda i: (0, i))
    return pl.pallas_call(
        _router_kernel,
        grid=(geom.nt // tm,),
        in_specs=[pl.BlockSpec((tm, D_MODEL), lambda i: (i, 0)),
                  pl.BlockSpec((D_MODEL, 128), lambda i: (0, 0)),
                  pl.BlockSpec((N_EXPERTS, 1), lambda i: (0, 0))],
        out_specs=[tok, tok, tok, pl.BlockSpec((N_EXPERTS, 128), lambda i: (0, 0))],
        out_shape=[jax.ShapeDtypeStruct((TOP_K, geom.nt), jnp.int32),
                   jax.ShapeDtypeStruct((TOP_K, geom.nt), F32),
                   jax.ShapeDtypeStruct((TOP_K, geom.nt), jnp.int32),
                   jax.ShapeDtypeStruct((N_EXPERTS, 128), F32)],
        scratch_shapes=[pltpu.VMEM((N_EXPERTS, 128), F32)],
        compiler_params=_params(1, 32),
        name="moe_router",
    )(h2, wr, router_bias.reshape(N_EXPERTS, 1))


def _row_copy(src, src_row, dst, dst_row, sem):
    return pltpu.make_async_copy(src.at[pl.ds(src_row, 1)], dst.at[pl.ds(dst_row, 1)], sem)


def _dispatch_kernel(dest_ref, h_hbm, xs_in, xs_out, sem):
    del xs_in
    tm = dest_ref.shape[1]
    base = pl.program_id(0) * tm

    def issue(tok, carry):
        for k in range(TOP_K):
            _row_copy(h_hbm, base + tok, xs_out, dest_ref[k, tok], sem).start()
        return carry

    lax.fori_loop(0, tm, issue, 0)

    def drain(tok, carry):
        for k in range(TOP_K):
            _row_copy(h_hbm, 0, xs_out, 0, sem).wait()
        return carry

    lax.fori_loop(0, tm, drain, 0)


def _dispatch_call(geom, dest, h2, n_slots):
    tm = geom.c
    zeros = jnp.zeros((n_slots, D_MODEL), F32)
    return pl.pallas_call(
        _dispatch_kernel,
        grid=(geom.nt // tm,),
        in_specs=[pl.BlockSpec((TOP_K, tm), lambda i: (0, i), memory_space=pltpu.SMEM),
                  pl.BlockSpec(memory_space=pl.ANY),
                  pl.BlockSpec(memory_space=pl.ANY)],
        out_specs=pl.BlockSpec(memory_space=pl.ANY),
        out_shape=jax.ShapeDtypeStruct((n_slots, D_MODEL), F32),
        scratch_shapes=[pltpu.SemaphoreType.DMA(())],
        input_output_aliases={2: 0},
        compiler_params=_params(1, 32),
        name="moe_dispatch",
    )(dest, h2, zeros)


def _expert_kernel(be_ref, nu_ref, x_ref, wgu_ref, wd_ref, o_ref):
    del be_ref

    @pl.when(pl.program_id(0) < nu_ref[0])
    def _run():
        hgu = jnp.dot(x_ref[...].astype(BF16), wgu_ref[0], preferred_element_type=F32)
        hid = _silu(hgu[:, :D_EXPERT]) * hgu[:, D_EXPERT:]
        o_ref[...] = jnp.dot(hid.astype(BF16), wd_ref[0], preferred_element_type=F32)


def _expert_call(block_e, n_used, xs, w_gu, w_down):
    n_blocks = xs.shape[0] // MOE_BLOCK
    live = lambda i, be, nu: jnp.minimum(i, nu[0] - 1)
    grid_spec = pltpu.PrefetchScalarGridSpec(
        num_scalar_prefetch=2,
        grid=(n_blocks,),
        in_specs=[pl.BlockSpec((MOE_BLOCK, D_MODEL), lambda i, be, nu: (live(i, be, nu), 0)),
                  pl.BlockSpec((1, D_MODEL, 2 * D_EXPERT), lambda i, be, nu: (be[live(i, be, nu)], 0, 0)),
                  pl.BlockSpec((1, D_EXPERT, D_MODEL), lambda i, be, nu: (be[live(i, be, nu)], 0, 0))],
        out_specs=pl.BlockSpec((MOE_BLOCK, D_MODEL), lambda i, be, nu: (live(i, be, nu), 0)),
    )
    return pl.pallas_call(
        _expert_kernel,
        grid_spec=grid_spec,
        out_shape=jax.ShapeDtypeStruct(xs.shape, F32),
        compiler_params=_params(1, 32),
        name="moe_experts",
    )(block_e, n_used, xs, w_gu, w_down)


def _combine_kernel(dcur_ref, dnxt_ref, wt_ref, ys_hbm, h_ref, x_ref, g2_ref, wgu_ref, wd_ref,
                    lng_ref, lnb_ref, o_ref, buf, sem, *, alpha):
    i = pl.program_id(0)
    n = pl.num_programs(0)
    tm = h_ref.shape[0]

    def issue(d_ref, slot):
        def body(tok, carry):
            for k in range(TOP_K):
                pltpu.make_async_copy(ys_hbm.at[pl.ds(d_ref[k, tok], 1)],
                                      buf.at[slot, k, pl.ds(tok, 1)], sem.at[slot]).start()
            return carry

        lax.fori_loop(0, tm, body, 0)

    @pl.when(i == 0)
    def _first():
        issue(dcur_ref, 0)

    @pl.when(i + 1 < n)
    def _ahead():
        issue(dnxt_ref, (i + 1) % 2)

    slot = i % 2

    def drain(tok, carry):
        for k in range(TOP_K):
            pltpu.make_async_copy(ys_hbm.at[pl.ds(0, 1)], buf.at[slot, k, pl.ds(0, 1)], sem.at[slot]).wait()
        return carry

    lax.fori_loop(0, tm, drain, 0)

    wt = wt_ref[...]
    routed = buf[slot, 0] * wt[:, 0:1]
    for k in range(1, TOP_K):
        routed = routed + buf[slot, k] * wt[:, k:k + 1]
    hgu = jnp.dot(h_ref[...].astype(BF16), wgu_ref[...], preferred_element_type=F32)
    hid = _silu(hgu[:, :D_SHARED]) * hgu[:, D_SHARED:]
    shared = jnp.dot(hid.astype(BF16), wd_ref[...], preferred_element_type=F32)
    o_ref[...] = _layer_norm(alpha * x_ref[...] + g2_ref[0] * (routed + shared), lng_ref[...], lnb_ref[...])


def _combine_call(geom, alpha, dest, w_tok, ys, h2, x1, mods, w_sh_gu, w_sh_down, ln_g, ln_b):
    tm = geom.c
    d = D_MODEL
    n = geom.nt // tm
    tile = pl.BlockSpec((tm, d), lambda i: (i, 0))
    vsp = pl.BlockSpec((1, d), lambda i: (0, 0))
    return pl.pallas_call(
        functools.partial(_combine_kernel, alpha=alpha),
        grid=(n,),
        in_specs=[pl.BlockSpec((TOP_K, tm), lambda i: (0, i), memory_space=pltpu.SMEM),
                  pl.BlockSpec((TOP_K, tm), lambda i: (0, jnp.minimum(i + 1, n - 1)), memory_space=pltpu.SMEM),
                  pl.BlockSpec((tm, TOP_K), lambda i: (i, 0)),
                  pl.BlockSpec(memory_space=pl.ANY),
                  tile, tile, _mod_spec(geom, tm, 5),
                  pl.BlockSpec((d, 2 * D_SHARED), lambda i: (0, 0)),
                  pl.BlockSpec((D_SHARED, d), lambda i: (0, 0)),
                  vsp, vsp],
        out_specs=tile,
        out_shape=jax.ShapeDtypeStruct((geom.nt, d), F32),
        scratch_shapes=[pltpu.VMEM((2, TOP_K, tm, d), F32), pltpu.SemaphoreType.DMA((2,))],
        compiler_params=_params(1, 48),
        name="moe_combine_ln2",
    )(dest, dest, w_tok, ys, h2, x1, mods, w_sh_gu, w_sh_down, ln_g.reshape(1, d), ln_b.reshape(1, d))


def _rope_tables(t):
    rows = t // GRID_W
    row = jnp.repeat(jnp.arange(rows, dtype=F32), GRID_W)
    col = jnp.tile(jnp.arange(GRID_W, dtype=F32), rows)
    n_freq = ATT_DH // 4
    inv_freq = ROPE_THETA ** (-jnp.arange(n_freq, dtype=F32) / n_freq)
    ang = jnp.concatenate([row[:, None] * inv_freq, col[:, None] * inv_freq], axis=-1)
    cos, sin = jnp.cos(ang), jnp.sin(ang)
    cos64 = jnp.concatenate([cos, cos], axis=-1)
    sin64 = jnp.concatenate([-sin, sin], axis=-1)
    return cos64, sin64


def kernel(x, c, ctx, c_ctx, w_ada, b_ada, w_in, ret_decay_logit, att_q_norm, att_k_norm, conv_dw, conv_db, conv_ln_g, conv_ln_b, w_ret_o, w_att_o, w_conv_o, w_out, ln1_g, ln1_b, w_router, router_bias, w_exp_gate, w_exp_up, w_exp_down, w_sh_gate, w_sh_up, w_sh_down, ln2_g, ln2_b):
    b, t, d = x.shape
    n_ctx = ctx.shape[1]
    depth = w_ada.shape[0]
    assert d == D_MODEL and w_in.shape[-1] == D_IN
    geom = _Geom(b, t, n_ctx)
    alpha = float((2 * depth) ** 0.25)

    cos64, sin64 = _rope_tables(t)
    cos128 = jnp.concatenate([cos64, cos64], axis=-1)
    sin128 = jnp.concatenate([sin64, sin64], axis=-1)

    n_rows = -(-(b + 1) // 8) * 8
    cvecs = jnp.zeros((n_rows, d), F32).at[:b].set(c).at[b].set(c_ctx)
    mods_all = _mods_call(cvecs, w_ada, b_ada).reshape(depth, n_rows * 6, 1, d)

    perm = _column_permutation()
    n_blocks = -(-(geom.nt * TOP_K + N_EXPERTS * (MOE_BLOCK - 1)) // MOE_BLOCK)
    n_slots = n_blocks * MOE_BLOCK

    xt = jnp.concatenate([x.reshape(geom.nl, d), ctx.reshape(geom.nc, d)], axis=0)
    for l in range(depth):
        mods = mods_all[l]
        w_in_l = jnp.take(w_in[l], perm, axis=1).astype(BF16)
        z = _inproj_call(geom, xt, mods, w_in_l)

        log_gamma = jax.nn.log_sigmoid(ret_decay_logit[l].astype(F32))
        ret = _retention_call(geom, z, log_gamma, cos128, sin128)
        att = _attention_call(geom, z, att_q_norm[l], att_k_norm[l], cos64, sin64)
        cv = _conv_call(geom, z, conv_dw[l], conv_db[l], conv_ln_g[l], conv_ln_b[l])
        x1, h2 = _mix_call(geom, alpha, ret, att, cv, z, xt, mods,
                           w_ret_o[l].astype(BF16), w_att_o[l].astype(BF16), w_conv_o[l].astype(BF16),
                           w_out[l].astype(BF16), ln1_g[l], ln1_b[l])

        top_e, gate_w, pos, counts = _router_call(geom, h2, w_router[l], router_bias[l])
        cnt = counts[:, 0].astype(jnp.int32)
        blocks_e = (cnt + MOE_BLOCK - 1) // MOE_BLOCK
        blocks_end = jnp.cumsum(blocks_e)
        start_row = (blocks_end - blocks_e) * MOE_BLOCK
        dest = start_row[top_e] + pos
        block_e = jnp.minimum(jnp.searchsorted(blocks_end, jnp.arange(n_blocks, dtype=jnp.int32), side="right"),
                              N_EXPERTS - 1).astype(jnp.int32)
        n_used = blocks_end[-1:].astype(jnp.int32)

        xs = _dispatch_call(geom, dest, h2, n_slots)
        w_gu = jnp.concatenate([w_exp_gate[l], w_exp_up[l]], axis=-1).astype(BF16)
        ys = _expert_call(block_e, n_used, xs, w_gu, w_exp_down[l].astype(BF16))
        w_sh_gu = jnp.concatenate([w_sh_gate[l], w_sh_up[l]], axis=-1).astype(BF16)
        xt = _combine_call(geom, alpha, dest, gate_w.T, ys, h2, x1, mods, w_sh_gu,
                           w_sh_down[l].astype(BF16), ln2_g[l], ln2_b[l])
    return xt[:geom.nl].reshape(b, t, d)
```

```python
import functools

import jax
import jax.numpy as jnp
from jax import lax
from jax.experimental import pallas as pl
from jax.experimental.pallas import tpu as pltpu

F32 = jnp.float32
BF16 = jnp.bfloat16
HIGHEST = lax.Precision.HIGHEST

D_MODEL = 1024
GRID_W = 64
EPS = 1e-6

RET_HEADS = 8
RET_DK = 64
RET_DV = 128
RET_CHUNK = 128
RET_W = RET_HEADS * RET_DV

ATT_HEADS = 16
ATT_KV_HEADS = 4
ATT_DH = 64
ATT_GROUP = ATT_HEADS // ATT_KV_HEADS
ATT_W = ATT_HEADS * ATT_DH
ROPE_THETA = 10000.0
ATT_KEY_BLOCK = 512

CONV_CH = 1024
CONV_K = 31
CONV_HALO = 16

N_EXPERTS = 64
TOP_K = 8
N_GROUPS = 8
TOPK_GROUPS = 4
D_EXPERT = 256
D_SHARED = 256
ROUTED_SCALE = 2.5
MOE_BLOCK = 256

_ORIG = dict(rq=0, rk=512, rv=1024, rg=2048, aq=3072, ak=4096, av=4352, cu=4608, gt=6656)
D_IN = 9728
COL_CU = 0
COL_GT = 2048
COL_RG = 5120
COL_RV = 6144
COL_RQ = 7168
COL_RK = 7680
COL_ATT = 8192
ATT_SECTION = ATT_GROUP * ATT_DH + 2 * ATT_DH


def _column_ranges():
    rng = [(_ORIG["cu"], _ORIG["cu"] + 2 * CONV_CH),
           (_ORIG["gt"], _ORIG["gt"] + 3 * D_MODEL),
           (_ORIG["rg"], _ORIG["rg"] + RET_W),
           (_ORIG["rv"], _ORIG["rv"] + RET_W),
           (_ORIG["rq"], _ORIG["rq"] + RET_HEADS * RET_DK),
           (_ORIG["rk"], _ORIG["rk"] + RET_HEADS * RET_DK)]
    for g in range(ATT_KV_HEADS):
        rng.append((_ORIG["aq"] + g * ATT_GROUP * ATT_DH, _ORIG["aq"] + (g + 1) * ATT_GROUP * ATT_DH))
        rng.append((_ORIG["ak"] + g * ATT_DH, _ORIG["ak"] + (g + 1) * ATT_DH))
        rng.append((_ORIG["av"] + g * ATT_DH, _ORIG["av"] + (g + 1) * ATT_DH))
    cols = [c for a, b in rng for c in range(a, b)]
    assert sorted(cols) == list(range(D_IN))
    return rng


def _permute_columns(w):
    return jnp.concatenate([w[:, a:b] for a, b in _column_ranges()], axis=1)


def _params(n_axes, vmem_mib):
    return pltpu.CompilerParams(dimension_semantics=("arbitrary",) * n_axes,
                                vmem_limit_bytes=vmem_mib * 1024 * 1024)


def _silu(v):
    return v * jax.nn.sigmoid(v)


def _layer_norm(v, g, b):
    mu = jnp.mean(v, axis=-1, keepdims=True)
    d = v - mu
    var = jnp.mean(d * d, axis=-1, keepdims=True)
    return d * lax.rsqrt(var + EPS) * g + b


def _mods_kernel(c_ref, w_ref, b_ref, o_ref):
    s = _silu(c_ref[...])
    o_ref[0] = jnp.dot(s, w_ref[0], preferred_element_type=F32, precision=HIGHEST) + b_ref[0]


def _mods_call(cvecs, w_ada, b_ada):
    n_layers = w_ada.shape[0]
    rows, d = cvecs.shape
    return pl.pallas_call(
        _mods_kernel,
        grid=(n_layers, 6),
        in_specs=[pl.BlockSpec((rows, d), lambda l, j: (0, 0)),
                  pl.BlockSpec((1, d, d), lambda l, j: (l, 0, j)),
                  pl.BlockSpec((1, 1, d), lambda l, j: (l, 0, j))],
        out_specs=pl.BlockSpec((1, rows, d), lambda l, j: (l, 0, j)),
        out_shape=jax.ShapeDtypeStruct((n_layers, rows, 6 * d), F32),
        compiler_params=_params(2, 32),
        name="adaln_mods",
    )(cvecs, w_ada, b_ada.reshape(n_layers, 1, 6 * d))


class _Geom:
    def __init__(self, b, t, c):
        assert t % c == 0 and c % RET_CHUNK == 0 and c % CONV_HALO == 0 and t % ATT_KEY_BLOCK == 0
        self.b, self.t, self.c = b, t, c
        self.nl, self.nc = b * t, b * c
        self.nt = self.nl + self.nc
        self.lat_blocks = t // c
        self.nlb = self.nl // c
        self.p = t + c

    def row_block(self, bi, r):
        return jnp.where(r < self.lat_blocks, bi * self.lat_blocks + r, self.nlb + bi)

    def mod_row(self, i, tm):
        return jnp.where(i * tm < self.nl, (i * tm) // self.t, self.b)


def _mod_spec(geom, tm, which, grid_pos=0):
    d = D_MODEL
    if grid_pos == 0:
        return pl.BlockSpec((1, 1, d), lambda i, *_: (geom.mod_row(i, tm) * 6 + which, 0, 0))
    return pl.BlockSpec((1, 1, d), lambda j, i: (geom.mod_row(i, tm) * 6 + which, 0, 0))


def _inproj_kernel(x_ref, sh_ref, sc_ref, w_ref, o_ref):
    h = x_ref[...] * (1.0 + sc_ref[0]) + sh_ref[0]
    o_ref[...] = jnp.dot(h.astype(BF16), w_ref[...], preferred_element_type=F32)


def _inproj_call(geom, x, mods, w_in_bf16):
    tm = 512 if geom.nc % 512 == 0 and geom.t % 512 == 0 else geom.c
    tn = D_IN // 4
    return pl.pallas_call(
        _inproj_kernel,
        grid=(D_IN // tn, geom.nt // tm),
        in_specs=[pl.BlockSpec((tm, D_MODEL), lambda j, i: (i, 0)),
                  _mod_spec(geom, tm, 0, grid_pos=1),
                  _mod_spec(geom, tm, 1, grid_pos=1),
                  pl.BlockSpec((D_MODEL, tn), lambda j, i: (0, j))],
        out_specs=pl.BlockSpec((tm, tn), lambda j, i: (i, j)),
        out_shape=jax.ShapeDtypeStruct((geom.nt, D_IN), F32),
        compiler_params=_params(2, 48),
        name="in_proj",
    )(x, mods, mods, w_in_bf16)


def _rot_half_128(v):
    lane = lax.broadcasted_iota(jnp.int32, v.shape, 1)
    return jnp.where((lane % 64) < 32, pltpu.roll(v, 96, 1), pltpu.roll(v, 32, 1))


def _ret_kernel(lg_ref, ql_ref, qc_ref, kl_ref, kc_ref, vl_ref, vc_ref, g_ref, cos_ref, sin_ref, o_ref,
                qs, kts, yf, yb, st, dm, qwb, kwb, gcs, *, t, c):
    ch = RET_CHUNK
    hp = pl.program_id(1)
    r = pl.program_id(2)
    lat_blocks = t // c
    n_lat, n_ctx = t // ch, c // ch

    @pl.when(r == 0)
    def _scan():
        ri = lax.broadcasted_iota(jnp.int32, (ch, ch), 0).astype(F32)
        ci = lax.broadcasted_iota(jnp.int32, (ch, ch), 1).astype(F32)
        for d in range(2):
            for h in range(2):
                u = 2 * d + h
                lg = lg_ref[d, 2 * hp + h]
                rel = (ri - ci) if d == 0 else (ci - ri)
                dm[u] = jnp.where(rel >= 0.0, jnp.exp(lg * jnp.maximum(rel, 0.0)), 0.0)
                qwb[u] = jnp.exp(lg * ((ri + 1.0) if d == 0 else (float(ch) - ri)))
                kwb[u] = jnp.exp(lg * ((float(ch) - 1.0 - ri) if d == 0 else ri))
                gcs[u] = jnp.exp(jnp.full((RET_DK, RET_DV), lg * float(ch), F32))
                st[u] = jnp.zeros((RET_DK, RET_DV), F32)

        def stage(q, k, seq_rows):
            qs[0, seq_rows, :] = q[:, :RET_DK].astype(BF16)
            qs[1, seq_rows, :] = q[:, RET_DK:].astype(BF16)
            kt = k.T
            kts[0, :, seq_rows] = kt[:RET_DK].astype(BF16)
            kts[1, :, seq_rows] = kt[RET_DK:].astype(BF16)

        kscale = RET_DK ** -0.5
        for cc in range(n_ctx):
            rows = pl.ds(cc * ch, ch)
            stage(qc_ref[rows, :], kc_ref[rows, :] * kscale, rows)

        def stage_lat(cc, carry):
            rows = pl.ds(pl.multiple_of(cc * ch, ch), ch)
            cs, sn = cos_ref[rows, :], sin_ref[rows, :]
            q = ql_ref[rows, :]
            k = kl_ref[rows, :]
            q = q * cs + _rot_half_128(q) * sn
            k = (k * cs + _rot_half_128(k) * sn) * kscale
            stage(q, k, pl.ds(pl.multiple_of(c + cc * ch, ch), ch))
            return carry

        lax.fori_loop(0, n_lat, stage_lat, 0)

        def run_segment(v_ref, seq_off, n):
            def body(i, carry):
                for d, cc in ((0, i), (1, n - 1 - i)):
                    vrows = pl.ds(pl.multiple_of(cc * ch, ch), ch)
                    srows = pl.ds(pl.multiple_of(seq_off + cc * ch, ch), ch)
                    for h in range(2):
                        u = 2 * d + h
                        q = qs[h, srows, :]
                        kt = kts[h, :, srows]
                        v = v_ref[vrows, h * RET_DV:(h + 1) * RET_DV]
                        s = jnp.dot(q, kt, preferred_element_type=F32)
                        y = jnp.dot((s * dm[u]).astype(BF16), v.astype(BF16), preferred_element_type=F32)
                        state = st[u]
                        y = y + jnp.dot(q, state.astype(BF16), preferred_element_type=F32) * qwb[u]
                        dst = yf if d == 0 else yb
                        dst[srows, h * RET_DV:(h + 1) * RET_DV] = y
                        kv = jnp.dot(kt, (v * kwb[u]).astype(BF16), preferred_element_type=F32)
                        st[u] = gcs[u] * state + kv
                return carry

            lax.fori_loop(0, n, body, 0)

        run_segment(vc_ref, 0, n_ctx)
        run_segment(vl_ref, c, n_lat)

    def finish(srows):
        y = yf[srows, :] + yb[srows, :]
        for h in range(2):
            cols = slice(h * RET_DV, (h + 1) * RET_DV)
            yh = y[:, cols]
            mu = jnp.mean(yh, axis=-1, keepdims=True)
            dlt = yh - mu
            var = jnp.mean(dlt * dlt, axis=-1, keepdims=True)
            o_ref[:, cols] = _silu(g_ref[:, cols]) * (dlt * lax.rsqrt(var + EPS))

    @pl.when(r < lat_blocks)
    def _fin_lat():
        finish(pl.ds(pl.multiple_of(c + r * c, c), c))

    @pl.when(r == lat_blocks)
    def _fin_ctx():
        finish(pl.ds(0, c))


def _retention_call(geom, z, log_gamma, cos128, sin128):
    t, c, p = geom.t, geom.c, geom.p
    hpairs = RET_HEADS // 2
    qb, kb = COL_RQ // 128, COL_RK // 128
    vb, gb = COL_RV // 256, COL_RG // 256
    rb = geom.row_block
    in_specs = [
        pl.BlockSpec(memory_space=pltpu.SMEM),
        pl.BlockSpec((t, 128), lambda b, h, r: (b, qb + h)),
        pl.BlockSpec((c, 128), lambda b, h, r: (geom.nlb + b, qb + h)),
        pl.BlockSpec((t, 128), lambda b, h, r: (b, kb + h)),
        pl.BlockSpec((c, 128), lambda b, h, r: (geom.nlb + b, kb + h)),
        pl.BlockSpec((t, 256), lambda b, h, r: (b, vb + h)),
        pl.BlockSpec((c, 256), lambda b, h, r: (geom.nlb + b, vb + h)),
        pl.BlockSpec((c, 256), lambda b, h, r: (rb(b, r), gb + h)),
        pl.BlockSpec((t, 128), lambda b, h, r: (0, 0)),
        pl.BlockSpec((t, 128), lambda b, h, r: (0, 0)),
    ]
    scratch = [
        pltpu.VMEM((2, p, RET_DK), BF16),
        pltpu.VMEM((2, RET_DK, p), BF16),
        pltpu.VMEM((p, 2 * RET_DV), F32),
        pltpu.VMEM((p, 2 * RET_DV), F32),
        pltpu.VMEM((4, RET_DK, RET_DV), F32),
        pltpu.VMEM((4, RET_CHUNK, RET_CHUNK), F32),
        pltpu.VMEM((4, RET_CHUNK, RET_CHUNK), F32),
        pltpu.VMEM((4, RET_CHUNK, RET_CHUNK), F32),
        pltpu.VMEM((4, RET_DK, RET_DV), F32),
    ]
    return pl.pallas_call(
        functools.partial(_ret_kernel, t=t, c=c),
        grid=(geom.b, hpairs, geom.lat_blocks + 1),
        in_specs=in_specs,
        out_specs=pl.BlockSpec((c, 256), lambda b, h, r: (rb(b, r), h)),
        out_shape=jax.ShapeDtypeStruct((geom.nt, RET_W), F32),
        scratch_shapes=scratch,
        compiler_params=_params(3, 56),
        name="retention",
    )(log_gamma, z, z, z, z, z, z, z, cos128, sin128)


def _rms_head(v, g):
    return v * lax.rsqrt(jnp.mean(v * v, axis=-1, keepdims=True) + EPS) * g


def _rope_head(v, cs, sn):
    half = ATT_DH // 2
    return v * cs + jnp.concatenate([v[:, half:], v[:, :half]], axis=1) * sn


def _att_kernel(qa_ref, qb_ref, kvl_ref, kvc_ref, qn_ref, kn_ref, cos_ref, sin_ref, o_ref,
                ks, vs, m_s, l_s, acc_s, *, t, c):
    r = pl.program_id(2)
    lat_blocks = t // c
    dh = ATT_DH
    tk = ATT_KEY_BLOCK

    @pl.when(r == 0)
    def _stage_kv():
        kn = kn_ref[...]
        kv = kvc_ref[...]
        ks[0:c, :] = _rms_head(kv[:, :dh], kn).astype(BF16)
        vs[0:c, :] = kv[:, dh:].astype(BF16)

        def stage(i, carry):
            rows = pl.ds(pl.multiple_of(i * c, c), c)
            dst = pl.ds(pl.multiple_of(c + i * c, c), c)
            kv = kvl_ref[rows, :]
            k = _rope_head(_rms_head(kv[:, :dh], kn), cos_ref[rows, :], sin_ref[rows, :])
            ks[dst, :] = k.astype(BF16)
            vs[dst, :] = kv[:, dh:].astype(BF16)
            return carry

        lax.fori_loop(0, lat_blocks, stage, 0)

    is_ctx = r == lat_blocks
    rows = pl.ds(pl.multiple_of(jnp.minimum(r, lat_blocks - 1) * c, c), c)
    cs, sn = cos_ref[rows, :], sin_ref[rows, :]
    qn = qn_ref[...]
    heads = []
    for src in (qa_ref, qb_ref):
        x = src[...]
        for half in range(2):
            xn = _rms_head(x[:, half * dh:(half + 1) * dh], qn)
            xr = jnp.where(is_ctx, xn, _rope_head(xn, cs, sn))
            heads.append((xr * (dh ** -0.5)).astype(BF16))
    q = jnp.concatenate(heads, axis=0)

    m_s[...] = jnp.full(m_s.shape, -jnp.inf, F32)
    l_s[...] = jnp.zeros(l_s.shape, F32)
    acc_s[...] = jnp.zeros(acc_s.shape, F32)

    def flash_step(k, v):
        n = k.shape[0]
        s = lax.dot_general(q, k, (((1,), (1,)), ((), ())), preferred_element_type=F32)
        m_prev = m_s[...]
        m_next = jnp.maximum(m_prev, jnp.max(s, axis=1, keepdims=True))
        prob = jnp.exp(s - jnp.concatenate([m_next] * (n // 128), axis=1))
        alpha = jnp.exp(m_prev - m_next)
        l_s[...] = alpha * l_s[...] + jnp.sum(prob, axis=1, keepdims=True)
        m_s[...] = m_next
        acc_s[...] = acc_s[...] * alpha[:, :dh] + jnp.dot(prob.astype(BF16), v, preferred_element_type=F32)

    flash_step(ks[0:c, :], vs[0:c, :])

    def lat_step(j, carry):
        krows = pl.ds(pl.multiple_of(c + j * tk, 128), tk)
        flash_step(ks[krows, :], vs[krows, :])
        return carry

    lax.fori_loop(0, jnp.where(is_ctx, 0, t // tk), lat_step, 0)

    out = acc_s[...] / l_s[...][:, :dh]
    for h in range(ATT_GROUP):
        o_ref[:, h * dh:(h + 1) * dh] = out[h * c:(h + 1) * c, :]


def _attention_call(geom, z, q_norm, k_norm, cos64, sin64):
    t, c, p = geom.t, geom.c, geom.p
    ab = COL_ATT // 128
    sec = ATT_SECTION // 128
    rb = geom.row_block
    in_specs = [
        pl.BlockSpec((c, 128), lambda b, g, r: (rb(b, r), ab + sec * g)),
        pl.BlockSpec((c, 128), lambda b, g, r: (rb(b, r), ab + sec * g + 1)),
        pl.BlockSpec((t, 128), lambda b, g, r: (b, ab + sec * g + 2)),
        pl.BlockSpec((c, 128), lambda b, g, r: (geom.nlb + b, ab + sec * g + 2)),
        pl.BlockSpec((1, ATT_DH), lambda b, g, r: (0, 0)),
        pl.BlockSpec((1, ATT_DH), lambda b, g, r: (0, 0)),
        pl.BlockSpec((t, ATT_DH), lambda b, g, r: (0, 0)),
        pl.BlockSpec((t, ATT_DH), lambda b, g, r: (0, 0)),
    ]
    scratch = [
        pltpu.VMEM((p, ATT_DH), BF16),
        pltpu.VMEM((p, ATT_DH), BF16),
        pltpu.VMEM((ATT_GROUP * c, 128), F32),
        pltpu.VMEM((ATT_GROUP * c, 128), F32),
        pltpu.VMEM((ATT_GROUP * c, ATT_DH), F32),
    ]
    return pl.pallas_call(
        functools.partial(_att_kernel, t=t, c=c),
        grid=(geom.b, ATT_KV_HEADS, geom.lat_blocks + 1),
        in_specs=in_specs,
        out_specs=pl.BlockSpec((c, ATT_GROUP * ATT_DH), lambda b, g, r: (rb(b, r), g)),
        out_shape=jax.ShapeDtypeStruct((geom.nt, ATT_W), F32),
        scratch_shapes=scratch,
        compiler_params=_params(3, 48),
        name="attention",
    )(z, z, z, z, q_norm.reshape(1, ATT_DH), k_norm.reshape(1, ATT_DH), cos64, sin64)


def _conv_kernel(a_ref, g_ref, ap_ref, gp_ref, an_ref, gn_ref, w_ref, b_ref, lng_ref, lnb_ref, o_ref,
                 ext, ys, *, t, c):
    r = pl.program_id(1)
    lat_blocks = t // c
    halo = CONV_HALO
    has_prev = jnp.logical_and(r != 0, r != lat_blocks)
    has_next = jnp.logical_and(r != lat_blocks - 1, r != lat_blocks)
    ext[halo:halo + c, :] = a_ref[...] * jax.nn.sigmoid(g_ref[...])
    ext[0:halo, :] = jnp.where(has_prev, ap_ref[...] * jax.nn.sigmoid(gp_ref[...]), 0.0)
    ext[halo + c:, :] = jnp.where(has_next, an_ref[...] * jax.nn.sigmoid(gn_ref[...]), 0.0)

    rt = 64
    first = halo - CONV_K // 2

    def lane_block(cb, carry):
        lanes = pl.ds(pl.multiple_of(cb * 128, 128), 128)
        for ti in range(c // rt):
            acc = jnp.zeros((rt, 128), F32)
            for j in range(CONV_K):
                acc = acc + w_ref[pl.ds(j, 1), lanes] * ext[pl.ds(ti * rt + first + j, rt), lanes]
            ys[pl.ds(ti * rt, rt), lanes] = acc
        return carry

    lax.fori_loop(0, CONV_CH // 128, lane_block, 0)
    y = ys[...] + b_ref[...]
    o_ref[...] = _silu(_layer_norm(y, lng_ref[...], lnb_ref[...]))


def _conv_call(geom, z, conv_dw, conv_db, ln_g, ln_b):
    t, c = geom.t, geom.c
    rb = geom.row_block
    hb = c // CONV_HALO
    last = geom.nt // CONV_HALO - 1
    prev = lambda b, r: jnp.maximum(rb(b, r) * hb - 1, 0)
    nxt = lambda b, r: jnp.minimum((rb(b, r) + 1) * hb, last)
    w = jnp.zeros((32, CONV_CH), F32).at[:CONV_K].set(conv_dw)
    vec = lambda v: v.reshape(1, CONV_CH)
    cst = pl.BlockSpec((1, CONV_CH), lambda b, r: (0, 0))
    in_specs = [
        pl.BlockSpec((c, CONV_CH), lambda b, r: (rb(b, r), 0)),
        pl.BlockSpec((c, CONV_CH), lambda b, r: (rb(b, r), 1)),
        pl.BlockSpec((CONV_HALO, CONV_CH), lambda b, r: (prev(b, r), 0)),
        pl.BlockSpec((CONV_HALO, CONV_CH), lambda b, r: (prev(b, r), 1)),
        pl.BlockSpec((CONV_HALO, CONV_CH), lambda b, r: (nxt(b, r), 0)),
        pl.BlockSpec((CONV_HALO, CONV_CH), lambda b, r: (nxt(b, r), 1)),
        pl.BlockSpec((32, CONV_CH), lambda b, r: (0, 0)),
        cst, cst, cst,
    ]
    return pl.pallas_call(
        functools.partial(_conv_kernel, t=t, c=c),
        grid=(geom.b, geom.lat_blocks + 1),
        in_specs=in_specs,
        out_specs=pl.BlockSpec((c, CONV_CH), lambda b, r: (rb(b, r), 0)),
        out_shape=jax.ShapeDtypeStruct((geom.nt, CONV_CH), F32),
        scratch_shapes=[pltpu.VMEM((c + 2 * CONV_HALO, CONV_CH), F32), pltpu.VMEM((c, CONV_CH), F32)],
        compiler_params=_params(2, 32),
        name="conformer_conv",
    )(z, z, z, z, z, z, w, vec(conv_db), vec(ln_g), vec(ln_b))


def _mix_kernel(ret_ref, att_ref, cv_ref, gr_ref, ga_ref, gc_ref, x_ref, g1_ref, sh2_ref, sc2_ref,
                wr_ref, wa_ref, wc_ref, wo_ref, lng_ref, lnb_ref, x1_ref, h2_ref, *, alpha):
    def proj(v_ref, w_ref):
        return jnp.dot(v_ref[...].astype(BF16), w_ref[...], preferred_element_type=F32)

    merged = (jax.nn.sigmoid(gr_ref[...]) * proj(ret_ref, wr_ref)
              + jax.nn.sigmoid(ga_ref[...]) * proj(att_ref, wa_ref)
              + jax.nn.sigmoid(gc_ref[...]) * proj(cv_ref, wc_ref))
    y = jnp.dot(merged.astype(BF16), wo_ref[...], preferred_element_type=F32)
    x1 = _layer_norm(alpha * x_ref[...] + g1_ref[0] * y, lng_ref[...], lnb_ref[...])
    x1_ref[...] = x1
    h2_ref[...] = x1 * (1.0 + sc2_ref[0]) + sh2_ref[0]


def _mix_call(geom, alpha, ret, att, cv, z, x, mods, w_ret_o, w_att_o, w_conv_o, w_out, ln_g, ln_b):
    tm = geom.c
    d = D_MODEL
    tile = pl.BlockSpec((tm, d), lambda i: (i, 0))
    gate = lambda k: pl.BlockSpec((tm, d), lambda i: (i, COL_GT // d + k))
    wsp = pl.BlockSpec((d, d), lambda i: (0, 0))
    vsp = pl.BlockSpec((1, d), lambda i: (0, 0))
    return pl.pallas_call(
        functools.partial(_mix_kernel, alpha=alpha),
        grid=(geom.nt // tm,),
        in_specs=[tile, tile, tile, gate(0), gate(1), gate(2), tile,
                  _mod_spec(geom, tm, 2), _mod_spec(geom, tm, 3), _mod_spec(geom, tm, 4),
                  wsp, wsp, wsp, wsp, vsp, vsp],
        out_specs=[tile, tile],
        out_shape=[jax.ShapeDtypeStruct((geom.nt, d), F32)] * 2,
        compiler_params=_params(1, 48),
        name="merge_ln1",
    )(ret, att, cv, z, z, z, x, mods, mods, mods, w_ret_o, w_att_o, w_conv_o, w_out,
      ln_g.reshape(1, d), ln_b.reshape(1, d))


def _router_kernel(h_ref, wr_ref, bias_ref, e_ref, w_ref, pos_ref, cnt_ref, cnt):
    i = pl.program_id(0)
    tm = h_ref.shape[0]
    ne, per = N_EXPERTS, N_EXPERTS // N_GROUPS
    neg = -jnp.inf

    @pl.when(i == 0)
    def _init():
        cnt[...] = jnp.zeros(cnt.shape, F32)

    logits = jnp.dot(h_ref[...], wr_ref[...], preferred_element_type=F32, precision=HIGHEST)
    scores = jax.nn.sigmoid(logits.T[:ne])
    sel = scores + bias_ref[...]

    member = lax.broadcasted_iota(jnp.int32, (per, tm), 0)
    grp_rows = []
    for g in range(N_GROUPS):
        blk = sel[g * per:(g + 1) * per]
        m1 = jnp.max(blk, axis=0, keepdims=True)
        first = jnp.min(jnp.where(blk == m1, member, per), axis=0, keepdims=True)
        m2 = jnp.max(jnp.where(member == first, neg, blk), axis=0, keepdims=True)
        grp_rows.append(m1 + m2)
    gs = jnp.concatenate(grp_rows, axis=0)

    gidx = lax.broadcasted_iota(jnp.int32, (N_GROUPS, tm), 0)
    rank = jnp.zeros((N_GROUPS, tm), jnp.int32)
    for g in range(N_GROUPS):
        row = gs[g:g + 1]
        ahead = jnp.logical_or(row > gs, jnp.logical_and(row == gs, g < gidx))
        rank = rank + ahead.astype(jnp.int32)
    keep = (rank < TOPK_GROUPS).astype(F32)
    keep_e = jnp.concatenate([jnp.broadcast_to(keep[g:g + 1], (per, tm)) for g in range(N_GROUPS)], axis=0)
    cand = jnp.where(keep_e > 0.5, sel, neg)

    eidx = lax.broadcasted_iota(jnp.int32, (ne, tm), 0)
    picks, gates, hots = [], [], []
    chosen = jnp.zeros((ne, tm), F32)
    for _ in range(TOP_K):
        m = jnp.max(cand, axis=0, keepdims=True)
        idx = jnp.min(jnp.where(cand == m, eidx, ne), axis=0, keepdims=True)
        hot = eidx == idx
        picks.append(idx)
        gates.append(jnp.sum(jnp.where(hot, scores, 0.0), axis=0, keepdims=True))
        hots.append(hot)
        chosen = jnp.where(hot, 1.0, chosen)
        cand = jnp.where(hot, neg, cand)
    total = gates[0]
    for gk in gates[1:]:
        total = total + gk

    ti = lax.broadcasted_iota(jnp.int32, (tm, tm), 0)
    tj = lax.broadcasted_iota(jnp.int32, (tm, tm), 1)
    before = jnp.where(ti < tj, 1.0, 0.0).astype(BF16)
    prior = jnp.dot(chosen.astype(BF16), before, preferred_element_type=F32) + cnt[...][:, :1]
    pos = [jnp.sum(jnp.where(hot, prior, 0.0), axis=0, keepdims=True) for hot in hots]

    e_ref[...] = jnp.concatenate(picks, axis=0)
    w_ref[...] = jnp.concatenate([ROUTED_SCALE * gk / total for gk in gates], axis=0)
    pos_ref[...] = jnp.concatenate(pos, axis=0).astype(jnp.int32)
    cnt[...] = cnt[...] + jnp.sum(chosen, axis=1, keepdims=True)
    cnt_ref[...] = cnt[...]


def _router_call(geom, h2, w_router, router_bias):
    tm = geom.c
    wr = jnp.zeros((D_MODEL, 128), F32).at[:, :N_EXPERTS].set(w_router)
    tok = pl.BlockSpec((TOP_K, tm), lambda i: (0, i))
    return pl.pallas_call(
        _router_kernel,
        grid=(geom.nt // tm,),
        in_specs=[pl.BlockSpec((tm, D_MODEL), lambda i: (i, 0)),
                  pl.BlockSpec((D_MODEL, 128), lambda i: (0, 0)),
                  pl.BlockSpec((N_EXPERTS, 1), lambda i: (0, 0))],
        out_specs=[tok, tok, tok, pl.BlockSpec((N_EXPERTS, 128), lambda i: (0, 0))],
        out_shape=[jax.ShapeDtypeStruct((TOP_K, geom.nt), jnp.int32),
                   jax.ShapeDtypeStruct((TOP_K, geom.nt), F32),
                   jax.ShapeDtypeStruct((TOP_K, geom.nt), jnp.int32),
                   jax.ShapeDtypeStruct((N_EXPERTS, 128), F32)],
        scratch_shapes=[pltpu.VMEM((N_EXPERTS, 128), F32)],
        compiler_params=_params(1, 32),
        name="moe_router",
    )(h2, wr, router_bias.reshape(N_EXPERTS, 1))


def _row_copy(src, src_row, dst, dst_row, sem):
    return pltpu.make_async_copy(src.at[pl.ds(src_row, 1)], dst.at[pl.ds(dst_row, 1)], sem)


def _dispatch_kernel(dest_ref, h_ref, xs_in, xs_out, sem):
    del xs_in
    tm = dest_ref.shape[1]

    def issue(tok, carry):
        for k in range(TOP_K):
            _row_copy(h_ref, tok, xs_out, dest_ref[k, tok], sem).start()
        return carry

    lax.fori_loop(0, tm, issue, 0)

    def drain(tok, carry):
        for k in range(TOP_K):
            _row_copy(h_ref, 0, xs_out, 0, sem).wait()
        return carry

    lax.fori_loop(0, tm, drain, 0)


def _dispatch_call(geom, dest, h2, n_slots):
    tm = geom.c
    zeros = jnp.zeros((n_slots, D_MODEL), F32)
    return pl.pallas_call(
        _dispatch_kernel,
        grid=(geom.nt // tm,),
        in_specs=[pl.BlockSpec((TOP_K, tm), lambda i: (0, i), memory_space=pltpu.SMEM),
                  pl.BlockSpec((tm, D_MODEL), lambda i: (i, 0)),
                  pl.BlockSpec(memory_space=pl.ANY)],
        out_specs=pl.BlockSpec(memory_space=pl.ANY),
        out_shape=jax.ShapeDtypeStruct((n_slots, D_MODEL), F32),
        scratch_shapes=[pltpu.SemaphoreType.DMA(())],
        input_output_aliases={2: 0},
        compiler_params=_params(1, 32),
        name="moe_dispatch",
    )(dest, h2, zeros)


def _expert_kernel(be_ref, nu_ref, x_ref, wgu_ref, wd_ref, o_ref):
    del be_ref

    @pl.when(pl.program_id(0) < nu_ref[0])
    def _run():
        hgu = jnp.dot(x_ref[...].astype(BF16), wgu_ref[0], preferred_element_type=F32)
        hid = _silu(hgu[:, :D_EXPERT]) * hgu[:, D_EXPERT:]
        o_ref[...] = jnp.dot(hid.astype(BF16), wd_ref[0], preferred_element_type=F32)


def _expert_call(block_e, n_used, xs, w_gu, w_down):
    n_blocks = xs.shape[0] // MOE_BLOCK
    live = lambda i, be, nu: jnp.minimum(i, nu[0] - 1)
    grid_spec = pltpu.PrefetchScalarGridSpec(
        num_scalar_prefetch=2,
        grid=(n_blocks,),
        in_specs=[pl.BlockSpec((MOE_BLOCK, D_MODEL), lambda i, be, nu: (live(i, be, nu), 0)),
                  pl.BlockSpec((1, D_MODEL, 2 * D_EXPERT), lambda i, be, nu: (be[live(i, be, nu)], 0, 0)),
                  pl.BlockSpec((1, D_EXPERT, D_MODEL), lambda i, be, nu: (be[live(i, be, nu)], 0, 0))],
        out_specs=pl.BlockSpec((MOE_BLOCK, D_MODEL), lambda i, be, nu: (live(i, be, nu), 0)),
    )
    return pl.pallas_call(
        _expert_kernel,
        grid_spec=grid_spec,
        out_shape=jax.ShapeDtypeStruct(xs.shape, F32),
        compiler_params=_params(1, 32),
        name="moe_experts",
    )(block_e, n_used, xs, w_gu, w_down)


def _combine_kernel(dcur_ref, dnxt_ref, wt_ref, ys_hbm, h_ref, x_ref, g2_ref, wgu_ref, wd_ref,
                    lng_ref, lnb_ref, o_ref, buf, sem, *, alpha):
    i = pl.program_id(0)
    n = pl.num_programs(0)
    tm = h_ref.shape[0]

    def issue(d_ref, slot):
        def body(tok, carry):
            for k in range(TOP_K):
                pltpu.make_async_copy(ys_hbm.at[pl.ds(d_ref[k, tok], 1)],
                                      buf.at[slot, k, pl.ds(tok, 1)], sem.at[slot]).start()
            return carry

        lax.fori_loop(0, tm, body, 0)

    @pl.when(i == 0)
    def _first():
        issue(dcur_ref, 0)

    @pl.when(i + 1 < n)
    def _ahead():
        issue(dnxt_ref, (i + 1) % 2)

    slot = i % 2

    def drain(tok, carry):
        for k in range(TOP_K):
            pltpu.make_async_copy(ys_hbm.at[pl.ds(0, 1)], buf.at[slot, k, pl.ds(0, 1)], sem.at[slot]).wait()
        return carry

    lax.fori_loop(0, tm, drain, 0)

    wt = wt_ref[...]
    routed = buf[slot, 0] * wt[:, 0:1]
    for k in range(1, TOP_K):
        routed = routed + buf[slot, k] * wt[:, k:k + 1]
    hgu = jnp.dot(h_ref[...].astype(BF16), wgu_ref[...], preferred_element_type=F32)
    hid = _silu(hgu[:, :D_SHARED]) * hgu[:, D_SHARED:]
    shared = jnp.dot(hid.astype(BF16), wd_ref[...], preferred_element_type=F32)
    o_ref[...] = _layer_norm(alpha * x_ref[...] + g2_ref[0] * (routed + shared), lng_ref[...], lnb_ref[...])


def _combine_call(geom, alpha, dest, w_tok, ys, h2, x1, mods, w_sh_gu, w_sh_down, ln_g, ln_b):
    tm = geom.c
    d = D_MODEL
    n = geom.nt // tm
    tile = pl.BlockSpec((tm, d), lambda i: (i, 0))
    vsp = pl.BlockSpec((1, d), lambda i: (0, 0))
    return pl.pallas_call(
        functools.partial(_combine_kernel, alpha=alpha),
        grid=(n,),
        in_specs=[pl.BlockSpec((TOP_K, tm), lambda i: (0, i), memory_space=pltpu.SMEM),
                  pl.BlockSpec((TOP_K, tm), lambda i: (0, jnp.minimum(i + 1, n - 1)), memory_space=pltpu.SMEM),
                  pl.BlockSpec((tm, TOP_K), lambda i: (i, 0)),
                  pl.BlockSpec(memory_space=pl.ANY),
                  tile, tile, _mod_spec(geom, tm, 5),
                  pl.BlockSpec((d, 2 * D_SHARED), lambda i: (0, 0)),
                  pl.BlockSpec((D_SHARED, d), lambda i: (0, 0)),
                  vsp, vsp],
        out_specs=tile,
        out_shape=jax.ShapeDtypeStruct((geom.nt, d), F32),
        scratch_shapes=[pltpu.VMEM((2, TOP_K, tm, d), F32), pltpu.SemaphoreType.DMA((2,))],
        compiler_params=_params(1, 48),
        name="moe_combine_ln2",
    )(dest, dest, w_tok, ys, h2, x1, mods, w_sh_gu, w_sh_down, ln_g.reshape(1, d), ln_b.reshape(1, d))


def _rope_tables(t):
    rows = t // GRID_W
    row = jnp.repeat(jnp.arange(rows, dtype=F32), GRID_W)
    col = jnp.tile(jnp.arange(GRID_W, dtype=F32), rows)
    n_freq = ATT_DH // 4
    inv_freq = ROPE_THETA ** (-jnp.arange(n_freq, dtype=F32) / n_freq)
    ang = jnp.concatenate([row[:, None] * inv_freq, col[:, None] * inv_freq], axis=-1)
    cos, sin = jnp.cos(ang), jnp.sin(ang)
    cos64 = jnp.concatenate([cos, cos], axis=-1)
    sin64 = jnp.concatenate([-sin, sin], axis=-1)
    return cos64, sin64


def kernel(x, c, ctx, c_ctx, w_ada, b_ada, w_in, ret_decay_logit, att_q_norm, att_k_norm, conv_dw, conv_db, conv_ln_g, conv_ln_b, w_ret_o, w_att_o, w_conv_o, w_out, ln1_g, ln1_b, w_router, router_bias, w_exp_gate, w_exp_up, w_exp_down, w_sh_gate, w_sh_up, w_sh_down, ln2_g, ln2_b):
    b, t, d = x.shape
    n_ctx = ctx.shape[1]
    depth = w_ada.shape[0]
    assert d == D_MODEL and w_in.shape[-1] == D_IN
    geom = _Geom(b, t, n_ctx)
    alpha = float((2 * depth) ** 0.25)

    cos64, sin64 = _rope_tables(t)
    cos128 = jnp.concatenate([cos64, cos64], axis=-1)
    sin128 = jnp.concatenate([sin64, sin64], axis=-1)

    n_rows = -(-(b + 1) // 8) * 8
    cvecs = jnp.zeros((n_rows, d), F32).at[:b].set(c).at[b].set(c_ctx)
    mods_all = _mods_call(cvecs, w_ada, b_ada).reshape(depth, n_rows * 6, 1, d)

    n_blocks = -(-(geom.nt * TOP_K + N_EXPERTS * (MOE_BLOCK - 1)) // MOE_BLOCK)
    n_slots = n_blocks * MOE_BLOCK

    xt = jnp.concatenate([x.reshape(geom.nl, d), ctx.reshape(geom.nc, d)], axis=0)
    for l in range(depth):
        mods = mods_all[l]
        w_in_l = _permute_columns(w_in[l]).astype(BF16)
        z = _inproj_call(geom, xt, mods, w_in_l)

        log_gamma = jax.nn.log_sigmoid(ret_decay_logit[l].astype(F32))
        ret = _retention_call(geom, z, log_gamma, cos128, sin128)
        att = _attention_call(geom, z, att_q_norm[l], att_k_norm[l], cos64, sin64)
        cv = _conv_call(geom, z, conv_dw[l], conv_db[l], conv_ln_g[l], conv_ln_b[l])
        x1, h2 = _mix_call(geom, alpha, ret, att, cv, z, xt, mods,
                           w_ret_o[l].astype(BF16), w_att_o[l].astype(BF16), w_conv_o[l].astype(BF16),
                           w_out[l].astype(BF16), ln1_g[l], ln1_b[l])

        top_e, gate_w, pos, counts = _router_call(geom, h2, w_router[l], router_bias[l])
        cnt = counts[:, 0].astype(jnp.int32)
        blocks_e = (cnt + MOE_BLOCK - 1) // MOE_BLOCK
        blocks_end = jnp.cumsum(blocks_e)
        start_row = (blocks_end - blocks_e) * MOE_BLOCK
        dest = start_row[top_e] + pos
        block_ids = jnp.arange(n_blocks, dtype=jnp.int32)
        block_e = jnp.minimum(jnp.sum((blocks_end[None, :] <= block_ids[:, None]).astype(jnp.int32), axis=1),
                              N_EXPERTS - 1)
        n_used = blocks_end[-1:].astype(jnp.int32)

        xs = _dispatch_call(geom, dest, h2, n_slots)
        w_gu = jnp.concatenate([w_exp_gate[l], w_exp_up[l]], axis=-1).astype(BF16)
        ys = _expert_call(block_e, n_used, xs, w_gu, w_exp_down[l].astype(BF16))
        w_sh_gu = jnp.concatenate([w_sh_gate[l], w_sh_up[l]], axis=-1).astype(BF16)
        xt = _combine_call(geom, alpha, dest, gate_w.T, ys, h2, x1, mods, w_sh_gu,
                           w_sh_down[l].astype(BF16), ln2_g[l], ln2_b[l])
    return xt[:geom.nl].reshape(b, t, d)
```

```python
import functools

import jax
import jax.numpy as jnp
from jax import lax
from jax.experimental import pallas as pl
from jax.experimental.pallas import tpu as pltpu

F32 = jnp.float32
BF16 = jnp.bfloat16
HIGHEST = lax.Precision.HIGHEST

D_MODEL = 1024
GRID_W = 64
EPS = 1e-6

RET_HEADS = 8
RET_DK = 64
RET_DV = 128
RET_CHUNK = 128
RET_W = RET_HEADS * RET_DV

ATT_HEADS = 16
ATT_KV_HEADS = 4
ATT_DH = 64
ATT_GROUP = ATT_HEADS // ATT_KV_HEADS
ATT_W = ATT_HEADS * ATT_DH
ROPE_THETA = 10000.0
ATT_KEY_BLOCK = 512

CONV_CH = 1024
CONV_K = 31
CONV_HALO = 16

N_EXPERTS = 64
TOP_K = 8
N_GROUPS = 8
TOPK_GROUPS = 4
D_EXPERT = 256
D_SHARED = 256
ROUTED_SCALE = 2.5
MOE_BLOCK = 256

_ORIG = dict(rq=0, rk=512, rv=1024, rg=2048, aq=3072, ak=4096, av=4352, cu=4608, gt=6656)
D_IN = 9728
COL_CU = 0
COL_GT = 2048
COL_RG = 5120
COL_RV = 6144
COL_RQ = 7168
COL_RK = 7680
COL_ATT = 8192
ATT_SECTION = ATT_GROUP * ATT_DH + 2 * ATT_DH


def _column_ranges():
    rng = [(_ORIG["cu"], _ORIG["cu"] + 2 * CONV_CH),
           (_ORIG["gt"], _ORIG["gt"] + 3 * D_MODEL),
           (_ORIG["rg"], _ORIG["rg"] + RET_W),
           (_ORIG["rv"], _ORIG["rv"] + RET_W),
           (_ORIG["rq"], _ORIG["rq"] + RET_HEADS * RET_DK),
           (_ORIG["rk"], _ORIG["rk"] + RET_HEADS * RET_DK)]
    for g in range(ATT_KV_HEADS):
        rng.append((_ORIG["aq"] + g * ATT_GROUP * ATT_DH, _ORIG["aq"] + (g + 1) * ATT_GROUP * ATT_DH))
        rng.append((_ORIG["ak"] + g * ATT_DH, _ORIG["ak"] + (g + 1) * ATT_DH))
        rng.append((_ORIG["av"] + g * ATT_DH, _ORIG["av"] + (g + 1) * ATT_DH))
    cols = [c for a, b in rng for c in range(a, b)]
    assert sorted(cols) == list(range(D_IN))
    return rng


def _permute_columns(w):
    return jnp.concatenate([w[:, a:b] for a, b in _column_ranges()], axis=1)


def _params(n_axes, vmem_mib):
    return pltpu.CompilerParams(dimension_semantics=("arbitrary",) * n_axes,
                                vmem_limit_bytes=vmem_mib * 1024 * 1024)


def _silu(v):
    return v * jax.nn.sigmoid(v)


def _layer_norm(v, g, b):
    mu = jnp.mean(v, axis=-1, keepdims=True)
    d = v - mu
    var = jnp.mean(d * d, axis=-1, keepdims=True)
    return d * lax.rsqrt(var + EPS) * g + b


def _mods_kernel(c_ref, w_ref, b_ref, o_ref):
    s = _silu(c_ref[...])
    o_ref[0] = jnp.dot(s, w_ref[0], preferred_element_type=F32, precision=HIGHEST) + b_ref[0]


def _mods_call(cvecs, w_ada, b_ada):
    n_layers = w_ada.shape[0]
    rows, d = cvecs.shape
    return pl.pallas_call(
        _mods_kernel,
        grid=(n_layers, 6),
        in_specs=[pl.BlockSpec((rows, d), lambda l, j: (0, 0)),
                  pl.BlockSpec((1, d, d), lambda l, j: (l, 0, j)),
                  pl.BlockSpec((1, 1, d), lambda l, j: (l, 0, j))],
        out_specs=pl.BlockSpec((1, rows, d), lambda l, j: (l, 0, j)),
        out_shape=jax.ShapeDtypeStruct((n_layers, rows, 6 * d), F32),
        compiler_params=_params(2, 32),
        name="adaln_mods",
    )(cvecs, w_ada, b_ada.reshape(n_layers, 1, 6 * d))


class _Geom:
    def __init__(self, b, t, c):
        assert t % c == 0 and c % RET_CHUNK == 0 and c % CONV_HALO == 0 and t % ATT_KEY_BLOCK == 0
        self.b, self.t, self.c = b, t, c
        self.nl, self.nc = b * t, b * c
        self.nt = self.nl + self.nc
        self.lat_blocks = t // c
        self.nlb = self.nl // c
        self.p = t + c

    def row_block(self, bi, r):
        return jnp.where(r < self.lat_blocks, bi * self.lat_blocks + r, self.nlb + bi)

    def mod_row(self, i, tm):
        return jnp.where(i * tm < self.nl, (i * tm) // self.t, self.b)


def _mod_spec(geom, tm, which, grid_pos=0):
    d = D_MODEL
    if grid_pos == 0:
        return pl.BlockSpec((1, 1, d), lambda i, *_: (geom.mod_row(i, tm) * 6 + which, 0, 0))
    return pl.BlockSpec((1, 1, d), lambda j, i: (geom.mod_row(i, tm) * 6 + which, 0, 0))


def _inproj_kernel(x_ref, sh_ref, sc_ref, w_ref, o_ref):
    h = x_ref[...] * (1.0 + sc_ref[0]) + sh_ref[0]
    o_ref[...] = jnp.dot(h.astype(BF16), w_ref[...], preferred_element_type=F32)


def _inproj_call(geom, x, mods, w_in_bf16):
    tm = 512 if geom.nc % 512 == 0 and geom.t % 512 == 0 else geom.c
    tn = D_IN // 4
    return pl.pallas_call(
        _inproj_kernel,
        grid=(D_IN // tn, geom.nt // tm),
        in_specs=[pl.BlockSpec((tm, D_MODEL), lambda j, i: (i, 0)),
                  _mod_spec(geom, tm, 0, grid_pos=1),
                  _mod_spec(geom, tm, 1, grid_pos=1),
                  pl.BlockSpec((D_MODEL, tn), lambda j, i: (0, j))],
        out_specs=pl.BlockSpec((tm, tn), lambda j, i: (i, j)),
        out_shape=jax.ShapeDtypeStruct((geom.nt, D_IN), F32),
        compiler_params=_params(2, 48),
        name="in_proj",
    )(x, mods, mods, w_in_bf16)


def _rot_half_128(v):
    lane = lax.broadcasted_iota(jnp.int32, v.shape, 1)
    return jnp.where((lane % 64) < 32, pltpu.roll(v, 96, 1), pltpu.roll(v, 32, 1))


def _ret_kernel(lg_ref, ql_ref, qc_ref, kl_ref, kc_ref, vl_ref, vc_ref, g_ref, cos_ref, sin_ref, o_ref,
                qs, kts, yf, yb, st, dm, qwb, kwb, gcs, *, t, c):
    ch = RET_CHUNK
    hp = pl.program_id(1)
    r = pl.program_id(2)
    lat_blocks = t // c
    n_lat, n_ctx = t // ch, c // ch

    @pl.when(r == 0)
    def _scan():
        ri = lax.broadcasted_iota(jnp.int32, (ch, ch), 0).astype(F32)
        ci = lax.broadcasted_iota(jnp.int32, (ch, ch), 1).astype(F32)
        for d in range(2):
            for h in range(2):
                u = 2 * d + h
                lg = lg_ref[d, 2 * hp + h]
                rel = (ri - ci) if d == 0 else (ci - ri)
                dm[u] = jnp.where(rel >= 0.0, jnp.exp(lg * jnp.maximum(rel, 0.0)), 0.0)
                qwb[u] = jnp.exp(lg * ((ri + 1.0) if d == 0 else (float(ch) - ri)))
                kwb[u] = jnp.exp(lg * ((float(ch) - 1.0 - ri) if d == 0 else ri))
                gcs[u] = jnp.exp(jnp.full((RET_DK, RET_DV), lg * float(ch), F32))
                st[u] = jnp.zeros((RET_DK, RET_DV), F32)

        def stage(q, k, seq_rows):
            qs[0, seq_rows, :] = q[:, :RET_DK].astype(BF16)
            qs[1, seq_rows, :] = q[:, RET_DK:].astype(BF16)
            kt = k.T
            kts[0, :, seq_rows] = kt[:RET_DK].astype(BF16)
            kts[1, :, seq_rows] = kt[RET_DK:].astype(BF16)

        kscale = RET_DK ** -0.5
        for cc in range(n_ctx):
            rows = pl.ds(cc * ch, ch)
            stage(qc_ref[rows, :], kc_ref[rows, :] * kscale, rows)

        def stage_lat(cc, carry):
            rows = pl.ds(pl.multiple_of(cc * ch, ch), ch)
            cs, sn = cos_ref[rows, :], sin_ref[rows, :]
            q = ql_ref[rows, :]
            k = kl_ref[rows, :]
            q = q * cs + _rot_half_128(q) * sn
            k = (k * cs + _rot_half_128(k) * sn) * kscale
            stage(q, k, pl.ds(pl.multiple_of(c + cc * ch, ch), ch))
            return carry

        lax.fori_loop(0, n_lat, stage_lat, 0)

        def run_segment(v_ref, seq_off, n):
            def body(i, carry):
                for d, cc in ((0, i), (1, n - 1 - i)):
                    vrows = pl.ds(pl.multiple_of(cc * ch, ch), ch)
                    srows = pl.ds(pl.multiple_of(seq_off + cc * ch, ch), ch)
                    for h in range(2):
                        u = 2 * d + h
                        q = qs[h, srows, :]
                        kt = kts[h, :, srows]
                        v = v_ref[vrows, h * RET_DV:(h + 1) * RET_DV]
                        s = jnp.dot(q, kt, preferred_element_type=F32)
                        y = jnp.dot((s * dm[u]).astype(BF16), v.astype(BF16), preferred_element_type=F32)
                        state = st[u]
                        y = y + jnp.dot(q, state.astype(BF16), preferred_element_type=F32) * qwb[u]
                        dst = yf if d == 0 else yb
                        dst[srows, h * RET_DV:(h + 1) * RET_DV] = y
                        kv = jnp.dot(kt, (v * kwb[u]).astype(BF16), preferred_element_type=F32)
                        st[u] = gcs[u] * state + kv
                return carry

            lax.fori_loop(0, n, body, 0)

        run_segment(vc_ref, 0, n_ctx)
        run_segment(vl_ref, c, n_lat)

    def finish(srows):
        y = yf[srows, :] + yb[srows, :]
        for h in range(2):
            cols = slice(h * RET_DV, (h + 1) * RET_DV)
            yh = y[:, cols]
            mu = jnp.mean(yh, axis=-1, keepdims=True)
            dlt = yh - mu
            var = jnp.mean(dlt * dlt, axis=-1, keepdims=True)
            o_ref[:, cols] = _silu(g_ref[:, cols]) * (dlt * lax.rsqrt(var + EPS))

    @pl.when(r < lat_blocks)
    def _fin_lat():
        finish(pl.ds(pl.multiple_of(c + r * c, c), c))

    @pl.when(r == lat_blocks)
    def _fin_ctx():
        finish(pl.ds(0, c))


def _retention_call(geom, z, log_gamma, cos128, sin128):
    t, c, p = geom.t, geom.c, geom.p
    hpairs = RET_HEADS // 2
    qb, kb = COL_RQ // 128, COL_RK // 128
    vb, gb = COL_RV // 256, COL_RG // 256
    rb = geom.row_block
    in_specs = [
        pl.BlockSpec(memory_space=pltpu.SMEM),
        pl.BlockSpec((t, 128), lambda b, h, r: (b, qb + h)),
        pl.BlockSpec((c, 128), lambda b, h, r: (geom.nlb + b, qb + h)),
        pl.BlockSpec((t, 128), lambda b, h, r: (b, kb + h)),
        pl.BlockSpec((c, 128), lambda b, h, r: (geom.nlb + b, kb + h)),
        pl.BlockSpec((t, 256), lambda b, h, r: (b, vb + h)),
        pl.BlockSpec((c, 256), lambda b, h, r: (geom.nlb + b, vb + h)),
        pl.BlockSpec((c, 256), lambda b, h, r: (rb(b, r), gb + h)),
        pl.BlockSpec((t, 128), lambda b, h, r: (0, 0)),
        pl.BlockSpec((t, 128), lambda b, h, r: (0, 0)),
    ]
    scratch = [
        pltpu.VMEM((2, p, RET_DK), BF16),
        pltpu.VMEM((2, RET_DK, p), BF16),
        pltpu.VMEM((p, 2 * RET_DV), F32),
        pltpu.VMEM((p, 2 * RET_DV), F32),
        pltpu.VMEM((4, RET_DK, RET_DV), F32),
        pltpu.VMEM((4, RET_CHUNK, RET_CHUNK), F32),
        pltpu.VMEM((4, RET_CHUNK, RET_CHUNK), F32),
        pltpu.VMEM((4, RET_CHUNK, RET_CHUNK), F32),
        pltpu.VMEM((4, RET_DK, RET_DV), F32),
    ]
    return pl.pallas_call(
        functools.partial(_ret_kernel, t=t, c=c),
        grid=(geom.b, hpairs, geom.lat_blocks + 1),
        in_specs=in_specs,
        out_specs=pl.BlockSpec((c, 256), lambda b, h, r: (rb(b, r), h)),
        out_shape=jax.ShapeDtypeStruct((geom.nt, RET_W), F32),
        scratch_shapes=scratch,
        compiler_params=_params(3, 56),
        name="retention",
    )(log_gamma, z, z, z, z, z, z, z, cos128, sin128)


def _rms_heads_128(v, g):
    li = lax.broadcasted_iota(jnp.int32, (128, 128), 0) // ATT_DH
    lj = lax.broadcasted_iota(jnp.int32, (128, 128), 1) // ATT_DH
    avg = jnp.where(li == lj, 1.0 / ATT_DH, 0.0).astype(BF16)
    sq = v * v
    hi = sq.astype(BF16)
    lo = (sq - hi.astype(F32)).astype(BF16)
    ms = jnp.dot(hi, avg, preferred_element_type=F32) + jnp.dot(lo, avg, preferred_element_type=F32)
    return v * lax.rsqrt(ms + EPS) * g


def _att_kernel(qa_ref, qb_ref, kvl_ref, kvc_ref, qn_ref, kn_ref, cos_ref, sin_ref, o_ref,
                kts, vs, m_s, acc_s, *, t, c):
    r = pl.program_id(2)
    lat_blocks = t // c
    dh = ATT_DH
    tk = ATT_KEY_BLOCK
    lane = lax.broadcasted_iota(jnp.int32, (c, 2 * dh), 1)

    def stage_tile(kv, dst, cs, sn):
        k = _rms_heads_128(kv, kn_ref[...])
        if cs is not None:
            k = k * cs + _rot_half_128(k) * sn
        kts[:, dst] = k.T[:dh].astype(BF16)
        vs[dst, :] = jnp.where(lane < dh, pltpu.roll(kv, dh, 1), 1.0).astype(BF16)

    @pl.when(r == 0)
    def _stage_kv():
        stage_tile(kvc_ref[...], pl.ds(0, c), None, None)

        def stage(i, carry):
            rows = pl.ds(pl.multiple_of(i * c, c), c)
            stage_tile(kvl_ref[rows, :], pl.ds(pl.multiple_of(c + i * c, c), c), cos_ref[rows, :], sin_ref[rows, :])
            return carry

        lax.fori_loop(0, lat_blocks, stage, 0)

    is_ctx = r == lat_blocks
    rows = pl.ds(pl.multiple_of(jnp.minimum(r, lat_blocks - 1) * c, c), c)
    cs, sn = cos_ref[rows, :], sin_ref[rows, :]
    q_heads = []
    for src in (qa_ref, qb_ref):
        xn = _rms_heads_128(src[...], qn_ref[...])
        xr = jnp.where(is_ctx, xn, xn * cs + _rot_half_128(xn) * sn) * (dh ** -0.5)
        q_heads.append(xr[:, :dh].astype(BF16))
        q_heads.append(pltpu.roll(xr, dh, 1)[:, :dh].astype(BF16))
    q = jnp.concatenate(q_heads, axis=0)

    m_s[...] = jnp.full(m_s.shape, -jnp.inf, F32)
    acc_s[...] = jnp.zeros(acc_s.shape, F32)

    def flash_step(kt, v):
        n = kt.shape[1]
        s = jnp.dot(q, kt, preferred_element_type=F32)
        m_prev = m_s[...]
        m_next = jnp.maximum(m_prev, jnp.max(s, axis=1, keepdims=True))
        prob = jnp.exp(s - jnp.concatenate([m_next] * (n // 128), axis=1))
        acc_s[...] = acc_s[...] * jnp.exp(m_prev - m_next) + jnp.dot(prob.astype(BF16), v, preferred_element_type=F32)
        m_s[...] = m_next

    flash_step(kts[:, 0:c], vs[0:c, :])

    @pl.when(jnp.logical_not(is_ctx))
    def _latent_keys():
        def lat_step(j, carry):
            krows = pl.ds(pl.multiple_of(c + j * tk, 128), tk)
            flash_step(kts[:, krows], vs[krows, :])
            return carry

        lax.fori_loop(0, t // tk, lat_step, 0, unroll=4)

    outs = []
    for h in range(ATT_GROUP):
        acc = acc_s[h * c:(h + 1) * c, :]
        outs.append(acc * pltpu.roll(1.0 / acc, dh, 1))
    for pair in range(ATT_GROUP // 2):
        both = jnp.where(lane < dh, outs[2 * pair], pltpu.roll(outs[2 * pair + 1], dh, 1))
        o_ref[:, pair * 2 * dh:(pair + 1) * 2 * dh] = both


def _attention_call(geom, z, q_norm, k_norm, cos128, sin128):
    t, c, p = geom.t, geom.c, geom.p
    ab = COL_ATT // 128
    sec = ATT_SECTION // 128
    rb = geom.row_block
    in_specs = [
        pl.BlockSpec((c, 128), lambda b, g, r: (rb(b, r), ab + sec * g)),
        pl.BlockSpec((c, 128), lambda b, g, r: (rb(b, r), ab + sec * g + 1)),
        pl.BlockSpec((t, 128), lambda b, g, r: (b, ab + sec * g + 2)),
        pl.BlockSpec((c, 128), lambda b, g, r: (geom.nlb + b, ab + sec * g + 2)),
        pl.BlockSpec((1, 128), lambda b, g, r: (0, 0)),
        pl.BlockSpec((1, 128), lambda b, g, r: (0, 0)),
        pl.BlockSpec((t, 128), lambda b, g, r: (0, 0)),
        pl.BlockSpec((t, 128), lambda b, g, r: (0, 0)),
    ]
    two_heads = lambda v: jnp.tile(v.reshape(1, ATT_DH), (1, 2))
    scratch = [
        pltpu.VMEM((ATT_DH, p), BF16),
        pltpu.VMEM((p, 2 * ATT_DH), BF16),
        pltpu.VMEM((ATT_GROUP * c, 128), F32),
        pltpu.VMEM((ATT_GROUP * c, 2 * ATT_DH), F32),
    ]
    return pl.pallas_call(
        functools.partial(_att_kernel, t=t, c=c),
        grid=(geom.b, ATT_KV_HEADS, geom.lat_blocks + 1),
        in_specs=in_specs,
        out_specs=pl.BlockSpec((c, ATT_GROUP * ATT_DH), lambda b, g, r: (rb(b, r), g)),
        out_shape=jax.ShapeDtypeStruct((geom.nt, ATT_W), F32),
        scratch_shapes=scratch,
        compiler_params=_params(3, 48),
        name="attention",
    )(z, z, z, z, two_heads(q_norm), two_heads(k_norm), cos128, sin128)


def _conv_kernel(a_ref, g_ref, ap_ref, gp_ref, an_ref, gn_ref, w_ref, b_ref, lng_ref, lnb_ref, o_ref,
                 ext, ys, *, t, c):
    r = pl.program_id(1)
    lat_blocks = t // c
    halo = CONV_HALO
    has_prev = jnp.logical_and(r != 0, r != lat_blocks)
    has_next = jnp.logical_and(r != lat_blocks - 1, r != lat_blocks)
    ext[halo:halo + c, :] = a_ref[...] * jax.nn.sigmoid(g_ref[...])
    ext[0:halo, :] = jnp.where(has_prev, ap_ref[...] * jax.nn.sigmoid(gp_ref[...]), 0.0)
    ext[halo + c:, :] = jnp.where(has_next, an_ref[...] * jax.nn.sigmoid(gn_ref[...]), 0.0)

    rt = 64
    first = halo - CONV_K // 2

    def lane_block(cb, carry):
        lanes = pl.ds(pl.multiple_of(cb * 128, 128), 128)
        for ti in range(c // rt):
            acc = jnp.zeros((rt, 128), F32)
            for j in range(CONV_K):
                acc = acc + w_ref[pl.ds(j, 1), lanes] * ext[pl.ds(ti * rt + first + j, rt), lanes]
            ys[pl.ds(ti * rt, rt), lanes] = acc
        return carry

    lax.fori_loop(0, CONV_CH // 128, lane_block, 0)
    y = ys[...] + b_ref[...]
    o_ref[...] = _silu(_layer_norm(y, lng_ref[...], lnb_ref[...]))


def _conv_call(geom, z, conv_dw, conv_db, ln_g, ln_b):
    t, c = geom.t, geom.c
    rb = geom.row_block
    hb = c // CONV_HALO
    last = geom.nt // CONV_HALO - 1
    prev = lambda b, r: jnp.maximum(rb(b, r) * hb - 1, 0)
    nxt = lambda b, r: jnp.minimum((rb(b, r) + 1) * hb, last)
    w = jnp.zeros((32, CONV_CH), F32).at[:CONV_K].set(conv_dw)
    vec = lambda v: v.reshape(1, CONV_CH)
    cst = pl.BlockSpec((1, CONV_CH), lambda b, r: (0, 0))
    in_specs = [
        pl.BlockSpec((c, CONV_CH), lambda b, r: (rb(b, r), 0)),
        pl.BlockSpec((c, CONV_CH), lambda b, r: (rb(b, r), 1)),
        pl.BlockSpec((CONV_HALO, CONV_CH), lambda b, r: (prev(b, r), 0)),
        pl.BlockSpec((CONV_HALO, CONV_CH), lambda b, r: (prev(b, r), 1)),
        pl.BlockSpec((CONV_HALO, CONV_CH), lambda b, r: (nxt(b, r), 0)),
        pl.BlockSpec((CONV_HALO, CONV_CH), lambda b, r: (nxt(b, r), 1)),
        pl.BlockSpec((32, CONV_CH), lambda b, r: (0, 0)),
        cst, cst, cst,
    ]
    return pl.pallas_call(
        functools.partial(_conv_kernel, t=t, c=c),
        grid=(geom.b, geom.lat_blocks + 1),
        in_specs=in_specs,
        out_specs=pl.BlockSpec((c, CONV_CH), lambda b, r: (rb(b, r), 0)),
        out_shape=jax.ShapeDtypeStruct((geom.nt, CONV_CH), F32),
        scratch_shapes=[pltpu.VMEM((c + 2 * CONV_HALO, CONV_CH), F32), pltpu.VMEM((c, CONV_CH), F32)],
        compiler_params=_params(2, 32),
        name="conformer_conv",
    )(z, z, z, z, z, z, w, vec(conv_db), vec(ln_g), vec(ln_b))


def _mix_kernel(ret_ref, att_ref, cv_ref, gr_ref, ga_ref, gc_ref, x_ref, g1_ref, sh2_ref, sc2_ref,
                wr_ref, wa_ref, wc_ref, wo_ref, lng_ref, lnb_ref, x1_ref, h2_ref, *, alpha):
    def proj(v_ref, w_ref):
        return jnp.dot(v_ref[...].astype(BF16), w_ref[...], preferred_element_type=F32)

    merged = (jax.nn.sigmoid(gr_ref[...]) * proj(ret_ref, wr_ref)
              + jax.nn.sigmoid(ga_ref[...]) * proj(att_ref, wa_ref)
              + jax.nn.sigmoid(gc_ref[...]) * proj(cv_ref, wc_ref))
    y = jnp.dot(merged.astype(BF16), wo_ref[...], preferred_element_type=F32)
    x1 = _layer_norm(alpha * x_ref[...] + g1_ref[0] * y, lng_ref[...], lnb_ref[...])
    x1_ref[...] = x1
    h2_ref[...] = x1 * (1.0 + sc2_ref[0]) + sh2_ref[0]


def _mix_call(geom, alpha, ret, att, cv, z, x, mods, w_ret_o, w_att_o, w_conv_o, w_out, ln_g, ln_b):
    tm = geom.c
    d = D_MODEL
    tile = pl.BlockSpec((tm, d), lambda i: (i, 0))
    gate = lambda k: pl.BlockSpec((tm, d), lambda i: (i, COL_GT // d + k))
    wsp = pl.BlockSpec((d, d), lambda i: (0, 0))
    vsp = pl.BlockSpec((1, d), lambda i: (0, 0))
    return pl.pallas_call(
        functools.partial(_mix_kernel, alpha=alpha),
        grid=(geom.nt // tm,),
        in_specs=[tile, tile, tile, gate(0), gate(1), gate(2), tile,
                  _mod_spec(geom, tm, 2), _mod_spec(geom, tm, 3), _mod_spec(geom, tm, 4),
                  wsp, wsp, wsp, wsp, vsp, vsp],
        out_specs=[tile, tile],
        out_shape=[jax.ShapeDtypeStruct((geom.nt, d), F32)] * 2,
        compiler_params=_params(1, 48),
        name="merge_ln1",
    )(ret, att, cv, z, z, z, x, mods, mods, mods, w_ret_o, w_att_o, w_conv_o, w_out,
      ln_g.reshape(1, d), ln_b.reshape(1, d))


def _router_kernel(h_ref, wr_ref, bias_ref, e_ref, w_ref, pos_ref, cnt_ref, cnt):
    i = pl.program_id(0)
    tm = h_ref.shape[0]
    ne, per = N_EXPERTS, N_EXPERTS // N_GROUPS
    neg = -jnp.inf

    @pl.when(i == 0)
    def _init():
        cnt[...] = jnp.zeros(cnt.shape, F32)

    logits = jnp.dot(h_ref[...], wr_ref[...], preferred_element_type=F32, precision=HIGHEST)
    scores = jax.nn.sigmoid(logits.T[:ne])
    sel = scores + bias_ref[...]

    member = lax.broadcasted_iota(jnp.int32, (per, tm), 0)
    grp_rows = []
    for g in range(N_GROUPS):
        blk = sel[g * per:(g + 1) * per]
        m1 = jnp.max(blk, axis=0, keepdims=True)
        first = jnp.min(jnp.where(blk == m1, member, per), axis=0, keepdims=True)
        m2 = jnp.max(jnp.where(member == first, neg, blk), axis=0, keepdims=True)
        grp_rows.append(m1 + m2)
    gs = jnp.concatenate(grp_rows, axis=0)

    gidx = lax.broadcasted_iota(jnp.int32, (N_GROUPS, tm), 0)
    rank = jnp.zeros((N_GROUPS, tm), jnp.int32)
    for g in range(N_GROUPS):
        row = gs[g:g + 1]
        ahead = jnp.logical_or(row > gs, jnp.logical_and(row == gs, g < gidx))
        rank = rank + ahead.astype(jnp.int32)
    keep = (rank < TOPK_GROUPS).astype(F32)
    keep_e = jnp.concatenate([jnp.broadcast_to(keep[g:g + 1], (per, tm)) for g in range(N_GROUPS)], axis=0)
    cand = jnp.where(keep_e > 0.5, sel, neg)

    eidx = lax.broadcasted_iota(jnp.int32, (ne, tm), 0)
    picks, gates, hots = [], [], []
    chosen = jnp.zeros((ne, tm), F32)
    for _ in range(TOP_K):
        m = jnp.max(cand, axis=0, keepdims=True)
        idx = jnp.min(jnp.where(cand == m, eidx, ne), axis=0, keepdims=True)
        hot = eidx == idx
        picks.append(idx)
        gates.append(jnp.sum(jnp.where(hot, scores, 0.0), axis=0, keepdims=True))
        hots.append(hot)
        chosen = jnp.where(hot, 1.0, chosen)
        cand = jnp.where(hot, neg, cand)
    total = gates[0]
    for gk in gates[1:]:
        total = total + gk

    ti = lax.broadcasted_iota(jnp.int32, (tm, tm), 0)
    tj = lax.broadcasted_iota(jnp.int32, (tm, tm), 1)
    before = jnp.where(ti < tj, 1.0, 0.0).astype(BF16)
    prior = jnp.dot(chosen.astype(BF16), before, preferred_element_type=F32) + cnt[...][:, :1]
    pos = [jnp.sum(jnp.where(hot, prior, 0.0), axis=0, keepdims=True) for hot in hots]

    e_ref[...] = jnp.concatenate(picks, axis=0)
    w_ref[...] = jnp.concatenate([ROUTED_SCALE * gk / total for gk in gates], axis=0)
    pos_ref[...] = jnp.concatenate(pos, axis=0).astype(jnp.int32)
    cnt[...] = cnt[...] + jnp.sum(chosen, axis=1, keepdims=True)
    cnt_ref[...] = cnt[...]


def _router_call(geom, h2, w_router, router_bias):
    tm = geom.c
    wr = jnp.zeros((D_MODEL, 128), F32).at[:, :N_EXPERTS].set(w_router)
    tok = pl.BlockSpec((TOP_K, tm), lambda i: (0, i))
    return pl.pallas_call(
        _router_kernel,
        grid=(geom.nt // tm,),
        in_specs=[pl.BlockSpec((tm, D_MODEL), lambda i: (i, 0)),
                  pl.BlockSpec((D_MODEL, 128), lambda i: (0, 0)),
                  pl.BlockSpec((N_EXPERTS, 1), lambda i: (0, 0))],
        out_specs=[tok, tok, tok, pl.BlockSpec((N_EXPERTS, 128), lambda i: (0, 0))],
        out_shape=[jax.ShapeDtypeStruct((TOP_K, geom.nt), jnp.int32),
                   jax.ShapeDtypeStruct((TOP_K, geom.nt), F32),
                   jax.ShapeDtypeStruct((TOP_K, geom.nt), jnp.int32),
                   jax.ShapeDtypeStruct((N_EXPERTS, 128), F32)],
        scratch_shapes=[pltpu.VMEM((N_EXPERTS, 128), F32)],
        compiler_params=_params(1, 32),
        name="moe_router",
    )(h2, wr, router_bias.reshape(N_EXPERTS, 1))


def _row_copy(src, src_row, dst, dst_row, sem):
    return pltpu.make_async_copy(src.at[pl.ds(src_row, 1)], dst.at[pl.ds(dst_row, 1)], sem)


def _dispatch_kernel(dest_ref, h_ref, xs_in, xs_out, sem):
    del xs_in
    tm = dest_ref.shape[1]

    def issue(tok, carry):
        for k in range(TOP_K):
            _row_copy(h_ref, tok, xs_out, dest_ref[k, tok], sem).start()
        return carry

    lax.fori_loop(0, tm, issue, 0)

    def drain(tok, carry):
        for k in range(TOP_K):
            _row_copy(h_ref, 0, xs_out, 0, sem).wait()
        return carry

    lax.fori_loop(0, tm, drain, 0)


def _dispatch_call(geom, dest, h2, n_slots):
    tm = geom.c
    zeros = jnp.zeros((n_slots, D_MODEL), F32)
    return pl.pallas_call(
        _dispatch_kernel,
        grid=(geom.nt // tm,),
        in_specs=[pl.BlockSpec((TOP_K, tm), lambda i: (0, i), memory_space=pltpu.SMEM),
                  pl.BlockSpec((tm, D_MODEL), lambda i: (i, 0)),
                  pl.BlockSpec(memory_space=pl.ANY)],
        out_specs=pl.BlockSpec(memory_space=pl.ANY),
        out_shape=jax.ShapeDtypeStruct((n_slots, D_MODEL), F32),
        scratch_shapes=[pltpu.SemaphoreType.DMA(())],
        input_output_aliases={2: 0},
        compiler_params=_params(1, 32),
        name="moe_dispatch",
    )(dest, h2, zeros)


def _expert_kernel(be_ref, nu_ref, x_ref, wgu_ref, wd_ref, o_ref):
    del be_ref

    @pl.when(pl.program_id(0) < nu_ref[0])
    def _run():
        hgu = jnp.dot(x_ref[...].astype(BF16), wgu_ref[0], preferred_element_type=F32)
        hid = _silu(hgu[:, :D_EXPERT]) * hgu[:, D_EXPERT:]
        o_ref[...] = jnp.dot(hid.astype(BF16), wd_ref[0], preferred_element_type=F32)


def _expert_call(block_e, n_used, xs, w_gu, w_down):
    n_blocks = xs.shape[0] // MOE_BLOCK
    live = lambda i, be, nu: jnp.minimum(i, nu[0] - 1)
    grid_spec = pltpu.PrefetchScalarGridSpec(
        num_scalar_prefetch=2,
        grid=(n_blocks,),
        in_specs=[pl.BlockSpec((MOE_BLOCK, D_MODEL), lambda i, be, nu: (live(i, be, nu), 0)),
                  pl.BlockSpec((1, D_MODEL, 2 * D_EXPERT), lambda i, be, nu: (be[live(i, be, nu)], 0, 0)),
                  pl.BlockSpec((1, D_EXPERT, D_MODEL), lambda i, be, nu: (be[live(i, be, nu)], 0, 0))],
        out_specs=pl.BlockSpec((MOE_BLOCK, D_MODEL), lambda i, be, nu: (live(i, be, nu), 0)),
    )
    return pl.pallas_call(
        _expert_kernel,
        grid_spec=grid_spec,
        out_shape=jax.ShapeDtypeStruct(xs.shape, F32),
        compiler_params=_params(1, 32),
        name="moe_experts",
    )(block_e, n_used, xs, w_gu, w_down)


def _combine_kernel(dcur_ref, dnxt_ref, wt_ref, ys_hbm, h_ref, x_ref, g2_ref, wgu_ref, wd_ref,
                    lng_ref, lnb_ref, o_ref, buf, sem, *, alpha):
    i = pl.program_id(0)
    n = pl.num_programs(0)
    tm = h_ref.shape[0]

    def issue(d_ref, slot):
        def body(tok, carry):
            for k in range(TOP_K):
                pltpu.make_async_copy(ys_hbm.at[pl.ds(d_ref[k, tok], 1)],
                                      buf.at[slot, k, pl.ds(tok, 1)], sem.at[slot]).start()
            return carry

        lax.fori_loop(0, tm, body, 0)

    @pl.when(i == 0)
    def _first():
        issue(dcur_ref, 0)

    @pl.when(i + 1 < n)
    def _ahead():
        issue(dnxt_ref, (i + 1) % 2)

    slot = i % 2

    def drain(tok, carry):
        for k in range(TOP_K):
            pltpu.make_async_copy(ys_hbm.at[pl.ds(0, 1)], buf.at[slot, k, pl.ds(0, 1)], sem.at[slot]).wait()
        return carry

    lax.fori_loop(0, tm, drain, 0)

    wt = wt_ref[...]
    routed = buf[slot, 0] * wt[:, 0:1]
    for k in range(1, TOP_K):
        routed = routed + buf[slot, k] * wt[:, k:k + 1]
    hgu = jnp.dot(h_ref[...].astype(BF16), wgu_ref[...], preferred_element_type=F32)
    hid = _silu(hgu[:, :D_SHARED]) * hgu[:, D_SHARED:]
    shared = jnp.dot(hid.astype(BF16), wd_ref[...], preferred_element_type=F32)
    o_ref[...] = _layer_norm(alpha * x_ref[...] + g2_ref[0] * (routed + shared), lng_ref[...], lnb_ref[...])


def _combine_call(geom, alpha, dest, w_tok, ys, h2, x1, mods, w_sh_gu, w_sh_down, ln_g, ln_b):
    tm = geom.c
    d = D_MODEL
    n = geom.nt // tm
    tile = pl.BlockSpec((tm, d), lambda i: (i, 0))
    vsp = pl.BlockSpec((1, d), lambda i: (0, 0))
    return pl.pallas_call(
        functools.partial(_combine_kernel, alpha=alpha),
        grid=(n,),
        in_specs=[pl.BlockSpec((TOP_K, tm), lambda i: (0, i), memory_space=pltpu.SMEM),
                  pl.BlockSpec((TOP_K, tm), lambda i: (0, jnp.minimum(i + 1, n - 1)), memory_space=pltpu.SMEM),
                  pl.BlockSpec((tm, TOP_K), lambda i: (i, 0)),
                  pl.BlockSpec(memory_space=pl.ANY),
                  tile, tile, _mod_spec(geom, tm, 5),
                  pl.BlockSpec((d, 2 * D_SHARED), lambda i: (0, 0)),
                  pl.BlockSpec((D_SHARED, d), lambda i: (0, 0)),
                  vsp, vsp],
        out_specs=tile,
        out_shape=jax.ShapeDtypeStruct((geom.nt, d), F32),
        scratch_shapes=[pltpu.VMEM((2, TOP_K, tm, d), F32), pltpu.SemaphoreType.DMA((2,))],
        compiler_params=_params(1, 48),
        name="moe_combine_ln2",
    )(dest, dest, w_tok, ys, h2, x1, mods, w_sh_gu, w_sh_down, ln_g.reshape(1, d), ln_b.reshape(1, d))


def _rope_tables(t):
    rows = t // GRID_W
    row = jnp.repeat(jnp.arange(rows, dtype=F32), GRID_W)
    col = jnp.tile(jnp.arange(GRID_W, dtype=F32), rows)
    n_freq = ATT_DH // 4
    inv_freq = ROPE_THETA ** (-jnp.arange(n_freq, dtype=F32) / n_freq)
    ang = jnp.concatenate([row[:, None] * inv_freq, col[:, None] * inv_freq], axis=-1)
    cos, sin = jnp.cos(ang), jnp.sin(ang)
    cos64 = jnp.concatenate([cos, cos], axis=-1)
    sin64 = jnp.concatenate([-sin, sin], axis=-1)
    return cos64, sin64


def kernel(x, c, ctx, c_ctx, w_ada, b_ada, w_in, ret_decay_logit, att_q_norm, att_k_norm, conv_dw, conv_db, conv_ln_g, conv_ln_b, w_ret_o, w_att_o, w_conv_o, w_out, ln1_g, ln1_b, w_router, router_bias, w_exp_gate, w_exp_up, w_exp_down, w_sh_gate, w_sh_up, w_sh_down, ln2_g, ln2_b):
    b, t, d = x.shape
    n_ctx = ctx.shape[1]
    depth = w_ada.shape[0]
    assert d == D_MODEL and w_in.shape[-1] == D_IN
    geom = _Geom(b, t, n_ctx)
    alpha = float((2 * depth) ** 0.25)

    cos64, sin64 = _rope_tables(t)
    cos128 = jnp.concatenate([cos64, cos64], axis=-1)
    sin128 = jnp.concatenate([sin64, sin64], axis=-1)

    n_rows = -(-(b + 1) // 8) * 8
    cvecs = jnp.zeros((n_rows, d), F32).at[:b].set(c).at[b].set(c_ctx)
    mods_all = _mods_call(cvecs, w_ada, b_ada).reshape(depth, n_rows * 6, 1, d)

    n_blocks = -(-(geom.nt * TOP_K + N_EXPERTS * (MOE_BLOCK - 1)) // MOE_BLOCK)
    n_slots = n_blocks * MOE_BLOCK

    xt = jnp.concatenate([x.reshape(geom.nl, d), ctx.reshape(geom.nc, d)], axis=0)
    for l in range(depth):
        mods = mods_all[l]
        w_in_l = _permute_columns(w_in[l]).astype(BF16)
        z = _inproj_call(geom, xt, mods, w_in_l)

        log_gamma = jax.nn.log_sigmoid(ret_decay_logit[l].astype(F32))
        ret = _retention_call(geom, z, log_gamma, cos128, sin128)
        att = _attention_call(geom, z, att_q_norm[l], att_k_norm[l], cos128, sin128)
        cv = _conv_call(geom, z, conv_dw[l], conv_db[l], conv_ln_g[l], conv_ln_b[l])
        x1, h2 = _mix_call(geom, alpha, ret, att, cv, z, xt, mods,
                           w_ret_o[l].astype(BF16), w_att_o[l].astype(BF16), w_conv_o[l].astype(BF16),
                           w_out[l].astype(BF16), ln1_g[l], ln1_b[l])

        top_e, gate_w, pos, counts = _router_call(geom, h2, w_router[l], router_bias[l])
        cnt = counts[:, 0].astype(jnp.int32)
        blocks_e = (cnt + MOE_BLOCK - 1) // MOE_BLOCK
        blocks_end = jnp.cumsum(blocks_e)
        start_row = (blocks_end - blocks_e) * MOE_BLOCK
        expert_ids = jnp.arange(N_EXPERTS, dtype=jnp.int32)[None, None, :]
        dest = jnp.sum(jnp.where(top_e[:, :, None] == expert_ids, start_row[None, None, :], 0), axis=-1) + pos
        block_ids = jnp.arange(n_blocks, dtype=jnp.int32)
        block_e = jnp.minimum(jnp.sum((blocks_end[None, :] <= block_ids[:, None]).astype(jnp.int32), axis=1),
                              N_EXPERTS - 1)
        n_used = blocks_end[-1:].astype(jnp.int32)

        xs = _dispatch_call(geom, dest, h2, n_slots)
        w_gu = jnp.concatenate([w_exp_gate[l], w_exp_up[l]], axis=-1).astype(BF16)
        ys = _expert_call(block_e, n_used, xs, w_gu, w_exp_down[l].astype(BF16))
        w_sh_gu = jnp.concatenate([w_sh_gate[l], w_sh_up[l]], axis=-1).astype(BF16)
        xt = _combine_call(geom, alpha, dest, gate_w.T, ys, h2, x1, mods, w_sh_gu,
                           w_sh_down[l].astype(BF16), ln2_g[l], ln2_b[l])
    return xt[:geom.nl].reshape(b, t, d)
```

```python
import functools

import jax
import jax.numpy as jnp
from jax import lax
from jax.experimental import pallas as pl
from jax.experimental.pallas import tpu as pltpu

F32 = jnp.float32
BF16 = jnp.bfloat16
HIGHEST = lax.Precision.HIGHEST

D_MODEL = 1024
GRID_W = 64
EPS = 1e-6

RET_HEADS = 8
RET_DK = 64
RET_DV = 128
RET_CHUNK = 128
RET_W = RET_HEADS * RET_DV

ATT_HEADS = 16
ATT_KV_HEADS = 4
ATT_DH = 64
ATT_GROUP = ATT_HEADS // ATT_KV_HEADS
ATT_W = ATT_HEADS * ATT_DH
ROPE_THETA = 10000.0
ATT_KEY_BLOCK = 512

CONV_CH = 1024
CONV_K = 31
CONV_HALO = 16

N_EXPERTS = 64
TOP_K = 8
N_GROUPS = 8
TOPK_GROUPS = 4
D_EXPERT = 256
D_SHARED = 256
ROUTED_SCALE = 2.5
MOE_BLOCK = 256

_ORIG = dict(rq=0, rk=512, rv=1024, rg=2048, aq=3072, ak=4096, av=4352, cu=4608, gt=6656)
D_IN = 9728
COL_CU = 0
COL_GT = 2048
COL_RG = 5120
COL_RV = 6144
COL_RQ = 7168
COL_RK = 7680
COL_ATT = 8192
ATT_SECTION = ATT_GROUP * ATT_DH + 2 * ATT_DH


def _column_ranges():
    rng = [(_ORIG["cu"], _ORIG["cu"] + 2 * CONV_CH),
           (_ORIG["gt"], _ORIG["gt"] + 3 * D_MODEL),
           (_ORIG["rg"], _ORIG["rg"] + RET_W),
           (_ORIG["rv"], _ORIG["rv"] + RET_W),
           (_ORIG["rq"], _ORIG["rq"] + RET_HEADS * RET_DK),
           (_ORIG["rk"], _ORIG["rk"] + RET_HEADS * RET_DK)]
    for g in range(ATT_KV_HEADS):
        rng.append((_ORIG["aq"] + g * ATT_GROUP * ATT_DH, _ORIG["aq"] + (g + 1) * ATT_GROUP * ATT_DH))
        rng.append((_ORIG["ak"] + g * ATT_DH, _ORIG["ak"] + (g + 1) * ATT_DH))
        rng.append((_ORIG["av"] + g * ATT_DH, _ORIG["av"] + (g + 1) * ATT_DH))
    cols = [c for a, b in rng for c in range(a, b)]
    assert sorted(cols) == list(range(D_IN))
    return rng


def _permute_columns(w):
    return jnp.concatenate([w[:, a:b] for a, b in _column_ranges()], axis=1)


def _params(n_axes, vmem_mib):
    return pltpu.CompilerParams(dimension_semantics=("arbitrary",) * n_axes,
                                vmem_limit_bytes=vmem_mib * 1024 * 1024)


def _silu(v):
    return v * jax.nn.sigmoid(v)


def _layer_norm(v, g, b):
    mu = jnp.mean(v, axis=-1, keepdims=True)
    d = v - mu
    var = jnp.mean(d * d, axis=-1, keepdims=True)
    return d * lax.rsqrt(var + EPS) * g + b


def _mods_kernel(c_ref, w_ref, b_ref, o_ref):
    s = _silu(c_ref[...])
    o_ref[0] = jnp.dot(s, w_ref[0], preferred_element_type=F32, precision=HIGHEST) + b_ref[0]


def _mods_call(cvecs, w_ada, b_ada):
    n_layers = w_ada.shape[0]
    rows, d = cvecs.shape
    return pl.pallas_call(
        _mods_kernel,
        grid=(n_layers, 6),
        in_specs=[pl.BlockSpec((rows, d), lambda l, j: (0, 0)),
                  pl.BlockSpec((1, d, d), lambda l, j: (l, 0, j)),
                  pl.BlockSpec((1, 1, d), lambda l, j: (l, 0, j))],
        out_specs=pl.BlockSpec((1, rows, d), lambda l, j: (l, 0, j)),
        out_shape=jax.ShapeDtypeStruct((n_layers, rows, 6 * d), F32),
        compiler_params=_params(2, 32),
        name="adaln_mods",
    )(cvecs, w_ada, b_ada.reshape(n_layers, 1, 6 * d))


class _Geom:
    def __init__(self, b, t, c):
        assert t % c == 0 and c % RET_CHUNK == 0 and c % CONV_HALO == 0 and t % ATT_KEY_BLOCK == 0
        self.b, self.t, self.c = b, t, c
        self.nl, self.nc = b * t, b * c
        self.nt = self.nl + self.nc
        self.lat_blocks = t // c
        self.nlb = self.nl // c
        self.p = t + c

    def row_block(self, bi, r):
        return jnp.where(r < self.lat_blocks, bi * self.lat_blocks + r, self.nlb + bi)

    def mod_row(self, i, tm):
        return jnp.where(i * tm < self.nl, (i * tm) // self.t, self.b)


def _mod_spec(geom, tm, which, grid_pos=0):
    d = D_MODEL
    if grid_pos == 0:
        return pl.BlockSpec((1, 1, d), lambda i, *_: (geom.mod_row(i, tm) * 6 + which, 0, 0))
    return pl.BlockSpec((1, 1, d), lambda j, i: (geom.mod_row(i, tm) * 6 + which, 0, 0))


def _inproj_kernel(x_ref, sh_ref, sc_ref, w_ref, o_ref):
    h = x_ref[...] * (1.0 + sc_ref[0]) + sh_ref[0]
    o_ref[...] = jnp.dot(h.astype(BF16), w_ref[...], preferred_element_type=F32)


def _inproj_call(geom, x, mods, w_in_bf16):
    tm = 512 if geom.nc % 512 == 0 and geom.t % 512 == 0 else geom.c
    tn = D_IN // 4
    return pl.pallas_call(
        _inproj_kernel,
        grid=(D_IN // tn, geom.nt // tm),
        in_specs=[pl.BlockSpec((tm, D_MODEL), lambda j, i: (i, 0)),
                  _mod_spec(geom, tm, 0, grid_pos=1),
                  _mod_spec(geom, tm, 1, grid_pos=1),
                  pl.BlockSpec((D_MODEL, tn), lambda j, i: (0, j))],
        out_specs=pl.BlockSpec((tm, tn), lambda j, i: (i, j)),
        out_shape=jax.ShapeDtypeStruct((geom.nt, D_IN), F32),
        compiler_params=_params(2, 48),
        name="in_proj",
    )(x, mods, mods, w_in_bf16)


def _rot_half_128(v):
    lane = lax.broadcasted_iota(jnp.int32, v.shape, 1)
    return jnp.where((lane % 64) < 32, pltpu.roll(v, 96, 1), pltpu.roll(v, 32, 1))


def _ret_kernel(lg_ref, ql_ref, qc_ref, kl_ref, kc_ref, vl_ref, vc_ref, g_ref, cos_ref, sin_ref, o_ref,
                qs, kts, yf, yb, st, dm, qwb, kwb, gcs, *, t, c):
    ch = RET_CHUNK
    hp = pl.program_id(1)
    r = pl.program_id(2)
    lat_blocks = t // c
    n_lat, n_ctx = t // ch, c // ch

    @pl.when(r == 0)
    def _scan():
        ri = lax.broadcasted_iota(jnp.int32, (ch, ch), 0).astype(F32)
        ci = lax.broadcasted_iota(jnp.int32, (ch, ch), 1).astype(F32)
        for d in range(2):
            for h in range(2):
                u = 2 * d + h
                lg = lg_ref[d, 2 * hp + h]
                rel = (ri - ci) if d == 0 else (ci - ri)
                dm[u] = jnp.where(rel >= 0.0, jnp.exp(lg * jnp.maximum(rel, 0.0)), 0.0)
                qwb[u] = jnp.exp(lg * ((ri + 1.0) if d == 0 else (float(ch) - ri)))
                kwb[u] = jnp.exp(lg * ((float(ch) - 1.0 - ri) if d == 0 else ri))
                gcs[u] = jnp.exp(jnp.full((RET_DK, RET_DV), lg * float(ch), F32))
                st[u] = jnp.zeros((RET_DK, RET_DV), F32)

        def stage(q, k, seq_rows):
            qs[0, seq_rows, :] = q[:, :RET_DK].astype(BF16)
            qs[1, seq_rows, :] = q[:, RET_DK:].astype(BF16)
            kt = k.T
            kts[0, :, seq_rows] = kt[:RET_DK].astype(BF16)
            kts[1, :, seq_rows] = kt[RET_DK:].astype(BF16)

        kscale = RET_DK ** -0.5
        for cc in range(n_ctx):
            rows = pl.ds(cc * ch, ch)
            stage(qc_ref[rows, :], kc_ref[rows, :] * kscale, rows)

        def stage_lat(cc, carry):
            rows = pl.ds(pl.multiple_of(cc * ch, ch), ch)
            cs, sn = cos_ref[rows, :], sin_ref[rows, :]
            q = ql_ref[rows, :]
            k = kl_ref[rows, :]
            q = q * cs + _rot_half_128(q) * sn
            k = (k * cs + _rot_half_128(k) * sn) * kscale
            stage(q, k, pl.ds(pl.multiple_of(c + cc * ch, ch), ch))
            return carry

        lax.fori_loop(0, n_lat, stage_lat, 0)

        def run_segment(v_ref, seq_off, n):
            def body(i, carry):
                for d, cc in ((0, i), (1, n - 1 - i)):
                    vrows = pl.ds(pl.multiple_of(cc * ch, ch), ch)
                    srows = pl.ds(pl.multiple_of(seq_off + cc * ch, ch), ch)
                    for h in range(2):
                        u = 2 * d + h
                        q = qs[h, srows, :]
                        kt = kts[h, :, srows]
                        v = v_ref[vrows, h * RET_DV:(h + 1) * RET_DV]
                        s = jnp.dot(q, kt, preferred_element_type=F32)
                        y = jnp.dot((s * dm[u]).astype(BF16), v.astype(BF16), preferred_element_type=F32)
                        state = st[u]
                        y = y + jnp.dot(q, state.astype(BF16), preferred_element_type=F32) * qwb[u]
                        dst = yf if d == 0 else yb
                        dst[srows, h * RET_DV:(h + 1) * RET_DV] = y
                        kv = jnp.dot(kt, (v * kwb[u]).astype(BF16), preferred_element_type=F32)
                        st[u] = gcs[u] * state + kv
                return carry

            lax.fori_loop(0, n, body, 0)

        run_segment(vc_ref, 0, n_ctx)
        run_segment(vl_ref, c, n_lat)

    def finish(srows):
        y = yf[srows, :] + yb[srows, :]
        for h in range(2):
            cols = slice(h * RET_DV, (h + 1) * RET_DV)
            yh = y[:, cols]
            mu = jnp.mean(yh, axis=-1, keepdims=True)
            dlt = yh - mu
            var = jnp.mean(dlt * dlt, axis=-1, keepdims=True)
            o_ref[:, cols] = _silu(g_ref[:, cols]) * (dlt * lax.rsqrt(var + EPS))

    @pl.when(r < lat_blocks)
    def _fin_lat():
        finish(pl.ds(pl.multiple_of(c + r * c, c), c))

    @pl.when(r == lat_blocks)
    def _fin_ctx():
        finish(pl.ds(0, c))


def _retention_call(geom, z, log_gamma, cos128, sin128):
    t, c, p = geom.t, geom.c, geom.p
    hpairs = RET_HEADS // 2
    qb, kb = COL_RQ // 128, COL_RK // 128
    vb, gb = COL_RV // 256, COL_RG // 256
    rb = geom.row_block
    in_specs = [
        pl.BlockSpec(memory_space=pltpu.SMEM),
        pl.BlockSpec((t, 128), lambda b, h, r: (b, qb + h)),
        pl.BlockSpec((c, 128), lambda b, h, r: (geom.nlb + b, qb + h)),
        pl.BlockSpec((t, 128), lambda b, h, r: (b, kb + h)),
        pl.BlockSpec((c, 128), lambda b, h, r: (geom.nlb + b, kb + h)),
        pl.BlockSpec((t, 256), lambda b, h, r: (b, vb + h)),
        pl.BlockSpec((c, 256), lambda b, h, r: (geom.nlb + b, vb + h)),
        pl.BlockSpec((c, 256), lambda b, h, r: (rb(b, r), gb + h)),
        pl.BlockSpec((t, 128), lambda b, h, r: (0, 0)),
        pl.BlockSpec((t, 128), lambda b, h, r: (0, 0)),
    ]
    scratch = [
        pltpu.VMEM((2, p, RET_DK), BF16),
        pltpu.VMEM((2, RET_DK, p), BF16),
        pltpu.VMEM((p, 2 * RET_DV), F32),
        pltpu.VMEM((p, 2 * RET_DV), F32),
        pltpu.VMEM((4, RET_DK, RET_DV), F32),
        pltpu.VMEM((4, RET_CHUNK, RET_CHUNK), F32),
        pltpu.VMEM((4, RET_CHUNK, RET_CHUNK), F32),
        pltpu.VMEM((4, RET_CHUNK, RET_CHUNK), F32),
        pltpu.VMEM((4, RET_DK, RET_DV), F32),
    ]
    return pl.pallas_call(
        functools.partial(_ret_kernel, t=t, c=c),
        grid=(geom.b, hpairs, geom.lat_blocks + 1),
        in_specs=in_specs,
        out_specs=pl.BlockSpec((c, 256), lambda b, h, r: (rb(b, r), h)),
        out_shape=jax.ShapeDtypeStruct((geom.nt, RET_W), F32),
        scratch_shapes=scratch,
        compiler_params=_params(3, 56),
        name="retention",
    )(log_gamma, z, z, z, z, z, z, z, cos128, sin128)


def _rms_heads_128(v, g):
    li = lax.broadcasted_iota(jnp.int32, (128, 128), 0) // ATT_DH
    lj = lax.broadcasted_iota(jnp.int32, (128, 128), 1) // ATT_DH
    avg = jnp.where(li == lj, 1.0 / ATT_DH, 0.0).astype(BF16)
    sq = v * v
    hi = sq.astype(BF16)
    lo = (sq - hi.astype(F32)).astype(BF16)
    ms = jnp.dot(hi, avg, preferred_element_type=F32) + jnp.dot(lo, avg, preferred_element_type=F32)
    return v * lax.rsqrt(ms + EPS) * g


def _att_kernel(qa_ref, qb_ref, kvl_ref, kvc_ref, qn_ref, kn_ref, cos_ref, sin_ref, o_ref,
                kts, vs, m_s, acc_s, *, t, c):
    r = pl.program_id(2)
    lat_blocks = t // c
    dh = ATT_DH
    tk = ATT_KEY_BLOCK
    lane = lax.broadcasted_iota(jnp.int32, (c, 2 * dh), 1)

    def stage_tile(kv, dst, cs, sn):
        k = _rms_heads_128(kv, kn_ref[...])
        if cs is not None:
            k = k * cs + _rot_half_128(k) * sn
        kts[:, dst] = k.T[:dh].astype(BF16)
        vs[dst, :] = jnp.where(lane < dh, pltpu.roll(kv, dh, 1), 1.0).astype(BF16)

    @pl.when(r == 0)
    def _stage_kv():
        stage_tile(kvc_ref[...], pl.ds(0, c), None, None)

        def stage(i, carry):
            rows = pl.ds(pl.multiple_of(i * c, c), c)
            stage_tile(kvl_ref[rows, :], pl.ds(pl.multiple_of(c + i * c, c), c), cos_ref[rows, :], sin_ref[rows, :])
            return carry

        lax.fori_loop(0, lat_blocks, stage, 0)

    is_ctx = r == lat_blocks
    rows = pl.ds(pl.multiple_of(jnp.minimum(r, lat_blocks - 1) * c, c), c)
    cs, sn = cos_ref[rows, :], sin_ref[rows, :]
    q_heads = []
    for src in (qa_ref, qb_ref):
        xn = _rms_heads_128(src[...], qn_ref[...])
        xr = jnp.where(is_ctx, xn, xn * cs + _rot_half_128(xn) * sn) * (dh ** -0.5)
        q_heads.append(xr[:, :dh].astype(BF16))
        q_heads.append(pltpu.roll(xr, dh, 1)[:, :dh].astype(BF16))
    q = jnp.concatenate(q_heads, axis=0)

    m_s[...] = jnp.full(m_s.shape, -jnp.inf, F32)
    acc_s[...] = jnp.zeros(acc_s.shape, F32)

    def flash_step(kt, v):
        n = kt.shape[1]
        s = jnp.dot(q, kt, preferred_element_type=F32)
        m_prev = m_s[...]
        m_next = jnp.maximum(m_prev, jnp.max(s, axis=1, keepdims=True))
        prob = jnp.exp(s - jnp.concatenate([m_next] * (n // 128), axis=1))
        acc_s[...] = acc_s[...] * jnp.exp(m_prev - m_next) + jnp.dot(prob.astype(BF16), v, preferred_element_type=F32)
        m_s[...] = m_next

    flash_step(kts[:, 0:c], vs[0:c, :])

    @pl.when(jnp.logical_not(is_ctx))
    def _latent_keys():
        def lat_step(j, carry):
            krows = pl.ds(pl.multiple_of(c + j * tk, 128), tk)
            flash_step(kts[:, krows], vs[krows, :])
            return carry

        lax.fori_loop(0, t // tk, lat_step, 0, unroll=4)

    outs = []
    for h in range(ATT_GROUP):
        acc = acc_s[h * c:(h + 1) * c, :]
        outs.append(acc * pltpu.roll(1.0 / acc, dh, 1))
    for pair in range(ATT_GROUP // 2):
        both = jnp.where(lane < dh, outs[2 * pair], pltpu.roll(outs[2 * pair + 1], dh, 1))
        o_ref[:, pair * 2 * dh:(pair + 1) * 2 * dh] = both


def _attention_call(geom, z, q_norm, k_norm, cos128, sin128):
    t, c, p = geom.t, geom.c, geom.p
    ab = COL_ATT // 128
    sec = ATT_SECTION // 128
    rb = geom.row_block
    in_specs = [
        pl.BlockSpec((c, 128), lambda b, g, r: (rb(b, r), ab + sec * g)),
        pl.BlockSpec((c, 128), lambda b, g, r: (rb(b, r), ab + sec * g + 1)),
        pl.BlockSpec((t, 128), lambda b, g, r: (b, ab + sec * g + 2)),
        pl.BlockSpec((c, 128), lambda b, g, r: (geom.nlb + b, ab + sec * g + 2)),
        pl.BlockSpec((1, 128), lambda b, g, r: (0, 0)),
        pl.BlockSpec((1, 128), lambda b, g, r: (0, 0)),
        pl.BlockSpec((t, 128), lambda b, g, r: (0, 0)),
        pl.BlockSpec((t, 128), lambda b, g, r: (0, 0)),
    ]
    two_heads = lambda v: jnp.tile(v.reshape(1, ATT_DH), (1, 2))
    scratch = [
        pltpu.VMEM((ATT_DH, p), BF16),
        pltpu.VMEM((p, 2 * ATT_DH), BF16),
        pltpu.VMEM((ATT_GROUP * c, 128), F32),
        pltpu.VMEM((ATT_GROUP * c, 2 * ATT_DH), F32),
    ]
    return pl.pallas_call(
        functools.partial(_att_kernel, t=t, c=c),
        grid=(geom.b, ATT_KV_HEADS, geom.lat_blocks + 1),
        in_specs=in_specs,
        out_specs=pl.BlockSpec((c, ATT_GROUP * ATT_DH), lambda b, g, r: (rb(b, r), g)),
        out_shape=jax.ShapeDtypeStruct((geom.nt, ATT_W), F32),
        scratch_shapes=scratch,
        compiler_params=_params(3, 48),
        name="attention",
    )(z, z, z, z, two_heads(q_norm), two_heads(k_norm), cos128, sin128)


def _conv_kernel(a_ref, g_ref, ap_ref, gp_ref, an_ref, gn_ref, w_ref, b_ref, lng_ref, lnb_ref, o_ref,
                 ext, ys, *, t, c):
    r = pl.program_id(1)
    lat_blocks = t // c
    halo = CONV_HALO
    has_prev = jnp.logical_and(r != 0, r != lat_blocks)
    has_next = jnp.logical_and(r != lat_blocks - 1, r != lat_blocks)
    ext[halo:halo + c, :] = a_ref[...] * jax.nn.sigmoid(g_ref[...])
    ext[0:halo, :] = jnp.where(has_prev, ap_ref[...] * jax.nn.sigmoid(gp_ref[...]), 0.0)
    ext[halo + c:, :] = jnp.where(has_next, an_ref[...] * jax.nn.sigmoid(gn_ref[...]), 0.0)

    rt = 64
    first = halo - CONV_K // 2

    def lane_block(cb, carry):
        lanes = pl.ds(pl.multiple_of(cb * 128, 128), 128)
        for ti in range(c // rt):
            acc = jnp.zeros((rt, 128), F32)
            for j in range(CONV_K):
                acc = acc + w_ref[pl.ds(j, 1), lanes] * ext[pl.ds(ti * rt + first + j, rt), lanes]
            ys[pl.ds(ti * rt, rt), lanes] = acc
        return carry

    lax.fori_loop(0, CONV_CH // 128, lane_block, 0)
    y = ys[...] + b_ref[...]
    o_ref[...] = _silu(_layer_norm(y, lng_ref[...], lnb_ref[...]))


def _conv_call(geom, z, conv_dw, conv_db, ln_g, ln_b):
    t, c = geom.t, geom.c
    rb = geom.row_block
    hb = c // CONV_HALO
    last = geom.nt // CONV_HALO - 1
    prev = lambda b, r: jnp.maximum(rb(b, r) * hb - 1, 0)
    nxt = lambda b, r: jnp.minimum((rb(b, r) + 1) * hb, last)
    w = jnp.zeros((32, CONV_CH), F32).at[:CONV_K].set(conv_dw)
    vec = lambda v: v.reshape(1, CONV_CH)
    cst = pl.BlockSpec((1, CONV_CH), lambda b, r: (0, 0))
    in_specs = [
        pl.BlockSpec((c, CONV_CH), lambda b, r: (rb(b, r), 0)),
        pl.BlockSpec((c, CONV_CH), lambda b, r: (rb(b, r), 1)),
        pl.BlockSpec((CONV_HALO, CONV_CH), lambda b, r: (prev(b, r), 0)),
        pl.BlockSpec((CONV_HALO, CONV_CH), lambda b, r: (prev(b, r), 1)),
        pl.BlockSpec((CONV_HALO, CONV_CH), lambda b, r: (nxt(b, r), 0)),
        pl.BlockSpec((CONV_HALO, CONV_CH), lambda b, r: (nxt(b, r), 1)),
        pl.BlockSpec((32, CONV_CH), lambda b, r: (0, 0)),
        cst, cst, cst,
    ]
    return pl.pallas_call(
        functools.partial(_conv_kernel, t=t, c=c),
        grid=(geom.b, geom.lat_blocks + 1),
        in_specs=in_specs,
        out_specs=pl.BlockSpec((c, CONV_CH), lambda b, r: (rb(b, r), 0)),
        out_shape=jax.ShapeDtypeStruct((geom.nt, CONV_CH), F32),
        scratch_shapes=[pltpu.VMEM((c + 2 * CONV_HALO, CONV_CH), F32), pltpu.VMEM((c, CONV_CH), F32)],
        compiler_params=_params(2, 32),
        name="conformer_conv",
    )(z, z, z, z, z, z, w, vec(conv_db), vec(ln_g), vec(ln_b))


def _mix_kernel(ret_ref, att_ref, cv_ref, gr_ref, ga_ref, gc_ref, x_ref, g1_ref, sh2_ref, sc2_ref,
                wr_ref, wa_ref, wc_ref, wo_ref, lng_ref, lnb_ref, x1_ref, h2_ref, *, alpha):
    def proj(v_ref, w_ref):
        return jnp.dot(v_ref[...].astype(BF16), w_ref[...], preferred_element_type=F32)

    merged = (jax.nn.sigmoid(gr_ref[...]) * proj(ret_ref, wr_ref)
              + jax.nn.sigmoid(ga_ref[...]) * proj(att_ref, wa_ref)
              + jax.nn.sigmoid(gc_ref[...]) * proj(cv_ref, wc_ref))
    y = jnp.dot(merged.astype(BF16), wo_ref[...], preferred_element_type=F32)
    x1 = _layer_norm(alpha * x_ref[...] + g1_ref[0] * y, lng_ref[...], lnb_ref[...])
    x1_ref[...] = x1
    h2_ref[...] = x1 * (1.0 + sc2_ref[0]) + sh2_ref[0]


def _mix_call(geom, alpha, ret, att, cv, z, x, mods, w_ret_o, w_att_o, w_conv_o, w_out, ln_g, ln_b):
    tm = geom.c
    d = D_MODEL
    tile = pl.BlockSpec((tm, d), lambda i: (i, 0))
    gate = lambda k: pl.BlockSpec((tm, d), lambda i: (i, COL_GT // d + k))
    wsp = pl.BlockSpec((d, d), lambda i: (0, 0))
    vsp = pl.BlockSpec((1, d), lambda i: (0, 0))
    return pl.pallas_call(
        functools.partial(_mix_kernel, alpha=alpha),
        grid=(geom.nt // tm,),
        in_specs=[tile, tile, tile, gate(0), gate(1), gate(2), tile,
                  _mod_spec(geom, tm, 2), _mod_spec(geom, tm, 3), _mod_spec(geom, tm, 4),
                  wsp, wsp, wsp, wsp, vsp, vsp],
        out_specs=[tile, tile],
        out_shape=[jax.ShapeDtypeStruct((geom.nt, d), F32)] * 2,
        compiler_params=_params(1, 48),
        name="merge_ln1",
    )(ret, att, cv, z, z, z, x, mods, mods, mods, w_ret_o, w_att_o, w_conv_o, w_out,
      ln_g.reshape(1, d), ln_b.reshape(1, d))


def _router_kernel(h_ref, wr_ref, bias_ref, e_ref, w_ref, pos_ref, cnt_ref, cnt):
    i = pl.program_id(0)
    tm = h_ref.shape[0]
    ne, per = N_EXPERTS, N_EXPERTS // N_GROUPS
    neg = -jnp.inf

    @pl.when(i == 0)
    def _init():
        cnt[...] = jnp.zeros(cnt.shape, F32)

    logits = jnp.dot(h_ref[...], wr_ref[...], preferred_element_type=F32, precision=HIGHEST)
    scores = jax.nn.sigmoid(logits.T[:ne])
    sel = scores + bias_ref[...]

    member = lax.broadcasted_iota(jnp.int32, (per, tm), 0)
    grp_rows = []
    for g in range(N_GROUPS):
        blk = sel[g * per:(g + 1) * per]
        m1 = jnp.max(blk, axis=0, keepdims=True)
        first = jnp.min(jnp.where(blk == m1, member, per), axis=0, keepdims=True)
        m2 = jnp.max(jnp.where(member == first, neg, blk), axis=0, keepdims=True)
        grp_rows.append(m1 + m2)
    gs = jnp.concatenate(grp_rows, axis=0)

    gidx = lax.broadcasted_iota(jnp.int32, (N_GROUPS, tm), 0)
    rank = jnp.zeros((N_GROUPS, tm), jnp.int32)
    for g in range(N_GROUPS):
        row = gs[g:g + 1]
        ahead = jnp.logical_or(row > gs, jnp.logical_and(row == gs, g < gidx))
        rank = rank + ahead.astype(jnp.int32)
    keep = (rank < TOPK_GROUPS).astype(F32)
    keep_e = jnp.concatenate([jnp.broadcast_to(keep[g:g + 1], (per, tm)) for g in range(N_GROUPS)], axis=0)
    cand = jnp.where(keep_e > 0.5, sel, neg)

    eidx = lax.broadcasted_iota(jnp.int32, (ne, tm), 0)
    picks, gates, hots = [], [], []
    chosen = jnp.zeros((ne, tm), F32)
    for _ in range(TOP_K):
        m = jnp.max(cand, axis=0, keepdims=True)
        idx = jnp.min(jnp.where(cand == m, eidx, ne), axis=0, keepdims=True)
        hot = eidx == idx
        picks.append(idx)
        gates.append(jnp.sum(jnp.where(hot, scores, 0.0), axis=0, keepdims=True))
        hots.append(hot)
        chosen = jnp.where(hot, 1.0, chosen)
        cand = jnp.where(hot, neg, cand)
    total = gates[0]
    for gk in gates[1:]:
        total = total + gk

    ti = lax.broadcasted_iota(jnp.int32, (tm, tm), 0)
    tj = lax.broadcasted_iota(jnp.int32, (tm, tm), 1)
    before = jnp.where(ti < tj, 1.0, 0.0).astype(BF16)
    prior = jnp.dot(chosen.astype(BF16), before, preferred_element_type=F32) + cnt[...][:, :1]
    pos = [jnp.sum(jnp.where(hot, prior, 0.0), axis=0, keepdims=True) for hot in hots]

    e_ref[...] = jnp.concatenate(picks, axis=0)
    w_ref[...] = jnp.concatenate([ROUTED_SCALE * gk / total for gk in gates], axis=0)
    pos_ref[...] = jnp.concatenate(pos, axis=0).astype(jnp.int32)
    cnt[...] = cnt[...] + jnp.sum(chosen, axis=1, keepdims=True)
    cnt_ref[...] = cnt[...]


def _router_call(geom, h2, w_router, router_bias):
    tm = geom.c
    wr = jnp.zeros((D_MODEL, 128), F32).at[:, :N_EXPERTS].set(w_router)
    tok = pl.BlockSpec((TOP_K, tm), lambda i: (0, i))
    return pl.pallas_call(
        _router_kernel,
        grid=(geom.nt // tm,),
        in_specs=[pl.BlockSpec((tm, D_MODEL), lambda i: (i, 0)),
                  pl.BlockSpec((D_MODEL, 128), lambda i: (0, 0)),
                  pl.BlockSpec((N_EXPERTS, 1), lambda i: (0, 0))],
        out_specs=[tok, tok, tok, pl.BlockSpec((N_EXPERTS, 128), lambda i: (0, 0))],
        out_shape=[jax.ShapeDtypeStruct((TOP_K, geom.nt), jnp.int32),
                   jax.ShapeDtypeStruct((TOP_K, geom.nt), F32),
                   jax.ShapeDtypeStruct((TOP_K, geom.nt), jnp.int32),
                   jax.ShapeDtypeStruct((N_EXPERTS, 128), F32)],
        scratch_shapes=[pltpu.VMEM((N_EXPERTS, 128), F32)],
        compiler_params=_params(1, 32),
        name="moe_router",
    )(h2, wr, router_bias.reshape(N_EXPERTS, 1))


def _row_copy(src, src_row, dst, dst_row, sem):
    return pltpu.make_async_copy(src.at[pl.ds(src_row, 1)], dst.at[pl.ds(dst_row, 1)], sem)


HALF = D_MODEL // 2


def _pack_bf16_pairs(v):
    lo = pltpu.bitcast(v[:, :HALF].astype(BF16).astype(F32), jnp.uint32)
    hi = pltpu.bitcast(v[:, HALF:].astype(BF16).astype(F32), jnp.uint32)
    return jnp.bitwise_or(jnp.right_shift(lo, jnp.uint32(16)), hi)


def _unpack_bf16_pairs(w):
    lo = pltpu.bitcast(jnp.left_shift(w, jnp.uint32(16)), F32)
    hi = pltpu.bitcast(jnp.bitwise_and(w, jnp.uint32(0xFFFF0000)), F32)
    return lo, hi


def _dispatch_kernel(last_ref, dest_ref, h_ref, xs_out, packed, zblk, sem, zsem):
    tm = h_ref.shape[0]

    @pl.when(pl.program_id(0) == 0)
    def _zero_tail_blocks():
        zblk[...] = jnp.zeros(zblk.shape, zblk.dtype)

        def zero_copy(e):
            return pltpu.make_async_copy(zblk, xs_out.at[pl.ds(last_ref[e] * MOE_BLOCK, MOE_BLOCK)], zsem)

        def start(e, carry):
            zero_copy(e).start()
            return carry

        def wait(e, carry):
            zero_copy(e).wait()
            return carry

        lax.fori_loop(0, N_EXPERTS, start, 0)
        lax.fori_loop(0, N_EXPERTS, wait, 0)

    packed[...] = _pack_bf16_pairs(h_ref[...])

    def issue(tok, carry):
        for k in range(TOP_K):
            _row_copy(packed, tok, xs_out, dest_ref[tok * TOP_K + k], sem).start()
        return carry

    lax.fori_loop(0, tm, issue, 0)

    def drain(tok, carry):
        for k in range(TOP_K):
            _row_copy(packed, 0, xs_out, 0, sem).wait()
        return carry

    lax.fori_loop(0, tm, drain, 0)


def _dispatch_call(geom, last_block, dest_flat, h2, n_blocks):
    tm = geom.c
    grid_spec = pltpu.PrefetchScalarGridSpec(
        num_scalar_prefetch=1,
        grid=(geom.nt // tm,),
        in_specs=[pl.BlockSpec((tm * TOP_K,), lambda i, lb: (i,), memory_space=pltpu.SMEM),
                  pl.BlockSpec((tm, D_MODEL), lambda i, lb: (i, 0))],
        out_specs=pl.BlockSpec(memory_space=pl.ANY),
        scratch_shapes=[pltpu.VMEM((tm, HALF), jnp.uint32), pltpu.VMEM((MOE_BLOCK, HALF), jnp.uint32),
                        pltpu.SemaphoreType.DMA(()), pltpu.SemaphoreType.DMA(())],
    )
    return pl.pallas_call(
        _dispatch_kernel,
        grid_spec=grid_spec,
        out_shape=jax.ShapeDtypeStruct(((n_blocks + 1) * MOE_BLOCK, HALF), jnp.uint32),
        compiler_params=_params(1, 32),
        name="moe_dispatch",
    )(last_block, dest_flat, h2)


def _expert_kernel(be_ref, nu_ref, x_ref, wgu_ref, wd_ref, o_ref):
    del be_ref

    @pl.when(pl.program_id(0) < nu_ref[0])
    def _run():
        lo, hi = _unpack_bf16_pairs(x_ref[...])
        x = jnp.concatenate([lo, hi], axis=1).astype(BF16)
        hgu = jnp.dot(x, wgu_ref[0], preferred_element_type=F32)
        hid = _silu(hgu[:, :D_EXPERT]) * hgu[:, D_EXPERT:]
        o_ref[...] = _pack_bf16_pairs(jnp.dot(hid.astype(BF16), wd_ref[0], preferred_element_type=F32))


def _expert_call(block_e, n_used, xs, w_gu, w_down):
    n_blocks = xs.shape[0] // MOE_BLOCK - 1
    live = lambda i, be, nu: jnp.minimum(i, nu[0] - 1)
    grid_spec = pltpu.PrefetchScalarGridSpec(
        num_scalar_prefetch=2,
        grid=(n_blocks,),
        in_specs=[pl.BlockSpec((MOE_BLOCK, HALF), lambda i, be, nu: (live(i, be, nu), 0)),
                  pl.BlockSpec((1, D_MODEL, 2 * D_EXPERT), lambda i, be, nu: (be[live(i, be, nu)], 0, 0)),
                  pl.BlockSpec((1, D_EXPERT, D_MODEL), lambda i, be, nu: (be[live(i, be, nu)], 0, 0))],
        out_specs=pl.BlockSpec((MOE_BLOCK, HALF), lambda i, be, nu: (live(i, be, nu), 0)),
    )
    return pl.pallas_call(
        _expert_kernel,
        grid_spec=grid_spec,
        out_shape=jax.ShapeDtypeStruct(xs.shape, jnp.uint32),
        compiler_params=_params(1, 32),
        name="moe_experts",
    )(block_e, n_used, xs, w_gu, w_down)


def _combine_kernel(dcur_ref, dnxt_ref, wt_ref, ys_hbm, h_ref, x_ref, g2_ref, wgu_ref, wd_ref,
                    lng_ref, lnb_ref, o_ref, buf_a, buf_b, sem, *, alpha):
    i = pl.program_id(0)
    n = pl.num_programs(0)
    tm = h_ref.shape[0]
    even = i % 2 == 0

    def gather(d_ref, buf, slot):
        def body(tok, carry):
            for k in range(TOP_K):
                pltpu.make_async_copy(ys_hbm.at[pl.ds(d_ref[tok * TOP_K + k], 1)],
                                      buf.at[k, pl.ds(tok, 1)], sem.at[slot]).start()
            return carry

        lax.fori_loop(0, tm, body, 0)

    def drain(buf, slot):
        def body(tok, carry):
            for k in range(TOP_K):
                pltpu.make_async_copy(ys_hbm.at[pl.ds(0, 1)], buf.at[k, pl.ds(0, 1)], sem.at[slot]).wait()
            return carry

        lax.fori_loop(0, tm, body, 0)

    def finish(buf):
        wt = wt_ref[...]
        lo, hi = _unpack_bf16_pairs(buf[0])
        routed_lo, routed_hi = lo * wt[:, 0:1], hi * wt[:, 0:1]
        for k in range(1, TOP_K):
            lo, hi = _unpack_bf16_pairs(buf[k])
            routed_lo = routed_lo + lo * wt[:, k:k + 1]
            routed_hi = routed_hi + hi * wt[:, k:k + 1]
        routed = jnp.concatenate([routed_lo, routed_hi], axis=1)
        hgu = jnp.dot(h_ref[...].astype(BF16), wgu_ref[...], preferred_element_type=F32)
        hid = _silu(hgu[:, :D_SHARED]) * hgu[:, D_SHARED:]
        shared = jnp.dot(hid.astype(BF16), wd_ref[...], preferred_element_type=F32)
        o_ref[...] = _layer_norm(alpha * x_ref[...] + g2_ref[0] * (routed + shared), lng_ref[...], lnb_ref[...])

    @pl.when(i == 0)
    def _first():
        gather(dcur_ref, buf_a, 0)

    @pl.when(jnp.logical_and(even, i + 1 < n))
    def _ahead_b():
        gather(dnxt_ref, buf_b, 1)

    @pl.when(jnp.logical_and(jnp.logical_not(even), i + 1 < n))
    def _ahead_a():
        gather(dnxt_ref, buf_a, 0)

    @pl.when(even)
    def _finish_a():
        drain(buf_a, 0)
        finish(buf_a)

    @pl.when(jnp.logical_not(even))
    def _finish_b():
        drain(buf_b, 1)
        finish(buf_b)


def _combine_call(geom, alpha, dest, w_tok, ys, h2, x1, mods, w_sh_gu, w_sh_down, ln_g, ln_b):
    tm = geom.c
    d = D_MODEL
    n = geom.nt // tm
    tile = pl.BlockSpec((tm, d), lambda i: (i, 0))
    vsp = pl.BlockSpec((1, d), lambda i: (0, 0))
    return pl.pallas_call(
        functools.partial(_combine_kernel, alpha=alpha),
        grid=(n,),
        in_specs=[pl.BlockSpec((tm * TOP_K,), lambda i: (i,), memory_space=pltpu.SMEM),
                  pl.BlockSpec((tm * TOP_K,), lambda i: (jnp.minimum(i + 1, n - 1),), memory_space=pltpu.SMEM),
                  pl.BlockSpec((tm, TOP_K), lambda i: (i, 0)),
                  pl.BlockSpec(memory_space=pl.ANY),
                  tile, tile, _mod_spec(geom, tm, 5),
                  pl.BlockSpec((d, 2 * D_SHARED), lambda i: (0, 0)),
                  pl.BlockSpec((D_SHARED, d), lambda i: (0, 0)),
                  vsp, vsp],
        out_specs=tile,
        out_shape=jax.ShapeDtypeStruct((geom.nt, d), F32),
        scratch_shapes=[pltpu.VMEM((TOP_K, tm, HALF), jnp.uint32), pltpu.VMEM((TOP_K, tm, HALF), jnp.uint32),
                        pltpu.SemaphoreType.DMA((2,))],
        compiler_params=_params(1, 48),
        name="moe_combine_ln2",
    )(dest, dest, w_tok, ys, h2, x1, mods, w_sh_gu, w_sh_down, ln_g.reshape(1, d), ln_b.reshape(1, d))


def _rope_tables(t):
    rows = t // GRID_W
    row = jnp.repeat(jnp.arange(rows, dtype=F32), GRID_W)
    col = jnp.tile(jnp.arange(GRID_W, dtype=F32), rows)
    n_freq = ATT_DH // 4
    inv_freq = ROPE_THETA ** (-jnp.arange(n_freq, dtype=F32) / n_freq)
    ang = jnp.concatenate([row[:, None] * inv_freq, col[:, None] * inv_freq], axis=-1)
    cos, sin = jnp.cos(ang), jnp.sin(ang)
    cos64 = jnp.concatenate([cos, cos], axis=-1)
    sin64 = jnp.concatenate([-sin, sin], axis=-1)
    return cos64, sin64


def kernel(x, c, ctx, c_ctx, w_ada, b_ada, w_in, ret_decay_logit, att_q_norm, att_k_norm, conv_dw, conv_db, conv_ln_g, conv_ln_b, w_ret_o, w_att_o, w_conv_o, w_out, ln1_g, ln1_b, w_router, router_bias, w_exp_gate, w_exp_up, w_exp_down, w_sh_gate, w_sh_up, w_sh_down, ln2_g, ln2_b):
    b, t, d = x.shape
    n_ctx = ctx.shape[1]
    depth = w_ada.shape[0]
    assert d == D_MODEL and w_in.shape[-1] == D_IN
    geom = _Geom(b, t, n_ctx)
    alpha = float((2 * depth) ** 0.25)

    cos64, sin64 = _rope_tables(t)
    cos128 = jnp.concatenate([cos64, cos64], axis=-1)
    sin128 = jnp.concatenate([sin64, sin64], axis=-1)

    n_rows = -(-(b + 1) // 8) * 8
    cvecs = jnp.zeros((n_rows, d), F32).at[:b].set(c).at[b].set(c_ctx)
    mods_all = _mods_call(cvecs, w_ada, b_ada).reshape(depth, n_rows * 6, 1, d)

    n_blocks = -(-(geom.nt * TOP_K + N_EXPERTS * (MOE_BLOCK - 1)) // MOE_BLOCK)

    xt = jnp.concatenate([x.reshape(geom.nl, d), ctx.reshape(geom.nc, d)], axis=0)
    for l in range(depth):
        mods = mods_all[l]
        w_in_l = _permute_columns(w_in[l]).astype(BF16)
        z = _inproj_call(geom, xt, mods, w_in_l)

        log_gamma = jax.nn.log_sigmoid(ret_decay_logit[l].astype(F32))
        ret = _retention_call(geom, z, log_gamma, cos128, sin128)
        att = _attention_call(geom, z, att_q_norm[l], att_k_norm[l], cos128, sin128)
        cv = _conv_call(geom, z, conv_dw[l], conv_db[l], conv_ln_g[l], conv_ln_b[l])
        x1, h2 = _mix_call(geom, alpha, ret, att, cv, z, xt, mods,
                           w_ret_o[l].astype(BF16), w_att_o[l].astype(BF16), w_conv_o[l].astype(BF16),
                           w_out[l].astype(BF16), ln1_g[l], ln1_b[l])

        top_e, gate_w, pos, counts = _router_call(geom, h2, w_router[l], router_bias[l])
        cnt = counts[:, 0].astype(jnp.int32)
        blocks_e = (cnt + MOE_BLOCK - 1) // MOE_BLOCK
        blocks_end = jnp.cumsum(blocks_e)
        start_row = (blocks_end - blocks_e) * MOE_BLOCK
        expert_ids = jnp.arange(N_EXPERTS, dtype=jnp.int32)[None, None, :]
        dest = jnp.sum(jnp.where(top_e[:, :, None] == expert_ids, start_row[None, None, :], 0), axis=-1) + pos
        block_ids = jnp.arange(n_blocks, dtype=jnp.int32)
        block_e = jnp.minimum(jnp.sum((blocks_end[None, :] <= block_ids[:, None]).astype(jnp.int32), axis=1),
                              N_EXPERTS - 1)
        n_used = blocks_end[-1:].astype(jnp.int32)
        last_block = jnp.where(blocks_e > 0, blocks_end - 1, n_blocks).astype(jnp.int32)
        dest_flat = dest.T.reshape(-1)

        xs = _dispatch_call(geom, last_block, dest_flat, h2, n_blocks)
        w_gu = jnp.concatenate([w_exp_gate[l], w_exp_up[l]], axis=-1).astype(BF16)
        ys = _expert_call(block_e, n_used, xs, w_gu, w_exp_down[l].astype(BF16))
        w_sh_gu = jnp.concatenate([w_sh_gate[l], w_sh_up[l]], axis=-1).astype(BF16)
        xt = _combine_call(geom, alpha, dest_flat, gate_w.T, ys, h2, x1, mods, w_sh_gu,
                           w_sh_down[l].astype(BF16), ln2_g[l], ln2_b[l])
    return xt[:geom.nl].reshape(b, t, d)
```

```python
import functools

import jax
import jax.numpy as jnp
from jax import lax
from jax.experimental import pallas as pl
from jax.experimental.pallas import tpu as pltpu

F32 = jnp.float32
BF16 = jnp.bfloat16
HIGHEST = lax.Precision.HIGHEST

D_MODEL = 1024
GRID_W = 64
EPS = 1e-6

RET_HEADS = 8
RET_DK = 64
RET_DV = 128
RET_CHUNK = 128
RET_W = RET_HEADS * RET_DV

ATT_HEADS = 16
ATT_KV_HEADS = 4
ATT_DH = 64
ATT_GROUP = ATT_HEADS // ATT_KV_HEADS
ATT_W = ATT_HEADS * ATT_DH
ROPE_THETA = 10000.0
ATT_KEY_BLOCK = 1024

CONV_CH = 1024
CONV_K = 31
CONV_HALO = 16

N_EXPERTS = 64
TOP_K = 8
N_GROUPS = 8
TOPK_GROUPS = 4
D_EXPERT = 256
D_SHARED = 256
ROUTED_SCALE = 2.5
MOE_BLOCK = 512

_ORIG = dict(rq=0, rk=512, rv=1024, rg=2048, aq=3072, ak=4096, av=4352, cu=4608, gt=6656)
D_IN = 9728
COL_CU = 0
COL_GT = 2048
COL_RG = 5120
COL_RV = 6144
COL_RQ = 7168
COL_RK = 7680
COL_ATT = 8192
ATT_SECTION = ATT_GROUP * ATT_DH + 2 * ATT_DH


def _column_ranges():
    rng = [(_ORIG["cu"], _ORIG["cu"] + 2 * CONV_CH),
           (_ORIG["gt"], _ORIG["gt"] + 3 * D_MODEL),
           (_ORIG["rg"], _ORIG["rg"] + RET_W),
           (_ORIG["rv"], _ORIG["rv"] + RET_W),
           (_ORIG["rq"], _ORIG["rq"] + RET_HEADS * RET_DK),
           (_ORIG["rk"], _ORIG["rk"] + RET_HEADS * RET_DK)]
    for g in range(ATT_KV_HEADS):
        rng.append((_ORIG["aq"] + g * ATT_GROUP * ATT_DH, _ORIG["aq"] + (g + 1) * ATT_GROUP * ATT_DH))
        rng.append((_ORIG["ak"] + g * ATT_DH, _ORIG["ak"] + (g + 1) * ATT_DH))
        rng.append((_ORIG["av"] + g * ATT_DH, _ORIG["av"] + (g + 1) * ATT_DH))
    cols = [c for a, b in rng for c in range(a, b)]
    assert sorted(cols) == list(range(D_IN))
    return rng


def _permute_columns(w):
    return jnp.concatenate([w[:, a:b] for a, b in _column_ranges()], axis=1)


def _params(n_axes, vmem_mib):
    return pltpu.CompilerParams(dimension_semantics=("arbitrary",) * n_axes,
                                vmem_limit_bytes=vmem_mib * 1024 * 1024)


def _silu(v):
    return v * jax.nn.sigmoid(v)


def _layer_norm(v, g, b):
    mu = jnp.mean(v, axis=-1, keepdims=True)
    d = v - mu
    var = jnp.mean(d * d, axis=-1, keepdims=True)
    return d * lax.rsqrt(var + EPS) * g + b


def _mods_kernel(c_ref, w_ref, b_ref, o_ref):
    s = _silu(c_ref[...])
    o_ref[0] = jnp.dot(s, w_ref[0], preferred_element_type=F32, precision=HIGHEST) + b_ref[0]


def _mods_call(cvecs, w_ada, b_ada):
    n_layers = w_ada.shape[0]
    rows, d = cvecs.shape
    return pl.pallas_call(
        _mods_kernel,
        grid=(n_layers, 6),
        in_specs=[pl.BlockSpec((rows, d), lambda l, j: (0, 0)),
                  pl.BlockSpec((1, d, d), lambda l, j: (l, 0, j)),
                  pl.BlockSpec((1, 1, d), lambda l, j: (l, 0, j))],
        out_specs=pl.BlockSpec((1, rows, d), lambda l, j: (l, 0, j)),
        out_shape=jax.ShapeDtypeStruct((n_layers, rows, 6 * d), F32),
        compiler_params=_params(2, 32),
        name="adaln_mods",
    )(cvecs, w_ada, b_ada.reshape(n_layers, 1, 6 * d))


class _Geom:
    def __init__(self, b, t, c):
        assert t % c == 0 and c % RET_CHUNK == 0 and c % CONV_HALO == 0 and t % ATT_KEY_BLOCK == 0
        self.b, self.t, self.c = b, t, c
        self.nl, self.nc = b * t, b * c
        self.nt = self.nl + self.nc
        self.lat_blocks = t // c
        self.nlb = self.nl // c
        self.p = t + c

    def row_block(self, bi, r):
        return jnp.where(r < self.lat_blocks, bi * self.lat_blocks + r, self.nlb + bi)

    def mod_row(self, i, tm):
        return jnp.where(i * tm < self.nl, (i * tm) // self.t, self.b)


def _mod_spec(geom, tm, which, grid_pos=0):
    d = D_MODEL
    if grid_pos == 0:
        return pl.BlockSpec((1, 1, d), lambda i, *_: (geom.mod_row(i, tm) * 6 + which, 0, 0))
    return pl.BlockSpec((1, 1, d), lambda j, i: (geom.mod_row(i, tm) * 6 + which, 0, 0))


def _inproj_kernel(x_ref, sh_ref, sc_ref, w_ref, o_ref):
    h = x_ref[...] * (1.0 + sc_ref[0]) + sh_ref[0]
    o_ref[...] = jnp.dot(h.astype(BF16), w_ref[...], preferred_element_type=F32)


def _inproj_call(geom, x, mods, w_in_bf16):
    tm = 512 if geom.nc % 512 == 0 and geom.t % 512 == 0 else geom.c
    tn = D_IN // 4
    return pl.pallas_call(
        _inproj_kernel,
        grid=(D_IN // tn, geom.nt // tm),
        in_specs=[pl.BlockSpec((tm, D_MODEL), lambda j, i: (i, 0)),
                  _mod_spec(geom, tm, 0, grid_pos=1),
                  _mod_spec(geom, tm, 1, grid_pos=1),
                  pl.BlockSpec((D_MODEL, tn), lambda j, i: (0, j))],
        out_specs=pl.BlockSpec((tm, tn), lambda j, i: (i, j)),
        out_shape=jax.ShapeDtypeStruct((geom.nt, D_IN), F32),
        compiler_params=_params(2, 48),
        name="in_proj",
    )(x, mods, mods, w_in_bf16)


def _rot_half_128(v):
    lane = lax.broadcasted_iota(jnp.int32, v.shape, 1)
    return jnp.where((lane % 64) < 32, pltpu.roll(v, 96, 1), pltpu.roll(v, 32, 1))


def _ret_kernel(lg_ref, ql_ref, qc_ref, kl_ref, kc_ref, vl_ref, vc_ref, g_ref, cos_ref, sin_ref, o_ref,
                qs, kts, yf, yb, st, dm, qwb, kwb, gcs, *, t, c):
    ch = RET_CHUNK
    hp = pl.program_id(1)
    r = pl.program_id(2)
    lat_blocks = t // c
    n_lat, n_ctx = t // ch, c // ch

    @pl.when(r == 0)
    def _scan():
        ri = lax.broadcasted_iota(jnp.int32, (ch, ch), 0).astype(F32)
        ci = lax.broadcasted_iota(jnp.int32, (ch, ch), 1).astype(F32)
        for d in range(2):
            for h in range(2):
                u = 2 * d + h
                lg = lg_ref[d, 2 * hp + h]
                rel = (ri - ci) if d == 0 else (ci - ri)
                dm[u] = jnp.where(rel >= 0.0, jnp.exp(lg * jnp.maximum(rel, 0.0)), 0.0)
                qwb[u] = jnp.exp(lg * ((ri + 1.0) if d == 0 else (float(ch) - ri)))
                kwb[u] = jnp.exp(lg * ((float(ch) - 1.0 - ri) if d == 0 else ri))
                gcs[u] = jnp.exp(jnp.full((RET_DK, RET_DV), lg * float(ch), F32))
                st[u] = jnp.zeros((RET_DK, RET_DV), F32)

        def stage(q, k, seq_rows):
            qs[0, seq_rows, :] = q[:, :RET_DK].astype(BF16)
            qs[1, seq_rows, :] = q[:, RET_DK:].astype(BF16)
            kt = k.T
            kts[0, :, seq_rows] = kt[:RET_DK].astype(BF16)
            kts[1, :, seq_rows] = kt[RET_DK:].astype(BF16)

        kscale = RET_DK ** -0.5
        for cc in range(n_ctx):
            rows = pl.ds(cc * ch, ch)
            stage(qc_ref[rows, :], kc_ref[rows, :] * kscale, rows)

        def stage_lat(cc, carry):
            rows = pl.ds(pl.multiple_of(cc * ch, ch), ch)
            cs, sn = cos_ref[rows, :], sin_ref[rows, :]
            q = ql_ref[rows, :]
            k = kl_ref[rows, :]
            q = q * cs + _rot_half_128(q) * sn
            k = (k * cs + _rot_half_128(k) * sn) * kscale
            stage(q, k, pl.ds(pl.multiple_of(c + cc * ch, ch), ch))
            return carry

        lax.fori_loop(0, n_lat, stage_lat, 0)

        def run_segment(v_ref, seq_off, n):
            def body(i, carry):
                for d, cc in ((0, i), (1, n - 1 - i)):
                    vrows = pl.ds(pl.multiple_of(cc * ch, ch), ch)
                    srows = pl.ds(pl.multiple_of(seq_off + cc * ch, ch), ch)
                    for h in range(2):
                        u = 2 * d + h
                        q = qs[h, srows, :]
                        kt = kts[h, :, srows]
                        v = v_ref[vrows, h * RET_DV:(h + 1) * RET_DV]
                        s = jnp.dot(q, kt, preferred_element_type=F32)
                        y = jnp.dot((s * dm[u]).astype(BF16), v.astype(BF16), preferred_element_type=F32)
                        state = st[u]
                        y = y + jnp.dot(q, state.astype(BF16), preferred_element_type=F32) * qwb[u]
                        dst = yf if d == 0 else yb
                        dst[srows, h * RET_DV:(h + 1) * RET_DV] = y
                        kv = jnp.dot(kt, (v * kwb[u]).astype(BF16), preferred_element_type=F32)
                        st[u] = gcs[u] * state + kv
                return carry

            lax.fori_loop(0, n, body, 0, unroll=2)

        run_segment(vc_ref, 0, n_ctx)
        run_segment(vl_ref, c, n_lat)

    def finish(srows):
        y = yf[srows, :] + yb[srows, :]
        for h in range(2):
            cols = slice(h * RET_DV, (h + 1) * RET_DV)
            yh = y[:, cols]
            mu = jnp.mean(yh, axis=-1, keepdims=True)
            dlt = yh - mu
            var = jnp.mean(dlt * dlt, axis=-1, keepdims=True)
            o_ref[:, cols] = _silu(g_ref[:, cols]) * (dlt * lax.rsqrt(var + EPS))

    @pl.when(r < lat_blocks)
    def _fin_lat():
        finish(pl.ds(pl.multiple_of(c + r * c, c), c))

    @pl.when(r == lat_blocks)
    def _fin_ctx():
        finish(pl.ds(0, c))


def _retention_call(geom, z, log_gamma, cos128, sin128):
    t, c, p = geom.t, geom.c, geom.p
    hpairs = RET_HEADS // 2
    qb, kb = COL_RQ // 128, COL_RK // 128
    vb, gb = COL_RV // 256, COL_RG // 256
    rb = geom.row_block
    in_specs = [
        pl.BlockSpec(memory_space=pltpu.SMEM),
        pl.BlockSpec((t, 128), lambda b, h, r: (b, qb + h)),
        pl.BlockSpec((c, 128), lambda b, h, r: (geom.nlb + b, qb + h)),
        pl.BlockSpec((t, 128), lambda b, h, r: (b, kb + h)),
        pl.BlockSpec((c, 128), lambda b, h, r: (geom.nlb + b, kb + h)),
        pl.BlockSpec((t, 256), lambda b, h, r: (b, vb + h)),
        pl.BlockSpec((c, 256), lambda b, h, r: (geom.nlb + b, vb + h)),
        pl.BlockSpec((c, 256), lambda b, h, r: (rb(b, r), gb + h)),
        pl.BlockSpec((t, 128), lambda b, h, r: (0, 0)),
        pl.BlockSpec((t, 128), lambda b, h, r: (0, 0)),
    ]
    scratch = [
        pltpu.VMEM((2, p, RET_DK), BF16),
        pltpu.VMEM((2, RET_DK, p), BF16),
        pltpu.VMEM((p, 2 * RET_DV), F32),
        pltpu.VMEM((p, 2 * RET_DV), F32),
        pltpu.VMEM((4, RET_DK, RET_DV), F32),
        pltpu.VMEM((4, RET_CHUNK, RET_CHUNK), F32),
        pltpu.VMEM((4, RET_CHUNK, RET_CHUNK), F32),
        pltpu.VMEM((4, RET_CHUNK, RET_CHUNK), F32),
        pltpu.VMEM((4, RET_DK, RET_DV), F32),
    ]
    return pl.pallas_call(
        functools.partial(_ret_kernel, t=t, c=c),
        grid=(geom.b, hpairs, geom.lat_blocks + 1),
        in_specs=in_specs,
        out_specs=pl.BlockSpec((c, 256), lambda b, h, r: (rb(b, r), h)),
        out_shape=jax.ShapeDtypeStruct((geom.nt, RET_W), F32),
        scratch_shapes=scratch,
        compiler_params=_params(3, 56),
        name="retention",
    )(log_gamma, z, z, z, z, z, z, z, cos128, sin128)


def _rms_heads_128(v, g):
    li = lax.broadcasted_iota(jnp.int32, (128, 128), 0) // ATT_DH
    lj = lax.broadcasted_iota(jnp.int32, (128, 128), 1) // ATT_DH
    avg = jnp.where(li == lj, 1.0 / ATT_DH, 0.0).astype(BF16)
    sq = v * v
    hi = sq.astype(BF16)
    lo = (sq - hi.astype(F32)).astype(BF16)
    ms = jnp.dot(hi, avg, preferred_element_type=F32) + jnp.dot(lo, avg, preferred_element_type=F32)
    return v * lax.rsqrt(ms + EPS) * g


def _att_kernel(qa_ref, qb_ref, kvl_ref, kvc_ref, qn_ref, kn_ref, cos_ref, sin_ref, o_ref,
                kts, vs, m_s, acc_s, *, t, c):
    r = pl.program_id(2)
    lat_blocks = t // c
    dh = ATT_DH
    tk = ATT_KEY_BLOCK
    lane = lax.broadcasted_iota(jnp.int32, (c, 2 * dh), 1)

    def stage_tile(kv, dst, cs, sn):
        k = _rms_heads_128(kv, kn_ref[...])
        if cs is not None:
            k = k * cs + _rot_half_128(k) * sn
        kts[:, dst] = k.T[:dh].astype(BF16)
        vs[dst, :] = jnp.where(lane < dh, pltpu.roll(kv, dh, 1), 1.0).astype(BF16)

    @pl.when(r == 0)
    def _stage_kv():
        stage_tile(kvc_ref[...], pl.ds(0, c), None, None)

        def stage(i, carry):
            rows = pl.ds(pl.multiple_of(i * c, c), c)
            stage_tile(kvl_ref[rows, :], pl.ds(pl.multiple_of(c + i * c, c), c), cos_ref[rows, :], sin_ref[rows, :])
            return carry

        lax.fori_loop(0, lat_blocks, stage, 0)

    is_ctx = r == lat_blocks
    rows = pl.ds(pl.multiple_of(jnp.minimum(r, lat_blocks - 1) * c, c), c)
    cs, sn = cos_ref[rows, :], sin_ref[rows, :]
    q_heads = []
    for src in (qa_ref, qb_ref):
        xn = _rms_heads_128(src[...], qn_ref[...])
        xr = jnp.where(is_ctx, xn, xn * cs + _rot_half_128(xn) * sn) * (dh ** -0.5)
        q_heads.append(xr[:, :dh].astype(BF16))
        q_heads.append(pltpu.roll(xr, dh, 1)[:, :dh].astype(BF16))
    q = jnp.concatenate(q_heads, axis=0)

    m_s[...] = jnp.full(m_s.shape, -jnp.inf, F32)
    acc_s[...] = jnp.zeros(acc_s.shape, F32)

    def flash_step(kt, v):
        n = kt.shape[1]
        s = jnp.dot(q, kt, preferred_element_type=F32)
        m_prev = m_s[...]
        m_next = jnp.maximum(m_prev, jnp.max(s, axis=1, keepdims=True))
        prob = jnp.exp(s - jnp.concatenate([m_next] * (n // 128), axis=1))
        acc_s[...] = acc_s[...] * jnp.exp(m_prev - m_next) + jnp.dot(prob.astype(BF16), v, preferred_element_type=F32)
        m_s[...] = m_next

    flash_step(kts[:, 0:c], vs[0:c, :])

    @pl.when(jnp.logical_not(is_ctx))
    def _latent_keys():
        def lat_step(j, carry):
            krows = pl.ds(pl.multiple_of(c + j * tk, 128), tk)
            flash_step(kts[:, krows], vs[krows, :])
            return carry

        lax.fori_loop(0, t // tk, lat_step, 0, unroll=2)

    outs = []
    for h in range(ATT_GROUP):
        acc = acc_s[h * c:(h + 1) * c, :]
        outs.append(acc * pltpu.roll(1.0 / acc, dh, 1))
    for pair in range(ATT_GROUP // 2):
        both = jnp.where(lane < dh, outs[2 * pair], pltpu.roll(outs[2 * pair + 1], dh, 1))
        o_ref[:, pair * 2 * dh:(pair + 1) * 2 * dh] = both


def _attention_call(geom, z, q_norm, k_norm, cos128, sin128):
    t, c, p = geom.t, geom.c, geom.p
    ab = COL_ATT // 128
    sec = ATT_SECTION // 128
    rb = geom.row_block
    in_specs = [
        pl.BlockSpec((c, 128), lambda b, g, r: (rb(b, r), ab + sec * g)),
        pl.BlockSpec((c, 128), lambda b, g, r: (rb(b, r), ab + sec * g + 1)),
        pl.BlockSpec((t, 128), lambda b, g, r: (b, ab + sec * g + 2)),
        pl.BlockSpec((c, 128), lambda b, g, r: (geom.nlb + b, ab + sec * g + 2)),
        pl.BlockSpec((1, 128), lambda b, g, r: (0, 0)),
        pl.BlockSpec((1, 128), lambda b, g, r: (0, 0)),
        pl.BlockSpec((t, 128), lambda b, g, r: (0, 0)),
        pl.BlockSpec((t, 128), lambda b, g, r: (0, 0)),
    ]
    two_heads = lambda v: jnp.tile(v.reshape(1, ATT_DH), (1, 2))
    scratch = [
        pltpu.VMEM((ATT_DH, p), BF16),
        pltpu.VMEM((p, 2 * ATT_DH), BF16),
        pltpu.VMEM((ATT_GROUP * c, 128), F32),
        pltpu.VMEM((ATT_GROUP * c, 2 * ATT_DH), F32),
    ]
    return pl.pallas_call(
        functools.partial(_att_kernel, t=t, c=c),
        grid=(geom.b, ATT_KV_HEADS, geom.lat_blocks + 1),
        in_specs=in_specs,
        out_specs=pl.BlockSpec((c, ATT_GROUP * ATT_DH), lambda b, g, r: (rb(b, r), g)),
        out_shape=jax.ShapeDtypeStruct((geom.nt, ATT_W), F32),
        scratch_shapes=scratch,
        compiler_params=_params(3, 48),
        name="attention",
    )(z, z, z, z, two_heads(q_norm), two_heads(k_norm), cos128, sin128)


def _conv_kernel(a_ref, g_ref, ap_ref, gp_ref, an_ref, gn_ref, w_ref, b_ref, lng_ref, lnb_ref, o_ref,
                 ext, ys, *, t, c):
    r = pl.program_id(1)
    lat_blocks = t // c
    halo = CONV_HALO
    has_prev = jnp.logical_and(r != 0, r != lat_blocks)
    has_next = jnp.logical_and(r != lat_blocks - 1, r != lat_blocks)
    ext[halo:halo + c, :] = a_ref[...] * jax.nn.sigmoid(g_ref[...])
    ext[0:halo, :] = jnp.where(has_prev, ap_ref[...] * jax.nn.sigmoid(gp_ref[...]), 0.0)
    ext[halo + c:, :] = jnp.where(has_next, an_ref[...] * jax.nn.sigmoid(gn_ref[...]), 0.0)

    rt = 64
    first = halo - CONV_K // 2

    def lane_block(cb, carry):
        lanes = pl.ds(pl.multiple_of(cb * 128, 128), 128)
        for ti in range(c // rt):
            acc = jnp.zeros((rt, 128), F32)
            for j in range(CONV_K):
                acc = acc + w_ref[pl.ds(j, 1), lanes] * ext[pl.ds(ti * rt + first + j, rt), lanes]
            ys[pl.ds(ti * rt, rt), lanes] = acc
        return carry

    lax.fori_loop(0, CONV_CH // 128, lane_block, 0)
    y = ys[...] + b_ref[...]
    o_ref[...] = _silu(_layer_norm(y, lng_ref[...], lnb_ref[...]))


def _conv_call(geom, z, conv_dw, conv_db, ln_g, ln_b):
    t, c = geom.t, geom.c
    rb = geom.row_block
    hb = c // CONV_HALO
    last = geom.nt // CONV_HALO - 1
    prev = lambda b, r: jnp.maximum(rb(b, r) * hb - 1, 0)
    nxt = lambda b, r: jnp.minimum((rb(b, r) + 1) * hb, last)
    w = jnp.zeros((32, CONV_CH), F32).at[:CONV_K].set(conv_dw)
    vec = lambda v: v.reshape(1, CONV_CH)
    cst = pl.BlockSpec((1, CONV_CH), lambda b, r: (0, 0))
    in_specs = [
        pl.BlockSpec((c, CONV_CH), lambda b, r: (rb(b, r), 0)),
        pl.BlockSpec((c, CONV_CH), lambda b, r: (rb(b, r), 1)),
        pl.BlockSpec((CONV_HALO, CONV_CH), lambda b, r: (prev(b, r), 0)),
        pl.BlockSpec((CONV_HALO, CONV_CH), lambda b, r: (prev(b, r), 1)),
        pl.BlockSpec((CONV_HALO, CONV_CH), lambda b, r: (nxt(b, r), 0)),
        pl.BlockSpec((CONV_HALO, CONV_CH), lambda b, r: (nxt(b, r), 1)),
        pl.BlockSpec((32, CONV_CH), lambda b, r: (0, 0)),
        cst, cst, cst,
    ]
    return pl.pallas_call(
        functools.partial(_conv_kernel, t=t, c=c),
        grid=(geom.b, geom.lat_blocks + 1),
        in_specs=in_specs,
        out_specs=pl.BlockSpec((c, CONV_CH), lambda b, r: (rb(b, r), 0)),
        out_shape=jax.ShapeDtypeStruct((geom.nt, CONV_CH), F32),
        scratch_shapes=[pltpu.VMEM((c + 2 * CONV_HALO, CONV_CH), F32), pltpu.VMEM((c, CONV_CH), F32)],
        compiler_params=_params(2, 32),
        name="conformer_conv",
    )(z, z, z, z, z, z, w, vec(conv_db), vec(ln_g), vec(ln_b))


def _mix_kernel(ret_ref, att_ref, cv_ref, gr_ref, ga_ref, gc_ref, x_ref, g1_ref, sh2_ref, sc2_ref,
                wr_ref, wa_ref, wc_ref, wo_ref, lng_ref, lnb_ref, x1_ref, h2_ref, *, alpha):
    def proj(v_ref, w_ref):
        return jnp.dot(v_ref[...].astype(BF16), w_ref[...], preferred_element_type=F32)

    merged = (jax.nn.sigmoid(gr_ref[...]) * proj(ret_ref, wr_ref)
              + jax.nn.sigmoid(ga_ref[...]) * proj(att_ref, wa_ref)
              + jax.nn.sigmoid(gc_ref[...]) * proj(cv_ref, wc_ref))
    y = jnp.dot(merged.astype(BF16), wo_ref[...], preferred_element_type=F32)
    x1 = _layer_norm(alpha * x_ref[...] + g1_ref[0] * y, lng_ref[...], lnb_ref[...])
    x1_ref[...] = x1
    h2_ref[...] = x1 * (1.0 + sc2_ref[0]) + sh2_ref[0]


def _mix_call(geom, alpha, ret, att, cv, z, x, mods, w_ret_o, w_att_o, w_conv_o, w_out, ln_g, ln_b):
    tm = geom.c
    d = D_MODEL
    tile = pl.BlockSpec((tm, d), lambda i: (i, 0))
    gate = lambda k: pl.BlockSpec((tm, d), lambda i: (i, COL_GT // d + k))
    wsp = pl.BlockSpec((d, d), lambda i: (0, 0))
    vsp = pl.BlockSpec((1, d), lambda i: (0, 0))
    return pl.pallas_call(
        functools.partial(_mix_kernel, alpha=alpha),
        grid=(geom.nt // tm,),
        in_specs=[tile, tile, tile, gate(0), gate(1), gate(2), tile,
                  _mod_spec(geom, tm, 2), _mod_spec(geom, tm, 3), _mod_spec(geom, tm, 4),
                  wsp, wsp, wsp, wsp, vsp, vsp],
        out_specs=[tile, tile],
        out_shape=[jax.ShapeDtypeStruct((geom.nt, d), F32)] * 2,
        compiler_params=_params(1, 48),
        name="merge_ln1",
    )(ret, att, cv, z, z, z, x, mods, mods, mods, w_ret_o, w_att_o, w_conv_o, w_out,
      ln_g.reshape(1, d), ln_b.reshape(1, d))


def _router_kernel(h_ref, wr_ref, bias_ref, e_ref, w_ref, pos_ref, cnt_ref, cnt):
    i = pl.program_id(0)
    tm = h_ref.shape[0]
    ne, per = N_EXPERTS, N_EXPERTS // N_GROUPS
    neg = -jnp.inf

    @pl.when(i == 0)
    def _init():
        cnt[...] = jnp.zeros(cnt.shape, F32)

    logits = jnp.dot(h_ref[...], wr_ref[...], preferred_element_type=F32, precision=HIGHEST)
    scores = jax.nn.sigmoid(logits.T[:ne])
    sel = scores + bias_ref[...]

    member = lax.broadcasted_iota(jnp.int32, (per, tm), 0)
    grp_rows = []
    for g in range(N_GROUPS):
        blk = sel[g * per:(g + 1) * per]
        m1 = jnp.max(blk, axis=0, keepdims=True)
        first = jnp.min(jnp.where(blk == m1, member, per), axis=0, keepdims=True)
        m2 = jnp.max(jnp.where(member == first, neg, blk), axis=0, keepdims=True)
        grp_rows.append(m1 + m2)
    gs = jnp.concatenate(grp_rows, axis=0)

    gidx = lax.broadcasted_iota(jnp.int32, (N_GROUPS, tm), 0)
    rank = jnp.zeros((N_GROUPS, tm), jnp.int32)
    for g in range(N_GROUPS):
        row = gs[g:g + 1]
        ahead = jnp.logical_or(row > gs, jnp.logical_and(row == gs, g < gidx))
        rank = rank + ahead.astype(jnp.int32)
    keep = (rank < TOPK_GROUPS).astype(F32)
    keep_e = jnp.concatenate([jnp.broadcast_to(keep[g:g + 1], (per, tm)) for g in range(N_GROUPS)], axis=0)
    cand = jnp.where(keep_e > 0.5, sel, neg)

    eidx = lax.broadcasted_iota(jnp.int32, (ne, tm), 0)
    picks, gates, hots = [], [], []
    chosen = jnp.zeros((ne, tm), F32)
    for _ in range(TOP_K):
        m = jnp.max(cand, axis=0, keepdims=True)
        idx = jnp.min(jnp.where(cand == m, eidx, ne), axis=0, keepdims=True)
        hot = eidx == idx
        picks.append(idx)
        gates.append(jnp.sum(jnp.where(hot, scores, 0.0), axis=0, keepdims=True))
        hots.append(hot)
        chosen = jnp.where(hot, 1.0, chosen)
        cand = jnp.where(hot, neg, cand)
    total = gates[0]
    for gk in gates[1:]:
        total = total + gk

    ti = lax.broadcasted_iota(jnp.int32, (tm, tm), 0)
    tj = lax.broadcasted_iota(jnp.int32, (tm, tm), 1)
    before = jnp.where(ti < tj, 1.0, 0.0).astype(BF16)
    prior = jnp.dot(chosen.astype(BF16), before, preferred_element_type=F32) + cnt[...][:, :1]
    pos = [jnp.sum(jnp.where(hot, prior, 0.0), axis=0, keepdims=True) for hot in hots]

    e_ref[...] = jnp.concatenate(picks, axis=0)
    w_ref[...] = jnp.concatenate([ROUTED_SCALE * gk / total for gk in gates], axis=0)
    pos_ref[...] = jnp.concatenate(pos, axis=0).astype(jnp.int32)
    cnt[...] = cnt[...] + jnp.sum(chosen, axis=1, keepdims=True)
    cnt_ref[...] = cnt[...]


def _router_call(geom, h2, w_router, router_bias):
    tm = geom.c
    wr = jnp.zeros((D_MODEL, 128), F32).at[:, :N_EXPERTS].set(w_router)
    tok = pl.BlockSpec((TOP_K, tm), lambda i: (0, i))
    return pl.pallas_call(
        _router_kernel,
        grid=(geom.nt // tm,),
        in_specs=[pl.BlockSpec((tm, D_MODEL), lambda i: (i, 0)),
                  pl.BlockSpec((D_MODEL, 128), lambda i: (0, 0)),
                  pl.BlockSpec((N_EXPERTS, 1), lambda i: (0, 0))],
        out_specs=[tok, tok, tok, pl.BlockSpec((N_EXPERTS, 128), lambda i: (0, 0))],
        out_shape=[jax.ShapeDtypeStruct((TOP_K, geom.nt), jnp.int32),
                   jax.ShapeDtypeStruct((TOP_K, geom.nt), F32),
                   jax.ShapeDtypeStruct((TOP_K, geom.nt), jnp.int32),
                   jax.ShapeDtypeStruct((N_EXPERTS, 128), F32)],
        scratch_shapes=[pltpu.VMEM((N_EXPERTS, 128), F32)],
        compiler_params=_params(1, 32),
        name="moe_router",
    )(h2, wr, router_bias.reshape(N_EXPERTS, 1))


def _row_copy(src, src_row, dst, dst_row, sem):
    return pltpu.make_async_copy(src.at[pl.ds(src_row, 1)], dst.at[pl.ds(dst_row, 1)], sem)


HALF = D_MODEL // 2


def _pack_bf16_pairs(v):
    lo = pltpu.bitcast(v[:, :HALF].astype(BF16).astype(F32), jnp.uint32)
    hi = pltpu.bitcast(v[:, HALF:].astype(BF16).astype(F32), jnp.uint32)
    return jnp.bitwise_or(jnp.right_shift(lo, jnp.uint32(16)), hi)


def _unpack_bf16_pairs(w):
    lo = pltpu.bitcast(jnp.left_shift(w, jnp.uint32(16)), F32)
    hi = pltpu.bitcast(jnp.bitwise_and(w, jnp.uint32(0xFFFF0000)), F32)
    return lo, hi


def _dispatch_kernel(last_ref, dest_ref, h_ref, xs_out, packed, zblk, sem, zsem):
    tm = h_ref.shape[0]

    @pl.when(pl.program_id(0) == 0)
    def _zero_tail_blocks():
        zblk[...] = jnp.zeros(zblk.shape, zblk.dtype)

        def zero_copy(e):
            return pltpu.make_async_copy(zblk, xs_out.at[pl.ds(last_ref[e] * MOE_BLOCK, MOE_BLOCK)], zsem)

        def start(e, carry):
            zero_copy(e).start()
            return carry

        def wait(e, carry):
            zero_copy(e).wait()
            return carry

        lax.fori_loop(0, N_EXPERTS, start, 0)
        lax.fori_loop(0, N_EXPERTS, wait, 0)

    packed[...] = _pack_bf16_pairs(h_ref[...])

    def issue(tok, carry):
        for k in range(TOP_K):
            _row_copy(packed, tok, xs_out, dest_ref[tok * TOP_K + k], sem).start()
        return carry

    lax.fori_loop(0, tm, issue, 0)

    def drain(tok, carry):
        for k in range(TOP_K):
            _row_copy(packed, 0, xs_out, 0, sem).wait()
        return carry

    lax.fori_loop(0, tm, drain, 0)


def _dispatch_call(geom, last_block, dest_flat, h2, n_blocks):
    tm = geom.c
    grid_spec = pltpu.PrefetchScalarGridSpec(
        num_scalar_prefetch=1,
        grid=(geom.nt // tm,),
        in_specs=[pl.BlockSpec((tm * TOP_K,), lambda i, lb: (i,), memory_space=pltpu.SMEM),
                  pl.BlockSpec((tm, D_MODEL), lambda i, lb: (i, 0))],
        out_specs=pl.BlockSpec(memory_space=pl.ANY),
        scratch_shapes=[pltpu.VMEM((tm, HALF), jnp.uint32), pltpu.VMEM((MOE_BLOCK, HALF), jnp.uint32),
                        pltpu.SemaphoreType.DMA(()), pltpu.SemaphoreType.DMA(())],
    )
    return pl.pallas_call(
        _dispatch_kernel,
        grid_spec=grid_spec,
        out_shape=jax.ShapeDtypeStruct(((n_blocks + 1) * MOE_BLOCK, HALF), jnp.uint32),
        compiler_params=_params(1, 32),
        name="moe_dispatch",
    )(last_block, dest_flat, h2)


def _expert_kernel(be_ref, nu_ref, x_ref, wgu_ref, wd_ref, o_ref):
    del be_ref

    @pl.when(pl.program_id(0) < nu_ref[0])
    def _run():
        lo, hi = _unpack_bf16_pairs(x_ref[...])
        x = jnp.concatenate([lo, hi], axis=1).astype(BF16)
        hgu = jnp.dot(x, wgu_ref[0], preferred_element_type=F32)
        hid = _silu(hgu[:, :D_EXPERT]) * hgu[:, D_EXPERT:]
        o_ref[...] = _pack_bf16_pairs(jnp.dot(hid.astype(BF16), wd_ref[0], preferred_element_type=F32))


def _expert_call(block_e, n_used, xs, w_gu, w_down):
    n_blocks = xs.shape[0] // MOE_BLOCK - 1
    live = lambda i, be, nu: jnp.minimum(i, nu[0] - 1)
    grid_spec = pltpu.PrefetchScalarGridSpec(
        num_scalar_prefetch=2,
        grid=(n_blocks,),
        in_specs=[pl.BlockSpec((MOE_BLOCK, HALF), lambda i, be, nu: (live(i, be, nu), 0)),
                  pl.BlockSpec((1, D_MODEL, 2 * D_EXPERT), lambda i, be, nu: (be[live(i, be, nu)], 0, 0)),
                  pl.BlockSpec((1, D_EXPERT, D_MODEL), lambda i, be, nu: (be[live(i, be, nu)], 0, 0))],
        out_specs=pl.BlockSpec((MOE_BLOCK, HALF), lambda i, be, nu: (live(i, be, nu), 0)),
    )
    return pl.pallas_call(
        _expert_kernel,
        grid_spec=grid_spec,
        out_shape=jax.ShapeDtypeStruct(xs.shape, jnp.uint32),
        compiler_params=_params(1, 32),
        name="moe_experts",
    )(block_e, n_used, xs, w_gu, w_down)


def _combine_kernel(dcur_ref, dnxt_ref, wt_ref, ys_hbm, h_ref, x_ref, g2_ref, wgu_ref, wd_ref,
                    lng_ref, lnb_ref, o_ref, buf_a, buf_b, sem, *, alpha):
    i = pl.program_id(0)
    n = pl.num_programs(0)
    tm = h_ref.shape[0]
    even = i % 2 == 0

    def gather(d_ref, buf, slot):
        def body(tok, carry):
            for k in range(TOP_K):
                pltpu.make_async_copy(ys_hbm.at[pl.ds(d_ref[tok * TOP_K + k], 1)],
                                      buf.at[k, pl.ds(tok, 1)], sem.at[slot]).start()
            return carry

        lax.fori_loop(0, tm, body, 0)

    def drain(buf, slot):
        def body(tok, carry):
            for k in range(TOP_K):
                pltpu.make_async_copy(ys_hbm.at[pl.ds(0, 1)], buf.at[k, pl.ds(0, 1)], sem.at[slot]).wait()
            return carry

        lax.fori_loop(0, tm, body, 0)

    def finish(buf):
        wt = wt_ref[...]
        lo, hi = _unpack_bf16_pairs(buf[0])
        routed_lo, routed_hi = lo * wt[:, 0:1], hi * wt[:, 0:1]
        for k in range(1, TOP_K):
            lo, hi = _unpack_bf16_pairs(buf[k])
            routed_lo = routed_lo + lo * wt[:, k:k + 1]
            routed_hi = routed_hi + hi * wt[:, k:k + 1]
        routed = jnp.concatenate([routed_lo, routed_hi], axis=1)
        hgu = jnp.dot(h_ref[...].astype(BF16), wgu_ref[...], preferred_element_type=F32)
        hid = _silu(hgu[:, :D_SHARED]) * hgu[:, D_SHARED:]
        shared = jnp.dot(hid.astype(BF16), wd_ref[...], preferred_element_type=F32)
        o_ref[...] = _layer_norm(alpha * x_ref[...] + g2_ref[0] * (routed + shared), lng_ref[...], lnb_ref[...])

    @pl.when(i == 0)
    def _first():
        gather(dcur_ref, buf_a, 0)

    @pl.when(jnp.logical_and(even, i + 1 < n))
    def _ahead_b():
        gather(dnxt_ref, buf_b, 1)

    @pl.when(jnp.logical_and(jnp.logical_not(even), i + 1 < n))
    def _ahead_a():
        gather(dnxt_ref, buf_a, 0)

    @pl.when(even)
    def _finish_a():
        drain(buf_a, 0)
        finish(buf_a)

    @pl.when(jnp.logical_not(even))
    def _finish_b():
        drain(buf_b, 1)
        finish(buf_b)


def _combine_call(geom, alpha, dest, w_tok, ys, h2, x1, mods, w_sh_gu, w_sh_down, ln_g, ln_b):
    tm = geom.c
    d = D_MODEL
    n = geom.nt // tm
    tile = pl.BlockSpec((tm, d), lambda i: (i, 0))
    vsp = pl.BlockSpec((1, d), lambda i: (0, 0))
    return pl.pallas_call(
        functools.partial(_combine_kernel, alpha=alpha),
        grid=(n,),
        in_specs=[pl.BlockSpec((tm * TOP_K,), lambda i: (i,), memory_space=pltpu.SMEM),
                  pl.BlockSpec((tm * TOP_K,), lambda i: (jnp.minimum(i + 1, n - 1),), memory_space=pltpu.SMEM),
                  pl.BlockSpec((tm, TOP_K), lambda i: (i, 0)),
                  pl.BlockSpec(memory_space=pl.ANY),
                  tile, tile, _mod_spec(geom, tm, 5),
                  pl.BlockSpec((d, 2 * D_SHARED), lambda i: (0, 0)),
                  pl.BlockSpec((D_SHARED, d), lambda i: (0, 0)),
                  vsp, vsp],
        out_specs=tile,
        out_shape=jax.ShapeDtypeStruct((geom.nt, d), F32),
        scratch_shapes=[pltpu.VMEM((TOP_K, tm, HALF), jnp.uint32), pltpu.VMEM((TOP_K, tm, HALF), jnp.uint32),
                        pltpu.SemaphoreType.DMA((2,))],
        compiler_params=_params(1, 48),
        name="moe_combine_ln2",
    )(dest, dest, w_tok, ys, h2, x1, mods, w_sh_gu, w_sh_down, ln_g.reshape(1, d), ln_b.reshape(1, d))


def _rope_tables(t):
    rows = t // GRID_W
    row = jnp.repeat(jnp.arange(rows, dtype=F32), GRID_W)
    col = jnp.tile(jnp.arange(GRID_W, dtype=F32), rows)
    n_freq = ATT_DH // 4
    inv_freq = ROPE_THETA ** (-jnp.arange(n_freq, dtype=F32) / n_freq)
    ang = jnp.concatenate([row[:, None] * inv_freq, col[:, None] * inv_freq], axis=-1)
    cos, sin = jnp.cos(ang), jnp.sin(ang)
    cos64 = jnp.concatenate([cos, cos], axis=-1)
    sin64 = jnp.concatenate([-sin, sin], axis=-1)
    return cos64, sin64


def kernel(x, c, ctx, c_ctx, w_ada, b_ada, w_in, ret_decay_logit, att_q_norm, att_k_norm, conv_dw, conv_db, conv_ln_g, conv_ln_b, w_ret_o, w_att_o, w_conv_o, w_out, ln1_g, ln1_b, w_router, router_bias, w_exp_gate, w_exp_up, w_exp_down, w_sh_gate, w_sh_up, w_sh_down, ln2_g, ln2_b):
    b, t, d = x.shape
    n_ctx = ctx.shape[1]
    depth = w_ada.shape[0]
    assert d == D_MODEL and w_in.shape[-1] == D_IN
    geom = _Geom(b, t, n_ctx)
    alpha = float((2 * depth) ** 0.25)

    cos64, sin64 = _rope_tables(t)
    cos128 = jnp.concatenate([cos64, cos64], axis=-1)
    sin128 = jnp.concatenate([sin64, sin64], axis=-1)

    n_rows = -(-(b + 1) // 8) * 8
    cvecs = jnp.zeros((n_rows, d), F32).at[:b].set(c).at[b].set(c_ctx)
    mods_all = _mods_call(cvecs, w_ada, b_ada).reshape(depth, n_rows * 6, 1, d)

    n_blocks = -(-(geom.nt * TOP_K + N_EXPERTS * (MOE_BLOCK - 1)) // MOE_BLOCK)

    xt = jnp.concatenate([x.reshape(geom.nl, d), ctx.reshape(geom.nc, d)], axis=0)
    for l in range(depth):
        mods = mods_all[l]
        w_in_l = _permute_columns(w_in[l]).astype(BF16)
        z = _inproj_call(geom, xt, mods, w_in_l)

        log_gamma = jax.nn.log_sigmoid(ret_decay_logit[l].astype(F32))
        ret = _retention_call(geom, z, log_gamma, cos128, sin128)
        att = _attention_call(geom, z, att_q_norm[l], att_k_norm[l], cos128, sin128)
        cv = _conv_call(geom, z, conv_dw[l], conv_db[l], conv_ln_g[l], conv_ln_b[l])
        x1, h2 = _mix_call(geom, alpha, ret, att, cv, z, xt, mods,
                           w_ret_o[l].astype(BF16), w_att_o[l].astype(BF16), w_conv_o[l].astype(BF16),
                           w_out[l].astype(BF16), ln1_g[l], ln1_b[l])

        top_e, gate_w, pos, counts = _router_call(geom, h2, w_router[l], router_bias[l])
        cnt = counts[:, 0].astype(jnp.int32)
        blocks_e = (cnt + MOE_BLOCK - 1) // MOE_BLOCK
        blocks_end = jnp.cumsum(blocks_e)
        start_row = (blocks_end - blocks_e) * MOE_BLOCK
        expert_ids = jnp.arange(N_EXPERTS, dtype=jnp.int32)[None, None, :]
        dest = jnp.sum(jnp.where(top_e[:, :, None] == expert_ids, start_row[None, None, :], 0), axis=-1) + pos
        block_ids = jnp.arange(n_blocks, dtype=jnp.int32)
        block_e = jnp.minimum(jnp.sum((blocks_end[None, :] <= block_ids[:, None]).astype(jnp.int32), axis=1),
                              N_EXPERTS - 1)
        n_used = blocks_end[-1:].astype(jnp.int32)
        last_block = jnp.where(blocks_e > 0, blocks_end - 1, n_blocks).astype(jnp.int32)
        dest_flat = dest.T.reshape(-1)

        xs = _dispatch_call(geom, last_block, dest_flat, h2, n_blocks)
        w_gu = jnp.concatenate([w_exp_gate[l], w_exp_up[l]], axis=-1).astype(BF16)
        ys = _expert_call(block_e, n_used, xs, w_gu, w_exp_down[l].astype(BF16))
        w_sh_gu = jnp.concatenate([w_sh_gate[l], w_sh_up[l]], axis=-1).astype(BF16)
        xt = _combine_call(geom, alpha, dest_flat, gate_w.T, ys, h2, x1, mods, w_sh_gu,
                           w_sh_down[l].astype(BF16), ln2_g[l], ln2_b[l])
    return xt[:geom.nl].reshape(b, t, d)
```

```python
import functools

import jax
import jax.numpy as jnp
from jax import lax
from jax.experimental import pallas as pl
from jax.experimental.pallas import tpu as pltpu

F32 = jnp.float32
BF16 = jnp.bfloat16
HIGHEST = lax.Precision.HIGHEST

D_MODEL = 1024
GRID_W = 64
EPS = 1e-6

RET_HEADS = 8
RET_DK = 64
RET_DV = 128
RET_CHUNK = 128
RET_W = RET_HEADS * RET_DV

ATT_HEADS = 16
ATT_KV_HEADS = 4
ATT_DH = 64
ATT_GROUP = ATT_HEADS // ATT_KV_HEADS
ATT_W = ATT_HEADS * ATT_DH
ROPE_THETA = 10000.0
ATT_KEY_BLOCK = 1024

CONV_CH = 1024
CONV_K = 31
CONV_HALO = 16

N_EXPERTS = 64
TOP_K = 8
N_GROUPS = 8
TOPK_GROUPS = 4
D_EXPERT = 256
D_SHARED = 256
ROUTED_SCALE = 2.5
MOE_BLOCK = 512

_ORIG = dict(rq=0, rk=512, rv=1024, rg=2048, aq=3072, ak=4096, av=4352, cu=4608, gt=6656)
D_IN = 9728
COL_CU = 0
COL_GT = 2048
COL_RG = 5120
COL_RV = 6144
COL_RQ = 7168
COL_RK = 7680
COL_ATT = 8192
ATT_SECTION = ATT_GROUP * ATT_DH + 2 * ATT_DH


def _column_ranges():
    rng = [(_ORIG["cu"], _ORIG["cu"] + 2 * CONV_CH),
           (_ORIG["gt"], _ORIG["gt"] + 3 * D_MODEL),
           (_ORIG["rg"], _ORIG["rg"] + RET_W),
           (_ORIG["rv"], _ORIG["rv"] + RET_W),
           (_ORIG["rq"], _ORIG["rq"] + RET_HEADS * RET_DK),
           (_ORIG["rk"], _ORIG["rk"] + RET_HEADS * RET_DK)]
    for g in range(ATT_KV_HEADS):
        rng.append((_ORIG["aq"] + g * ATT_GROUP * ATT_DH, _ORIG["aq"] + (g + 1) * ATT_GROUP * ATT_DH))
        rng.append((_ORIG["ak"] + g * ATT_DH, _ORIG["ak"] + (g + 1) * ATT_DH))
        rng.append((_ORIG["av"] + g * ATT_DH, _ORIG["av"] + (g + 1) * ATT_DH))
    cols = [c for a, b in rng for c in range(a, b)]
    assert sorted(cols) == list(range(D_IN))
    return rng


def _permute_columns(w):
    return jnp.concatenate([w[:, a:b] for a, b in _column_ranges()], axis=1)


def _params(n_axes, vmem_mib):
    return pltpu.CompilerParams(dimension_semantics=("arbitrary",) * n_axes,
                                vmem_limit_bytes=vmem_mib * 1024 * 1024)


def _silu(v):
    return v * jax.nn.sigmoid(v)


def _layer_norm(v, g, b):
    mu = jnp.mean(v, axis=-1, keepdims=True)
    d = v - mu
    var = jnp.mean(d * d, axis=-1, keepdims=True)
    return d * lax.rsqrt(var + EPS) * g + b


def _mods_kernel(c_ref, w_ref, b_ref, o_ref):
    s = _silu(c_ref[...])
    o_ref[0] = jnp.dot(s, w_ref[0], preferred_element_type=F32, precision=HIGHEST) + b_ref[0]


def _mods_call(cvecs, w_ada, b_ada):
    n_layers = w_ada.shape[0]
    rows, d = cvecs.shape
    return pl.pallas_call(
        _mods_kernel,
        grid=(n_layers, 6),
        in_specs=[pl.BlockSpec((rows, d), lambda l, j: (0, 0)),
                  pl.BlockSpec((1, d, d), lambda l, j: (l, 0, j)),
                  pl.BlockSpec((1, 1, d), lambda l, j: (l, 0, j))],
        out_specs=pl.BlockSpec((1, rows, d), lambda l, j: (l, 0, j)),
        out_shape=jax.ShapeDtypeStruct((n_layers, rows, 6 * d), F32),
        compiler_params=_params(2, 32),
        name="adaln_mods",
    )(cvecs, w_ada, b_ada.reshape(n_layers, 1, 6 * d))


class _Geom:
    def __init__(self, b, t, c):
        assert t % c == 0 and c % RET_CHUNK == 0 and c % CONV_HALO == 0 and t % ATT_KEY_BLOCK == 0
        self.b, self.t, self.c = b, t, c
        self.nl, self.nc = b * t, b * c
        self.nt = self.nl + self.nc
        self.lat_blocks = t // c
        self.nlb = self.nl // c
        self.p = t + c

    def row_block(self, bi, r):
        return jnp.where(r < self.lat_blocks, bi * self.lat_blocks + r, self.nlb + bi)

    def mod_row(self, i, tm):
        return jnp.where(i * tm < self.nl, (i * tm) // self.t, self.b)


def _mod_spec(geom, tm, which, grid_pos=0):
    d = D_MODEL
    if grid_pos == 0:
        return pl.BlockSpec((1, 1, d), lambda i, *_: (geom.mod_row(i, tm) * 6 + which, 0, 0))
    return pl.BlockSpec((1, 1, d), lambda j, i: (geom.mod_row(i, tm) * 6 + which, 0, 0))


def _inproj_kernel(x_ref, sh_ref, sc_ref, w_ref, o_ref):
    h = x_ref[...] * (1.0 + sc_ref[0]) + sh_ref[0]
    o_ref[...] = jnp.dot(h.astype(BF16), w_ref[...], preferred_element_type=F32)


def _inproj_call(geom, x, mods, w_in_bf16):
    tm = 512 if geom.nc % 512 == 0 and geom.t % 512 == 0 else geom.c
    tn = D_IN // 4
    return pl.pallas_call(
        _inproj_kernel,
        grid=(D_IN // tn, geom.nt // tm),
        in_specs=[pl.BlockSpec((tm, D_MODEL), lambda j, i: (i, 0)),
                  _mod_spec(geom, tm, 0, grid_pos=1),
                  _mod_spec(geom, tm, 1, grid_pos=1),
                  pl.BlockSpec((D_MODEL, tn), lambda j, i: (0, j))],
        out_specs=pl.BlockSpec((tm, tn), lambda j, i: (i, j)),
        out_shape=jax.ShapeDtypeStruct((geom.nt, D_IN), F32),
        compiler_params=_params(2, 48),
        name="in_proj",
    )(x, mods, mods, w_in_bf16)


def _rot_half_128(v):
    lane = lax.broadcasted_iota(jnp.int32, v.shape, 1)
    return jnp.where((lane % 64) < 32, pltpu.roll(v, 96, 1), pltpu.roll(v, 32, 1))


def _ret_kernel(lg_ref, ql_ref, qc_ref, kl_ref, kc_ref, vl_ref, vc_ref, g_ref, cos_ref, sin_ref, o_ref,
                qs, kts, yf, yb, st, dm, qwb, kwb, gcs, *, t, c):
    ch = RET_CHUNK
    hp = pl.program_id(1)
    r = pl.program_id(2)
    lat_blocks = t // c
    n_lat, n_ctx = t // ch, c // ch

    @pl.when(r == 0)
    def _scan():
        ri = lax.broadcasted_iota(jnp.int32, (ch, ch), 0).astype(F32)
        ci = lax.broadcasted_iota(jnp.int32, (ch, ch), 1).astype(F32)
        for d in range(2):
            for h in range(2):
                u = 2 * d + h
                lg = lg_ref[d, 2 * hp + h]
                rel = (ri - ci) if d == 0 else (ci - ri)
                dm[u] = jnp.where(rel >= 0.0, jnp.exp(lg * jnp.maximum(rel, 0.0)), 0.0)
                qwb[u] = jnp.exp(lg * ((ri + 1.0) if d == 0 else (float(ch) - ri)))
                kwb[u] = jnp.exp(lg * ((float(ch) - 1.0 - ri) if d == 0 else ri))
                gcs[u] = jnp.exp(jnp.full((RET_DK, RET_DV), lg * float(ch), F32))
                st[u] = jnp.zeros((RET_DK, RET_DV), F32)

        def stage(q, k, seq_rows):
            qs[0, seq_rows, :] = q[:, :RET_DK].astype(BF16)
            qs[1, seq_rows, :] = q[:, RET_DK:].astype(BF16)
            kt = k.T
            kts[0, :, seq_rows] = kt[:RET_DK].astype(BF16)
            kts[1, :, seq_rows] = kt[RET_DK:].astype(BF16)

        kscale = RET_DK ** -0.5
        for cc in range(n_ctx):
            rows = pl.ds(cc * ch, ch)
            stage(qc_ref[rows, :], kc_ref[rows, :] * kscale, rows)

        def stage_lat(cc, carry):
            rows = pl.ds(pl.multiple_of(cc * ch, ch), ch)
            cs, sn = cos_ref[rows, :], sin_ref[rows, :]
            q = ql_ref[rows, :]
            k = kl_ref[rows, :]
            q = q * cs + _rot_half_128(q) * sn
            k = (k * cs + _rot_half_128(k) * sn) * kscale
            stage(q, k, pl.ds(pl.multiple_of(c + cc * ch, ch), ch))
            return carry

        lax.fori_loop(0, n_lat, stage_lat, 0)

        def run_segment(v_ref, seq_off, n):
            def body(i, carry):
                for d, cc in ((0, i), (1, n - 1 - i)):
                    vrows = pl.ds(pl.multiple_of(cc * ch, ch), ch)
                    srows = pl.ds(pl.multiple_of(seq_off + cc * ch, ch), ch)
                    for h in range(2):
                        u = 2 * d + h
                        q = qs[h, srows, :]
                        kt = kts[h, :, srows]
                        v = v_ref[vrows, h * RET_DV:(h + 1) * RET_DV]
                        s = jnp.dot(q, kt, preferred_element_type=F32)
                        y = jnp.dot((s * dm[u]).astype(BF16), v.astype(BF16), preferred_element_type=F32)
                        state = st[u]
                        y = y + jnp.dot(q, state.astype(BF16), preferred_element_type=F32) * qwb[u]
                        dst = yf if d == 0 else yb
                        dst[srows, h * RET_DV:(h + 1) * RET_DV] = y
                        kv = jnp.dot(kt, (v * kwb[u]).astype(BF16), preferred_element_type=F32)
                        st[u] = gcs[u] * state + kv
                return carry

            lax.fori_loop(0, n, body, 0, unroll=2)

        run_segment(vc_ref, 0, n_ctx)
        run_segment(vl_ref, c, n_lat)

    def finish(srows):
        y = yf[srows, :] + yb[srows, :]
        for h in range(2):
            cols = slice(h * RET_DV, (h + 1) * RET_DV)
            yh = y[:, cols]
            mu = jnp.mean(yh, axis=-1, keepdims=True)
            dlt = yh - mu
            var = jnp.mean(dlt * dlt, axis=-1, keepdims=True)
            o_ref[:, cols] = _silu(g_ref[:, cols]) * (dlt * lax.rsqrt(var + EPS))

    @pl.when(r < lat_blocks)
    def _fin_lat():
        finish(pl.ds(pl.multiple_of(c + r * c, c), c))

    @pl.when(r == lat_blocks)
    def _fin_ctx():
        finish(pl.ds(0, c))


def _retention_call(geom, z, log_gamma, cos128, sin128):
    t, c, p = geom.t, geom.c, geom.p
    hpairs = RET_HEADS // 2
    qb, kb = COL_RQ // 128, COL_RK // 128
    vb, gb = COL_RV // 256, COL_RG // 256
    rb = geom.row_block
    in_specs = [
        pl.BlockSpec(memory_space=pltpu.SMEM),
        pl.BlockSpec((t, 128), lambda b, h, r: (b, qb + h)),
        pl.BlockSpec((c, 128), lambda b, h, r: (geom.nlb + b, qb + h)),
        pl.BlockSpec((t, 128), lambda b, h, r: (b, kb + h)),
        pl.BlockSpec((c, 128), lambda b, h, r: (geom.nlb + b, kb + h)),
        pl.BlockSpec((t, 256), lambda b, h, r: (b, vb + h)),
        pl.BlockSpec((c, 256), lambda b, h, r: (geom.nlb + b, vb + h)),
        pl.BlockSpec((c, 256), lambda b, h, r: (rb(b, r), gb + h)),
        pl.BlockSpec((t, 128), lambda b, h, r: (0, 0)),
        pl.BlockSpec((t, 128), lambda b, h, r: (0, 0)),
    ]
    scratch = [
        pltpu.VMEM((2, p, RET_DK), BF16),
        pltpu.VMEM((2, RET_DK, p), BF16),
        pltpu.VMEM((p, 2 * RET_DV), F32),
        pltpu.VMEM((p, 2 * RET_DV), F32),
        pltpu.VMEM((4, RET_DK, RET_DV), F32),
        pltpu.VMEM((4, RET_CHUNK, RET_CHUNK), F32),
        pltpu.VMEM((4, RET_CHUNK, RET_CHUNK), F32),
        pltpu.VMEM((4, RET_CHUNK, RET_CHUNK), F32),
        pltpu.VMEM((4, RET_DK, RET_DV), F32),
    ]
    return pl.pallas_call(
        functools.partial(_ret_kernel, t=t, c=c),
        grid=(geom.b, hpairs, geom.lat_blocks + 1),
        in_specs=in_specs,
        out_specs=pl.BlockSpec((c, 256), lambda b, h, r: (rb(b, r), h)),
        out_shape=jax.ShapeDtypeStruct((geom.nt, RET_W), F32),
        scratch_shapes=scratch,
        compiler_params=_params(3, 56),
        name="retention",
    )(log_gamma, z, z, z, z, z, z, z, cos128, sin128)


def _rms_heads_128(v, g):
    li = lax.broadcasted_iota(jnp.int32, (128, 128), 0) // ATT_DH
    lj = lax.broadcasted_iota(jnp.int32, (128, 128), 1) // ATT_DH
    avg = jnp.where(li == lj, 1.0 / ATT_DH, 0.0).astype(BF16)
    sq = v * v
    hi = sq.astype(BF16)
    lo = (sq - hi.astype(F32)).astype(BF16)
    ms = jnp.dot(hi, avg, preferred_element_type=F32) + jnp.dot(lo, avg, preferred_element_type=F32)
    return v * lax.rsqrt(ms + EPS) * g


def _att_kernel(qa_ref, qb_ref, kvl_ref, kvc_ref, qn_ref, kn_ref, cos_ref, sin_ref, o_ref,
                kts, vs, m_s, acc_s, *, t, c):
    r = pl.program_id(2)
    lat_blocks = t // c
    dh = ATT_DH
    tk = ATT_KEY_BLOCK
    lane = lax.broadcasted_iota(jnp.int32, (c, 2 * dh), 1)

    def stage_tile(kv, dst, cs, sn):
        k = _rms_heads_128(kv, kn_ref[...])
        if cs is not None:
            k = k * cs + _rot_half_128(k) * sn
        kts[:, dst] = k.T[:dh].astype(BF16)
        vs[dst, :] = jnp.where(lane < dh, pltpu.roll(kv, dh, 1), 1.0).astype(BF16)

    @pl.when(r == 0)
    def _stage_kv():
        stage_tile(kvc_ref[...], pl.ds(0, c), None, None)

        def stage(i, carry):
            rows = pl.ds(pl.multiple_of(i * c, c), c)
            stage_tile(kvl_ref[rows, :], pl.ds(pl.multiple_of(c + i * c, c), c), cos_ref[rows, :], sin_ref[rows, :])
            return carry

        lax.fori_loop(0, lat_blocks, stage, 0)

    is_ctx = r == lat_blocks
    rows = pl.ds(pl.multiple_of(jnp.minimum(r, lat_blocks - 1) * c, c), c)
    cs, sn = cos_ref[rows, :], sin_ref[rows, :]
    q_heads = []
    for src in (qa_ref, qb_ref):
        xn = _rms_heads_128(src[...], qn_ref[...])
        xr = jnp.where(is_ctx, xn, xn * cs + _rot_half_128(xn) * sn) * (dh ** -0.5)
        q_heads.append(xr[:, :dh].astype(BF16))
        q_heads.append(pltpu.roll(xr, dh, 1)[:, :dh].astype(BF16))
    q = jnp.concatenate(q_heads, axis=0)

    m_s[...] = jnp.full(m_s.shape, -jnp.inf, F32)
    acc_s[...] = jnp.zeros(acc_s.shape, F32)

    def flash_step(kt, v):
        n = kt.shape[1]
        s = jnp.dot(q, kt, preferred_element_type=F32)
        m_prev = m_s[...]
        m_next = jnp.maximum(m_prev, jnp.max(s, axis=1, keepdims=True))
        prob = jnp.exp(s - jnp.concatenate([m_next] * (n // 128), axis=1))
        acc_s[...] = acc_s[...] * jnp.exp(m_prev - m_next) + jnp.dot(prob.astype(BF16), v, preferred_element_type=F32)
        m_s[...] = m_next

    flash_step(kts[:, 0:c], vs[0:c, :])

    @pl.when(jnp.logical_not(is_ctx))
    def _latent_keys():
        def lat_step(j, carry):
            krows = pl.ds(pl.multiple_of(c + j * tk, 128), tk)
            flash_step(kts[:, krows], vs[krows, :])
            return carry

        lax.fori_loop(0, t // tk, lat_step, 0, unroll=2)

    outs = []
    for h in range(ATT_GROUP):
        acc = acc_s[h * c:(h + 1) * c, :]
        outs.append(acc * pltpu.roll(1.0 / acc, dh, 1))
    for pair in range(ATT_GROUP // 2):
        both = jnp.where(lane < dh, outs[2 * pair], pltpu.roll(outs[2 * pair + 1], dh, 1))
        o_ref[:, pair * 2 * dh:(pair + 1) * 2 * dh] = both


def _attention_call(geom, z, q_norm, k_norm, cos128, sin128):
    t, c, p = geom.t, geom.c, geom.p
    ab = COL_ATT // 128
    sec = ATT_SECTION // 128
    rb = geom.row_block
    in_specs = [
        pl.BlockSpec((c, 128), lambda b, g, r: (rb(b, r), ab + sec * g)),
        pl.BlockSpec((c, 128), lambda b, g, r: (rb(b, r), ab + sec * g + 1)),
        pl.BlockSpec((t, 128), lambda b, g, r: (b, ab + sec * g + 2)),
        pl.BlockSpec((c, 128), lambda b, g, r: (geom.nlb + b, ab + sec * g + 2)),
        pl.BlockSpec((1, 128), lambda b, g, r: (0, 0)),
        pl.BlockSpec((1, 128), lambda b, g, r: (0, 0)),
        pl.BlockSpec((t, 128), lambda b, g, r: (0, 0)),
        pl.BlockSpec((t, 128), lambda b, g, r: (0, 0)),
    ]
    two_heads = lambda v: jnp.tile(v.reshape(1, ATT_DH), (1, 2))
    scratch = [
        pltpu.VMEM((ATT_DH, p), BF16),
        pltpu.VMEM((p, 2 * ATT_DH), BF16),
        pltpu.VMEM((ATT_GROUP * c, 128), F32),
        pltpu.VMEM((ATT_GROUP * c, 2 * ATT_DH), F32),
    ]
    return pl.pallas_call(
        functools.partial(_att_kernel, t=t, c=c),
        grid=(geom.b, ATT_KV_HEADS, geom.lat_blocks + 1),
        in_specs=in_specs,
        out_specs=pl.BlockSpec((c, ATT_GROUP * ATT_DH), lambda b, g, r: (rb(b, r), g)),
        out_shape=jax.ShapeDtypeStruct((geom.nt, ATT_W), F32),
        scratch_shapes=scratch,
        compiler_params=_params(3, 48),
        name="attention",
    )(z, z, z, z, two_heads(q_norm), two_heads(k_norm), cos128, sin128)


def _conv_kernel(a_ref, g_ref, ap_ref, gp_ref, an_ref, gn_ref, w_ref, b_ref, lng_ref, lnb_ref, o_ref,
                 ext, ys, *, t, c):
    r = pl.program_id(1)
    lat_blocks = t // c
    halo = CONV_HALO
    has_prev = jnp.logical_and(r != 0, r != lat_blocks)
    has_next = jnp.logical_and(r != lat_blocks - 1, r != lat_blocks)
    ext[halo:halo + c, :] = a_ref[...] * jax.nn.sigmoid(g_ref[...])
    ext[0:halo, :] = jnp.where(has_prev, ap_ref[...] * jax.nn.sigmoid(gp_ref[...]), 0.0)
    ext[halo + c:, :] = jnp.where(has_next, an_ref[...] * jax.nn.sigmoid(gn_ref[...]), 0.0)

    rt = 64
    first = halo - CONV_K // 2

    def lane_block(cb, carry):
        lanes = pl.ds(pl.multiple_of(cb * 128, 128), 128)
        for ti in range(c // rt):
            acc = jnp.zeros((rt, 128), F32)
            for j in range(CONV_K):
                acc = acc + w_ref[pl.ds(j, 1), lanes] * ext[pl.ds(ti * rt + first + j, rt), lanes]
            ys[pl.ds(ti * rt, rt), lanes] = acc
        return carry

    lax.fori_loop(0, CONV_CH // 128, lane_block, 0)
    y = ys[...] + b_ref[...]
    o_ref[...] = _silu(_layer_norm(y, lng_ref[...], lnb_ref[...]))


def _conv_call(geom, z, conv_dw, conv_db, ln_g, ln_b):
    t, c = geom.t, geom.c
    rb = geom.row_block
    hb = c // CONV_HALO
    last = geom.nt // CONV_HALO - 1
    prev = lambda b, r: jnp.maximum(rb(b, r) * hb - 1, 0)
    nxt = lambda b, r: jnp.minimum((rb(b, r) + 1) * hb, last)
    w = jnp.zeros((32, CONV_CH), F32).at[:CONV_K].set(conv_dw)
    vec = lambda v: v.reshape(1, CONV_CH)
    cst = pl.BlockSpec((1, CONV_CH), lambda b, r: (0, 0))
    in_specs = [
        pl.BlockSpec((c, CONV_CH), lambda b, r: (rb(b, r), 0)),
        pl.BlockSpec((c, CONV_CH), lambda b, r: (rb(b, r), 1)),
        pl.BlockSpec((CONV_HALO, CONV_CH), lambda b, r: (prev(b, r), 0)),
        pl.BlockSpec((CONV_HALO, CONV_CH), lambda b, r: (prev(b, r), 1)),
        pl.BlockSpec((CONV_HALO, CONV_CH), lambda b, r: (nxt(b, r), 0)),
        pl.BlockSpec((CONV_HALO, CONV_CH), lambda b, r: (nxt(b, r), 1)),
        pl.BlockSpec((32, CONV_CH), lambda b, r: (0, 0)),
        cst, cst, cst,
    ]
    return pl.pallas_call(
        functools.partial(_conv_kernel, t=t, c=c),
        grid=(geom.b, geom.lat_blocks + 1),
        in_specs=in_specs,
        out_specs=pl.BlockSpec((c, CONV_CH), lambda b, r: (rb(b, r), 0)),
        out_shape=jax.ShapeDtypeStruct((geom.nt, CONV_CH), F32),
        scratch_shapes=[pltpu.VMEM((c + 2 * CONV_HALO, CONV_CH), F32), pltpu.VMEM((c, CONV_CH), F32)],
        compiler_params=_params(2, 32),
        name="conformer_conv",
    )(z, z, z, z, z, z, w, vec(conv_db), vec(ln_g), vec(ln_b))


def _mix_kernel(ret_ref, att_ref, cv_ref, gr_ref, ga_ref, gc_ref, x_ref, g1_ref, sh2_ref, sc2_ref,
                wr_ref, wa_ref, wc_ref, wo_ref, lng_ref, lnb_ref, x1_ref, h2_ref, *, alpha):
    def proj(v_ref, w_ref):
        return jnp.dot(v_ref[...].astype(BF16), w_ref[...], preferred_element_type=F32)

    merged = (jax.nn.sigmoid(gr_ref[...]) * proj(ret_ref, wr_ref)
              + jax.nn.sigmoid(ga_ref[...]) * proj(att_ref, wa_ref)
              + jax.nn.sigmoid(gc_ref[...]) * proj(cv_ref, wc_ref))
    y = jnp.dot(merged.astype(BF16), wo_ref[...], preferred_element_type=F32)
    x1 = _layer_norm(alpha * x_ref[...] + g1_ref[0] * y, lng_ref[...], lnb_ref[...])
    x1_ref[...] = x1
    h2_ref[...] = x1 * (1.0 + sc2_ref[0]) + sh2_ref[0]


def _mix_call(geom, alpha, ret, att, cv, z, x, mods, w_ret_o, w_att_o, w_conv_o, w_out, ln_g, ln_b):
    tm = geom.c
    d = D_MODEL
    tile = pl.BlockSpec((tm, d), lambda i: (i, 0))
    gate = lambda k: pl.BlockSpec((tm, d), lambda i: (i, COL_GT // d + k))
    wsp = pl.BlockSpec((d, d), lambda i: (0, 0))
    vsp = pl.BlockSpec((1, d), lambda i: (0, 0))
    return pl.pallas_call(
        functools.partial(_mix_kernel, alpha=alpha),
        grid=(geom.nt // tm,),
        in_specs=[tile, tile, tile, gate(0), gate(1), gate(2), tile,
                  _mod_spec(geom, tm, 2), _mod_spec(geom, tm, 3), _mod_spec(geom, tm, 4),
                  wsp, wsp, wsp, wsp, vsp, vsp],
        out_specs=[tile, tile],
        out_shape=[jax.ShapeDtypeStruct((geom.nt, d), F32)] * 2,
        compiler_params=_params(1, 48),
        name="merge_ln1",
    )(ret, att, cv, z, z, z, x, mods, mods, mods, w_ret_o, w_att_o, w_conv_o, w_out,
      ln_g.reshape(1, d), ln_b.reshape(1, d))


def _router_kernel(h_ref, wr_ref, bias_ref, e_ref, w_ref, pos_ref, cnt_ref, hist_ref, cnt):
    i = pl.program_id(0)
    tm = h_ref.shape[0]
    ne, per = N_EXPERTS, N_EXPERTS // N_GROUPS
    neg = -jnp.inf

    @pl.when(i == 0)
    def _init():
        cnt[...] = jnp.zeros(cnt.shape, F32)

    logits = jnp.dot(h_ref[...], wr_ref[...], preferred_element_type=F32, precision=HIGHEST)
    scores = jax.nn.sigmoid(logits.T[:ne])
    sel = scores + bias_ref[...]

    member = lax.broadcasted_iota(jnp.int32, (per, tm), 0)
    grp_rows = []
    for g in range(N_GROUPS):
        blk = sel[g * per:(g + 1) * per]
        m1 = jnp.max(blk, axis=0, keepdims=True)
        first = jnp.min(jnp.where(blk == m1, member, per), axis=0, keepdims=True)
        m2 = jnp.max(jnp.where(member == first, neg, blk), axis=0, keepdims=True)
        grp_rows.append(m1 + m2)
    gs = jnp.concatenate(grp_rows, axis=0)

    gidx = lax.broadcasted_iota(jnp.int32, (N_GROUPS, tm), 0)
    rank = jnp.zeros((N_GROUPS, tm), jnp.int32)
    for g in range(N_GROUPS):
        row = gs[g:g + 1]
        ahead = jnp.logical_or(row > gs, jnp.logical_and(row == gs, g < gidx))
        rank = rank + ahead.astype(jnp.int32)
    keep = (rank < TOPK_GROUPS).astype(F32)
    keep_e = jnp.concatenate([jnp.broadcast_to(keep[g:g + 1], (per, tm)) for g in range(N_GROUPS)], axis=0)
    cand = jnp.where(keep_e > 0.5, sel, neg)

    eidx = lax.broadcasted_iota(jnp.int32, (ne, tm), 0)
    picks, gates, hots = [], [], []
    chosen = jnp.zeros((ne, tm), F32)
    for _ in range(TOP_K):
        m = jnp.max(cand, axis=0, keepdims=True)
        idx = jnp.min(jnp.where(cand == m, eidx, ne), axis=0, keepdims=True)
        hot = eidx == idx
        picks.append(idx)
        gates.append(jnp.sum(jnp.where(hot, scores, 0.0), axis=0, keepdims=True))
        hots.append(hot)
        chosen = jnp.where(hot, 1.0, chosen)
        cand = jnp.where(hot, neg, cand)
    total = gates[0]
    for gk in gates[1:]:
        total = total + gk

    ti = lax.broadcasted_iota(jnp.int32, (tm, tm), 0)
    tj = lax.broadcasted_iota(jnp.int32, (tm, tm), 1)
    before = jnp.where(ti < tj, 1.0, 0.0).astype(BF16)
    prior = jnp.dot(chosen.astype(BF16), before, preferred_element_type=F32) + cnt[...][:, :1]
    pos = [jnp.sum(jnp.where(hot, prior, 0.0), axis=0, keepdims=True) for hot in hots]

    e_ref[...] = jnp.concatenate(picks, axis=0)
    w_ref[...] = jnp.concatenate([ROUTED_SCALE * gk / total for gk in gates], axis=0)
    pos_ref[...] = jnp.concatenate(pos, axis=0).astype(jnp.int32)
    hist_ref[0] = cnt[...]
    cnt[...] = cnt[...] + jnp.sum(chosen, axis=1, keepdims=True)
    cnt_ref[...] = cnt[...]


def _router_call(geom, h2, w_router, router_bias):
    tm = geom.c
    wr = jnp.zeros((D_MODEL, 128), F32).at[:, :N_EXPERTS].set(w_router)
    tok = pl.BlockSpec((TOP_K, tm), lambda i: (0, i))
    return pl.pallas_call(
        _router_kernel,
        grid=(geom.nt // tm,),
        in_specs=[pl.BlockSpec((tm, D_MODEL), lambda i: (i, 0)),
                  pl.BlockSpec((D_MODEL, 128), lambda i: (0, 0)),
                  pl.BlockSpec((N_EXPERTS, 1), lambda i: (0, 0))],
        out_specs=[tok, tok, tok, pl.BlockSpec((N_EXPERTS, 128), lambda i: (0, 0)),
                   pl.BlockSpec((1, N_EXPERTS, 128), lambda i: (i, 0, 0))],
        out_shape=[jax.ShapeDtypeStruct((TOP_K, geom.nt), jnp.int32),
                   jax.ShapeDtypeStruct((TOP_K, geom.nt), F32),
                   jax.ShapeDtypeStruct((TOP_K, geom.nt), jnp.int32),
                   jax.ShapeDtypeStruct((N_EXPERTS, 128), F32),
                   jax.ShapeDtypeStruct((geom.nt // tm, N_EXPERTS, 128), F32)],
        scratch_shapes=[pltpu.VMEM((N_EXPERTS, 128), F32)],
        compiler_params=_params(1, 32),
        name="moe_router",
    )(h2, wr, router_bias.reshape(N_EXPERTS, 1))


HALF = D_MODEL // 2


def _pack_bf16_pairs(v):
    lo = pltpu.bitcast(v[:, :HALF].astype(BF16).astype(F32), jnp.uint32)
    hi = pltpu.bitcast(v[:, HALF:].astype(BF16).astype(F32), jnp.uint32)
    return jnp.bitwise_or(jnp.right_shift(lo, jnp.uint32(16)), hi)


def _unpack_bf16_pairs(w):
    lo = pltpu.bitcast(jnp.left_shift(w, jnp.uint32(16)), F32)
    hi = pltpu.bitcast(jnp.bitwise_and(w, jnp.uint32(0xFFFF0000)), F32)
    return lo, hi


RUN_ALIGN = 8
SORTED_ROWS = 256 * TOP_K + N_EXPERTS * RUN_ALIGN
RUN_BITS = tuple(range(8, 2, -1))


def _for_each_run_piece(n_ref, src_ref, tile, visit):
    def per_expert(e, off):
        n = n_ref[tile * N_EXPERTS + e]
        src = src_ref[tile * N_EXPERTS + e]
        for lb in RUN_BITS:
            done = (n >> (lb + 1)) << (lb + 1)

            @pl.when((n & (1 << lb)) != 0)
            def _piece():
                visit(pl.multiple_of(off + done, RUN_ALIGN), pl.multiple_of(src + done, RUN_ALIGN), 1 << lb)

        return off + n

    lax.fori_loop(0, N_EXPERTS, per_expert, 0)


def _dispatch_kernel(last_ref, n_ref, src_ref, sidx_ref, h_ref, xs_out, packed, zblk, sem, zsem):
    tm = h_ref.shape[0]
    i = pl.program_id(0)

    @pl.when(pl.program_id(0) == 0)
    def _zero_tail_blocks():
        zblk[...] = jnp.zeros(zblk.shape, zblk.dtype)

        def zero_copy(e):
            return pltpu.make_async_copy(zblk, xs_out.at[pl.ds(last_ref[e] * MOE_BLOCK, MOE_BLOCK)], zsem)

        def start(e, carry):
            zero_copy(e).start()
            return carry

        def wait(e, carry):
            zero_copy(e).wait()
            return carry

        lax.fori_loop(0, N_EXPERTS, start, 0)
        lax.fori_loop(0, N_EXPERTS, wait, 0)

    rows = lax.broadcasted_iota(jnp.int32, (SORTED_ROWS, tm), 0)
    pick = jnp.zeros((SORTED_ROWS, tm), F32)
    for k in range(TOP_K):
        pick = jnp.where(rows == sidx_ref[k:k + 1, :], 1.0, pick)
    sorted_rows = jnp.dot(pick.astype(BF16), h_ref[...].astype(BF16), preferred_element_type=F32)
    lo = pltpu.bitcast(sorted_rows[:, :HALF], jnp.uint32)
    hi = pltpu.bitcast(sorted_rows[:, HALF:], jnp.uint32)
    packed[...] = jnp.bitwise_or(jnp.right_shift(lo, jnp.uint32(16)), hi)

    def piece(sorted_row, slot_row, rows_):
        return pltpu.make_async_copy(packed.at[pl.ds(sorted_row, rows_)], xs_out.at[pl.ds(slot_row, rows_)], sem)

    _for_each_run_piece(n_ref, src_ref, i, lambda a, b, r: piece(a, b, r).start())
    _for_each_run_piece(n_ref, src_ref, i, lambda a, b, r: piece(a, b, r).wait())


def _dispatch_call(geom, last_block, run_rows, run_slot, sidx, h2, n_blocks):
    tm = geom.c
    assert tm * TOP_K + N_EXPERTS * RUN_ALIGN == SORTED_ROWS
    grid_spec = pltpu.PrefetchScalarGridSpec(
        num_scalar_prefetch=3,
        grid=(geom.nt // tm,),
        in_specs=[pl.BlockSpec((TOP_K, tm), lambda i, *_: (0, i)),
                  pl.BlockSpec((tm, D_MODEL), lambda i, *_: (i, 0))],
        out_specs=pl.BlockSpec(memory_space=pl.ANY),
        scratch_shapes=[pltpu.VMEM((SORTED_ROWS, HALF), jnp.uint32), pltpu.VMEM((MOE_BLOCK, HALF), jnp.uint32),
                        pltpu.SemaphoreType.DMA(()), pltpu.SemaphoreType.DMA(())],
    )
    return pl.pallas_call(
        _dispatch_kernel,
        grid_spec=grid_spec,
        out_shape=jax.ShapeDtypeStruct(((n_blocks + 1) * MOE_BLOCK, HALF), jnp.uint32),
        compiler_params=_params(1, 48),
        name="moe_dispatch",
    )(last_block, run_rows, run_slot, sidx, h2)


def _expert_kernel(be_ref, nu_ref, x_ref, wgu_ref, wd_ref, o_ref):
    del be_ref

    @pl.when(pl.program_id(0) < nu_ref[0])
    def _run():
        lo, hi = _unpack_bf16_pairs(x_ref[...])
        x = jnp.concatenate([lo, hi], axis=1).astype(BF16)
        hgu = jnp.dot(x, wgu_ref[0], preferred_element_type=F32)
        hid = _silu(hgu[:, :D_EXPERT]) * hgu[:, D_EXPERT:]
        o_ref[...] = _pack_bf16_pairs(jnp.dot(hid.astype(BF16), wd_ref[0], preferred_element_type=F32))


def _expert_call(block_e, n_used, xs, w_gu, w_down):
    n_blocks = xs.shape[0] // MOE_BLOCK - 1
    live = lambda i, be, nu: jnp.minimum(i, nu[0] - 1)
    grid_spec = pltpu.PrefetchScalarGridSpec(
        num_scalar_prefetch=2,
        grid=(n_blocks,),
        in_specs=[pl.BlockSpec((MOE_BLOCK, HALF), lambda i, be, nu: (live(i, be, nu), 0)),
                  pl.BlockSpec((1, D_MODEL, 2 * D_EXPERT), lambda i, be, nu: (be[live(i, be, nu)], 0, 0)),
                  pl.BlockSpec((1, D_EXPERT, D_MODEL), lambda i, be, nu: (be[live(i, be, nu)], 0, 0))],
        out_specs=pl.BlockSpec((MOE_BLOCK, HALF), lambda i, be, nu: (live(i, be, nu), 0)),
    )
    return pl.pallas_call(
        _expert_kernel,
        grid_spec=grid_spec,
        out_shape=jax.ShapeDtypeStruct(xs.shape, jnp.uint32),
        compiler_params=_params(1, 32),
        name="moe_experts",
    )(block_e, n_used, xs, w_gu, w_down)


def _combine_kernel(n_ref, src_ref, sidx_ref, wt_ref, ys_hbm, h_ref, x_ref, g2_ref, wgu_ref, wd_ref,
                    lng_ref, lnb_ref, o_ref, buf_a, buf_b, sem, *, alpha):
    i = pl.program_id(0)
    n = pl.num_programs(0)
    tm = h_ref.shape[0]
    even = i % 2 == 0

    def piece(buf, slot, sorted_row, slot_row, rows_):
        return pltpu.make_async_copy(ys_hbm.at[pl.ds(slot_row, rows_)], buf.at[pl.ds(sorted_row, rows_)], sem.at[slot])

    def gather(tile, buf, slot):
        _for_each_run_piece(n_ref, src_ref, tile, lambda a, b, r: piece(buf, slot, a, b, r).start())

    def drain(tile, buf, slot):
        _for_each_run_piece(n_ref, src_ref, tile, lambda a, b, r: piece(buf, slot, a, b, r).wait())

    def finish(buf):
        wt = wt_ref[...]
        cols = lax.broadcasted_iota(jnp.int32, (tm, SORTED_ROWS), 1)
        mix = jnp.zeros((tm, SORTED_ROWS), F32)
        for k in range(TOP_K):
            mix = jnp.where(cols == sidx_ref[:, k:k + 1], wt[:, k:k + 1], mix)
        mix = mix.astype(BF16)
        lo, hi = _unpack_bf16_pairs(buf[...])
        routed = jnp.concatenate([jnp.dot(mix, lo.astype(BF16), preferred_element_type=F32),
                                  jnp.dot(mix, hi.astype(BF16), preferred_element_type=F32)], axis=1)
        hgu = jnp.dot(h_ref[...].astype(BF16), wgu_ref[...], preferred_element_type=F32)
        hid = _silu(hgu[:, :D_SHARED]) * hgu[:, D_SHARED:]
        shared = jnp.dot(hid.astype(BF16), wd_ref[...], preferred_element_type=F32)
        o_ref[...] = _layer_norm(alpha * x_ref[...] + g2_ref[0] * (routed + shared), lng_ref[...], lnb_ref[...])

    @pl.when(i == 0)
    def _first():
        buf_a[...] = jnp.zeros(buf_a.shape, buf_a.dtype)
        buf_b[...] = jnp.zeros(buf_b.shape, buf_b.dtype)
        gather(i, buf_a, 0)

    @pl.when(jnp.logical_and(even, i + 1 < n))
    def _ahead_b():
        gather(i + 1, buf_b, 1)

    @pl.when(jnp.logical_and(jnp.logical_not(even), i + 1 < n))
    def _ahead_a():
        gather(i + 1, buf_a, 0)

    @pl.when(even)
    def _finish_a():
        drain(i, buf_a, 0)
        finish(buf_a)

    @pl.when(jnp.logical_not(even))
    def _finish_b():
        drain(i, buf_b, 1)
        finish(buf_b)


def _combine_call(geom, alpha, run_rows, run_slot, sidx_tok, w_tok, ys, h2, x1, mods, w_sh_gu, w_sh_down, ln_g, ln_b):
    tm = geom.c
    d = D_MODEL
    n = geom.nt // tm
    tile = pl.BlockSpec((tm, d), lambda i, *_: (i, 0))
    vsp = pl.BlockSpec((1, d), lambda i, *_: (0, 0))
    per_tok = pl.BlockSpec((tm, TOP_K), lambda i, *_: (i, 0))
    grid_spec = pltpu.PrefetchScalarGridSpec(
        num_scalar_prefetch=2,
        grid=(n,),
        in_specs=[per_tok, per_tok,
                  pl.BlockSpec(memory_space=pl.ANY),
                  tile, tile, _mod_spec(geom, tm, 5),
                  pl.BlockSpec((d, 2 * D_SHARED), lambda i, *_: (0, 0)),
                  pl.BlockSpec((D_SHARED, d), lambda i, *_: (0, 0)),
                  vsp, vsp],
        out_specs=tile,
        scratch_shapes=[pltpu.VMEM((SORTED_ROWS, HALF), jnp.uint32), pltpu.VMEM((SORTED_ROWS, HALF), jnp.uint32),
                        pltpu.SemaphoreType.DMA((2,))],
    )
    return pl.pallas_call(
        functools.partial(_combine_kernel, alpha=alpha),
        grid_spec=grid_spec,
        out_shape=jax.ShapeDtypeStruct((geom.nt, d), F32),
        compiler_params=_params(1, 56),
        name="moe_combine_ln2",
    )(run_rows, run_slot, sidx_tok, w_tok, ys, h2, x1, mods, w_sh_gu, w_sh_down,
      ln_g.reshape(1, d), ln_b.reshape(1, d))


def _rope_tables(t):
    rows = t // GRID_W
    row = jnp.repeat(jnp.arange(rows, dtype=F32), GRID_W)
    col = jnp.tile(jnp.arange(GRID_W, dtype=F32), rows)
    n_freq = ATT_DH // 4
    inv_freq = ROPE_THETA ** (-jnp.arange(n_freq, dtype=F32) / n_freq)
    ang = jnp.concatenate([row[:, None] * inv_freq, col[:, None] * inv_freq], axis=-1)
    cos, sin = jnp.cos(ang), jnp.sin(ang)
    cos64 = jnp.concatenate([cos, cos], axis=-1)
    sin64 = jnp.concatenate([-sin, sin], axis=-1)
    return cos64, sin64


def kernel(x, c, ctx, c_ctx, w_ada, b_ada, w_in, ret_decay_logit, att_q_norm, att_k_norm, conv_dw, conv_db, conv_ln_g, conv_ln_b, w_ret_o, w_att_o, w_conv_o, w_out, ln1_g, ln1_b, w_router, router_bias, w_exp_gate, w_exp_up, w_exp_down, w_sh_gate, w_sh_up, w_sh_down, ln2_g, ln2_b):
    b, t, d = x.shape
    n_ctx = ctx.shape[1]
    depth = w_ada.shape[0]
    assert d == D_MODEL and w_in.shape[-1] == D_IN
    geom = _Geom(b, t, n_ctx)
    alpha = float((2 * depth) ** 0.25)

    cos64, sin64 = _rope_tables(t)
    cos128 = jnp.concatenate([cos64, cos64], axis=-1)
    sin128 = jnp.concatenate([sin64, sin64], axis=-1)

    n_rows = -(-(b + 1) // 8) * 8
    cvecs = jnp.zeros((n_rows, d), F32).at[:b].set(c).at[b].set(c_ctx)
    mods_all = _mods_call(cvecs, w_ada, b_ada).reshape(depth, n_rows * 6, 1, d)

    n_tiles = geom.nt // geom.c
    n_blocks = -(-(geom.nt * TOP_K + n_tiles * N_EXPERTS * (RUN_ALIGN - 1)) // MOE_BLOCK) + N_EXPERTS

    xt = jnp.concatenate([x.reshape(geom.nl, d), ctx.reshape(geom.nc, d)], axis=0)
    for l in range(depth):
        mods = mods_all[l]
        w_in_l = _permute_columns(w_in[l]).astype(BF16)
        z = _inproj_call(geom, xt, mods, w_in_l)

        log_gamma = jax.nn.log_sigmoid(ret_decay_logit[l].astype(F32))
        ret = _retention_call(geom, z, log_gamma, cos128, sin128)
        att = _attention_call(geom, z, att_q_norm[l], att_k_norm[l], cos128, sin128)
        cv = _conv_call(geom, z, conv_dw[l], conv_db[l], conv_ln_g[l], conv_ln_b[l])
        x1, h2 = _mix_call(geom, alpha, ret, att, cv, z, xt, mods,
                           w_ret_o[l].astype(BF16), w_att_o[l].astype(BF16), w_conv_o[l].astype(BF16),
                           w_out[l].astype(BF16), ln1_g[l], ln1_b[l])

        top_e, gate_w, pos, counts, cnt_hist = _router_call(geom, h2, w_router[l], router_bias[l])
        before = cnt_hist[:, :, 0].astype(jnp.int32)
        total = counts[:, 0].astype(jnp.int32)
        tile_n = jnp.concatenate([before[1:], total[None, :]], axis=0) - before
        run_rows = (tile_n + RUN_ALIGN - 1) // RUN_ALIGN * RUN_ALIGN
        run_before = jnp.cumsum(run_rows, axis=0) - run_rows
        blocks_e = (jnp.sum(run_rows, axis=0) + MOE_BLOCK - 1) // MOE_BLOCK
        blocks_end = jnp.cumsum(blocks_e)
        start_row = (blocks_end - blocks_e) * MOE_BLOCK
        run_slot = start_row[None, :] + run_before
        run_sorted = jnp.cumsum(run_rows, axis=1) - run_rows
        onehot = top_e[:, :, None] == jnp.arange(N_EXPERTS, dtype=jnp.int32)[None, None, :]
        per_token = lambda table: jnp.sum(jnp.where(onehot, jnp.repeat(table, geom.c, axis=0)[None], 0), axis=-1)
        sidx = per_token(run_sorted) + pos - per_token(before)
        block_ids = jnp.arange(n_blocks, dtype=jnp.int32)
        block_e = jnp.minimum(jnp.sum((blocks_end[None, :] <= block_ids[:, None]).astype(jnp.int32), axis=1),
                              N_EXPERTS - 1)
        n_used = blocks_end[-1:].astype(jnp.int32)
        last_block = jnp.where(blocks_e > 0, blocks_end - 1, n_blocks).astype(jnp.int32)
        run_rows_flat, run_slot_flat = run_rows.reshape(-1), run_slot.reshape(-1).astype(jnp.int32)

        xs = _dispatch_call(geom, last_block, run_rows_flat, run_slot_flat, sidx, h2, n_blocks)
        w_gu = jnp.concatenate([w_exp_gate[l], w_exp_up[l]], axis=-1).astype(BF16)
        ys = _expert_call(block_e, n_used, xs, w_gu, w_exp_down[l].astype(BF16))
        w_sh_gu = jnp.concatenate([w_sh_gate[l], w_sh_up[l]], axis=-1).astype(BF16)
        xt = _combine_call(geom, alpha, run_rows_flat, run_slot_flat, sidx.T, gate_w.T, ys, h2, x1, mods, w_sh_gu,
                           w_sh_down[l].astype(BF16), ln2_g[l], ln2_b[l])
    return xt[:geom.nl].reshape(b, t, d)
```

```python
import functools

import jax
import jax.numpy as jnp
from jax import lax
from jax.experimental import pallas as pl
from jax.experimental.pallas import tpu as pltpu

F32 = jnp.float32
BF16 = jnp.bfloat16
HIGHEST = lax.Precision.HIGHEST

D_MODEL = 1024
GRID_W = 64
EPS = 1e-6

RET_HEADS = 8
RET_DK = 64
RET_DV = 128
RET_CHUNK = 128
RET_W = RET_HEADS * RET_DV

ATT_HEADS = 16
ATT_KV_HEADS = 4
ATT_DH = 64
ATT_GROUP = ATT_HEADS // ATT_KV_HEADS
ATT_W = ATT_HEADS * ATT_DH
ROPE_THETA = 10000.0
ATT_KEY_BLOCK = 1024

CONV_CH = 1024
CONV_K = 31
CONV_HALO = 16

N_EXPERTS = 64
TOP_K = 8
N_GROUPS = 8
TOPK_GROUPS = 4
D_EXPERT = 256
D_SHARED = 256
ROUTED_SCALE = 2.5
MOE_BLOCK = 512

_ORIG = dict(rq=0, rk=512, rv=1024, rg=2048, aq=3072, ak=4096, av=4352, cu=4608, gt=6656)
D_IN = 9728
COL_CU = 0
COL_GT = 2048
COL_RG = 5120
COL_RV = 6144
COL_RQ = 7168
COL_RK = 7680
COL_ATT = 8192
ATT_SECTION = ATT_GROUP * ATT_DH + 2 * ATT_DH


def _column_ranges():
    rng = [(_ORIG["cu"], _ORIG["cu"] + 2 * CONV_CH),
           (_ORIG["gt"], _ORIG["gt"] + 3 * D_MODEL),
           (_ORIG["rg"], _ORIG["rg"] + RET_W),
           (_ORIG["rv"], _ORIG["rv"] + RET_W),
           (_ORIG["rq"], _ORIG["rq"] + RET_HEADS * RET_DK),
           (_ORIG["rk"], _ORIG["rk"] + RET_HEADS * RET_DK)]
    for g in range(ATT_KV_HEADS):
        rng.append((_ORIG["aq"] + g * ATT_GROUP * ATT_DH, _ORIG["aq"] + (g + 1) * ATT_GROUP * ATT_DH))
        rng.append((_ORIG["ak"] + g * ATT_DH, _ORIG["ak"] + (g + 1) * ATT_DH))
        rng.append((_ORIG["av"] + g * ATT_DH, _ORIG["av"] + (g + 1) * ATT_DH))
    cols = [c for a, b in rng for c in range(a, b)]
    assert sorted(cols) == list(range(D_IN))
    return rng


def _permute_columns(w):
    return jnp.concatenate([w[:, a:b] for a, b in _column_ranges()], axis=1)


def _params(n_axes, vmem_mib):
    return pltpu.CompilerParams(dimension_semantics=("arbitrary",) * n_axes,
                                vmem_limit_bytes=vmem_mib * 1024 * 1024)


def _silu(v):
    return v * jax.nn.sigmoid(v)


def _layer_norm(v, g, b):
    mu = jnp.mean(v, axis=-1, keepdims=True)
    d = v - mu
    var = jnp.mean(d * d, axis=-1, keepdims=True)
    return d * lax.rsqrt(var + EPS) * g + b


def _mods_kernel(c_ref, w_ref, b_ref, o_ref):
    s = _silu(c_ref[...])
    o_ref[0] = jnp.dot(s, w_ref[0], preferred_element_type=F32, precision=HIGHEST) + b_ref[0]


def _mods_call(cvecs, w_ada, b_ada):
    n_layers = w_ada.shape[0]
    rows, d = cvecs.shape
    return pl.pallas_call(
        _mods_kernel,
        grid=(n_layers, 6),
        in_specs=[pl.BlockSpec((rows, d), lambda l, j: (0, 0)),
                  pl.BlockSpec((1, d, d), lambda l, j: (l, 0, j)),
                  pl.BlockSpec((1, 1, d), lambda l, j: (l, 0, j))],
        out_specs=pl.BlockSpec((1, rows, d), lambda l, j: (l, 0, j)),
        out_shape=jax.ShapeDtypeStruct((n_layers, rows, 6 * d), F32),
        compiler_params=_params(2, 32),
        name="adaln_mods",
    )(cvecs, w_ada, b_ada.reshape(n_layers, 1, 6 * d))


class _Geom:
    def __init__(self, b, t, c):
        assert t % c == 0 and c % RET_CHUNK == 0 and c % CONV_HALO == 0 and t % ATT_KEY_BLOCK == 0
        self.b, self.t, self.c = b, t, c
        self.nl, self.nc = b * t, b * c
        self.nt = self.nl + self.nc
        self.lat_blocks = t // c
        self.nlb = self.nl // c
        self.p = t + c

    def row_block(self, bi, r):
        return jnp.where(r < self.lat_blocks, bi * self.lat_blocks + r, self.nlb + bi)

    def mod_row(self, i, tm):
        return jnp.where(i * tm < self.nl, (i * tm) // self.t, self.b)


def _mod_spec(geom, tm, which, grid_pos=0):
    d = D_MODEL
    if grid_pos == 0:
        return pl.BlockSpec((1, 1, d), lambda i, *_: (geom.mod_row(i, tm) * 6 + which, 0, 0))
    return pl.BlockSpec((1, 1, d), lambda j, i: (geom.mod_row(i, tm) * 6 + which, 0, 0))


def _inproj_kernel(x_ref, sh_ref, sc_ref, w_ref, o_ref):
    h = x_ref[...] * (1.0 + sc_ref[0]) + sh_ref[0]
    o_ref[...] = jnp.dot(h.astype(BF16), w_ref[...], preferred_element_type=F32)


def _inproj_call(geom, x, mods, w_in_bf16):
    tm = 512 if geom.nc % 512 == 0 and geom.t % 512 == 0 else geom.c
    tn = D_IN // 4
    return pl.pallas_call(
        _inproj_kernel,
        grid=(D_IN // tn, geom.nt // tm),
        in_specs=[pl.BlockSpec((tm, D_MODEL), lambda j, i: (i, 0)),
                  _mod_spec(geom, tm, 0, grid_pos=1),
                  _mod_spec(geom, tm, 1, grid_pos=1),
                  pl.BlockSpec((D_MODEL, tn), lambda j, i: (0, j))],
        out_specs=pl.BlockSpec((tm, tn), lambda j, i: (i, j)),
        out_shape=jax.ShapeDtypeStruct((geom.nt, D_IN), F32),
        compiler_params=_params(2, 48),
        name="in_proj",
    )(x, mods, mods, w_in_bf16)


def _rot_half_128(v):
    lane = lax.broadcasted_iota(jnp.int32, v.shape, 1)
    return jnp.where((lane % 64) < 32, pltpu.roll(v, 96, 1), pltpu.roll(v, 32, 1))


def _ret_kernel(lg_ref, ql_ref, qc_ref, kl_ref, kc_ref, vl_ref, vc_ref, g_ref, cos_ref, sin_ref, o_ref,
                qs, kts, yf, yb, st, dm, qwb, kwb, gcs, *, t, c):
    ch = RET_CHUNK
    hp = pl.program_id(1)
    r = pl.program_id(2)
    lat_blocks = t // c
    n_lat, n_ctx = t // ch, c // ch

    @pl.when(r == 0)
    def _scan():
        ri = lax.broadcasted_iota(jnp.int32, (ch, ch), 0).astype(F32)
        ci = lax.broadcasted_iota(jnp.int32, (ch, ch), 1).astype(F32)
        for d in range(2):
            for h in range(2):
                u = 2 * d + h
                lg = lg_ref[d, 2 * hp + h]
                rel = (ri - ci) if d == 0 else (ci - ri)
                dm[u] = jnp.where(rel >= 0.0, jnp.exp(lg * jnp.maximum(rel, 0.0)), 0.0)
                qwb[u] = jnp.exp(lg * ((ri + 1.0) if d == 0 else (float(ch) - ri)))
                kwb[u] = jnp.exp(lg * ((float(ch) - 1.0 - ri) if d == 0 else ri))
                gcs[u] = jnp.exp(jnp.full((RET_DK, RET_DV), lg * float(ch), F32))
                st[u] = jnp.zeros((RET_DK, RET_DV), F32)

        def stage(q, k, seq_rows):
            qs[0, seq_rows, :] = q[:, :RET_DK].astype(BF16)
            qs[1, seq_rows, :] = q[:, RET_DK:].astype(BF16)
            kt = k.T
            kts[0, :, seq_rows] = kt[:RET_DK].astype(BF16)
            kts[1, :, seq_rows] = kt[RET_DK:].astype(BF16)

        kscale = RET_DK ** -0.5
        for cc in range(n_ctx):
            rows = pl.ds(cc * ch, ch)
            stage(qc_ref[rows, :], kc_ref[rows, :] * kscale, rows)

        def stage_lat(cc, carry):
            rows = pl.ds(pl.multiple_of(cc * ch, ch), ch)
            cs, sn = cos_ref[rows, :], sin_ref[rows, :]
            q = ql_ref[rows, :]
            k = kl_ref[rows, :]
            q = q * cs + _rot_half_128(q) * sn
            k = (k * cs + _rot_half_128(k) * sn) * kscale
            stage(q, k, pl.ds(pl.multiple_of(c + cc * ch, ch), ch))
            return carry

        lax.fori_loop(0, n_lat, stage_lat, 0)

        def run_segment(v_ref, seq_off, n):
            def body(i, carry):
                for d, cc in ((0, i), (1, n - 1 - i)):
                    vrows = pl.ds(pl.multiple_of(cc * ch, ch), ch)
                    srows = pl.ds(pl.multiple_of(seq_off + cc * ch, ch), ch)
                    for h in range(2):
                        u = 2 * d + h
                        q = qs[h, srows, :]
                        kt = kts[h, :, srows]
                        v = v_ref[vrows, h * RET_DV:(h + 1) * RET_DV]
                        s = jnp.dot(q, kt, preferred_element_type=F32)
                        y = jnp.dot((s * dm[u]).astype(BF16), v.astype(BF16), preferred_element_type=F32)
                        state = st[u]
                        y = y + jnp.dot(q, state.astype(BF16), preferred_element_type=F32) * qwb[u]
                        dst = yf if d == 0 else yb
                        dst[srows, h * RET_DV:(h + 1) * RET_DV] = y
                        kv = jnp.dot(kt, (v * kwb[u]).astype(BF16), preferred_element_type=F32)
                        st[u] = gcs[u] * state + kv
                return carry

            lax.fori_loop(0, n, body, 0, unroll=2)

        run_segment(vc_ref, 0, n_ctx)
        run_segment(vl_ref, c, n_lat)

    def finish(srows):
        y = yf[srows, :] + yb[srows, :]
        for h in range(2):
            cols = slice(h * RET_DV, (h + 1) * RET_DV)
            yh = y[:, cols]
            mu = jnp.mean(yh, axis=-1, keepdims=True)
            dlt = yh - mu
            var = jnp.mean(dlt * dlt, axis=-1, keepdims=True)
            o_ref[:, cols] = _silu(g_ref[:, cols]) * (dlt * lax.rsqrt(var + EPS))

    @pl.when(r < lat_blocks)
    def _fin_lat():
        finish(pl.ds(pl.multiple_of(c + r * c, c), c))

    @pl.when(r == lat_blocks)
    def _fin_ctx():
        finish(pl.ds(0, c))


def _retention_call(geom, z, log_gamma, cos128, sin128):
    t, c, p = geom.t, geom.c, geom.p
    hpairs = RET_HEADS // 2
    qb, kb = COL_RQ // 128, COL_RK // 128
    vb, gb = COL_RV // 256, COL_RG // 256
    rb = geom.row_block
    in_specs = [
        pl.BlockSpec(memory_space=pltpu.SMEM),
        pl.BlockSpec((t, 128), lambda b, h, r: (b, qb + h)),
        pl.BlockSpec((c, 128), lambda b, h, r: (geom.nlb + b, qb + h)),
        pl.BlockSpec((t, 128), lambda b, h, r: (b, kb + h)),
        pl.BlockSpec((c, 128), lambda b, h, r: (geom.nlb + b, kb + h)),
        pl.BlockSpec((t, 256), lambda b, h, r: (b, vb + h)),
        pl.BlockSpec((c, 256), lambda b, h, r: (geom.nlb + b, vb + h)),
        pl.BlockSpec((c, 256), lambda b, h, r: (rb(b, r), gb + h)),
        pl.BlockSpec((t, 128), lambda b, h, r: (0, 0)),
        pl.BlockSpec((t, 128), lambda b, h, r: (0, 0)),
    ]
    scratch = [
        pltpu.VMEM((2, p, RET_DK), BF16),
        pltpu.VMEM((2, RET_DK, p), BF16),
        pltpu.VMEM((p, 2 * RET_DV), F32),
        pltpu.VMEM((p, 2 * RET_DV), F32),
        pltpu.VMEM((4, RET_DK, RET_DV), F32),
        pltpu.VMEM((4, RET_CHUNK, RET_CHUNK), F32),
        pltpu.VMEM((4, RET_CHUNK, RET_CHUNK), F32),
        pltpu.VMEM((4, RET_CHUNK, RET_CHUNK), F32),
        pltpu.VMEM((4, RET_DK, RET_DV), F32),
    ]
    return pl.pallas_call(
        functools.partial(_ret_kernel, t=t, c=c),
        grid=(geom.b, hpairs, geom.lat_blocks + 1),
        in_specs=in_specs,
        out_specs=pl.BlockSpec((c, 256), lambda b, h, r: (rb(b, r), h)),
        out_shape=jax.ShapeDtypeStruct((geom.nt, RET_W), F32),
        scratch_shapes=scratch,
        compiler_params=_params(3, 56),
        name="retention",
    )(log_gamma, z, z, z, z, z, z, z, cos128, sin128)


def _rms_heads_128(v, g):
    li = lax.broadcasted_iota(jnp.int32, (128, 128), 0) // ATT_DH
    lj = lax.broadcasted_iota(jnp.int32, (128, 128), 1) // ATT_DH
    avg = jnp.where(li == lj, 1.0 / ATT_DH, 0.0).astype(BF16)
    sq = v * v
    hi = sq.astype(BF16)
    lo = (sq - hi.astype(F32)).astype(BF16)
    ms = jnp.dot(hi, avg, preferred_element_type=F32) + jnp.dot(lo, avg, preferred_element_type=F32)
    return v * lax.rsqrt(ms + EPS) * g


def _att_kernel(qa_ref, qb_ref, kvl_ref, kvc_ref, qn_ref, kn_ref, cos_ref, sin_ref, o_ref,
                kts, vs, m_s, acc_s, *, t, c):
    r = pl.program_id(2)
    lat_blocks = t // c
    dh = ATT_DH
    tk = ATT_KEY_BLOCK
    lane = lax.broadcasted_iota(jnp.int32, (c, 2 * dh), 1)

    def stage_tile(kv, dst, cs, sn):
        k = _rms_heads_128(kv, kn_ref[...])
        if cs is not None:
            k = k * cs + _rot_half_128(k) * sn
        kts[:, dst] = k.T[:dh].astype(BF16)
        vs[dst, :] = jnp.where(lane < dh, pltpu.roll(kv, dh, 1), 1.0).astype(BF16)

    @pl.when(r == 0)
    def _stage_kv():
        stage_tile(kvc_ref[...], pl.ds(0, c), None, None)

        def stage(i, carry):
            rows = pl.ds(pl.multiple_of(i * c, c), c)
            stage_tile(kvl_ref[rows, :], pl.ds(pl.multiple_of(c + i * c, c), c), cos_ref[rows, :], sin_ref[rows, :])
            return carry

        lax.fori_loop(0, lat_blocks, stage, 0)

    is_ctx = r == lat_blocks
    rows = pl.ds(pl.multiple_of(jnp.minimum(r, lat_blocks - 1) * c, c), c)
    cs, sn = cos_ref[rows, :], sin_ref[rows, :]
    q_heads = []
    for src in (qa_ref, qb_ref):
        xn = _rms_heads_128(src[...], qn_ref[...])
        xr = jnp.where(is_ctx, xn, xn * cs + _rot_half_128(xn) * sn) * (dh ** -0.5)
        q_heads.append(xr[:, :dh].astype(BF16))
        q_heads.append(pltpu.roll(xr, dh, 1)[:, :dh].astype(BF16))
    q = jnp.concatenate(q_heads, axis=0)

    m_s[...] = jnp.full(m_s.shape, -jnp.inf, F32)
    acc_s[...] = jnp.zeros(acc_s.shape, F32)

    def flash_step(kt, v):
        n = kt.shape[1]
        s = jnp.dot(q, kt, preferred_element_type=F32)
        m_prev = m_s[...]
        m_next = jnp.maximum(m_prev, jnp.max(s, axis=1, keepdims=True))
        prob = jnp.exp(s - jnp.concatenate([m_next] * (n // 128), axis=1))
        acc_s[...] = acc_s[...] * jnp.exp(m_prev - m_next) + jnp.dot(prob.astype(BF16), v, preferred_element_type=F32)
        m_s[...] = m_next

    flash_step(kts[:, 0:c], vs[0:c, :])

    @pl.when(jnp.logical_not(is_ctx))
    def _latent_keys():
        def lat_step(j, carry):
            krows = pl.ds(pl.multiple_of(c + j * tk, 128), tk)
            flash_step(kts[:, krows], vs[krows, :])
            return carry

        lax.fori_loop(0, t // tk, lat_step, 0, unroll=2)

    outs = []
    for h in range(ATT_GROUP):
        acc = acc_s[h * c:(h + 1) * c, :]
        outs.append(acc * pltpu.roll(1.0 / acc, dh, 1))
    for pair in range(ATT_GROUP // 2):
        both = jnp.where(lane < dh, outs[2 * pair], pltpu.roll(outs[2 * pair + 1], dh, 1))
        o_ref[:, pair * 2 * dh:(pair + 1) * 2 * dh] = both


def _attention_call(geom, z, q_norm, k_norm, cos128, sin128):
    t, c, p = geom.t, geom.c, geom.p
    ab = COL_ATT // 128
    sec = ATT_SECTION // 128
    rb = geom.row_block
    in_specs = [
        pl.BlockSpec((c, 128), lambda b, g, r: (rb(b, r), ab + sec * g)),
        pl.BlockSpec((c, 128), lambda b, g, r: (rb(b, r), ab + sec * g + 1)),
        pl.BlockSpec((t, 128), lambda b, g, r: (b, ab + sec * g + 2)),
        pl.BlockSpec((c, 128), lambda b, g, r: (geom.nlb + b, ab + sec * g + 2)),
        pl.BlockSpec((1, 128), lambda b, g, r: (0, 0)),
        pl.BlockSpec((1, 128), lambda b, g, r: (0, 0)),
        pl.BlockSpec((t, 128), lambda b, g, r: (0, 0)),
        pl.BlockSpec((t, 128), lambda b, g, r: (0, 0)),
    ]
    two_heads = lambda v: jnp.tile(v.reshape(1, ATT_DH), (1, 2))
    scratch = [
        pltpu.VMEM((ATT_DH, p), BF16),
        pltpu.VMEM((p, 2 * ATT_DH), BF16),
        pltpu.VMEM((ATT_GROUP * c, 128), F32),
        pltpu.VMEM((ATT_GROUP * c, 2 * ATT_DH), F32),
    ]
    return pl.pallas_call(
        functools.partial(_att_kernel, t=t, c=c),
        grid=(geom.b, ATT_KV_HEADS, geom.lat_blocks + 1),
        in_specs=in_specs,
        out_specs=pl.BlockSpec((c, ATT_GROUP * ATT_DH), lambda b, g, r: (rb(b, r), g)),
        out_shape=jax.ShapeDtypeStruct((geom.nt, ATT_W), F32),
        scratch_shapes=scratch,
        compiler_params=_params(3, 48),
        name="attention",
    )(z, z, z, z, two_heads(q_norm), two_heads(k_norm), cos128, sin128)


def _conv_kernel(a_ref, g_ref, ap_ref, gp_ref, an_ref, gn_ref, w_ref, b_ref, lng_ref, lnb_ref, o_ref,
                 ext, ys, shifted, *, t, c):
    r = pl.program_id(1)
    lat_blocks = t // c
    halo = CONV_HALO
    has_prev = jnp.logical_and(r != 0, r != lat_blocks)
    has_next = jnp.logical_and(r != lat_blocks - 1, r != lat_blocks)
    ext[halo:halo + c, :] = a_ref[...] * jax.nn.sigmoid(g_ref[...])
    ext[0:halo, :] = jnp.where(has_prev, ap_ref[...] * jax.nn.sigmoid(gp_ref[...]), 0.0)
    ext[halo + c:, :] = jnp.where(has_next, an_ref[...] * jax.nn.sigmoid(gn_ref[...]), 0.0)

    rt = 64
    first = halo - CONV_K // 2
    span = c + 2 * halo - 8
    for s in range(1, 8):
        shifted[s - 1, 0:span, :] = ext[s:s + span, :]

    def lane_block(cb, carry):
        lanes = pl.ds(pl.multiple_of(cb * 128, 128), 128)
        for ti in range(c // rt):
            acc = jnp.zeros((rt, 128), F32)
            for j in range(CONV_K):
                row, s = divmod(ti * rt + first + j, 8)
                src = ext if s == 0 else shifted.at[s - 1]
                acc = acc + w_ref[pl.ds(j, 1), lanes] * src[pl.ds(row * 8, rt), lanes]
            ys[pl.ds(ti * rt, rt), lanes] = acc
        return carry

    lax.fori_loop(0, CONV_CH // 128, lane_block, 0)
    y = ys[...] + b_ref[...]
    o_ref[...] = _silu(_layer_norm(y, lng_ref[...], lnb_ref[...]))


def _conv_call(geom, z, conv_dw, conv_db, ln_g, ln_b):
    t, c = geom.t, geom.c
    rb = geom.row_block
    hb = c // CONV_HALO
    last = geom.nt // CONV_HALO - 1
    prev = lambda b, r: jnp.maximum(rb(b, r) * hb - 1, 0)
    nxt = lambda b, r: jnp.minimum((rb(b, r) + 1) * hb, last)
    w = jnp.zeros((32, CONV_CH), F32).at[:CONV_K].set(conv_dw)
    vec = lambda v: v.reshape(1, CONV_CH)
    cst = pl.BlockSpec((1, CONV_CH), lambda b, r: (0, 0))
    in_specs = [
        pl.BlockSpec((c, CONV_CH), lambda b, r: (rb(b, r), 0)),
        pl.BlockSpec((c, CONV_CH), lambda b, r: (rb(b, r), 1)),
        pl.BlockSpec((CONV_HALO, CONV_CH), lambda b, r: (prev(b, r), 0)),
        pl.BlockSpec((CONV_HALO, CONV_CH), lambda b, r: (prev(b, r), 1)),
        pl.BlockSpec((CONV_HALO, CONV_CH), lambda b, r: (nxt(b, r), 0)),
        pl.BlockSpec((CONV_HALO, CONV_CH), lambda b, r: (nxt(b, r), 1)),
        pl.BlockSpec((32, CONV_CH), lambda b, r: (0, 0)),
        cst, cst, cst,
    ]
    return pl.pallas_call(
        functools.partial(_conv_kernel, t=t, c=c),
        grid=(geom.b, geom.lat_blocks + 1),
        in_specs=in_specs,
        out_specs=pl.BlockSpec((c, CONV_CH), lambda b, r: (rb(b, r), 0)),
        out_shape=jax.ShapeDtypeStruct((geom.nt, CONV_CH), F32),
        scratch_shapes=[pltpu.VMEM((c + 2 * CONV_HALO, CONV_CH), F32), pltpu.VMEM((c, CONV_CH), F32),
                        pltpu.VMEM((7, c + 2 * CONV_HALO, CONV_CH), F32)],
        compiler_params=_params(2, 32),
        name="conformer_conv",
    )(z, z, z, z, z, z, w, vec(conv_db), vec(ln_g), vec(ln_b))


def _mix_kernel(ret_ref, att_ref, cv_ref, gr_ref, ga_ref, gc_ref, x_ref, g1_ref, sh2_ref, sc2_ref,
                wr_ref, wa_ref, wc_ref, wo_ref, lng_ref, lnb_ref, x1_ref, h2_ref, *, alpha):
    def proj(v_ref, w_ref):
        return jnp.dot(v_ref[...].astype(BF16), w_ref[...], preferred_element_type=F32)

    merged = (jax.nn.sigmoid(gr_ref[...]) * proj(ret_ref, wr_ref)
              + jax.nn.sigmoid(ga_ref[...]) * proj(att_ref, wa_ref)
              + jax.nn.sigmoid(gc_ref[...]) * proj(cv_ref, wc_ref))
    y = jnp.dot(merged.astype(BF16), wo_ref[...], preferred_element_type=F32)
    x1 = _layer_norm(alpha * x_ref[...] + g1_ref[0] * y, lng_ref[...], lnb_ref[...])
    x1_ref[...] = x1
    h2_ref[...] = x1 * (1.0 + sc2_ref[0]) + sh2_ref[0]


def _mix_call(geom, alpha, ret, att, cv, z, x, mods, w_ret_o, w_att_o, w_conv_o, w_out, ln_g, ln_b):
    tm = geom.c
    d = D_MODEL
    tile = pl.BlockSpec((tm, d), lambda i: (i, 0))
    gate = lambda k: pl.BlockSpec((tm, d), lambda i: (i, COL_GT // d + k))
    wsp = pl.BlockSpec((d, d), lambda i: (0, 0))
    vsp = pl.BlockSpec((1, d), lambda i: (0, 0))
    return pl.pallas_call(
        functools.partial(_mix_kernel, alpha=alpha),
        grid=(geom.nt // tm,),
        in_specs=[tile, tile, tile, gate(0), gate(1), gate(2), tile,
                  _mod_spec(geom, tm, 2), _mod_spec(geom, tm, 3), _mod_spec(geom, tm, 4),
                  wsp, wsp, wsp, wsp, vsp, vsp],
        out_specs=[tile, tile],
        out_shape=[jax.ShapeDtypeStruct((geom.nt, d), F32)] * 2,
        compiler_params=_params(1, 48),
        name="merge_ln1",
    )(ret, att, cv, z, z, z, x, mods, mods, mods, w_ret_o, w_att_o, w_conv_o, w_out,
      ln_g.reshape(1, d), ln_b.reshape(1, d))


def _router_kernel(h_ref, wr_ref, bias_ref, e_ref, w_ref, pos_ref, cnt_ref, hist_ref, cnt):
    i = pl.program_id(0)
    tm = h_ref.shape[0]
    ne, per = N_EXPERTS, N_EXPERTS // N_GROUPS
    neg = -jnp.inf

    @pl.when(i == 0)
    def _init():
        cnt[...] = jnp.zeros(cnt.shape, F32)

    logits = jnp.dot(h_ref[...], wr_ref[...], preferred_element_type=F32, precision=HIGHEST)
    scores = jax.nn.sigmoid(logits.T[:ne])
    sel = scores + bias_ref[...]

    member = lax.broadcasted_iota(jnp.int32, (per, tm), 0)
    grp_rows = []
    for g in range(N_GROUPS):
        blk = sel[g * per:(g + 1) * per]
        m1 = jnp.max(blk, axis=0, keepdims=True)
        first = jnp.min(jnp.where(blk == m1, member, per), axis=0, keepdims=True)
        m2 = jnp.max(jnp.where(member == first, neg, blk), axis=0, keepdims=True)
        grp_rows.append(m1 + m2)
    gs = jnp.concatenate(grp_rows, axis=0)

    gidx = lax.broadcasted_iota(jnp.int32, (N_GROUPS, tm), 0)
    rank = jnp.zeros((N_GROUPS, tm), jnp.int32)
    for g in range(N_GROUPS):
        row = gs[g:g + 1]
        ahead = jnp.logical_or(row > gs, jnp.logical_and(row == gs, g < gidx))
        rank = rank + ahead.astype(jnp.int32)
    keep = (rank < TOPK_GROUPS).astype(F32)
    keep_e = jnp.concatenate([jnp.broadcast_to(keep[g:g + 1], (per, tm)) for g in range(N_GROUPS)], axis=0)
    cand = jnp.where(keep_e > 0.5, sel, neg)

    eidx = lax.broadcasted_iota(jnp.int32, (ne, tm), 0)
    picks, gates, hots = [], [], []
    chosen = jnp.zeros((ne, tm), F32)
    for _ in range(TOP_K):
        m = jnp.max(cand, axis=0, keepdims=True)
        idx = jnp.min(jnp.where(cand == m, eidx, ne), axis=0, keepdims=True)
        hot = eidx == idx
        picks.append(idx)
        gates.append(jnp.sum(jnp.where(hot, scores, 0.0), axis=0, keepdims=True))
        hots.append(hot)
        chosen = jnp.where(hot, 1.0, chosen)
        cand = jnp.where(hot, neg, cand)
    total = gates[0]
    for gk in gates[1:]:
        total = total + gk

    ti = lax.broadcasted_iota(jnp.int32, (tm, tm), 0)
    tj = lax.broadcasted_iota(jnp.int32, (tm, tm), 1)
    before = jnp.where(ti < tj, 1.0, 0.0).astype(BF16)
    prior = jnp.dot(chosen.astype(BF16), before, preferred_element_type=F32) + cnt[...][:, :1]
    pos = [jnp.sum(jnp.where(hot, prior, 0.0), axis=0, keepdims=True) for hot in hots]

    e_ref[...] = jnp.concatenate(picks, axis=0)
    w_ref[...] = jnp.concatenate([ROUTED_SCALE * gk / total for gk in gates], axis=0)
    pos_ref[...] = jnp.concatenate(pos, axis=0).astype(jnp.int32)
    hist_ref[0] = cnt[...]
    cnt[...] = cnt[...] + jnp.sum(chosen, axis=1, keepdims=True)
    cnt_ref[...] = cnt[...]


def _router_call(geom, h2, w_router, router_bias):
    tm = geom.c
    wr = jnp.zeros((D_MODEL, 128), F32).at[:, :N_EXPERTS].set(w_router)
    tok = pl.BlockSpec((TOP_K, tm), lambda i: (0, i))
    return pl.pallas_call(
        _router_kernel,
        grid=(geom.nt // tm,),
        in_specs=[pl.BlockSpec((tm, D_MODEL), lambda i: (i, 0)),
                  pl.BlockSpec((D_MODEL, 128), lambda i: (0, 0)),
                  pl.BlockSpec((N_EXPERTS, 1), lambda i: (0, 0))],
        out_specs=[tok, tok, tok, pl.BlockSpec((N_EXPERTS, 128), lambda i: (0, 0)),
                   pl.BlockSpec((1, N_EXPERTS, 128), lambda i: (i, 0, 0))],
        out_shape=[jax.ShapeDtypeStruct((TOP_K, geom.nt), jnp.int32),
                   jax.ShapeDtypeStruct((TOP_K, geom.nt), F32),
                   jax.ShapeDtypeStruct((TOP_K, geom.nt), jnp.int32),
                   jax.ShapeDtypeStruct((N_EXPERTS, 128), F32),
                   jax.ShapeDtypeStruct((geom.nt // tm, N_EXPERTS, 128), F32)],
        scratch_shapes=[pltpu.VMEM((N_EXPERTS, 128), F32)],
        compiler_params=_params(1, 32),
        name="moe_router",
    )(h2, wr, router_bias.reshape(N_EXPERTS, 1))


HALF = D_MODEL // 2


def _pack_bf16_pairs(v):
    lo = pltpu.bitcast(v[:, :HALF].astype(BF16).astype(F32), jnp.uint32)
    hi = pltpu.bitcast(v[:, HALF:].astype(BF16).astype(F32), jnp.uint32)
    return jnp.bitwise_or(jnp.right_shift(lo, jnp.uint32(16)), hi)


def _unpack_bf16_pairs(w):
    lo = pltpu.bitcast(jnp.left_shift(w, jnp.uint32(16)), F32)
    hi = pltpu.bitcast(jnp.bitwise_and(w, jnp.uint32(0xFFFF0000)), F32)
    return lo, hi


RUN_ALIGN = 8
SORTED_ROWS = 256 * TOP_K + N_EXPERTS * RUN_ALIGN
RUN_BITS = tuple(range(8, 2, -1))


def _for_each_run_piece(n_ref, src_ref, tile, visit):
    def per_expert(e, off):
        n = n_ref[tile * N_EXPERTS + e]
        src = src_ref[tile * N_EXPERTS + e]
        for lb in RUN_BITS:
            done = (n >> (lb + 1)) << (lb + 1)

            @pl.when((n & (1 << lb)) != 0)
            def _piece():
                visit(pl.multiple_of(off + done, RUN_ALIGN), pl.multiple_of(src + done, RUN_ALIGN), 1 << lb)

        return off + n

    lax.fori_loop(0, N_EXPERTS, per_expert, 0)


TOTAL_BITS = tuple(range(11, 2, -1))


def _wait_rows(total, wait_piece):
    for lb in TOTAL_BITS:
        @pl.when((total & (1 << lb)) != 0)
        def _amount():
            wait_piece(1 << lb)


def _dispatch_kernel(last_ref, n_ref, src_ref, tot_ref, sidx_ref, h_ref, xs_out, packed, zblk, sem, zsem):
    tm = h_ref.shape[0]
    i = pl.program_id(0)

    @pl.when(pl.program_id(0) == 0)
    def _zero_tail_blocks():
        zblk[...] = jnp.zeros(zblk.shape, zblk.dtype)

        def zero_copy(e):
            return pltpu.make_async_copy(zblk, xs_out.at[pl.ds(last_ref[e] * MOE_BLOCK, MOE_BLOCK)], zsem)

        def start(e, carry):
            zero_copy(e).start()
            return carry

        def wait(e, carry):
            zero_copy(e).wait()
            return carry

        lax.fori_loop(0, N_EXPERTS, start, 0)
        lax.fori_loop(0, N_EXPERTS, wait, 0)

    rows = lax.broadcasted_iota(jnp.int32, (SORTED_ROWS, tm), 0)
    pick = jnp.zeros((SORTED_ROWS, tm), F32)
    for k in range(TOP_K):
        pick = jnp.where(rows == sidx_ref[k:k + 1, :], 1.0, pick)
    sorted_rows = jnp.dot(pick.astype(BF16), h_ref[...].astype(BF16), preferred_element_type=F32)
    lo = pltpu.bitcast(sorted_rows[:, :HALF], jnp.uint32)
    hi = pltpu.bitcast(sorted_rows[:, HALF:], jnp.uint32)
    packed[...] = jnp.bitwise_or(jnp.right_shift(lo, jnp.uint32(16)), hi)

    def piece(sorted_row, slot_row, rows_):
        return pltpu.make_async_copy(packed.at[pl.ds(sorted_row, rows_)], xs_out.at[pl.ds(slot_row, rows_)], sem)

    _for_each_run_piece(n_ref, src_ref, i, lambda a, b, r: piece(a, b, r).start())
    _wait_rows(tot_ref[i], lambda r: piece(0, 0, r).wait())


def _dispatch_call(geom, last_block, run_rows, run_slot, tile_rows, sidx, h2, n_blocks):
    tm = geom.c
    assert tm * TOP_K + N_EXPERTS * RUN_ALIGN == SORTED_ROWS
    grid_spec = pltpu.PrefetchScalarGridSpec(
        num_scalar_prefetch=4,
        grid=(geom.nt // tm,),
        in_specs=[pl.BlockSpec((TOP_K, tm), lambda i, *_: (0, i)),
                  pl.BlockSpec((tm, D_MODEL), lambda i, *_: (i, 0))],
        out_specs=pl.BlockSpec(memory_space=pl.ANY),
        scratch_shapes=[pltpu.VMEM((SORTED_ROWS, HALF), jnp.uint32), pltpu.VMEM((MOE_BLOCK, HALF), jnp.uint32),
                        pltpu.SemaphoreType.DMA(()), pltpu.SemaphoreType.DMA(())],
    )
    return pl.pallas_call(
        _dispatch_kernel,
        grid_spec=grid_spec,
        out_shape=jax.ShapeDtypeStruct(((n_blocks + 1) * MOE_BLOCK, HALF), jnp.uint32),
        compiler_params=_params(1, 48),
        name="moe_dispatch",
    )(last_block, run_rows, run_slot, tile_rows, sidx, h2)


def _expert_kernel(be_ref, nu_ref, x_ref, wgu_ref, wd_ref, o_ref):
    del be_ref

    @pl.when(pl.program_id(0) < nu_ref[0])
    def _run():
        lo, hi = _unpack_bf16_pairs(x_ref[...])
        x = jnp.concatenate([lo, hi], axis=1).astype(BF16)
        hgu = jnp.dot(x, wgu_ref[0], preferred_element_type=F32)
        hid = _silu(hgu[:, :D_EXPERT]) * hgu[:, D_EXPERT:]
        o_ref[...] = _pack_bf16_pairs(jnp.dot(hid.astype(BF16), wd_ref[0], preferred_element_type=F32))


def _expert_call(block_e, n_used, xs, w_gu, w_down):
    n_blocks = xs.shape[0] // MOE_BLOCK - 1
    live = lambda i, be, nu: jnp.minimum(i, nu[0] - 1)
    grid_spec = pltpu.PrefetchScalarGridSpec(
        num_scalar_prefetch=2,
        grid=(n_blocks,),
        in_specs=[pl.BlockSpec((MOE_BLOCK, HALF), lambda i, be, nu: (live(i, be, nu), 0)),
                  pl.BlockSpec((1, D_MODEL, 2 * D_EXPERT), lambda i, be, nu: (be[live(i, be, nu)], 0, 0)),
                  pl.BlockSpec((1, D_EXPERT, D_MODEL), lambda i, be, nu: (be[live(i, be, nu)], 0, 0))],
        out_specs=pl.BlockSpec((MOE_BLOCK, HALF), lambda i, be, nu: (live(i, be, nu), 0)),
    )
    return pl.pallas_call(
        _expert_kernel,
        grid_spec=grid_spec,
        out_shape=jax.ShapeDtypeStruct(xs.shape, jnp.uint32),
        compiler_params=_params(1, 32),
        name="moe_experts",
    )(block_e, n_used, xs, w_gu, w_down)


def _combine_kernel(n_ref, src_ref, tot_ref, sidx_ref, wt_ref, ys_hbm, h_ref, x_ref, g2_ref, wgu_ref, wd_ref,
                    lng_ref, lnb_ref, o_ref, buf_a, buf_b, sem, *, alpha):
    i = pl.program_id(0)
    n = pl.num_programs(0)
    tm = h_ref.shape[0]
    even = i % 2 == 0

    def piece(buf, slot, sorted_row, slot_row, rows_):
        return pltpu.make_async_copy(ys_hbm.at[pl.ds(slot_row, rows_)], buf.at[pl.ds(sorted_row, rows_)], sem.at[slot])

    def gather(tile, buf, slot):
        _for_each_run_piece(n_ref, src_ref, tile, lambda a, b, r: piece(buf, slot, a, b, r).start())

    def drain(tile, buf, slot):
        _wait_rows(tot_ref[tile], lambda r: piece(buf, slot, 0, 0, r).wait())

    def finish(buf):
        wt = wt_ref[...]
        cols = lax.broadcasted_iota(jnp.int32, (tm, SORTED_ROWS), 1)
        mix = jnp.zeros((tm, SORTED_ROWS), F32)
        for k in range(TOP_K):
            mix = jnp.where(cols == sidx_ref[:, k:k + 1], wt[:, k:k + 1], mix)
        mix = mix.astype(BF16)
        lo, hi = _unpack_bf16_pairs(buf[...])
        routed = jnp.concatenate([jnp.dot(mix, lo.astype(BF16), preferred_element_type=F32),
                                  jnp.dot(mix, hi.astype(BF16), preferred_element_type=F32)], axis=1)
        hgu = jnp.dot(h_ref[...].astype(BF16), wgu_ref[...], preferred_element_type=F32)
        hid = _silu(hgu[:, :D_SHARED]) * hgu[:, D_SHARED:]
        shared = jnp.dot(hid.astype(BF16), wd_ref[...], preferred_element_type=F32)
        o_ref[...] = _layer_norm(alpha * x_ref[...] + g2_ref[0] * (routed + shared), lng_ref[...], lnb_ref[...])

    @pl.when(i == 0)
    def _first():
        buf_a[...] = jnp.zeros(buf_a.shape, buf_a.dtype)
        buf_b[...] = jnp.zeros(buf_b.shape, buf_b.dtype)
        gather(i, buf_a, 0)

    @pl.when(jnp.logical_and(even, i + 1 < n))
    def _ahead_b():
        gather(i + 1, buf_b, 1)

    @pl.when(jnp.logical_and(jnp.logical_not(even), i + 1 < n))
    def _ahead_a():
        gather(i + 1, buf_a, 0)

    @pl.when(even)
    def _finish_a():
        drain(i, buf_a, 0)
        finish(buf_a)

    @pl.when(jnp.logical_not(even))
    def _finish_b():
        drain(i, buf_b, 1)
        finish(buf_b)


def _combine_call(geom, alpha, run_rows, run_slot, tile_rows, sidx_tok, w_tok, ys, h2, x1, mods, w_sh_gu, w_sh_down, ln_g, ln_b):
    tm = geom.c
    d = D_MODEL
    n = geom.nt // tm
    tile = pl.BlockSpec((tm, d), lambda i, *_: (i, 0))
    vsp = pl.BlockSpec((1, d), lambda i, *_: (0, 0))
    per_tok = pl.BlockSpec((tm, TOP_K), lambda i, *_: (i, 0))
    grid_spec = pltpu.PrefetchScalarGridSpec(
        num_scalar_prefetch=3,
        grid=(n,),
        in_specs=[per_tok, per_tok,
                  pl.BlockSpec(memory_space=pl.ANY),
                  tile, tile, _mod_spec(geom, tm, 5),
                  pl.BlockSpec((d, 2 * D_SHARED), lambda i, *_: (0, 0)),
                  pl.BlockSpec((D_SHARED, d), lambda i, *_: (0, 0)),
                  vsp, vsp],
        out_specs=tile,
        scratch_shapes=[pltpu.VMEM((SORTED_ROWS, HALF), jnp.uint32), pltpu.VMEM((SORTED_ROWS, HALF), jnp.uint32),
                        pltpu.SemaphoreType.DMA((2,))],
    )
    return pl.pallas_call(
        functools.partial(_combine_kernel, alpha=alpha),
        grid_spec=grid_spec,
        out_shape=jax.ShapeDtypeStruct((geom.nt, d), F32),
        compiler_params=_params(1, 56),
        name="moe_combine_ln2",
    )(run_rows, run_slot, tile_rows, sidx_tok, w_tok, ys, h2, x1, mods, w_sh_gu, w_sh_down,
      ln_g.reshape(1, d), ln_b.reshape(1, d))


def _rope_tables(t):
    rows = t // GRID_W
    row = jnp.repeat(jnp.arange(rows, dtype=F32), GRID_W)
    col = jnp.tile(jnp.arange(GRID_W, dtype=F32), rows)
    n_freq = ATT_DH // 4
    inv_freq = ROPE_THETA ** (-jnp.arange(n_freq, dtype=F32) / n_freq)
    ang = jnp.concatenate([row[:, None] * inv_freq, col[:, None] * inv_freq], axis=-1)
    cos, sin = jnp.cos(ang), jnp.sin(ang)
    cos64 = jnp.concatenate([cos, cos], axis=-1)
    sin64 = jnp.concatenate([-sin, sin], axis=-1)
    return cos64, sin64


def kernel(x, c, ctx, c_ctx, w_ada, b_ada, w_in, ret_decay_logit, att_q_norm, att_k_norm, conv_dw, conv_db, conv_ln_g, conv_ln_b, w_ret_o, w_att_o, w_conv_o, w_out, ln1_g, ln1_b, w_router, router_bias, w_exp_gate, w_exp_up, w_exp_down, w_sh_gate, w_sh_up, w_sh_down, ln2_g, ln2_b):
    b, t, d = x.shape
    n_ctx = ctx.shape[1]
    depth = w_ada.shape[0]
    assert d == D_MODEL and w_in.shape[-1] == D_IN
    geom = _Geom(b, t, n_ctx)
    alpha = float((2 * depth) ** 0.25)

    cos64, sin64 = _rope_tables(t)
    cos128 = jnp.concatenate([cos64, cos64], axis=-1)
    sin128 = jnp.concatenate([sin64, sin64], axis=-1)

    n_rows = -(-(b + 1) // 8) * 8
    cvecs = jnp.zeros((n_rows, d), F32).at[:b].set(c).at[b].set(c_ctx)
    mods_all = _mods_call(cvecs, w_ada, b_ada).reshape(depth, n_rows * 6, 1, d)

    n_tiles = geom.nt // geom.c
    n_blocks = -(-(geom.nt * TOP_K + n_tiles * N_EXPERTS * (RUN_ALIGN - 1)) // MOE_BLOCK) + N_EXPERTS

    xt = jnp.concatenate([x.reshape(geom.nl, d), ctx.reshape(geom.nc, d)], axis=0)
    for l in range(depth):
        mods = mods_all[l]
        w_in_l = _permute_columns(w_in[l]).astype(BF16)
        z = _inproj_call(geom, xt, mods, w_in_l)

        log_gamma = jax.nn.log_sigmoid(ret_decay_logit[l].astype(F32))
        ret = _retention_call(geom, z, log_gamma, cos128, sin128)
        att = _attention_call(geom, z, att_q_norm[l], att_k_norm[l], cos128, sin128)
        cv = _conv_call(geom, z, conv_dw[l], conv_db[l], conv_ln_g[l], conv_ln_b[l])
        x1, h2 = _mix_call(geom, alpha, ret, att, cv, z, xt, mods,
                           w_ret_o[l].astype(BF16), w_att_o[l].astype(BF16), w_conv_o[l].astype(BF16),
                           w_out[l].astype(BF16), ln1_g[l], ln1_b[l])

        top_e, gate_w, pos, counts, cnt_hist = _router_call(geom, h2, w_router[l], router_bias[l])
        before = cnt_hist[:, :, 0].astype(jnp.int32)
        total = counts[:, 0].astype(jnp.int32)
        tile_n = jnp.concatenate([before[1:], total[None, :]], axis=0) - before
        run_rows = (tile_n + RUN_ALIGN - 1) // RUN_ALIGN * RUN_ALIGN
        run_before = jnp.cumsum(run_rows, axis=0) - run_rows
        blocks_e = (jnp.sum(run_rows, axis=0) + MOE_BLOCK - 1) // MOE_BLOCK
        blocks_end = jnp.cumsum(blocks_e)
        start_row = (blocks_end - blocks_e) * MOE_BLOCK
        run_slot = start_row[None, :] + run_before
        run_sorted = jnp.cumsum(run_rows, axis=1) - run_rows
        onehot = top_e[:, :, None] == jnp.arange(N_EXPERTS, dtype=jnp.int32)[None, None, :]
        per_token = lambda table: jnp.sum(jnp.where(onehot, jnp.repeat(table, geom.c, axis=0)[None], 0), axis=-1)
        sidx = per_token(run_sorted) + pos - per_token(before)
        block_ids = jnp.arange(n_blocks, dtype=jnp.int32)
        block_e = jnp.minimum(jnp.sum((blocks_end[None, :] <= block_ids[:, None]).astype(jnp.int32), axis=1),
                              N_EXPERTS - 1)
        n_used = blocks_end[-1:].astype(jnp.int32)
        last_block = jnp.where(blocks_e > 0, blocks_end - 1, n_blocks).astype(jnp.int32)
        run_rows_flat, run_slot_flat = run_rows.reshape(-1), run_slot.reshape(-1).astype(jnp.int32)
        tile_rows = jnp.sum(run_rows, axis=1)

        xs = _dispatch_call(geom, last_block, run_rows_flat, run_slot_flat, tile_rows, sidx, h2, n_blocks)
        w_gu = jnp.concatenate([w_exp_gate[l], w_exp_up[l]], axis=-1).astype(BF16)
        ys = _expert_call(block_e, n_used, xs, w_gu, w_exp_down[l].astype(BF16))
        w_sh_gu = jnp.concatenate([w_sh_gate[l], w_sh_up[l]], axis=-1).astype(BF16)
        xt = _combine_call(geom, alpha, run_rows_flat, run_slot_flat, tile_rows, sidx.T, gate_w.T, ys, h2, x1, mods, w_sh_gu,
                           w_sh_down[l].astype(BF16), ln2_g[l], ln2_b[l])
    return xt[:geom.nl].reshape(b, t, d)
```

```python
import functools

import jax
import jax.numpy as jnp
from jax import lax
from jax.experimental import pallas as pl
from jax.experimental.pallas import tpu as pltpu

F32 = jnp.float32
BF16 = jnp.bfloat16
HIGHEST = lax.Precision.HIGHEST

D_MODEL = 1024
GRID_W = 64
EPS = 1e-6

RET_HEADS = 8
RET_DK = 64
RET_DV = 128
RET_CHUNK = 128
RET_W = RET_HEADS * RET_DV

ATT_HEADS = 16
ATT_KV_HEADS = 4
ATT_DH = 64
ATT_GROUP = ATT_HEADS // ATT_KV_HEADS
ATT_W = ATT_HEADS * ATT_DH
ROPE_THETA = 10000.0
ATT_KEY_BLOCK = 1024

CONV_CH = 1024
CONV_K = 31
CONV_HALO = 16

N_EXPERTS = 64
TOP_K = 8
N_GROUPS = 8
TOPK_GROUPS = 4
D_EXPERT = 256
D_SHARED = 256
ROUTED_SCALE = 2.5
MOE_BLOCK = 512

_ORIG = dict(rq=0, rk=512, rv=1024, rg=2048, aq=3072, ak=4096, av=4352, cu=4608, gt=6656)
D_IN = 9728
COL_CU = 0
COL_GT = 2048
COL_RG = 5120
COL_RV = 6144
COL_RQ = 7168
COL_RK = 7680
COL_ATT = 8192
ATT_SECTION = ATT_GROUP * ATT_DH + 2 * ATT_DH


def _column_ranges():
    rng = [(_ORIG["cu"], _ORIG["cu"] + 2 * CONV_CH),
           (_ORIG["gt"], _ORIG["gt"] + 3 * D_MODEL),
           (_ORIG["rg"], _ORIG["rg"] + RET_W),
           (_ORIG["rv"], _ORIG["rv"] + RET_W),
           (_ORIG["rq"], _ORIG["rq"] + RET_HEADS * RET_DK),
           (_ORIG["rk"], _ORIG["rk"] + RET_HEADS * RET_DK)]
    for g in range(ATT_KV_HEADS):
        rng.append((_ORIG["aq"] + g * ATT_GROUP * ATT_DH, _ORIG["aq"] + (g + 1) * ATT_GROUP * ATT_DH))
        rng.append((_ORIG["ak"] + g * ATT_DH, _ORIG["ak"] + (g + 1) * ATT_DH))
        rng.append((_ORIG["av"] + g * ATT_DH, _ORIG["av"] + (g + 1) * ATT_DH))
    cols = [c for a, b in rng for c in range(a, b)]
    assert sorted(cols) == list(range(D_IN))
    return rng


def _permute_columns(w):
    return jnp.concatenate([w[:, a:b] for a, b in _column_ranges()], axis=1)


def _params(n_axes, vmem_mib):
    return pltpu.CompilerParams(dimension_semantics=("arbitrary",) * n_axes,
                                vmem_limit_bytes=vmem_mib * 1024 * 1024)


def _silu(v):
    return v * jax.nn.sigmoid(v)


def _layer_norm(v, g, b):
    mu = jnp.mean(v, axis=-1, keepdims=True)
    d = v - mu
    var = jnp.mean(d * d, axis=-1, keepdims=True)
    return d * lax.rsqrt(var + EPS) * g + b


def _mods_kernel(c_ref, w_ref, b_ref, o_ref):
    s = _silu(c_ref[...])
    o_ref[0] = jnp.dot(s, w_ref[0], preferred_element_type=F32, precision=HIGHEST) + b_ref[0]


def _mods_call(cvecs, w_ada, b_ada):
    n_layers = w_ada.shape[0]
    rows, d = cvecs.shape
    return pl.pallas_call(
        _mods_kernel,
        grid=(n_layers, 6),
        in_specs=[pl.BlockSpec((rows, d), lambda l, j: (0, 0)),
                  pl.BlockSpec((1, d, d), lambda l, j: (l, 0, j)),
                  pl.BlockSpec((1, 1, d), lambda l, j: (l, 0, j))],
        out_specs=pl.BlockSpec((1, rows, d), lambda l, j: (l, 0, j)),
        out_shape=jax.ShapeDtypeStruct((n_layers, rows, 6 * d), F32),
        compiler_params=_params(2, 32),
        name="adaln_mods",
    )(cvecs, w_ada, b_ada.reshape(n_layers, 1, 6 * d))


class _Geom:
    def __init__(self, b, t, c):
        assert t % c == 0 and c % RET_CHUNK == 0 and c % CONV_HALO == 0 and t % ATT_KEY_BLOCK == 0
        self.b, self.t, self.c = b, t, c
        self.nl, self.nc = b * t, b * c
        self.nt = self.nl + self.nc
        self.lat_blocks = t // c
        self.nlb = self.nl // c
        self.p = t + c

    def row_block(self, bi, r):
        return jnp.where(r < self.lat_blocks, bi * self.lat_blocks + r, self.nlb + bi)

    def mod_row(self, i, tm):
        return jnp.where(i * tm < self.nl, (i * tm) // self.t, self.b)


def _mod_spec(geom, tm, which, grid_pos=0):
    d = D_MODEL
    if grid_pos == 0:
        return pl.BlockSpec((1, 1, d), lambda i, *_: (geom.mod_row(i, tm) * 6 + which, 0, 0))
    return pl.BlockSpec((1, 1, d), lambda j, i: (geom.mod_row(i, tm) * 6 + which, 0, 0))


def _inproj_kernel(x_ref, sh_ref, sc_ref, w_ref, o_ref):
    h = x_ref[...] * (1.0 + sc_ref[0]) + sh_ref[0]
    o_ref[...] = jnp.dot(h.astype(BF16), w_ref[...], preferred_element_type=F32).astype(o_ref.dtype)


def _inproj_call(geom, x, mods, w_in_bf16):
    tm = 512 if geom.nc % 512 == 0 and geom.t % 512 == 0 else geom.c
    tn = D_IN // 2
    return pl.pallas_call(
        _inproj_kernel,
        grid=(D_IN // tn, geom.nt // tm),
        in_specs=[pl.BlockSpec((tm, D_MODEL), lambda j, i: (i, 0)),
                  _mod_spec(geom, tm, 0, grid_pos=1),
                  _mod_spec(geom, tm, 1, grid_pos=1),
                  pl.BlockSpec((D_MODEL, tn), lambda j, i: (0, j))],
        out_specs=pl.BlockSpec((tm, tn), lambda j, i: (i, j)),
        out_shape=jax.ShapeDtypeStruct((geom.nt, D_IN), BF16),
        compiler_params=_params(2, 48),
        name="in_proj",
    )(x, mods, mods, w_in_bf16)


def _rot_half_128(v):
    lane = lax.broadcasted_iota(jnp.int32, v.shape, 1)
    return jnp.where((lane % 64) < 32, pltpu.roll(v, 96, 1), pltpu.roll(v, 32, 1))


def _ret_kernel(lg_ref, ql_ref, qc_ref, kl_ref, kc_ref, vl_ref, vc_ref, g_ref, cos_ref, sin_ref, o_ref,
                qs, kts, yf, yb, st, dm, qwb, kwb, gcs, *, t, c):
    ch = RET_CHUNK
    hp = pl.program_id(1)
    r = pl.program_id(2)
    lat_blocks = t // c
    n_lat, n_ctx = t // ch, c // ch

    @pl.when(r == 0)
    def _scan():
        ri = lax.broadcasted_iota(jnp.int32, (ch, ch), 0).astype(F32)
        ci = lax.broadcasted_iota(jnp.int32, (ch, ch), 1).astype(F32)
        for d in range(2):
            for h in range(2):
                u = 2 * d + h
                lg = lg_ref[d, 2 * hp + h]
                rel = (ri - ci) if d == 0 else (ci - ri)
                dm[u] = jnp.where(rel >= 0.0, jnp.exp(lg * jnp.maximum(rel, 0.0)), 0.0)
                qwb[u] = jnp.exp(lg * ((ri + 1.0) if d == 0 else (float(ch) - ri)))
                kwb[u] = jnp.exp(lg * ((float(ch) - 1.0 - ri) if d == 0 else ri))
                gcs[u] = jnp.exp(jnp.full((RET_DK, RET_DV), lg * float(ch), F32))
                st[u] = jnp.zeros((RET_DK, RET_DV), F32)

        def stage(q, k, seq_rows):
            qs[0, seq_rows, :] = q[:, :RET_DK].astype(BF16)
            qs[1, seq_rows, :] = q[:, RET_DK:].astype(BF16)
            kt = k.T
            kts[0, :, seq_rows] = kt[:RET_DK].astype(BF16)
            kts[1, :, seq_rows] = kt[RET_DK:].astype(BF16)

        kscale = RET_DK ** -0.5
        for cc in range(n_ctx):
            rows = pl.ds(cc * ch, ch)
            stage(qc_ref[rows, :].astype(F32), kc_ref[rows, :].astype(F32) * kscale, rows)

        def stage_lat(cc, carry):
            rows = pl.ds(pl.multiple_of(cc * ch, ch), ch)
            cs, sn = cos_ref[rows, :], sin_ref[rows, :]
            q = ql_ref[rows, :].astype(F32)
            k = kl_ref[rows, :].astype(F32)
            q = q * cs + _rot_half_128(q) * sn
            k = (k * cs + _rot_half_128(k) * sn) * kscale
            stage(q, k, pl.ds(pl.multiple_of(c + cc * ch, ch), ch))
            return carry

        lax.fori_loop(0, n_lat, stage_lat, 0)

        def run_segment(v_ref, seq_off, n):
            def body(i, carry):
                for d, cc in ((0, i), (1, n - 1 - i)):
                    vrows = pl.ds(pl.multiple_of(cc * ch, ch), ch)
                    srows = pl.ds(pl.multiple_of(seq_off + cc * ch, ch), ch)
                    for h in range(2):
                        u = 2 * d + h
                        q = qs[h, srows, :]
                        kt = kts[h, :, srows]
                        v = v_ref[vrows, h * RET_DV:(h + 1) * RET_DV].astype(F32)
                        s = jnp.dot(q, kt, preferred_element_type=F32)
                        y = jnp.dot((s * dm[u]).astype(BF16), v.astype(BF16), preferred_element_type=F32)
                        state = st[u]
                        y = y + jnp.dot(q, state.astype(BF16), preferred_element_type=F32) * qwb[u]
                        dst = yf if d == 0 else yb
                        dst[srows, h * RET_DV:(h + 1) * RET_DV] = y
                        kv = jnp.dot(kt, (v * kwb[u]).astype(BF16), preferred_element_type=F32)
                        st[u] = gcs[u] * state + kv
                return carry

            lax.fori_loop(0, n, body, 0, unroll=2)

        run_segment(vc_ref, 0, n_ctx)
        run_segment(vl_ref, c, n_lat)

    def finish(srows):
        y = yf[srows, :] + yb[srows, :]
        for h in range(2):
            cols = slice(h * RET_DV, (h + 1) * RET_DV)
            yh = y[:, cols]
            mu = jnp.mean(yh, axis=-1, keepdims=True)
            dlt = yh - mu
            var = jnp.mean(dlt * dlt, axis=-1, keepdims=True)
            o_ref[:, cols] = _silu(g_ref[:, cols].astype(F32)) * (dlt * lax.rsqrt(var + EPS))

    @pl.when(r < lat_blocks)
    def _fin_lat():
        finish(pl.ds(pl.multiple_of(c + r * c, c), c))

    @pl.when(r == lat_blocks)
    def _fin_ctx():
        finish(pl.ds(0, c))


def _retention_call(geom, z, log_gamma, cos128, sin128):
    t, c, p = geom.t, geom.c, geom.p
    hpairs = RET_HEADS // 2
    qb, kb = COL_RQ // 128, COL_RK // 128
    vb, gb = COL_RV // 256, COL_RG // 256
    rb = geom.row_block
    in_specs = [
        pl.BlockSpec(memory_space=pltpu.SMEM),
        pl.BlockSpec((t, 128), lambda b, h, r: (b, qb + h)),
        pl.BlockSpec((c, 128), lambda b, h, r: (geom.nlb + b, qb + h)),
        pl.BlockSpec((t, 128), lambda b, h, r: (b, kb + h)),
        pl.BlockSpec((c, 128), lambda b, h, r: (geom.nlb + b, kb + h)),
        pl.BlockSpec((t, 256), lambda b, h, r: (b, vb + h)),
        pl.BlockSpec((c, 256), lambda b, h, r: (geom.nlb + b, vb + h)),
        pl.BlockSpec((c, 256), lambda b, h, r: (rb(b, r), gb + h)),
        pl.BlockSpec((t, 128), lambda b, h, r: (0, 0)),
        pl.BlockSpec((t, 128), lambda b, h, r: (0, 0)),
    ]
    scratch = [
        pltpu.VMEM((2, p, RET_DK), BF16),
        pltpu.VMEM((2, RET_DK, p), BF16),
        pltpu.VMEM((p, 2 * RET_DV), F32),
        pltpu.VMEM((p, 2 * RET_DV), F32),
        pltpu.VMEM((4, RET_DK, RET_DV), F32),
        pltpu.VMEM((4, RET_CHUNK, RET_CHUNK), F32),
        pltpu.VMEM((4, RET_CHUNK, RET_CHUNK), F32),
        pltpu.VMEM((4, RET_CHUNK, RET_CHUNK), F32),
        pltpu.VMEM((4, RET_DK, RET_DV), F32),
    ]
    return pl.pallas_call(
        functools.partial(_ret_kernel, t=t, c=c),
        grid=(geom.b, hpairs, geom.lat_blocks + 1),
        in_specs=in_specs,
        out_specs=pl.BlockSpec((c, 256), lambda b, h, r: (rb(b, r), h)),
        out_shape=jax.ShapeDtypeStruct((geom.nt, RET_W), F32),
        scratch_shapes=scratch,
        compiler_params=_params(3, 56),
        name="retention",
    )(log_gamma, z, z, z, z, z, z, z, cos128, sin128)


def _rms_heads_128(v, g):
    li = lax.broadcasted_iota(jnp.int32, (128, 128), 0) // ATT_DH
    lj = lax.broadcasted_iota(jnp.int32, (128, 128), 1) // ATT_DH
    avg = jnp.where(li == lj, 1.0 / ATT_DH, 0.0).astype(BF16)
    sq = v * v
    hi = sq.astype(BF16)
    lo = (sq - hi.astype(F32)).astype(BF16)
    ms = jnp.dot(hi, avg, preferred_element_type=F32) + jnp.dot(lo, avg, preferred_element_type=F32)
    return v * lax.rsqrt(ms + EPS) * g


def _att_kernel(qa_ref, qb_ref, kvl_ref, kvc_ref, qn_ref, kn_ref, cos_ref, sin_ref, o_ref,
                kts, vs, m_s, acc_s, *, t, c):
    r = pl.program_id(2)
    lat_blocks = t // c
    dh = ATT_DH
    tk = ATT_KEY_BLOCK
    lane = lax.broadcasted_iota(jnp.int32, (c, 2 * dh), 1)

    def stage_tile(kv, dst, cs, sn):
        k = _rms_heads_128(kv, kn_ref[...])
        if cs is not None:
            k = k * cs + _rot_half_128(k) * sn
        kts[:, dst] = k.T[:dh].astype(BF16)
        vs[dst, :] = jnp.where(lane < dh, pltpu.roll(kv, dh, 1), 1.0).astype(BF16)

    @pl.when(r == 0)
    def _stage_kv():
        stage_tile(kvc_ref[...].astype(F32), pl.ds(0, c), None, None)

        def stage(i, carry):
            rows = pl.ds(pl.multiple_of(i * c, c), c)
            stage_tile(kvl_ref[rows, :].astype(F32), pl.ds(pl.multiple_of(c + i * c, c), c),
                       cos_ref[rows, :], sin_ref[rows, :])
            return carry

        lax.fori_loop(0, lat_blocks, stage, 0)

    is_ctx = r == lat_blocks
    rows = pl.ds(pl.multiple_of(jnp.minimum(r, lat_blocks - 1) * c, c), c)
    cs, sn = cos_ref[rows, :], sin_ref[rows, :]
    q_heads = []
    for src in (qa_ref, qb_ref):
        xn = _rms_heads_128(src[...].astype(F32), qn_ref[...])
        xr = jnp.where(is_ctx, xn, xn * cs + _rot_half_128(xn) * sn) * (dh ** -0.5)
        q_heads.append(xr[:, :dh].astype(BF16))
        q_heads.append(pltpu.roll(xr, dh, 1)[:, :dh].astype(BF16))
    q = jnp.concatenate(q_heads, axis=0)

    m_s[...] = jnp.full(m_s.shape, -jnp.inf, F32)
    acc_s[...] = jnp.zeros(acc_s.shape, F32)

    def flash_step(kt, v):
        n = kt.shape[1]
        s = jnp.dot(q, kt, preferred_element_type=F32)
        m_prev = m_s[...]
        m_next = jnp.maximum(m_prev, jnp.max(s, axis=1, keepdims=True))
        prob = jnp.exp(s - jnp.concatenate([m_next] * (n // 128), axis=1))
        acc_s[...] = acc_s[...] * jnp.exp(m_prev - m_next) + jnp.dot(prob.astype(BF16), v, preferred_element_type=F32)
        m_s[...] = m_next

    flash_step(kts[:, 0:c], vs[0:c, :])

    @pl.when(jnp.logical_not(is_ctx))
    def _latent_keys():
        def lat_step(j, carry):
            krows = pl.ds(pl.multiple_of(c + j * tk, 128), tk)
            flash_step(kts[:, krows], vs[krows, :])
            return carry

        lax.fori_loop(0, t // tk, lat_step, 0, unroll=2)

    outs = []
    for h in range(ATT_GROUP):
        acc = acc_s[h * c:(h + 1) * c, :]
        outs.append(acc * pltpu.roll(1.0 / acc, dh, 1))
    for pair in range(ATT_GROUP // 2):
        both = jnp.where(lane < dh, outs[2 * pair], pltpu.roll(outs[2 * pair + 1], dh, 1))
        o_ref[:, pair * 2 * dh:(pair + 1) * 2 * dh] = both


def _attention_call(geom, z, q_norm, k_norm, cos128, sin128):
    t, c, p = geom.t, geom.c, geom.p
    ab = COL_ATT // 128
    sec = ATT_SECTION // 128
    rb = geom.row_block
    in_specs = [
        pl.BlockSpec((c, 128), lambda b, g, r: (rb(b, r), ab + sec * g)),
        pl.BlockSpec((c, 128), lambda b, g, r: (rb(b, r), ab + sec * g + 1)),
        pl.BlockSpec((t, 128), lambda b, g, r: (b, ab + sec * g + 2)),
        pl.BlockSpec((c, 128), lambda b, g, r: (geom.nlb + b, ab + sec * g + 2)),
        pl.BlockSpec((1, 128), lambda b, g, r: (0, 0)),
        pl.BlockSpec((1, 128), lambda b, g, r: (0, 0)),
        pl.BlockSpec((t, 128), lambda b, g, r: (0, 0)),
        pl.BlockSpec((t, 128), lambda b, g, r: (0, 0)),
    ]
    two_heads = lambda v: jnp.tile(v.reshape(1, ATT_DH), (1, 2))
    scratch = [
        pltpu.VMEM((ATT_DH, p), BF16),
        pltpu.VMEM((p, 2 * ATT_DH), BF16),
        pltpu.VMEM((ATT_GROUP * c, 128), F32),
        pltpu.VMEM((ATT_GROUP * c, 2 * ATT_DH), F32),
    ]
    return pl.pallas_call(
        functools.partial(_att_kernel, t=t, c=c),
        grid=(geom.b, ATT_KV_HEADS, geom.lat_blocks + 1),
        in_specs=in_specs,
        out_specs=pl.BlockSpec((c, ATT_GROUP * ATT_DH), lambda b, g, r: (rb(b, r), g)),
        out_shape=jax.ShapeDtypeStruct((geom.nt, ATT_W), F32),
        scratch_shapes=scratch,
        compiler_params=_params(3, 48),
        name="attention",
    )(z, z, z, z, two_heads(q_norm), two_heads(k_norm), cos128, sin128)


def _conv_kernel(a_ref, g_ref, ap_ref, gp_ref, an_ref, gn_ref, w_ref, b_ref, lng_ref, lnb_ref, o_ref,
                 ext, ys, shifted, *, t, c):
    r = pl.program_id(1)
    lat_blocks = t // c
    halo = CONV_HALO
    has_prev = jnp.logical_and(r != 0, r != lat_blocks)
    has_next = jnp.logical_and(r != lat_blocks - 1, r != lat_blocks)
    glu = lambda a, g: a[...].astype(F32) * jax.nn.sigmoid(g[...].astype(F32))
    ext[halo:halo + c, :] = glu(a_ref, g_ref)
    ext[0:halo, :] = jnp.where(has_prev, glu(ap_ref, gp_ref), 0.0)
    ext[halo + c:, :] = jnp.where(has_next, glu(an_ref, gn_ref), 0.0)

    rt = 64
    first = halo - CONV_K // 2
    span = c + 2 * halo - 8
    for s in range(1, 8):
        shifted[s - 1, 0:span, :] = ext[s:s + span, :]

    def lane_block(cb, carry):
        lanes = pl.ds(pl.multiple_of(cb * 128, 128), 128)
        for ti in range(c // rt):
            acc = jnp.zeros((rt, 128), F32)
            for j in range(CONV_K):
                row, s = divmod(ti * rt + first + j, 8)
                src = ext if s == 0 else shifted.at[s - 1]
                acc = acc + w_ref[pl.ds(j, 1), lanes] * src[pl.ds(row * 8, rt), lanes]
            ys[pl.ds(ti * rt, rt), lanes] = acc
        return carry

    lax.fori_loop(0, CONV_CH // 128, lane_block, 0)
    y = ys[...] + b_ref[...]
    o_ref[...] = _silu(_layer_norm(y, lng_ref[...], lnb_ref[...]))


def _conv_call(geom, z, conv_dw, conv_db, ln_g, ln_b):
    t, c = geom.t, geom.c
    rb = geom.row_block
    hb = c // CONV_HALO
    last = geom.nt // CONV_HALO - 1
    prev = lambda b, r: jnp.maximum(rb(b, r) * hb - 1, 0)
    nxt = lambda b, r: jnp.minimum((rb(b, r) + 1) * hb, last)
    w = jnp.zeros((32, CONV_CH), F32).at[:CONV_K].set(conv_dw)
    vec = lambda v: v.reshape(1, CONV_CH)
    cst = pl.BlockSpec((1, CONV_CH), lambda b, r: (0, 0))
    in_specs = [
        pl.BlockSpec((c, CONV_CH), lambda b, r: (rb(b, r), 0)),
        pl.BlockSpec((c, CONV_CH), lambda b, r: (rb(b, r), 1)),
        pl.BlockSpec((CONV_HALO, CONV_CH), lambda b, r: (prev(b, r), 0)),
        pl.BlockSpec((CONV_HALO, CONV_CH), lambda b, r: (prev(b, r), 1)),
        pl.BlockSpec((CONV_HALO, CONV_CH), lambda b, r: (nxt(b, r), 0)),
        pl.BlockSpec((CONV_HALO, CONV_CH), lambda b, r: (nxt(b, r), 1)),
        pl.BlockSpec((32, CONV_CH), lambda b, r: (0, 0)),
        cst, cst, cst,
    ]
    return pl.pallas_call(
        functools.partial(_conv_kernel, t=t, c=c),
        grid=(geom.b, geom.lat_blocks + 1),
        in_specs=in_specs,
        out_specs=pl.BlockSpec((c, CONV_CH), lambda b, r: (rb(b, r), 0)),
        out_shape=jax.ShapeDtypeStruct((geom.nt, CONV_CH), F32),
        scratch_shapes=[pltpu.VMEM((c + 2 * CONV_HALO, CONV_CH), F32), pltpu.VMEM((c, CONV_CH), F32),
                        pltpu.VMEM((7, c + 2 * CONV_HALO, CONV_CH), F32)],
        compiler_params=_params(2, 32),
        name="conformer_conv",
    )(z, z, z, z, z, z, w, vec(conv_db), vec(ln_g), vec(ln_b))


def _mix_kernel(ret_ref, att_ref, cv_ref, gr_ref, ga_ref, gc_ref, x_ref, g1_ref, sh2_ref, sc2_ref,
                wr_ref, wa_ref, wc_ref, wo_ref, lng_ref, lnb_ref, x1_ref, h2_ref, *, alpha):
    def proj(v_ref, w_ref):
        return jnp.dot(v_ref[...].astype(BF16), w_ref[...], preferred_element_type=F32)

    gate = lambda g_ref: jax.nn.sigmoid(g_ref[...].astype(F32))
    merged = (gate(gr_ref) * proj(ret_ref, wr_ref)
              + gate(ga_ref) * proj(att_ref, wa_ref)
              + gate(gc_ref) * proj(cv_ref, wc_ref))
    y = jnp.dot(merged.astype(BF16), wo_ref[...], preferred_element_type=F32)
    x1 = _layer_norm(alpha * x_ref[...] + g1_ref[0] * y, lng_ref[...], lnb_ref[...])
    x1_ref[...] = x1
    h2_ref[...] = x1 * (1.0 + sc2_ref[0]) + sh2_ref[0]


def _mix_call(geom, alpha, ret, att, cv, z, x, mods, w_ret_o, w_att_o, w_conv_o, w_out, ln_g, ln_b):
    tm = geom.c
    d = D_MODEL
    tile = pl.BlockSpec((tm, d), lambda i: (i, 0))
    gate = lambda k: pl.BlockSpec((tm, d), lambda i: (i, COL_GT // d + k))
    wsp = pl.BlockSpec((d, d), lambda i: (0, 0))
    vsp = pl.BlockSpec((1, d), lambda i: (0, 0))
    return pl.pallas_call(
        functools.partial(_mix_kernel, alpha=alpha),
        grid=(geom.nt // tm,),
        in_specs=[tile, tile, tile, gate(0), gate(1), gate(2), tile,
                  _mod_spec(geom, tm, 2), _mod_spec(geom, tm, 3), _mod_spec(geom, tm, 4),
                  wsp, wsp, wsp, wsp, vsp, vsp],
        out_specs=[tile, tile],
        out_shape=[jax.ShapeDtypeStruct((geom.nt, d), F32)] * 2,
        compiler_params=_params(1, 48),
        name="merge_ln1",
    )(ret, att, cv, z, z, z, x, mods, mods, mods, w_ret_o, w_att_o, w_conv_o, w_out,
      ln_g.reshape(1, d), ln_b.reshape(1, d))


def _router_kernel(h_ref, wr_ref, bias_ref, e_ref, w_ref, pos_ref, cnt_ref, hist_ref, cnt):
    i = pl.program_id(0)
    tm = h_ref.shape[0]
    ne, per = N_EXPERTS, N_EXPERTS // N_GROUPS
    neg = -jnp.inf

    @pl.when(i == 0)
    def _init():
        cnt[...] = jnp.zeros(cnt.shape, F32)

    logits = jnp.dot(h_ref[...], wr_ref[...], preferred_element_type=F32, precision=HIGHEST)
    scores = jax.nn.sigmoid(logits.T[:ne])
    sel = scores + bias_ref[...]

    member = lax.broadcasted_iota(jnp.int32, (per, tm), 0)
    grp_rows = []
    for g in range(N_GROUPS):
        blk = sel[g * per:(g + 1) * per]
        m1 = jnp.max(blk, axis=0, keepdims=True)
        first = jnp.min(jnp.where(blk == m1, member, per), axis=0, keepdims=True)
        m2 = jnp.max(jnp.where(member == first, neg, blk), axis=0, keepdims=True)
        grp_rows.append(m1 + m2)
    gs = jnp.concatenate(grp_rows, axis=0)

    gidx = lax.broadcasted_iota(jnp.int32, (N_GROUPS, tm), 0)
    rank = jnp.zeros((N_GROUPS, tm), jnp.int32)
    for g in range(N_GROUPS):
        row = gs[g:g + 1]
        ahead = jnp.logical_or(row > gs, jnp.logical_and(row == gs, g < gidx))
        rank = rank + ahead.astype(jnp.int32)
    keep = (rank < TOPK_GROUPS).astype(F32)
    keep_e = jnp.concatenate([jnp.broadcast_to(keep[g:g + 1], (per, tm)) for g in range(N_GROUPS)], axis=0)
    cand = jnp.where(keep_e > 0.5, sel, neg)

    eidx = lax.broadcasted_iota(jnp.int32, (ne, tm), 0)
    picks, gates, hots = [], [], []
    chosen = jnp.zeros((ne, tm), F32)
    for _ in range(TOP_K):
        m = jnp.max(cand, axis=0, keepdims=True)
        idx = jnp.min(jnp.where(cand == m, eidx, ne), axis=0, keepdims=True)
        hot = eidx == idx
        picks.append(idx)
        gates.append(jnp.sum(jnp.where(hot, scores, 0.0), axis=0, keepdims=True))
        hots.append(hot)
        chosen = jnp.where(hot, 1.0, chosen)
        cand = jnp.where(hot, neg, cand)
    total = gates[0]
    for gk in gates[1:]:
        total = total + gk

    ti = lax.broadcasted_iota(jnp.int32, (tm, tm), 0)
    tj = lax.broadcasted_iota(jnp.int32, (tm, tm), 1)
    before = jnp.where(ti < tj, 1.0, 0.0).astype(BF16)
    prior = jnp.dot(chosen.astype(BF16), before, preferred_element_type=F32) + cnt[...][:, :1]
    pos = [jnp.sum(jnp.where(hot, prior, 0.0), axis=0, keepdims=True) for hot in hots]

    e_ref[...] = jnp.concatenate(picks, axis=0)
    w_ref[...] = jnp.concatenate([ROUTED_SCALE * gk / total for gk in gates], axis=0)
    pos_ref[...] = jnp.concatenate(pos, axis=0).astype(jnp.int32)
    hist_ref[0] = cnt[...]
    cnt[...] = cnt[...] + jnp.sum(chosen, axis=1, keepdims=True)
    cnt_ref[...] = cnt[...]


def _router_call(geom, h2, w_router, router_bias):
    tm = geom.c
    wr = jnp.zeros((D_MODEL, 128), F32).at[:, :N_EXPERTS].set(w_router)
    tok = pl.BlockSpec((TOP_K, tm), lambda i: (0, i))
    return pl.pallas_call(
        _router_kernel,
        grid=(geom.nt // tm,),
        in_specs=[pl.BlockSpec((tm, D_MODEL), lambda i: (i, 0)),
                  pl.BlockSpec((D_MODEL, 128), lambda i: (0, 0)),
                  pl.BlockSpec((N_EXPERTS, 1), lambda i: (0, 0))],
        out_specs=[tok, tok, tok, pl.BlockSpec((N_EXPERTS, 128), lambda i: (0, 0)),
                   pl.BlockSpec((1, N_EXPERTS, 128), lambda i: (i, 0, 0))],
        out_shape=[jax.ShapeDtypeStruct((TOP_K, geom.nt), jnp.int32),
                   jax.ShapeDtypeStruct((TOP_K, geom.nt), F32),
                   jax.ShapeDtypeStruct((TOP_K, geom.nt), jnp.int32),
                   jax.ShapeDtypeStruct((N_EXPERTS, 128), F32),
                   jax.ShapeDtypeStruct((geom.nt // tm, N_EXPERTS, 128), F32)],
        scratch_shapes=[pltpu.VMEM((N_EXPERTS, 128), F32)],
        compiler_params=_params(1, 32),
        name="moe_router",
    )(h2, wr, router_bias.reshape(N_EXPERTS, 1))


HALF = D_MODEL // 2


def _pack_bf16_pairs(v):
    lo = pltpu.bitcast(v[:, :HALF].astype(BF16).astype(F32), jnp.uint32)
    hi = pltpu.bitcast(v[:, HALF:].astype(BF16).astype(F32), jnp.uint32)
    return jnp.bitwise_or(jnp.right_shift(lo, jnp.uint32(16)), hi)


def _unpack_bf16_pairs(w):
    lo = pltpu.bitcast(jnp.left_shift(w, jnp.uint32(16)), F32)
    hi = pltpu.bitcast(jnp.bitwise_and(w, jnp.uint32(0xFFFF0000)), F32)
    return lo, hi


RUN_ALIGN = 8
SORTED_ROWS = 256 * TOP_K + N_EXPERTS * RUN_ALIGN
RUN_BITS = tuple(range(8, 2, -1))


def _for_each_run_piece(n_ref, src_ref, tile, visit):
    def per_expert(e, off):
        n = n_ref[tile * N_EXPERTS + e]
        src = src_ref[tile * N_EXPERTS + e]
        for lb in RUN_BITS:
            done = (n >> (lb + 1)) << (lb + 1)

            @pl.when((n & (1 << lb)) != 0)
            def _piece():
                visit(pl.multiple_of(off + done, RUN_ALIGN), pl.multiple_of(src + done, RUN_ALIGN), 1 << lb)

        return off + n

    lax.fori_loop(0, N_EXPERTS, per_expert, 0)


TOTAL_BITS = tuple(range(11, 2, -1))


def _wait_rows(total, wait_piece):
    for lb in TOTAL_BITS:
        @pl.when((total & (1 << lb)) != 0)
        def _amount():
            wait_piece(1 << lb)


def _dispatch_kernel(last_ref, n_ref, src_ref, tot_ref, sidx_ref, h_ref, xs_out, packed, zblk, sem, zsem):
    tm = h_ref.shape[0]
    i = pl.program_id(0)

    @pl.when(pl.program_id(0) == 0)
    def _zero_tail_blocks():
        zblk[...] = jnp.zeros(zblk.shape, zblk.dtype)

        def zero_copy(e):
            return pltpu.make_async_copy(zblk, xs_out.at[pl.ds(last_ref[e] * MOE_BLOCK, MOE_BLOCK)], zsem)

        def start(e, carry):
            zero_copy(e).start()
            return carry

        def wait(e, carry):
            zero_copy(e).wait()
            return carry

        lax.fori_loop(0, N_EXPERTS, start, 0)
        lax.fori_loop(0, N_EXPERTS, wait, 0)

    rows = lax.broadcasted_iota(jnp.int32, (SORTED_ROWS, tm), 0)
    pick = jnp.zeros((SORTED_ROWS, tm), F32)
    for k in range(TOP_K):
        pick = jnp.where(rows == sidx_ref[k:k + 1, :], 1.0, pick)
    sorted_rows = jnp.dot(pick.astype(BF16), h_ref[...].astype(BF16), preferred_element_type=F32)
    lo = pltpu.bitcast(sorted_rows[:, :HALF], jnp.uint32)
    hi = pltpu.bitcast(sorted_rows[:, HALF:], jnp.uint32)
    packed[...] = jnp.bitwise_or(jnp.right_shift(lo, jnp.uint32(16)), hi)

    def piece(sorted_row, slot_row, rows_):
        return pltpu.make_async_copy(packed.at[pl.ds(sorted_row, rows_)], xs_out.at[pl.ds(slot_row, rows_)], sem)

    _for_each_run_piece(n_ref, src_ref, i, lambda a, b, r: piece(a, b, r).start())
    _wait_rows(tot_ref[i], lambda r: piece(0, 0, r).wait())


def _dispatch_call(geom, last_block, run_rows, run_slot, tile_rows, sidx, h2, n_blocks):
    tm = geom.c
    assert tm * TOP_K + N_EXPERTS * RUN_ALIGN == SORTED_ROWS
    grid_spec = pltpu.PrefetchScalarGridSpec(
        num_scalar_prefetch=4,
        grid=(geom.nt // tm,),
        in_specs=[pl.BlockSpec((TOP_K, tm), lambda i, *_: (0, i)),
                  pl.BlockSpec((tm, D_MODEL), lambda i, *_: (i, 0))],
        out_specs=pl.BlockSpec(memory_space=pl.ANY),
        scratch_shapes=[pltpu.VMEM((SORTED_ROWS, HALF), jnp.uint32), pltpu.VMEM((MOE_BLOCK, HALF), jnp.uint32),
                        pltpu.SemaphoreType.DMA(()), pltpu.SemaphoreType.DMA(())],
    )
    return pl.pallas_call(
        _dispatch_kernel,
        grid_spec=grid_spec,
        out_shape=jax.ShapeDtypeStruct(((n_blocks + 1) * MOE_BLOCK, HALF), jnp.uint32),
        compiler_params=_params(1, 48),
        name="moe_dispatch",
    )(last_block, run_rows, run_slot, tile_rows, sidx, h2)


def _expert_kernel(be_ref, nu_ref, x_ref, wgu_ref, wd_ref, o_ref):
    del be_ref

    @pl.when(pl.program_id(0) < nu_ref[0])
    def _run():
        lo, hi = _unpack_bf16_pairs(x_ref[...])
        x = jnp.concatenate([lo, hi], axis=1).astype(BF16)
        hgu = jnp.dot(x, wgu_ref[0], preferred_element_type=F32)
        hid = _silu(hgu[:, :D_EXPERT]) * hgu[:, D_EXPERT:]
        o_ref[...] = _pack_bf16_pairs(jnp.dot(hid.astype(BF16), wd_ref[0], preferred_element_type=F32))


def _expert_call(block_e, n_used, xs, w_gu, w_down):
    n_blocks = xs.shape[0] // MOE_BLOCK - 1
    live = lambda i, be, nu: jnp.minimum(i, nu[0] - 1)
    grid_spec = pltpu.PrefetchScalarGridSpec(
        num_scalar_prefetch=2,
        grid=(n_blocks,),
        in_specs=[pl.BlockSpec((MOE_BLOCK, HALF), lambda i, be, nu: (live(i, be, nu), 0)),
                  pl.BlockSpec((1, D_MODEL, 2 * D_EXPERT), lambda i, be, nu: (be[live(i, be, nu)], 0, 0)),
                  pl.BlockSpec((1, D_EXPERT, D_MODEL), lambda i, be, nu: (be[live(i, be, nu)], 0, 0))],
        out_specs=pl.BlockSpec((MOE_BLOCK, HALF), lambda i, be, nu: (live(i, be, nu), 0)),
    )
    return pl.pallas_call(
        _expert_kernel,
        grid_spec=grid_spec,
        out_shape=jax.ShapeDtypeStruct(xs.shape, jnp.uint32),
        compiler_params=_params(1, 32),
        name="moe_experts",
    )(block_e, n_used, xs, w_gu, w_down)


def _combine_kernel(n_ref, src_ref, tot_ref, sidx_ref, wt_ref, ys_hbm, h_ref, x_ref, g2_ref, wgu_ref, wd_ref,
                    lng_ref, lnb_ref, o_ref, buf_a, buf_b, sem, *, alpha):
    i = pl.program_id(0)
    n = pl.num_programs(0)
    tm = h_ref.shape[0]
    even = i % 2 == 0

    def piece(buf, slot, sorted_row, slot_row, rows_):
        return pltpu.make_async_copy(ys_hbm.at[pl.ds(slot_row, rows_)], buf.at[pl.ds(sorted_row, rows_)], sem.at[slot])

    def gather(tile, buf, slot):
        _for_each_run_piece(n_ref, src_ref, tile, lambda a, b, r: piece(buf, slot, a, b, r).start())

    def drain(tile, buf, slot):
        _wait_rows(tot_ref[tile], lambda r: piece(buf, slot, 0, 0, r).wait())

    def finish(buf):
        wt = wt_ref[...]
        cols = lax.broadcasted_iota(jnp.int32, (tm, SORTED_ROWS), 1)
        mix = jnp.zeros((tm, SORTED_ROWS), F32)
        for k in range(TOP_K):
            mix = jnp.where(cols == sidx_ref[:, k:k + 1], wt[:, k:k + 1], mix)
        mix = mix.astype(BF16)
        lo, hi = _unpack_bf16_pairs(buf[...])
        routed = jnp.concatenate([jnp.dot(mix, lo.astype(BF16), preferred_element_type=F32),
                                  jnp.dot(mix, hi.astype(BF16), preferred_element_type=F32)], axis=1)
        hgu = jnp.dot(h_ref[...].astype(BF16), wgu_ref[...], preferred_element_type=F32)
        hid = _silu(hgu[:, :D_SHARED]) * hgu[:, D_SHARED:]
        shared = jnp.dot(hid.astype(BF16), wd_ref[...], preferred_element_type=F32)
        o_ref[...] = _layer_norm(alpha * x_ref[...] + g2_ref[0] * (routed + shared), lng_ref[...], lnb_ref[...])

    @pl.when(i == 0)
    def _first():
        buf_a[...] = jnp.zeros(buf_a.shape, buf_a.dtype)
        buf_b[...] = jnp.zeros(buf_b.shape, buf_b.dtype)
        gather(i, buf_a, 0)

    @pl.when(jnp.logical_and(even, i + 1 < n))
    def _ahead_b():
        gather(i + 1, buf_b, 1)

    @pl.when(jnp.logical_and(jnp.logical_not(even), i + 1 < n))
    def _ahead_a():
        gather(i + 1, buf_a, 0)

    @pl.when(even)
    def _finish_a():
        drain(i, buf_a, 0)
        finish(buf_a)

    @pl.when(jnp.logical_not(even))
    def _finish_b():
        drain(i, buf_b, 1)
        finish(buf_b)


def _combine_call(geom, alpha, run_rows, run_slot, tile_rows, sidx_tok, w_tok, ys, h2, x1, mods, w_sh_gu, w_sh_down, ln_g, ln_b):
    tm = geom.c
    d = D_MODEL
    n = geom.nt // tm
    tile = pl.BlockSpec((tm, d), lambda i, *_: (i, 0))
    vsp = pl.BlockSpec((1, d), lambda i, *_: (0, 0))
    per_tok = pl.BlockSpec((tm, TOP_K), lambda i, *_: (i, 0))
    grid_spec = pltpu.PrefetchScalarGridSpec(
        num_scalar_prefetch=3,
        grid=(n,),
        in_specs=[per_tok, per_tok,
                  pl.BlockSpec(memory_space=pl.ANY),
                  tile, tile, _mod_spec(geom, tm, 5),
                  pl.BlockSpec((d, 2 * D_SHARED), lambda i, *_: (0, 0)),
                  pl.BlockSpec((D_SHARED, d), lambda i, *_: (0, 0)),
                  vsp, vsp],
        out_specs=tile,
        scratch_shapes=[pltpu.VMEM((SORTED_ROWS, HALF), jnp.uint32), pltpu.VMEM((SORTED_ROWS, HALF), jnp.uint32),
                        pltpu.SemaphoreType.DMA((2,))],
    )
    return pl.pallas_call(
        functools.partial(_combine_kernel, alpha=alpha),
        grid_spec=grid_spec,
        out_shape=jax.ShapeDtypeStruct((geom.nt, d), F32),
        compiler_params=_params(1, 56),
        name="moe_combine_ln2",
    )(run_rows, run_slot, tile_rows, sidx_tok, w_tok, ys, h2, x1, mods, w_sh_gu, w_sh_down,
      ln_g.reshape(1, d), ln_b.reshape(1, d))


def _rope_tables(t):
    rows = t // GRID_W
    row = jnp.repeat(jnp.arange(rows, dtype=F32), GRID_W)
    col = jnp.tile(jnp.arange(GRID_W, dtype=F32), rows)
    n_freq = ATT_DH // 4
    inv_freq = ROPE_THETA ** (-jnp.arange(n_freq, dtype=F32) / n_freq)
    ang = jnp.concatenate([row[:, None] * inv_freq, col[:, None] * inv_freq], axis=-1)
    cos, sin = jnp.cos(ang), jnp.sin(ang)
    cos64 = jnp.concatenate([cos, cos], axis=-1)
    sin64 = jnp.concatenate([-sin, sin], axis=-1)
    return cos64, sin64


def kernel(x, c, ctx, c_ctx, w_ada, b_ada, w_in, ret_decay_logit, att_q_norm, att_k_norm, conv_dw, conv_db, conv_ln_g, conv_ln_b, w_ret_o, w_att_o, w_conv_o, w_out, ln1_g, ln1_b, w_router, router_bias, w_exp_gate, w_exp_up, w_exp_down, w_sh_gate, w_sh_up, w_sh_down, ln2_g, ln2_b):
    b, t, d = x.shape
    n_ctx = ctx.shape[1]
    depth = w_ada.shape[0]
    assert d == D_MODEL and w_in.shape[-1] == D_IN
    geom = _Geom(b, t, n_ctx)
    alpha = float((2 * depth) ** 0.25)

    cos64, sin64 = _rope_tables(t)
    cos128 = jnp.concatenate([cos64, cos64], axis=-1)
    sin128 = jnp.concatenate([sin64, sin64], axis=-1)

    n_rows = -(-(b + 1) // 8) * 8
    cvecs = jnp.zeros((n_rows, d), F32).at[:b].set(c).at[b].set(c_ctx)
    mods_all = _mods_call(cvecs, w_ada, b_ada).reshape(depth, n_rows * 6, 1, d)

    n_tiles = geom.nt // geom.c
    n_blocks = -(-(geom.nt * TOP_K + n_tiles * N_EXPERTS * (RUN_ALIGN - 1)) // MOE_BLOCK) + N_EXPERTS

    xt = jnp.concatenate([x.reshape(geom.nl, d), ctx.reshape(geom.nc, d)], axis=0)
    for l in range(depth):
        mods = mods_all[l]
        w_in_l = _permute_columns(w_in[l]).astype(BF16)
        z = _inproj_call(geom, xt, mods, w_in_l)

        log_gamma = jax.nn.log_sigmoid(ret_decay_logit[l].astype(F32))
        ret = _retention_call(geom, z, log_gamma, cos128, sin128)
        att = _attention_call(geom, z, att_q_norm[l], att_k_norm[l], cos128, sin128)
        cv = _conv_call(geom, z, conv_dw[l], conv_db[l], conv_ln_g[l], conv_ln_b[l])
        x1, h2 = _mix_call(geom, alpha, ret, att, cv, z, xt, mods,
                           w_ret_o[l].astype(BF16), w_att_o[l].astype(BF16), w_conv_o[l].astype(BF16),
                           w_out[l].astype(BF16), ln1_g[l], ln1_b[l])

        top_e, gate_w, pos, counts, cnt_hist = _router_call(geom, h2, w_router[l], router_bias[l])
        before = cnt_hist[:, :, 0].astype(jnp.int32)
        total = counts[:, 0].astype(jnp.int32)
        tile_n = jnp.concatenate([before[1:], total[None, :]], axis=0) - before
        run_rows = (tile_n + RUN_ALIGN - 1) // RUN_ALIGN * RUN_ALIGN
        run_before = jnp.cumsum(run_rows, axis=0) - run_rows
        blocks_e = (jnp.sum(run_rows, axis=0) + MOE_BLOCK - 1) // MOE_BLOCK
        blocks_end = jnp.cumsum(blocks_e)
        start_row = (blocks_end - blocks_e) * MOE_BLOCK
        run_slot = start_row[None, :] + run_before
        run_sorted = jnp.cumsum(run_rows, axis=1) - run_rows
        onehot = top_e[:, :, None] == jnp.arange(N_EXPERTS, dtype=jnp.int32)[None, None, :]
        per_token = lambda table: jnp.sum(jnp.where(onehot, jnp.repeat(table, geom.c, axis=0)[None], 0), axis=-1)
        sidx = per_token(run_sorted) + pos - per_token(before)
        block_ids = jnp.arange(n_blocks, dtype=jnp.int32)
        block_e = jnp.minimum(jnp.sum((blocks_end[None, :] <= block_ids[:, None]).astype(jnp.int32), axis=1),
                              N_EXPERTS - 1)
        n_used = blocks_end[-1:].astype(jnp.int32)
        last_block = jnp.where(blocks_e > 0, blocks_end - 1, n_blocks).astype(jnp.int32)
        run_rows_flat, run_slot_flat = run_rows.reshape(-1), run_slot.reshape(-1).astype(jnp.int32)
        tile_rows = jnp.sum(run_rows, axis=1)

        xs = _dispatch_call(geom, last_block, run_rows_flat, run_slot_flat, tile_rows, sidx, h2, n_blocks)
        w_gu = jnp.concatenate([w_exp_gate[l], w_exp_up[l]], axis=-1).astype(BF16)
        ys = _expert_call(block_e, n_used, xs, w_gu, w_exp_down[l].astype(BF16))
        w_sh_gu = jnp.concatenate([w_sh_gate[l], w_sh_up[l]], axis=-1).astype(BF16)
        xt = _combine_call(geom, alpha, run_rows_flat, run_slot_flat, tile_rows, sidx.T, gate_w.T, ys, h2, x1, mods, w_sh_gu,
                           w_sh_down[l].astype(BF16), ln2_g[l], ln2_b[l])
    return xt[:geom.nl].reshape(b, t, d)
```

```python
import functools

import jax
import jax.numpy as jnp
from jax import lax
from jax.experimental import pallas as pl
from jax.experimental.pallas import tpu as pltpu

F32 = jnp.float32
BF16 = jnp.bfloat16
HIGHEST = lax.Precision.HIGHEST

D_MODEL = 1024
GRID_W = 64
EPS = 1e-6

RET_HEADS = 8
RET_DK = 64
RET_DV = 128
RET_CHUNK = 128
RET_W = RET_HEADS * RET_DV

ATT_HEADS = 16
ATT_KV_HEADS = 4
ATT_DH = 64
ATT_GROUP = ATT_HEADS // ATT_KV_HEADS
ATT_W = ATT_HEADS * ATT_DH
ROPE_THETA = 10000.0
ATT_KEY_BLOCK = 1024

CONV_CH = 1024
CONV_K = 31
CONV_HALO = 16

N_EXPERTS = 64
TOP_K = 8
N_GROUPS = 8
TOPK_GROUPS = 4
D_EXPERT = 256
D_SHARED = 256
ROUTED_SCALE = 2.5
MOE_BLOCK = 512

_ORIG = dict(rq=0, rk=512, rv=1024, rg=2048, aq=3072, ak=4096, av=4352, cu=4608, gt=6656)
D_IN = 9728
COL_CU = 0
COL_GT = 2048
COL_RG = 5120
COL_RV = 6144
COL_RQ = 7168
COL_RK = 7680
COL_ATT = 8192
ATT_SECTION = ATT_GROUP * ATT_DH + 2 * ATT_DH


def _column_ranges():
    rng = [(_ORIG["cu"], _ORIG["cu"] + 2 * CONV_CH),
           (_ORIG["gt"], _ORIG["gt"] + 3 * D_MODEL),
           (_ORIG["rg"], _ORIG["rg"] + RET_W),
           (_ORIG["rv"], _ORIG["rv"] + RET_W),
           (_ORIG["rq"], _ORIG["rq"] + RET_HEADS * RET_DK),
           (_ORIG["rk"], _ORIG["rk"] + RET_HEADS * RET_DK)]
    for g in range(ATT_KV_HEADS):
        rng.append((_ORIG["aq"] + g * ATT_GROUP * ATT_DH, _ORIG["aq"] + (g + 1) * ATT_GROUP * ATT_DH))
        rng.append((_ORIG["ak"] + g * ATT_DH, _ORIG["ak"] + (g + 1) * ATT_DH))
        rng.append((_ORIG["av"] + g * ATT_DH, _ORIG["av"] + (g + 1) * ATT_DH))
    cols = [c for a, b in rng for c in range(a, b)]
    assert sorted(cols) == list(range(D_IN))
    return rng


def _permute_columns(w):
    return jnp.concatenate([w[:, a:b] for a, b in _column_ranges()], axis=1)


def _params(n_axes, vmem_mib):
    return pltpu.CompilerParams(dimension_semantics=("arbitrary",) * n_axes,
                                vmem_limit_bytes=vmem_mib * 1024 * 1024)


def _silu(v):
    return v * jax.nn.sigmoid(v)


def _layer_norm(v, g, b):
    mu = jnp.mean(v, axis=-1, keepdims=True)
    d = v - mu
    var = jnp.mean(d * d, axis=-1, keepdims=True)
    return d * lax.rsqrt(var + EPS) * g + b


def _mods_kernel(c_ref, w_ref, b_ref, o_ref):
    s = _silu(c_ref[...])
    o_ref[0] = jnp.dot(s, w_ref[0], preferred_element_type=F32, precision=HIGHEST) + b_ref[0]


def _mods_call(cvecs, w_ada, b_ada):
    n_layers = w_ada.shape[0]
    rows, d = cvecs.shape
    return pl.pallas_call(
        _mods_kernel,
        grid=(n_layers, 6),
        in_specs=[pl.BlockSpec((rows, d), lambda l, j: (0, 0)),
                  pl.BlockSpec((1, d, d), lambda l, j: (l, 0, j)),
                  pl.BlockSpec((1, 1, d), lambda l, j: (l, 0, j))],
        out_specs=pl.BlockSpec((1, rows, d), lambda l, j: (l, 0, j)),
        out_shape=jax.ShapeDtypeStruct((n_layers, rows, 6 * d), F32),
        compiler_params=_params(2, 32),
        name="adaln_mods",
    )(cvecs, w_ada, b_ada.reshape(n_layers, 1, 6 * d))


class _Geom:
    def __init__(self, b, t, c):
        assert t % c == 0 and c % RET_CHUNK == 0 and c % CONV_HALO == 0 and t % ATT_KEY_BLOCK == 0
        self.b, self.t, self.c = b, t, c
        self.nl, self.nc = b * t, b * c
        self.nt = self.nl + self.nc
        self.lat_blocks = t // c
        self.nlb = self.nl // c
        self.p = t + c

    def row_block(self, bi, r):
        return jnp.where(r < self.lat_blocks, bi * self.lat_blocks + r, self.nlb + bi)

    def mod_row(self, i, tm):
        return jnp.where(i * tm < self.nl, (i * tm) // self.t, self.b)


def _mod_spec(geom, tm, which, grid_pos=0):
    d = D_MODEL
    if grid_pos == 0:
        return pl.BlockSpec((1, 1, d), lambda i, *_: (geom.mod_row(i, tm) * 6 + which, 0, 0))
    return pl.BlockSpec((1, 1, d), lambda j, i: (geom.mod_row(i, tm) * 6 + which, 0, 0))


def _inproj_kernel(x_ref, sh_ref, sc_ref, w_ref, o_ref):
    h = x_ref[...] * (1.0 + sc_ref[0]) + sh_ref[0]
    o_ref[...] = jnp.dot(h.astype(BF16), w_ref[...], preferred_element_type=F32).astype(o_ref.dtype)


def _inproj_call(geom, x, mods, w_in_bf16):
    tm = 512 if geom.nc % 512 == 0 and geom.t % 512 == 0 else geom.c
    tn = D_IN // 2
    return pl.pallas_call(
        _inproj_kernel,
        grid=(D_IN // tn, geom.nt // tm),
        in_specs=[pl.BlockSpec((tm, D_MODEL), lambda j, i: (i, 0)),
                  _mod_spec(geom, tm, 0, grid_pos=1),
                  _mod_spec(geom, tm, 1, grid_pos=1),
                  pl.BlockSpec((D_MODEL, tn), lambda j, i: (0, j))],
        out_specs=pl.BlockSpec((tm, tn), lambda j, i: (i, j)),
        out_shape=jax.ShapeDtypeStruct((geom.nt, D_IN), BF16),
        compiler_params=_params(2, 48),
        name="in_proj",
    )(x, mods, mods, w_in_bf16)


def _rot_half_128(v):
    lane = lax.broadcasted_iota(jnp.int32, v.shape, 1)
    return jnp.where((lane % 64) < 32, pltpu.roll(v, 96, 1), pltpu.roll(v, 32, 1))


def _ret_kernel(lg_ref, ql_ref, qc_ref, kl_ref, kc_ref, vl_ref, vc_ref, g_ref, cos_ref, sin_ref, o_ref,
                qs, kts, yf, yb, st, dm, qwb, kwb, gcs, *, t, c):
    ch = RET_CHUNK
    hp = pl.program_id(1)
    r = pl.program_id(2)
    lat_blocks = t // c
    n_lat, n_ctx = t // ch, c // ch

    @pl.when(r == 0)
    def _scan():
        ri = lax.broadcasted_iota(jnp.int32, (ch, ch), 0).astype(F32)
        ci = lax.broadcasted_iota(jnp.int32, (ch, ch), 1).astype(F32)
        for d in range(2):
            for h in range(2):
                u = 2 * d + h
                lg = lg_ref[d, 2 * hp + h]
                rel = (ri - ci) if d == 0 else (ci - ri)
                dm[u] = jnp.where(rel >= 0.0, jnp.exp(lg * jnp.maximum(rel, 0.0)), 0.0)
                qwb[u] = jnp.exp(lg * ((ri + 1.0) if d == 0 else (float(ch) - ri)))
                kwb[u] = jnp.exp(lg * ((float(ch) - 1.0 - ri) if d == 0 else ri))
                gcs[u] = jnp.exp(jnp.full((RET_DK, RET_DV), lg * float(ch), F32))
                st[u] = jnp.zeros((RET_DK, RET_DV), F32)

        def stage(q, k, seq_rows):
            qs[0, seq_rows, :] = q[:, :RET_DK].astype(BF16)
            qs[1, seq_rows, :] = q[:, RET_DK:].astype(BF16)
            kt = k.T
            kts[0, :, seq_rows] = kt[:RET_DK].astype(BF16)
            kts[1, :, seq_rows] = kt[RET_DK:].astype(BF16)

        kscale = RET_DK ** -0.5
        for cc in range(n_ctx):
            rows = pl.ds(cc * ch, ch)
            stage(qc_ref[rows, :].astype(F32), kc_ref[rows, :].astype(F32) * kscale, rows)

        def stage_lat(cc, carry):
            rows = pl.ds(pl.multiple_of(cc * ch, ch), ch)
            cs, sn = cos_ref[rows, :], sin_ref[rows, :]
            q = ql_ref[rows, :].astype(F32)
            k = kl_ref[rows, :].astype(F32)
            q = q * cs + _rot_half_128(q) * sn
            k = (k * cs + _rot_half_128(k) * sn) * kscale
            stage(q, k, pl.ds(pl.multiple_of(c + cc * ch, ch), ch))
            return carry

        lax.fori_loop(0, n_lat, stage_lat, 0)

        def run_segment(v_ref, seq_off, n):
            def body(i, carry):
                for d, cc in ((0, i), (1, n - 1 - i)):
                    vrows = pl.ds(pl.multiple_of(cc * ch, ch), ch)
                    srows = pl.ds(pl.multiple_of(seq_off + cc * ch, ch), ch)
                    for h in range(2):
                        u = 2 * d + h
                        q = qs[h, srows, :]
                        kt = kts[h, :, srows]
                        v = v_ref[vrows, h * RET_DV:(h + 1) * RET_DV].astype(F32)
                        s = jnp.dot(q, kt, preferred_element_type=F32)
                        y = jnp.dot((s * dm[u]).astype(BF16), v.astype(BF16), preferred_element_type=F32)
                        state = st[u]
                        y = y + jnp.dot(q, state.astype(BF16), preferred_element_type=F32) * qwb[u]
                        dst = yf if d == 0 else yb
                        dst[srows, h * RET_DV:(h + 1) * RET_DV] = y
                        kv = jnp.dot(kt, (v * kwb[u]).astype(BF16), preferred_element_type=F32)
                        st[u] = gcs[u] * state + kv
                return carry

            lax.fori_loop(0, n, body, 0, unroll=2)

        run_segment(vc_ref, 0, n_ctx)
        run_segment(vl_ref, c, n_lat)

    def finish(srows):
        y = yf[srows, :] + yb[srows, :]
        for h in range(2):
            cols = slice(h * RET_DV, (h + 1) * RET_DV)
            yh = y[:, cols]
            mu = jnp.mean(yh, axis=-1, keepdims=True)
            dlt = yh - mu
            var = jnp.mean(dlt * dlt, axis=-1, keepdims=True)
            out = _silu(g_ref[:, cols].astype(F32)) * (dlt * lax.rsqrt(var + EPS))
            o_ref[:, cols] = out.astype(o_ref.dtype)

    @pl.when(r < lat_blocks)
    def _fin_lat():
        finish(pl.ds(pl.multiple_of(c + r * c, c), c))

    @pl.when(r == lat_blocks)
    def _fin_ctx():
        finish(pl.ds(0, c))


def _retention_call(geom, z, log_gamma, cos128, sin128):
    t, c, p = geom.t, geom.c, geom.p
    hpairs = RET_HEADS // 2
    qb, kb = COL_RQ // 128, COL_RK // 128
    vb, gb = COL_RV // 256, COL_RG // 256
    rb = geom.row_block
    in_specs = [
        pl.BlockSpec(memory_space=pltpu.SMEM),
        pl.BlockSpec((t, 128), lambda b, h, r: (b, qb + h)),
        pl.BlockSpec((c, 128), lambda b, h, r: (geom.nlb + b, qb + h)),
        pl.BlockSpec((t, 128), lambda b, h, r: (b, kb + h)),
        pl.BlockSpec((c, 128), lambda b, h, r: (geom.nlb + b, kb + h)),
        pl.BlockSpec((t, 256), lambda b, h, r: (b, vb + h)),
        pl.BlockSpec((c, 256), lambda b, h, r: (geom.nlb + b, vb + h)),
        pl.BlockSpec((c, 256), lambda b, h, r: (rb(b, r), gb + h)),
        pl.BlockSpec((t, 128), lambda b, h, r: (0, 0)),
        pl.BlockSpec((t, 128), lambda b, h, r: (0, 0)),
    ]
    scratch = [
        pltpu.VMEM((2, p, RET_DK), BF16),
        pltpu.VMEM((2, RET_DK, p), BF16),
        pltpu.VMEM((p, 2 * RET_DV), F32),
        pltpu.VMEM((p, 2 * RET_DV), F32),
        pltpu.VMEM((4, RET_DK, RET_DV), F32),
        pltpu.VMEM((4, RET_CHUNK, RET_CHUNK), F32),
        pltpu.VMEM((4, RET_CHUNK, RET_CHUNK), F32),
        pltpu.VMEM((4, RET_CHUNK, RET_CHUNK), F32),
        pltpu.VMEM((4, RET_DK, RET_DV), F32),
    ]
    return pl.pallas_call(
        functools.partial(_ret_kernel, t=t, c=c),
        grid=(geom.b, hpairs, geom.lat_blocks + 1),
        in_specs=in_specs,
        out_specs=pl.BlockSpec((c, 256), lambda b, h, r: (rb(b, r), h)),
        out_shape=jax.ShapeDtypeStruct((geom.nt, RET_W), BF16),
        scratch_shapes=scratch,
        compiler_params=_params(3, 56),
        name="retention",
    )(log_gamma, z, z, z, z, z, z, z, cos128, sin128)


def _rms_heads_128(v, g):
    li = lax.broadcasted_iota(jnp.int32, (128, 128), 0) // ATT_DH
    lj = lax.broadcasted_iota(jnp.int32, (128, 128), 1) // ATT_DH
    avg = jnp.where(li == lj, 1.0 / ATT_DH, 0.0).astype(BF16)
    sq = v * v
    hi = sq.astype(BF16)
    lo = (sq - hi.astype(F32)).astype(BF16)
    ms = jnp.dot(hi, avg, preferred_element_type=F32) + jnp.dot(lo, avg, preferred_element_type=F32)
    return v * lax.rsqrt(ms + EPS) * g


def _att_kernel(qa_ref, qb_ref, kvl_ref, kvc_ref, qn_ref, kn_ref, cos_ref, sin_ref, o_ref,
                kts, vs, m_s, acc_s, *, t, c):
    r = pl.program_id(2)
    lat_blocks = t // c
    dh = ATT_DH
    tk = ATT_KEY_BLOCK
    lane = lax.broadcasted_iota(jnp.int32, (c, 2 * dh), 1)

    def stage_tile(kv, dst, cs, sn):
        k = _rms_heads_128(kv, kn_ref[...])
        if cs is not None:
            k = k * cs + _rot_half_128(k) * sn
        kts[:, dst] = k.T[:dh].astype(BF16)
        vs[dst, :] = jnp.where(lane < dh, pltpu.roll(kv, dh, 1), 1.0).astype(BF16)

    @pl.when(r == 0)
    def _stage_kv():
        stage_tile(kvc_ref[...].astype(F32), pl.ds(0, c), None, None)

        def stage(i, carry):
            rows = pl.ds(pl.multiple_of(i * c, c), c)
            stage_tile(kvl_ref[rows, :].astype(F32), pl.ds(pl.multiple_of(c + i * c, c), c),
                       cos_ref[rows, :], sin_ref[rows, :])
            return carry

        lax.fori_loop(0, lat_blocks, stage, 0)

    is_ctx = r == lat_blocks
    rows = pl.ds(pl.multiple_of(jnp.minimum(r, lat_blocks - 1) * c, c), c)
    cs, sn = cos_ref[rows, :], sin_ref[rows, :]
    q_heads = []
    for src in (qa_ref, qb_ref):
        xn = _rms_heads_128(src[...].astype(F32), qn_ref[...])
        xr = jnp.where(is_ctx, xn, xn * cs + _rot_half_128(xn) * sn) * (dh ** -0.5)
        q_heads.append(xr[:, :dh].astype(BF16))
        q_heads.append(pltpu.roll(xr, dh, 1)[:, :dh].astype(BF16))
    q = jnp.concatenate(q_heads, axis=0)

    m_s[...] = jnp.full(m_s.shape, -jnp.inf, F32)
    acc_s[...] = jnp.zeros(acc_s.shape, F32)

    def flash_step(kt, v):
        n = kt.shape[1]
        s = jnp.dot(q, kt, preferred_element_type=F32)
        m_prev = m_s[...]
        m_next = jnp.maximum(m_prev, jnp.max(s, axis=1, keepdims=True))
        prob = jnp.exp(s - jnp.concatenate([m_next] * (n // 128), axis=1))
        acc_s[...] = acc_s[...] * jnp.exp(m_prev - m_next) + jnp.dot(prob.astype(BF16), v, preferred_element_type=F32)
        m_s[...] = m_next

    flash_step(kts[:, 0:c], vs[0:c, :])

    @pl.when(jnp.logical_not(is_ctx))
    def _latent_keys():
        def lat_step(j, carry):
            krows = pl.ds(pl.multiple_of(c + j * tk, 128), tk)
            flash_step(kts[:, krows], vs[krows, :])
            return carry

        lax.fori_loop(0, t // tk, lat_step, 0, unroll=2)

    outs = []
    for h in range(ATT_GROUP):
        acc = acc_s[h * c:(h + 1) * c, :]
        outs.append(acc * pltpu.roll(1.0 / acc, dh, 1))
    for pair in range(ATT_GROUP // 2):
        both = jnp.where(lane < dh, outs[2 * pair], pltpu.roll(outs[2 * pair + 1], dh, 1))
        o_ref[:, pair * 2 * dh:(pair + 1) * 2 * dh] = both.astype(o_ref.dtype)


def _attention_call(geom, z, q_norm, k_norm, cos128, sin128):
    t, c, p = geom.t, geom.c, geom.p
    ab = COL_ATT // 128
    sec = ATT_SECTION // 128
    rb = geom.row_block
    in_specs = [
        pl.BlockSpec((c, 128), lambda b, g, r: (rb(b, r), ab + sec * g)),
        pl.BlockSpec((c, 128), lambda b, g, r: (rb(b, r), ab + sec * g + 1)),
        pl.BlockSpec((t, 128), lambda b, g, r: (b, ab + sec * g + 2)),
        pl.BlockSpec((c, 128), lambda b, g, r: (geom.nlb + b, ab + sec * g + 2)),
        pl.BlockSpec((1, 128), lambda b, g, r: (0, 0)),
        pl.BlockSpec((1, 128), lambda b, g, r: (0, 0)),
        pl.BlockSpec((t, 128), lambda b, g, r: (0, 0)),
        pl.BlockSpec((t, 128), lambda b, g, r: (0, 0)),
    ]
    two_heads = lambda v: jnp.tile(v.reshape(1, ATT_DH), (1, 2))
    scratch = [
        pltpu.VMEM((ATT_DH, p), BF16),
        pltpu.VMEM((p, 2 * ATT_DH), BF16),
        pltpu.VMEM((ATT_GROUP * c, 128), F32),
        pltpu.VMEM((ATT_GROUP * c, 2 * ATT_DH), F32),
    ]
    return pl.pallas_call(
        functools.partial(_att_kernel, t=t, c=c),
        grid=(geom.b, ATT_KV_HEADS, geom.lat_blocks + 1),
        in_specs=in_specs,
        out_specs=pl.BlockSpec((c, ATT_GROUP * ATT_DH), lambda b, g, r: (rb(b, r), g)),
        out_shape=jax.ShapeDtypeStruct((geom.nt, ATT_W), BF16),
        scratch_shapes=scratch,
        compiler_params=_params(3, 48),
        name="attention",
    )(z, z, z, z, two_heads(q_norm), two_heads(k_norm), cos128, sin128)


def _conv_kernel(a_ref, g_ref, ap_ref, gp_ref, an_ref, gn_ref, w_ref, b_ref, lng_ref, lnb_ref, o_ref,
                 ext, ys, shifted, *, t, c):
    r = pl.program_id(1)
    lat_blocks = t // c
    halo = CONV_HALO
    has_prev = jnp.logical_and(r != 0, r != lat_blocks)
    has_next = jnp.logical_and(r != lat_blocks - 1, r != lat_blocks)
    glu = lambda a, g: a[...].astype(F32) * jax.nn.sigmoid(g[...].astype(F32))
    ext[halo:halo + c, :] = glu(a_ref, g_ref)
    ext[0:halo, :] = jnp.where(has_prev, glu(ap_ref, gp_ref), 0.0)
    ext[halo + c:, :] = jnp.where(has_next, glu(an_ref, gn_ref), 0.0)

    rt = 64
    first = halo - CONV_K // 2
    span = c + 2 * halo - 8
    for s in range(1, 8):
        shifted[s - 1, 0:span, :] = ext[s:s + span, :]

    def lane_block(cb, carry):
        lanes = pl.ds(pl.multiple_of(cb * 128, 128), 128)
        for ti in range(c // rt):
            acc = jnp.zeros((rt, 128), F32)
            for j in range(CONV_K):
                row, s = divmod(ti * rt + first + j, 8)
                src = ext if s == 0 else shifted.at[s - 1]
                acc = acc + w_ref[pl.ds(j, 1), lanes] * src[pl.ds(row * 8, rt), lanes]
            ys[pl.ds(ti * rt, rt), lanes] = acc
        return carry

    lax.fori_loop(0, CONV_CH // 128, lane_block, 0)
    y = ys[...] + b_ref[...]
    o_ref[...] = _silu(_layer_norm(y, lng_ref[...], lnb_ref[...])).astype(o_ref.dtype)


def _conv_call(geom, z, conv_dw, conv_db, ln_g, ln_b):
    t, c = geom.t, geom.c
    rb = geom.row_block
    hb = c // CONV_HALO
    last = geom.nt // CONV_HALO - 1
    prev = lambda b, r: jnp.maximum(rb(b, r) * hb - 1, 0)
    nxt = lambda b, r: jnp.minimum((rb(b, r) + 1) * hb, last)
    w = jnp.zeros((32, CONV_CH), F32).at[:CONV_K].set(conv_dw)
    vec = lambda v: v.reshape(1, CONV_CH)
    cst = pl.BlockSpec((1, CONV_CH), lambda b, r: (0, 0))
    in_specs = [
        pl.BlockSpec((c, CONV_CH), lambda b, r: (rb(b, r), 0)),
        pl.BlockSpec((c, CONV_CH), lambda b, r: (rb(b, r), 1)),
        pl.BlockSpec((CONV_HALO, CONV_CH), lambda b, r: (prev(b, r), 0)),
        pl.BlockSpec((CONV_HALO, CONV_CH), lambda b, r: (prev(b, r), 1)),
        pl.BlockSpec((CONV_HALO, CONV_CH), lambda b, r: (nxt(b, r), 0)),
        pl.BlockSpec((CONV_HALO, CONV_CH), lambda b, r: (nxt(b, r), 1)),
        pl.BlockSpec((32, CONV_CH), lambda b, r: (0, 0)),
        cst, cst, cst,
    ]
    return pl.pallas_call(
        functools.partial(_conv_kernel, t=t, c=c),
        grid=(geom.b, geom.lat_blocks + 1),
        in_specs=in_specs,
        out_specs=pl.BlockSpec((c, CONV_CH), lambda b, r: (rb(b, r), 0)),
        out_shape=jax.ShapeDtypeStruct((geom.nt, CONV_CH), BF16),
        scratch_shapes=[pltpu.VMEM((c + 2 * CONV_HALO, CONV_CH), F32), pltpu.VMEM((c, CONV_CH), F32),
                        pltpu.VMEM((7, c + 2 * CONV_HALO, CONV_CH), F32)],
        compiler_params=_params(2, 32),
        name="conformer_conv",
    )(z, z, z, z, z, z, w, vec(conv_db), vec(ln_g), vec(ln_b))


def _mix_kernel(ret_ref, att_ref, cv_ref, gr_ref, ga_ref, gc_ref, x_ref, g1_ref, sh2_ref, sc2_ref,
                wr_ref, wa_ref, wc_ref, wo_ref, lng_ref, lnb_ref, x1_ref, h2_ref, *, alpha):
    def proj(v_ref, w_ref):
        return jnp.dot(v_ref[...].astype(BF16), w_ref[...], preferred_element_type=F32)

    gate = lambda g_ref: jax.nn.sigmoid(g_ref[...].astype(F32))
    merged = (gate(gr_ref) * proj(ret_ref, wr_ref)
              + gate(ga_ref) * proj(att_ref, wa_ref)
              + gate(gc_ref) * proj(cv_ref, wc_ref))
    y = jnp.dot(merged.astype(BF16), wo_ref[...], preferred_element_type=F32)
    x1 = _layer_norm(alpha * x_ref[...] + g1_ref[0] * y, lng_ref[...], lnb_ref[...])
    x1_ref[...] = x1
    h2_ref[...] = x1 * (1.0 + sc2_ref[0]) + sh2_ref[0]


def _mix_call(geom, alpha, ret, att, cv, z, x, mods, w_ret_o, w_att_o, w_conv_o, w_out, ln_g, ln_b):
    tm = geom.c
    d = D_MODEL
    tile = pl.BlockSpec((tm, d), lambda i: (i, 0))
    gate = lambda k: pl.BlockSpec((tm, d), lambda i: (i, COL_GT // d + k))
    wsp = pl.BlockSpec((d, d), lambda i: (0, 0))
    vsp = pl.BlockSpec((1, d), lambda i: (0, 0))
    return pl.pallas_call(
        functools.partial(_mix_kernel, alpha=alpha),
        grid=(geom.nt // tm,),
        in_specs=[tile, tile, tile, gate(0), gate(1), gate(2), tile,
                  _mod_spec(geom, tm, 2), _mod_spec(geom, tm, 3), _mod_spec(geom, tm, 4),
                  wsp, wsp, wsp, wsp, vsp, vsp],
        out_specs=[tile, tile],
        out_shape=[jax.ShapeDtypeStruct((geom.nt, d), F32)] * 2,
        compiler_params=_params(1, 48),
        name="merge_ln1",
    )(ret, att, cv, z, z, z, x, mods, mods, mods, w_ret_o, w_att_o, w_conv_o, w_out,
      ln_g.reshape(1, d), ln_b.reshape(1, d))


def _router_kernel(h_ref, wr_ref, bias_ref, e_ref, w_ref, pos_ref, cnt_ref, hist_ref, cnt):
    i = pl.program_id(0)
    tm = h_ref.shape[0]
    ne, per = N_EXPERTS, N_EXPERTS // N_GROUPS
    neg = -jnp.inf

    @pl.when(i == 0)
    def _init():
        cnt[...] = jnp.zeros(cnt.shape, F32)

    logits = jnp.dot(h_ref[...], wr_ref[...], preferred_element_type=F32, precision=HIGHEST)
    scores = jax.nn.sigmoid(logits.T[:ne])
    sel = scores + bias_ref[...]

    member = lax.broadcasted_iota(jnp.int32, (per, tm), 0)
    grp_rows = []
    for g in range(N_GROUPS):
        blk = sel[g * per:(g + 1) * per]
        m1 = jnp.max(blk, axis=0, keepdims=True)
        first = jnp.min(jnp.where(blk == m1, member, per), axis=0, keepdims=True)
        m2 = jnp.max(jnp.where(member == first, neg, blk), axis=0, keepdims=True)
        grp_rows.append(m1 + m2)
    gs = jnp.concatenate(grp_rows, axis=0)

    gidx = lax.broadcasted_iota(jnp.int32, (N_GROUPS, tm), 0)
    rank = jnp.zeros((N_GROUPS, tm), jnp.int32)
    for g in range(N_GROUPS):
        row = gs[g:g + 1]
        ahead = jnp.logical_or(row > gs, jnp.logical_and(row == gs, g < gidx))
        rank = rank + ahead.astype(jnp.int32)
    keep = (rank < TOPK_GROUPS).astype(F32)
    keep_e = jnp.concatenate([jnp.broadcast_to(keep[g:g + 1], (per, tm)) for g in range(N_GROUPS)], axis=0)
    cand = jnp.where(keep_e > 0.5, sel, neg)

    eidx = lax.broadcasted_iota(jnp.int32, (ne, tm), 0)
    picks, gates, hots = [], [], []
    chosen = jnp.zeros((ne, tm), F32)
    for _ in range(TOP_K):
        m = jnp.max(cand, axis=0, keepdims=True)
        idx = jnp.min(jnp.where(cand == m, eidx, ne), axis=0, keepdims=True)
        hot = eidx == idx
        picks.append(idx)
        gates.append(jnp.sum(jnp.where(hot, scores, 0.0), axis=0, keepdims=True))
        hots.append(hot)
        chosen = jnp.where(hot, 1.0, chosen)
        cand = jnp.where(hot, neg, cand)
    total = gates[0]
    for gk in gates[1:]:
        total = total + gk

    ti = lax.broadcasted_iota(jnp.int32, (tm, tm), 0)
    tj = lax.broadcasted_iota(jnp.int32, (tm, tm), 1)
    before = jnp.where(ti < tj, 1.0, 0.0).astype(BF16)
    prior = jnp.dot(chosen.astype(BF16), before, preferred_element_type=F32) + cnt[...][:, :1]
    pos = [jnp.sum(jnp.where(hot, prior, 0.0), axis=0, keepdims=True) for hot in hots]

    e_ref[...] = jnp.concatenate(picks, axis=0)
    w_ref[...] = jnp.concatenate([ROUTED_SCALE * gk / total for gk in gates], axis=0)
    pos_ref[...] = jnp.concatenate(pos, axis=0).astype(jnp.int32)
    hist_ref[0] = cnt[...]
    cnt[...] = cnt[...] + jnp.sum(chosen, axis=1, keepdims=True)
    cnt_ref[...] = cnt[...]


def _router_call(geom, h2, w_router, router_bias):
    tm = geom.c
    wr = jnp.zeros((D_MODEL, 128), F32).at[:, :N_EXPERTS].set(w_router)
    tok = pl.BlockSpec((TOP_K, tm), lambda i: (0, i))
    return pl.pallas_call(
        _router_kernel,
        grid=(geom.nt // tm,),
        in_specs=[pl.BlockSpec((tm, D_MODEL), lambda i: (i, 0)),
                  pl.BlockSpec((D_MODEL, 128), lambda i: (0, 0)),
                  pl.BlockSpec((N_EXPERTS, 1), lambda i: (0, 0))],
        out_specs=[tok, tok, tok, pl.BlockSpec((N_EXPERTS, 128), lambda i: (0, 0)),
                   pl.BlockSpec((1, N_EXPERTS, 128), lambda i: (i, 0, 0))],
        out_shape=[jax.ShapeDtypeStruct((TOP_K, geom.nt), jnp.int32),
                   jax.ShapeDtypeStruct((TOP_K, geom.nt), F32),
                   jax.ShapeDtypeStruct((TOP_K, geom.nt), jnp.int32),
                   jax.ShapeDtypeStruct((N_EXPERTS, 128), F32),
                   jax.ShapeDtypeStruct((geom.nt // tm, N_EXPERTS, 128), F32)],
        scratch_shapes=[pltpu.VMEM((N_EXPERTS, 128), F32)],
        compiler_params=_params(1, 32),
        name="moe_router",
    )(h2, wr, router_bias.reshape(N_EXPERTS, 1))


HALF = D_MODEL // 2


def _pack_bf16_pairs(v):
    lo = pltpu.bitcast(v[:, :HALF].astype(BF16).astype(F32), jnp.uint32)
    hi = pltpu.bitcast(v[:, HALF:].astype(BF16).astype(F32), jnp.uint32)
    return jnp.bitwise_or(jnp.right_shift(lo, jnp.uint32(16)), hi)


def _unpack_bf16_pairs(w):
    lo = pltpu.bitcast(jnp.left_shift(w, jnp.uint32(16)), F32)
    hi = pltpu.bitcast(jnp.bitwise_and(w, jnp.uint32(0xFFFF0000)), F32)
    return lo, hi


RUN_ALIGN = 8
SORTED_ROWS = 256 * TOP_K + N_EXPERTS * RUN_ALIGN
RUN_BITS = tuple(range(8, 2, -1))


def _for_each_run_piece(n_ref, src_ref, tile, visit):
    def per_expert(e, off):
        n = n_ref[tile * N_EXPERTS + e]
        src = src_ref[tile * N_EXPERTS + e]
        for lb in RUN_BITS:
            done = (n >> (lb + 1)) << (lb + 1)

            @pl.when((n & (1 << lb)) != 0)
            def _piece():
                visit(pl.multiple_of(off + done, RUN_ALIGN), pl.multiple_of(src + done, RUN_ALIGN), 1 << lb)

        return off + n

    lax.fori_loop(0, N_EXPERTS, per_expert, 0)


TOTAL_BITS = tuple(range(11, 2, -1))


def _wait_rows(total, wait_piece):
    for lb in TOTAL_BITS:
        @pl.when((total & (1 << lb)) != 0)
        def _amount():
            wait_piece(1 << lb)


def _dispatch_kernel(last_ref, n_ref, src_ref, tot_ref, sidx_ref, h_ref, xs_out, packed, zblk, sem, zsem):
    tm = h_ref.shape[0]
    i = pl.program_id(0)

    @pl.when(pl.program_id(0) == 0)
    def _zero_tail_blocks():
        zblk[...] = jnp.zeros(zblk.shape, zblk.dtype)

        def zero_copy(e):
            return pltpu.make_async_copy(zblk, xs_out.at[pl.ds(last_ref[e] * MOE_BLOCK, MOE_BLOCK)], zsem)

        def start(e, carry):
            zero_copy(e).start()
            return carry

        def wait(e, carry):
            zero_copy(e).wait()
            return carry

        lax.fori_loop(0, N_EXPERTS, start, 0)
        lax.fori_loop(0, N_EXPERTS, wait, 0)

    rows = lax.broadcasted_iota(jnp.int32, (SORTED_ROWS, tm), 0)
    pick = jnp.zeros((SORTED_ROWS, tm), F32)
    for k in range(TOP_K):
        pick = jnp.where(rows == sidx_ref[k:k + 1, :], 1.0, pick)
    sorted_rows = jnp.dot(pick.astype(BF16), h_ref[...].astype(BF16), preferred_element_type=F32)
    lo = pltpu.bitcast(sorted_rows[:, :HALF], jnp.uint32)
    hi = pltpu.bitcast(sorted_rows[:, HALF:], jnp.uint32)
    packed[...] = jnp.bitwise_or(jnp.right_shift(lo, jnp.uint32(16)), hi)

    def piece(sorted_row, slot_row, rows_):
        return pltpu.make_async_copy(packed.at[pl.ds(sorted_row, rows_)], xs_out.at[pl.ds(slot_row, rows_)], sem)

    _for_each_run_piece(n_ref, src_ref, i, lambda a, b, r: piece(a, b, r).start())
    _wait_rows(tot_ref[i], lambda r: piece(0, 0, r).wait())


def _dispatch_call(geom, last_block, run_rows, run_slot, tile_rows, sidx, h2, n_blocks):
    tm = geom.c
    assert tm * TOP_K + N_EXPERTS * RUN_ALIGN == SORTED_ROWS
    grid_spec = pltpu.PrefetchScalarGridSpec(
        num_scalar_prefetch=4,
        grid=(geom.nt // tm,),
        in_specs=[pl.BlockSpec((TOP_K, tm), lambda i, *_: (0, i)),
                  pl.BlockSpec((tm, D_MODEL), lambda i, *_: (i, 0))],
        out_specs=pl.BlockSpec(memory_space=pl.ANY),
        scratch_shapes=[pltpu.VMEM((SORTED_ROWS, HALF), jnp.uint32), pltpu.VMEM((MOE_BLOCK, HALF), jnp.uint32),
                        pltpu.SemaphoreType.DMA(()), pltpu.SemaphoreType.DMA(())],
    )
    return pl.pallas_call(
        _dispatch_kernel,
        grid_spec=grid_spec,
        out_shape=jax.ShapeDtypeStruct(((n_blocks + 1) * MOE_BLOCK, HALF), jnp.uint32),
        compiler_params=_params(1, 48),
        name="moe_dispatch",
    )(last_block, run_rows, run_slot, tile_rows, sidx, h2)


def _expert_kernel(be_ref, nu_ref, x_ref, wg_ref, wu_ref, wd_ref, o_ref, wgu_s, wd_s):
    i = pl.program_id(0)
    live = i < nu_ref[0]
    changed = jnp.logical_or(i == 0, be_ref[i] != be_ref[jnp.maximum(i - 1, 0)])

    @pl.when(jnp.logical_and(live, changed))
    def _load_expert():
        wgu_s[:, :D_EXPERT] = wg_ref[0, 0].astype(BF16)
        wgu_s[:, D_EXPERT:] = wu_ref[0, 0].astype(BF16)
        wd_s[...] = wd_ref[0, 0].astype(BF16)

    @pl.when(live)
    def _run():
        lo, hi = _unpack_bf16_pairs(x_ref[...])
        x = jnp.concatenate([lo, hi], axis=1).astype(BF16)
        hgu = jnp.dot(x, wgu_s[...], preferred_element_type=F32)
        hid = _silu(hgu[:, :D_EXPERT]) * hgu[:, D_EXPERT:]
        o_ref[...] = _pack_bf16_pairs(jnp.dot(hid.astype(BF16), wd_s[...], preferred_element_type=F32))


def _expert_call(layer, block_e, n_used, xs, w_gate, w_up, w_down):
    n_blocks = xs.shape[0] // MOE_BLOCK - 1
    live = lambda i, be, nu: jnp.minimum(i, nu[0] - 1)
    expert = lambda i, be, nu: (layer, be[live(i, be, nu)], 0, 0)
    grid_spec = pltpu.PrefetchScalarGridSpec(
        num_scalar_prefetch=2,
        grid=(n_blocks,),
        in_specs=[pl.BlockSpec((MOE_BLOCK, HALF), lambda i, be, nu: (live(i, be, nu), 0)),
                  pl.BlockSpec((1, 1, D_MODEL, D_EXPERT), expert),
                  pl.BlockSpec((1, 1, D_MODEL, D_EXPERT), expert),
                  pl.BlockSpec((1, 1, D_EXPERT, D_MODEL), expert)],
        out_specs=pl.BlockSpec((MOE_BLOCK, HALF), lambda i, be, nu: (live(i, be, nu), 0)),
        scratch_shapes=[pltpu.VMEM((D_MODEL, 2 * D_EXPERT), BF16), pltpu.VMEM((D_EXPERT, D_MODEL), BF16)],
    )
    return pl.pallas_call(
        _expert_kernel,
        grid_spec=grid_spec,
        out_shape=jax.ShapeDtypeStruct(xs.shape, jnp.uint32),
        compiler_params=_params(1, 32),
        name="moe_experts",
    )(block_e, n_used, xs, w_gate, w_up, w_down)


def _combine_kernel(n_ref, src_ref, tot_ref, sidx_ref, wt_ref, ys_hbm, h_ref, x_ref, g2_ref, wgu_ref, wd_ref,
                    lng_ref, lnb_ref, o_ref, buf_a, buf_b, sem, *, alpha):
    i = pl.program_id(0)
    n = pl.num_programs(0)
    tm = h_ref.shape[0]
    even = i % 2 == 0

    def piece(buf, slot, sorted_row, slot_row, rows_):
        return pltpu.make_async_copy(ys_hbm.at[pl.ds(slot_row, rows_)], buf.at[pl.ds(sorted_row, rows_)], sem.at[slot])

    def gather(tile, buf, slot):
        _for_each_run_piece(n_ref, src_ref, tile, lambda a, b, r: piece(buf, slot, a, b, r).start())

    def drain(tile, buf, slot):
        _wait_rows(tot_ref[tile], lambda r: piece(buf, slot, 0, 0, r).wait())

    def finish(buf):
        wt = wt_ref[...]
        cols = lax.broadcasted_iota(jnp.int32, (tm, SORTED_ROWS), 1)
        mix = jnp.zeros((tm, SORTED_ROWS), F32)
        for k in range(TOP_K):
            mix = jnp.where(cols == sidx_ref[:, k:k + 1], wt[:, k:k + 1], mix)
        mix = mix.astype(BF16)
        lo, hi = _unpack_bf16_pairs(buf[...])
        routed = jnp.concatenate([jnp.dot(mix, lo.astype(BF16), preferred_element_type=F32),
                                  jnp.dot(mix, hi.astype(BF16), preferred_element_type=F32)], axis=1)
        hgu = jnp.dot(h_ref[...].astype(BF16), wgu_ref[...], preferred_element_type=F32)
        hid = _silu(hgu[:, :D_SHARED]) * hgu[:, D_SHARED:]
        shared = jnp.dot(hid.astype(BF16), wd_ref[...], preferred_element_type=F32)
        o_ref[...] = _layer_norm(alpha * x_ref[...] + g2_ref[0] * (routed + shared), lng_ref[...], lnb_ref[...])

    @pl.when(i == 0)
    def _first():
        buf_a[...] = jnp.zeros(buf_a.shape, buf_a.dtype)
        buf_b[...] = jnp.zeros(buf_b.shape, buf_b.dtype)
        gather(i, buf_a, 0)

    @pl.when(jnp.logical_and(even, i + 1 < n))
    def _ahead_b():
        gather(i + 1, buf_b, 1)

    @pl.when(jnp.logical_and(jnp.logical_not(even), i + 1 < n))
    def _ahead_a():
        gather(i + 1, buf_a, 0)

    @pl.when(even)
    def _finish_a():
        drain(i, buf_a, 0)
        finish(buf_a)

    @pl.when(jnp.logical_not(even))
    def _finish_b():
        drain(i, buf_b, 1)
        finish(buf_b)


def _combine_call(geom, alpha, run_rows, run_slot, tile_rows, sidx_tok, w_tok, ys, h2, x1, mods, w_sh_gu, w_sh_down, ln_g, ln_b):
    tm = geom.c
    d = D_MODEL
    n = geom.nt // tm
    tile = pl.BlockSpec((tm, d), lambda i, *_: (i, 0))
    vsp = pl.BlockSpec((1, d), lambda i, *_: (0, 0))
    per_tok = pl.BlockSpec((tm, TOP_K), lambda i, *_: (i, 0))
    grid_spec = pltpu.PrefetchScalarGridSpec(
        num_scalar_prefetch=3,
        grid=(n,),
        in_specs=[per_tok, per_tok,
                  pl.BlockSpec(memory_space=pl.ANY),
                  tile, tile, _mod_spec(geom, tm, 5),
                  pl.BlockSpec((d, 2 * D_SHARED), lambda i, *_: (0, 0)),
                  pl.BlockSpec((D_SHARED, d), lambda i, *_: (0, 0)),
                  vsp, vsp],
        out_specs=tile,
        scratch_shapes=[pltpu.VMEM((SORTED_ROWS, HALF), jnp.uint32), pltpu.VMEM((SORTED_ROWS, HALF), jnp.uint32),
                        pltpu.SemaphoreType.DMA((2,))],
    )
    return pl.pallas_call(
        functools.partial(_combine_kernel, alpha=alpha),
        grid_spec=grid_spec,
        out_shape=jax.ShapeDtypeStruct((geom.nt, d), F32),
        compiler_params=_params(1, 56),
        name="moe_combine_ln2",
    )(run_rows, run_slot, tile_rows, sidx_tok, w_tok, ys, h2, x1, mods, w_sh_gu, w_sh_down,
      ln_g.reshape(1, d), ln_b.reshape(1, d))


def _rope_tables(t):
    rows = t // GRID_W
    row = jnp.repeat(jnp.arange(rows, dtype=F32), GRID_W)
    col = jnp.tile(jnp.arange(GRID_W, dtype=F32), rows)
    n_freq = ATT_DH // 4
    inv_freq = ROPE_THETA ** (-jnp.arange(n_freq, dtype=F32) / n_freq)
    ang = jnp.concatenate([row[:, None] * inv_freq, col[:, None] * inv_freq], axis=-1)
    cos, sin = jnp.cos(ang), jnp.sin(ang)
    cos64 = jnp.concatenate([cos, cos], axis=-1)
    sin64 = jnp.concatenate([-sin, sin], axis=-1)
    return cos64, sin64


def kernel(x, c, ctx, c_ctx, w_ada, b_ada, w_in, ret_decay_logit, att_q_norm, att_k_norm, conv_dw, conv_db, conv_ln_g, conv_ln_b, w_ret_o, w_att_o, w_conv_o, w_out, ln1_g, ln1_b, w_router, router_bias, w_exp_gate, w_exp_up, w_exp_down, w_sh_gate, w_sh_up, w_sh_down, ln2_g, ln2_b):
    b, t, d = x.shape
    n_ctx = ctx.shape[1]
    depth = w_ada.shape[0]
    assert d == D_MODEL and w_in.shape[-1] == D_IN
    geom = _Geom(b, t, n_ctx)
    alpha = float((2 * depth) ** 0.25)

    cos64, sin64 = _rope_tables(t)
    cos128 = jnp.concatenate([cos64, cos64], axis=-1)
    sin128 = jnp.concatenate([sin64, sin64], axis=-1)

    n_rows = -(-(b + 1) // 8) * 8
    cvecs = jnp.zeros((n_rows, d), F32).at[:b].set(c).at[b].set(c_ctx)
    mods_all = _mods_call(cvecs, w_ada, b_ada).reshape(depth, n_rows * 6, 1, d)

    n_tiles = geom.nt // geom.c
    n_blocks = -(-(geom.nt * TOP_K + n_tiles * N_EXPERTS * (RUN_ALIGN - 1)) // MOE_BLOCK) + N_EXPERTS

    xt = jnp.concatenate([x.reshape(geom.nl, d), ctx.reshape(geom.nc, d)], axis=0)
    for l in range(depth):
        mods = mods_all[l]
        w_in_l = _permute_columns(w_in[l]).astype(BF16)
        z = _inproj_call(geom, xt, mods, w_in_l)

        log_gamma = jax.nn.log_sigmoid(ret_decay_logit[l].astype(F32))
        ret = _retention_call(geom, z, log_gamma, cos128, sin128)
        att = _attention_call(geom, z, att_q_norm[l], att_k_norm[l], cos128, sin128)
        cv = _conv_call(geom, z, conv_dw[l], conv_db[l], conv_ln_g[l], conv_ln_b[l])
        x1, h2 = _mix_call(geom, alpha, ret, att, cv, z, xt, mods,
                           w_ret_o[l].astype(BF16), w_att_o[l].astype(BF16), w_conv_o[l].astype(BF16),
                           w_out[l].astype(BF16), ln1_g[l], ln1_b[l])

        top_e, gate_w, pos, counts, cnt_hist = _router_call(geom, h2, w_router[l], router_bias[l])
        before = cnt_hist[:, :, 0].astype(jnp.int32)
        total = counts[:, 0].astype(jnp.int32)
        tile_n = jnp.concatenate([before[1:], total[None, :]], axis=0) - before
        run_rows = (tile_n + RUN_ALIGN - 1) // RUN_ALIGN * RUN_ALIGN
        run_before = jnp.cumsum(run_rows, axis=0) - run_rows
        blocks_e = (jnp.sum(run_rows, axis=0) + MOE_BLOCK - 1) // MOE_BLOCK
        blocks_end = jnp.cumsum(blocks_e)
        start_row = (blocks_end - blocks_e) * MOE_BLOCK
        run_slot = start_row[None, :] + run_before
        run_sorted = jnp.cumsum(run_rows, axis=1) - run_rows
        onehot = top_e[:, :, None] == jnp.arange(N_EXPERTS, dtype=jnp.int32)[None, None, :]
        per_token = lambda table: jnp.sum(jnp.where(onehot, jnp.repeat(table, geom.c, axis=0)[None], 0), axis=-1)
        sidx = per_token(run_sorted) + pos - per_token(before)
        block_ids = jnp.arange(n_blocks, dtype=jnp.int32)
        block_e = jnp.minimum(jnp.sum((blocks_end[None, :] <= block_ids[:, None]).astype(jnp.int32), axis=1),
                              N_EXPERTS - 1)
        n_used = blocks_end[-1:].astype(jnp.int32)
        last_block = jnp.where(blocks_e > 0, blocks_end - 1, n_blocks).astype(jnp.int32)
        run_rows_flat, run_slot_flat = run_rows.reshape(-1), run_slot.reshape(-1).astype(jnp.int32)
        tile_rows = jnp.sum(run_rows, axis=1)

        xs = _dispatch_call(geom, last_block, run_rows_flat, run_slot_flat, tile_rows, sidx, h2, n_blocks)
        ys = _expert_call(l, block_e, n_used, xs, w_exp_gate, w_exp_up, w_exp_down)
        w_sh_gu = jnp.concatenate([w_sh_gate[l], w_sh_up[l]], axis=-1).astype(BF16)
        xt = _combine_call(geom, alpha, run_rows_flat, run_slot_flat, tile_rows, sidx.T, gate_w.T, ys, h2, x1, mods, w_sh_gu,
                           w_sh_down[l].astype(BF16), ln2_g[l], ln2_b[l])
    return xt[:geom.nl].reshape(b, t, d)
```

```python
import functools

import jax
import jax.numpy as jnp
from jax import lax
from jax.experimental import pallas as pl
from jax.experimental.pallas import tpu as pltpu

F32 = jnp.float32
BF16 = jnp.bfloat16
HIGHEST = lax.Precision.HIGHEST

D_MODEL = 1024
GRID_W = 64
EPS = 1e-6

RET_HEADS = 8
RET_DK = 64
RET_DV = 128
RET_CHUNK = 256
RET_W = RET_HEADS * RET_DV

ATT_HEADS = 16
ATT_KV_HEADS = 4
ATT_DH = 64
ATT_GROUP = ATT_HEADS // ATT_KV_HEADS
ATT_W = ATT_HEADS * ATT_DH
ROPE_THETA = 10000.0
ATT_KEY_BLOCK = 1024

CONV_CH = 1024
CONV_K = 31
CONV_HALO = 16

N_EXPERTS = 64
TOP_K = 8
N_GROUPS = 8
TOPK_GROUPS = 4
D_EXPERT = 256
D_SHARED = 256
ROUTED_SCALE = 2.5
MOE_BLOCK = 512

_ORIG = dict(rq=0, rk=512, rv=1024, rg=2048, aq=3072, ak=4096, av=4352, cu=4608, gt=6656)
D_IN = 9728
COL_CU = 0
COL_GT = 2048
COL_RG = 5120
COL_RV = 6144
COL_RQ = 7168
COL_RK = 7680
COL_ATT = 8192
ATT_SECTION = ATT_GROUP * ATT_DH + 2 * ATT_DH


def _column_ranges():
    rng = [(_ORIG["cu"], _ORIG["cu"] + 2 * CONV_CH),
           (_ORIG["gt"], _ORIG["gt"] + 3 * D_MODEL),
           (_ORIG["rg"], _ORIG["rg"] + RET_W),
           (_ORIG["rv"], _ORIG["rv"] + RET_W),
           (_ORIG["rq"], _ORIG["rq"] + RET_HEADS * RET_DK),
           (_ORIG["rk"], _ORIG["rk"] + RET_HEADS * RET_DK)]
    for g in range(ATT_KV_HEADS):
        rng.append((_ORIG["aq"] + g * ATT_GROUP * ATT_DH, _ORIG["aq"] + (g + 1) * ATT_GROUP * ATT_DH))
        rng.append((_ORIG["ak"] + g * ATT_DH, _ORIG["ak"] + (g + 1) * ATT_DH))
        rng.append((_ORIG["av"] + g * ATT_DH, _ORIG["av"] + (g + 1) * ATT_DH))
    cols = [c for a, b in rng for c in range(a, b)]
    assert sorted(cols) == list(range(D_IN))
    return rng


def _permute_columns(w):
    return jnp.concatenate([w[:, a:b] for a, b in _column_ranges()], axis=1)


def _params(n_axes, vmem_mib):
    return pltpu.CompilerParams(dimension_semantics=("arbitrary",) * n_axes,
                                vmem_limit_bytes=vmem_mib * 1024 * 1024)


def _silu(v):
    return v * jax.nn.sigmoid(v)


def _layer_norm(v, g, b):
    mu = jnp.mean(v, axis=-1, keepdims=True)
    d = v - mu
    var = jnp.mean(d * d, axis=-1, keepdims=True)
    return d * lax.rsqrt(var + EPS) * g + b


def _mods_kernel(c_ref, w_ref, b_ref, o_ref):
    s = _silu(c_ref[...])
    o_ref[0] = jnp.dot(s, w_ref[0], preferred_element_type=F32, precision=HIGHEST) + b_ref[0]


def _mods_call(cvecs, w_ada, b_ada):
    n_layers = w_ada.shape[0]
    rows, d = cvecs.shape
    return pl.pallas_call(
        _mods_kernel,
        grid=(n_layers, 6),
        in_specs=[pl.BlockSpec((rows, d), lambda l, j: (0, 0)),
                  pl.BlockSpec((1, d, d), lambda l, j: (l, 0, j)),
                  pl.BlockSpec((1, 1, d), lambda l, j: (l, 0, j))],
        out_specs=pl.BlockSpec((1, rows, d), lambda l, j: (l, 0, j)),
        out_shape=jax.ShapeDtypeStruct((n_layers, rows, 6 * d), F32),
        compiler_params=_params(2, 32),
        name="adaln_mods",
    )(cvecs, w_ada, b_ada.reshape(n_layers, 1, 6 * d))


class _Geom:
    def __init__(self, b, t, c):
        assert t % c == 0 and c % RET_CHUNK == 0 and c % CONV_HALO == 0 and t % ATT_KEY_BLOCK == 0
        self.b, self.t, self.c = b, t, c
        self.nl, self.nc = b * t, b * c
        self.nt = self.nl + self.nc
        self.lat_blocks = t // c
        self.nlb = self.nl // c
        self.p = t + c

    def row_block(self, bi, r):
        return jnp.where(r < self.lat_blocks, bi * self.lat_blocks + r, self.nlb + bi)

    def mod_row(self, i, tm):
        return jnp.where(i * tm < self.nl, (i * tm) // self.t, self.b)


def _mod_spec(geom, tm, which, grid_pos=0):
    d = D_MODEL
    if grid_pos == 0:
        return pl.BlockSpec((1, 1, d), lambda i, *_: (geom.mod_row(i, tm) * 6 + which, 0, 0))
    return pl.BlockSpec((1, 1, d), lambda j, i: (geom.mod_row(i, tm) * 6 + which, 0, 0))


def _inproj_kernel(x_ref, sh_ref, sc_ref, w_ref, o_ref):
    h = x_ref[...] * (1.0 + sc_ref[0]) + sh_ref[0]
    o_ref[...] = jnp.dot(h.astype(BF16), w_ref[...], preferred_element_type=F32).astype(o_ref.dtype)


def _inproj_call(geom, x, mods, w_in_bf16):
    tm = 512 if geom.nc % 512 == 0 and geom.t % 512 == 0 else geom.c
    tn = D_IN // 2
    return pl.pallas_call(
        _inproj_kernel,
        grid=(D_IN // tn, geom.nt // tm),
        in_specs=[pl.BlockSpec((tm, D_MODEL), lambda j, i: (i, 0)),
                  _mod_spec(geom, tm, 0, grid_pos=1),
                  _mod_spec(geom, tm, 1, grid_pos=1),
                  pl.BlockSpec((D_MODEL, tn), lambda j, i: (0, j))],
        out_specs=pl.BlockSpec((tm, tn), lambda j, i: (i, j)),
        out_shape=jax.ShapeDtypeStruct((geom.nt, D_IN), BF16),
        compiler_params=_params(2, 48),
        name="in_proj",
    )(x, mods, mods, w_in_bf16)


def _rot_half_128(v):
    lane = lax.broadcasted_iota(jnp.int32, v.shape, 1)
    return jnp.where((lane % 64) < 32, pltpu.roll(v, 96, 1), pltpu.roll(v, 32, 1))


def _ret_kernel(lg_ref, ql_ref, qc_ref, kl_ref, kc_ref, vl_ref, vc_ref, g_ref, cos_ref, sin_ref, o_ref,
                qs, kts, yf, yb, st, dm, qwb, kwb, gcs, *, t, c):
    ch = RET_CHUNK
    hp = pl.program_id(1)
    r = pl.program_id(2)
    lat_blocks = t // c
    n_lat, n_ctx = t // ch, c // ch

    @pl.when(r == 0)
    def _scan():
        ri = lax.broadcasted_iota(jnp.int32, (ch, ch), 0).astype(F32)
        ci = lax.broadcasted_iota(jnp.int32, (ch, ch), 1).astype(F32)
        rv = lax.broadcasted_iota(jnp.int32, (ch, RET_DV), 0).astype(F32)
        for d in range(2):
            for h in range(2):
                u = 2 * d + h
                lg = lg_ref[d, 2 * hp + h]
                rel = (ri - ci) if d == 0 else (ci - ri)
                dm[u] = jnp.where(rel >= 0.0, jnp.exp(lg * jnp.maximum(rel, 0.0)), 0.0)
                qwb[u] = jnp.exp(lg * ((rv + 1.0) if d == 0 else (float(ch) - rv)))
                kwb[u] = jnp.exp(lg * ((float(ch) - 1.0 - rv) if d == 0 else rv))
                gcs[u] = jnp.exp(jnp.full((RET_DK, RET_DV), lg * float(ch), F32))
                st[u] = jnp.zeros((RET_DK, RET_DV), F32)

        def stage(q, k, seq_rows):
            qs[0, seq_rows, :] = q[:, :RET_DK].astype(BF16)
            qs[1, seq_rows, :] = q[:, RET_DK:].astype(BF16)
            kt = k.T
            kts[0, :, seq_rows] = kt[:RET_DK].astype(BF16)
            kts[1, :, seq_rows] = kt[RET_DK:].astype(BF16)

        kscale = RET_DK ** -0.5
        for cc in range(n_ctx):
            rows = pl.ds(cc * ch, ch)
            stage(qc_ref[rows, :].astype(F32), kc_ref[rows, :].astype(F32) * kscale, rows)

        def stage_lat(cc, carry):
            rows = pl.ds(pl.multiple_of(cc * ch, ch), ch)
            cs, sn = cos_ref[rows, :], sin_ref[rows, :]
            q = ql_ref[rows, :].astype(F32)
            k = kl_ref[rows, :].astype(F32)
            q = q * cs + _rot_half_128(q) * sn
            k = (k * cs + _rot_half_128(k) * sn) * kscale
            stage(q, k, pl.ds(pl.multiple_of(c + cc * ch, ch), ch))
            return carry

        lax.fori_loop(0, n_lat, stage_lat, 0)

        def run_segment(v_ref, seq_off, n):
            def body(i, carry):
                for d, cc in ((0, i), (1, n - 1 - i)):
                    vrows = pl.ds(pl.multiple_of(cc * ch, ch), ch)
                    srows = pl.ds(pl.multiple_of(seq_off + cc * ch, ch), ch)
                    for h in range(2):
                        u = 2 * d + h
                        q = qs[h, srows, :]
                        kt = kts[h, :, srows]
                        v = v_ref[vrows, h * RET_DV:(h + 1) * RET_DV].astype(F32)
                        s = jnp.dot(q, kt, preferred_element_type=F32)
                        y = jnp.dot((s * dm[u]).astype(BF16), v.astype(BF16), preferred_element_type=F32)
                        state = st[u]
                        y = y + jnp.dot(q, state.astype(BF16), preferred_element_type=F32) * qwb[u]
                        dst = yf if d == 0 else yb
                        dst[srows, h * RET_DV:(h + 1) * RET_DV] = y
                        kv = jnp.dot(kt, (v * kwb[u]).astype(BF16), preferred_element_type=F32)
                        st[u] = gcs[u] * state + kv
                return carry

            lax.fori_loop(0, n, body, 0, unroll=2)

        run_segment(vc_ref, 0, n_ctx)
        run_segment(vl_ref, c, n_lat)

    def finish(srows):
        y = yf[srows, :] + yb[srows, :]
        for h in range(2):
            cols = slice(h * RET_DV, (h + 1) * RET_DV)
            yh = y[:, cols]
            mu = jnp.mean(yh, axis=-1, keepdims=True)
            dlt = yh - mu
            var = jnp.mean(dlt * dlt, axis=-1, keepdims=True)
            out = _silu(g_ref[:, cols].astype(F32)) * (dlt * lax.rsqrt(var + EPS))
            o_ref[:, cols] = out.astype(o_ref.dtype)

    @pl.when(r < lat_blocks)
    def _fin_lat():
        finish(pl.ds(pl.multiple_of(c + r * c, c), c))

    @pl.when(r == lat_blocks)
    def _fin_ctx():
        finish(pl.ds(0, c))


def _retention_call(geom, z, log_gamma, cos128, sin128):
    t, c, p = geom.t, geom.c, geom.p
    hpairs = RET_HEADS // 2
    qb, kb = COL_RQ // 128, COL_RK // 128
    vb, gb = COL_RV // 256, COL_RG // 256
    rb = geom.row_block
    in_specs = [
        pl.BlockSpec(memory_space=pltpu.SMEM),
        pl.BlockSpec((t, 128), lambda b, h, r: (b, qb + h)),
        pl.BlockSpec((c, 128), lambda b, h, r: (geom.nlb + b, qb + h)),
        pl.BlockSpec((t, 128), lambda b, h, r: (b, kb + h)),
        pl.BlockSpec((c, 128), lambda b, h, r: (geom.nlb + b, kb + h)),
        pl.BlockSpec((t, 256), lambda b, h, r: (b, vb + h)),
        pl.BlockSpec((c, 256), lambda b, h, r: (geom.nlb + b, vb + h)),
        pl.BlockSpec((c, 256), lambda b, h, r: (rb(b, r), gb + h)),
        pl.BlockSpec((t, 128), lambda b, h, r: (0, 0)),
        pl.BlockSpec((t, 128), lambda b, h, r: (0, 0)),
    ]
    scratch = [
        pltpu.VMEM((2, p, RET_DK), BF16),
        pltpu.VMEM((2, RET_DK, p), BF16),
        pltpu.VMEM((p, 2 * RET_DV), F32),
        pltpu.VMEM((p, 2 * RET_DV), F32),
        pltpu.VMEM((4, RET_DK, RET_DV), F32),
        pltpu.VMEM((4, RET_CHUNK, RET_CHUNK), F32),
        pltpu.VMEM((4, RET_CHUNK, RET_DV), F32),
        pltpu.VMEM((4, RET_CHUNK, RET_DV), F32),
        pltpu.VMEM((4, RET_DK, RET_DV), F32),
    ]
    return pl.pallas_call(
        functools.partial(_ret_kernel, t=t, c=c),
        grid=(geom.b, hpairs, geom.lat_blocks + 1),
        in_specs=in_specs,
        out_specs=pl.BlockSpec((c, 256), lambda b, h, r: (rb(b, r), h)),
        out_shape=jax.ShapeDtypeStruct((geom.nt, RET_W), BF16),
        scratch_shapes=scratch,
        compiler_params=_params(3, 56),
        name="retention",
    )(log_gamma, z, z, z, z, z, z, z, cos128, sin128)


def _rms_heads_128(v, g):
    li = lax.broadcasted_iota(jnp.int32, (128, 128), 0) // ATT_DH
    lj = lax.broadcasted_iota(jnp.int32, (128, 128), 1) // ATT_DH
    avg = jnp.where(li == lj, 1.0 / ATT_DH, 0.0).astype(BF16)
    sq = v * v
    hi = sq.astype(BF16)
    lo = (sq - hi.astype(F32)).astype(BF16)
    ms = jnp.dot(hi, avg, preferred_element_type=F32) + jnp.dot(lo, avg, preferred_element_type=F32)
    return v * lax.rsqrt(ms + EPS) * g


def _att_kernel(qa_ref, qb_ref, kvl_ref, kvc_ref, qn_ref, kn_ref, cos_ref, sin_ref, o_ref,
                kts, vs, m_s, acc_s, *, t, c):
    r = pl.program_id(2)
    lat_blocks = t // c
    dh = ATT_DH
    tk = ATT_KEY_BLOCK
    lane = lax.broadcasted_iota(jnp.int32, (c, 2 * dh), 1)

    def stage_tile(kv, dst, cs, sn):
        k = _rms_heads_128(kv, kn_ref[...])
        if cs is not None:
            k = k * cs + _rot_half_128(k) * sn
        kts[:, dst] = k.T[:dh].astype(BF16)
        vs[dst, :] = jnp.where(lane < dh, pltpu.roll(kv, dh, 1), 1.0).astype(BF16)

    @pl.when(r == 0)
    def _stage_kv():
        stage_tile(kvc_ref[...].astype(F32), pl.ds(0, c), None, None)

        def stage(i, carry):
            rows = pl.ds(pl.multiple_of(i * c, c), c)
            stage_tile(kvl_ref[rows, :].astype(F32), pl.ds(pl.multiple_of(c + i * c, c), c),
                       cos_ref[rows, :], sin_ref[rows, :])
            return carry

        lax.fori_loop(0, lat_blocks, stage, 0)

    is_ctx = r == lat_blocks
    rows = pl.ds(pl.multiple_of(jnp.minimum(r, lat_blocks - 1) * c, c), c)
    cs, sn = cos_ref[rows, :], sin_ref[rows, :]
    q_heads = []
    for src in (qa_ref, qb_ref):
        xn = _rms_heads_128(src[...].astype(F32), qn_ref[...])
        xr = jnp.where(is_ctx, xn, xn * cs + _rot_half_128(xn) * sn) * (dh ** -0.5)
        q_heads.append(xr[:, :dh].astype(BF16))
        q_heads.append(pltpu.roll(xr, dh, 1)[:, :dh].astype(BF16))
    q = jnp.concatenate(q_heads, axis=0)

    m_s[...] = jnp.full(m_s.shape, -jnp.inf, F32)
    acc_s[...] = jnp.zeros(acc_s.shape, F32)

    def flash_step(kt, v):
        n = kt.shape[1]
        s = jnp.dot(q, kt, preferred_element_type=F32)
        m_prev = m_s[...]
        m_next = jnp.maximum(m_prev, jnp.max(s, axis=1, keepdims=True))
        prob = jnp.exp(s - jnp.concatenate([m_next] * (n // 128), axis=1))
        acc_s[...] = acc_s[...] * jnp.exp(m_prev - m_next) + jnp.dot(prob.astype(BF16), v, preferred_element_type=F32)
        m_s[...] = m_next

    flash_step(kts[:, 0:c], vs[0:c, :])

    @pl.when(jnp.logical_not(is_ctx))
    def _latent_keys():
        def lat_step(j, carry):
            krows = pl.ds(pl.multiple_of(c + j * tk, 128), tk)
            flash_step(kts[:, krows], vs[krows, :])
            return carry

        lax.fori_loop(0, t // tk, lat_step, 0, unroll=2)

    outs = []
    for h in range(ATT_GROUP):
        acc = acc_s[h * c:(h + 1) * c, :]
        outs.append(acc * pltpu.roll(1.0 / acc, dh, 1))
    for pair in range(ATT_GROUP // 2):
        both = jnp.where(lane < dh, outs[2 * pair], pltpu.roll(outs[2 * pair + 1], dh, 1))
        o_ref[:, pair * 2 * dh:(pair + 1) * 2 * dh] = both.astype(o_ref.dtype)


def _attention_call(geom, z, q_norm, k_norm, cos128, sin128):
    t, c, p = geom.t, geom.c, geom.p
    ab = COL_ATT // 128
    sec = ATT_SECTION // 128
    rb = geom.row_block
    in_specs = [
        pl.BlockSpec((c, 128), lambda b, g, r: (rb(b, r), ab + sec * g)),
        pl.BlockSpec((c, 128), lambda b, g, r: (rb(b, r), ab + sec * g + 1)),
        pl.BlockSpec((t, 128), lambda b, g, r: (b, ab + sec * g + 2)),
        pl.BlockSpec((c, 128), lambda b, g, r: (geom.nlb + b, ab + sec * g + 2)),
        pl.BlockSpec((1, 128), lambda b, g, r: (0, 0)),
        pl.BlockSpec((1, 128), lambda b, g, r: (0, 0)),
        pl.BlockSpec((t, 128), lambda b, g, r: (0, 0)),
        pl.BlockSpec((t, 128), lambda b, g, r: (0, 0)),
    ]
    two_heads = lambda v: jnp.tile(v.reshape(1, ATT_DH), (1, 2))
    scratch = [
        pltpu.VMEM((ATT_DH, p), BF16),
        pltpu.VMEM((p, 2 * ATT_DH), BF16),
        pltpu.VMEM((ATT_GROUP * c, 128), F32),
        pltpu.VMEM((ATT_GROUP * c, 2 * ATT_DH), F32),
    ]
    return pl.pallas_call(
        functools.partial(_att_kernel, t=t, c=c),
        grid=(geom.b, ATT_KV_HEADS, geom.lat_blocks + 1),
        in_specs=in_specs,
        out_specs=pl.BlockSpec((c, ATT_GROUP * ATT_DH), lambda b, g, r: (rb(b, r), g)),
        out_shape=jax.ShapeDtypeStruct((geom.nt, ATT_W), BF16),
        scratch_shapes=scratch,
        compiler_params=_params(3, 48),
        name="attention",
    )(z, z, z, z, two_heads(q_norm), two_heads(k_norm), cos128, sin128)


def _conv_kernel(a_ref, g_ref, ap_ref, gp_ref, an_ref, gn_ref, w_ref, b_ref, lng_ref, lnb_ref, o_ref,
                 ext, ys, shifted, *, t, c):
    r = pl.program_id(1)
    lat_blocks = t // c
    halo = CONV_HALO
    has_prev = jnp.logical_and(r != 0, r != lat_blocks)
    has_next = jnp.logical_and(r != lat_blocks - 1, r != lat_blocks)
    glu = lambda a, g: a[...].astype(F32) * jax.nn.sigmoid(g[...].astype(F32))
    ext[halo:halo + c, :] = glu(a_ref, g_ref)
    ext[0:halo, :] = jnp.where(has_prev, glu(ap_ref, gp_ref), 0.0)
    ext[halo + c:, :] = jnp.where(has_next, glu(an_ref, gn_ref), 0.0)

    rt = 64
    first = halo - CONV_K // 2
    span = c + 2 * halo - 8
    for s in range(1, 8):
        shifted[s - 1, 0:span, :] = ext[s:s + span, :]

    def lane_block(cb, carry):
        lanes = pl.ds(pl.multiple_of(cb * 128, 128), 128)
        for ti in range(c // rt):
            acc = jnp.zeros((rt, 128), F32)
            for j in range(CONV_K):
                row, s = divmod(ti * rt + first + j, 8)
                src = ext if s == 0 else shifted.at[s - 1]
                acc = acc + w_ref[pl.ds(j, 1), lanes] * src[pl.ds(row * 8, rt), lanes]
            ys[pl.ds(ti * rt, rt), lanes] = acc
        return carry

    lax.fori_loop(0, CONV_CH // 128, lane_block, 0)
    y = ys[...] + b_ref[...]
    o_ref[...] = _silu(_layer_norm(y, lng_ref[...], lnb_ref[...])).astype(o_ref.dtype)


def _conv_call(geom, z, conv_dw, conv_db, ln_g, ln_b):
    t, c = geom.t, geom.c
    rb = geom.row_block
    hb = c // CONV_HALO
    last = geom.nt // CONV_HALO - 1
    prev = lambda b, r: jnp.maximum(rb(b, r) * hb - 1, 0)
    nxt = lambda b, r: jnp.minimum((rb(b, r) + 1) * hb, last)
    w = jnp.zeros((32, CONV_CH), F32).at[:CONV_K].set(conv_dw)
    vec = lambda v: v.reshape(1, CONV_CH)
    cst = pl.BlockSpec((1, CONV_CH), lambda b, r: (0, 0))
    in_specs = [
        pl.BlockSpec((c, CONV_CH), lambda b, r: (rb(b, r), 0)),
        pl.BlockSpec((c, CONV_CH), lambda b, r: (rb(b, r), 1)),
        pl.BlockSpec((CONV_HALO, CONV_CH), lambda b, r: (prev(b, r), 0)),
        pl.BlockSpec((CONV_HALO, CONV_CH), lambda b, r: (prev(b, r), 1)),
        pl.BlockSpec((CONV_HALO, CONV_CH), lambda b, r: (nxt(b, r), 0)),
        pl.BlockSpec((CONV_HALO, CONV_CH), lambda b, r: (nxt(b, r), 1)),
        pl.BlockSpec((32, CONV_CH), lambda b, r: (0, 0)),
        cst, cst, cst,
    ]
    return pl.pallas_call(
        functools.partial(_conv_kernel, t=t, c=c),
        grid=(geom.b, geom.lat_blocks + 1),
        in_specs=in_specs,
        out_specs=pl.BlockSpec((c, CONV_CH), lambda b, r: (rb(b, r), 0)),
        out_shape=jax.ShapeDtypeStruct((geom.nt, CONV_CH), BF16),
        scratch_shapes=[pltpu.VMEM((c + 2 * CONV_HALO, CONV_CH), F32), pltpu.VMEM((c, CONV_CH), F32),
                        pltpu.VMEM((7, c + 2 * CONV_HALO, CONV_CH), F32)],
        compiler_params=_params(2, 32),
        name="conformer_conv",
    )(z, z, z, z, z, z, w, vec(conv_db), vec(ln_g), vec(ln_b))


def _mix_kernel(ret_ref, att_ref, cv_ref, gr_ref, ga_ref, gc_ref, x_ref, g1_ref, sh2_ref, sc2_ref,
                wr_ref, wa_ref, wc_ref, wo_ref, lng_ref, lnb_ref, x1_ref, h2_ref, *, alpha):
    def proj(v_ref, w_ref):
        return jnp.dot(v_ref[...].astype(BF16), w_ref[...], preferred_element_type=F32)

    gate = lambda g_ref: jax.nn.sigmoid(g_ref[...].astype(F32))
    merged = (gate(gr_ref) * proj(ret_ref, wr_ref)
              + gate(ga_ref) * proj(att_ref, wa_ref)
              + gate(gc_ref) * proj(cv_ref, wc_ref))
    y = jnp.dot(merged.astype(BF16), wo_ref[...], preferred_element_type=F32)
    x1 = _layer_norm(alpha * x_ref[...] + g1_ref[0] * y, lng_ref[...], lnb_ref[...])
    x1_ref[...] = x1
    h2_ref[...] = x1 * (1.0 + sc2_ref[0]) + sh2_ref[0]


def _mix_call(geom, alpha, ret, att, cv, z, x, mods, w_ret_o, w_att_o, w_conv_o, w_out, ln_g, ln_b):
    tm = geom.c
    d = D_MODEL
    tile = pl.BlockSpec((tm, d), lambda i: (i, 0))
    gate = lambda k: pl.BlockSpec((tm, d), lambda i: (i, COL_GT // d + k))
    wsp = pl.BlockSpec((d, d), lambda i: (0, 0))
    vsp = pl.BlockSpec((1, d), lambda i: (0, 0))
    return pl.pallas_call(
        functools.partial(_mix_kernel, alpha=alpha),
        grid=(geom.nt // tm,),
        in_specs=[tile, tile, tile, gate(0), gate(1), gate(2), tile,
                  _mod_spec(geom, tm, 2), _mod_spec(geom, tm, 3), _mod_spec(geom, tm, 4),
                  wsp, wsp, wsp, wsp, vsp, vsp],
        out_specs=[tile, tile],
        out_shape=[jax.ShapeDtypeStruct((geom.nt, d), F32)] * 2,
        compiler_params=_params(1, 48),
        name="merge_ln1",
    )(ret, att, cv, z, z, z, x, mods, mods, mods, w_ret_o, w_att_o, w_conv_o, w_out,
      ln_g.reshape(1, d), ln_b.reshape(1, d))


def _router_kernel(h_ref, wr_ref, bias_ref, e_ref, w_ref, pos_ref, cnt_ref, hist_ref, cnt):
    i = pl.program_id(0)
    tm = h_ref.shape[0]
    ne, per = N_EXPERTS, N_EXPERTS // N_GROUPS
    neg = -jnp.inf

    @pl.when(i == 0)
    def _init():
        cnt[...] = jnp.zeros(cnt.shape, F32)

    logits = jnp.dot(h_ref[...], wr_ref[...], preferred_element_type=F32, precision=HIGHEST)
    scores = jax.nn.sigmoid(logits.T[:ne])
    sel = scores + bias_ref[...]

    member = lax.broadcasted_iota(jnp.int32, (per, tm), 0)
    grp_rows = []
    for g in range(N_GROUPS):
        blk = sel[g * per:(g + 1) * per]
        m1 = jnp.max(blk, axis=0, keepdims=True)
        first = jnp.min(jnp.where(blk == m1, member, per), axis=0, keepdims=True)
        m2 = jnp.max(jnp.where(member == first, neg, blk), axis=0, keepdims=True)
        grp_rows.append(m1 + m2)
    gs = jnp.concatenate(grp_rows, axis=0)

    gidx = lax.broadcasted_iota(jnp.int32, (N_GROUPS, tm), 0)
    rank = jnp.zeros((N_GROUPS, tm), jnp.int32)
    for g in range(N_GROUPS):
        row = gs[g:g + 1]
        ahead = jnp.logical_or(row > gs, jnp.logical_and(row == gs, g < gidx))
        rank = rank + ahead.astype(jnp.int32)
    keep = (rank < TOPK_GROUPS).astype(F32)
    keep_e = jnp.concatenate([jnp.broadcast_to(keep[g:g + 1], (per, tm)) for g in range(N_GROUPS)], axis=0)
    cand = jnp.where(keep_e > 0.5, sel, neg)

    eidx = lax.broadcasted_iota(jnp.int32, (ne, tm), 0)
    picks, gates, hots = [], [], []
    chosen = jnp.zeros((ne, tm), F32)
    for _ in range(TOP_K):
        m = jnp.max(cand, axis=0, keepdims=True)
        idx = jnp.min(jnp.where(cand == m, eidx, ne), axis=0, keepdims=True)
        hot = eidx == idx
        picks.append(idx)
        gates.append(jnp.sum(jnp.where(hot, scores, 0.0), axis=0, keepdims=True))
        hots.append(hot)
        chosen = jnp.where(hot, 1.0, chosen)
        cand = jnp.where(hot, neg, cand)
    total = gates[0]
    for gk in gates[1:]:
        total = total + gk

    ti = lax.broadcasted_iota(jnp.int32, (tm, tm), 0)
    tj = lax.broadcasted_iota(jnp.int32, (tm, tm), 1)
    before = jnp.where(ti < tj, 1.0, 0.0).astype(BF16)
    prior = jnp.dot(chosen.astype(BF16), before, preferred_element_type=F32) + cnt[...][:, :1]
    pos = [jnp.sum(jnp.where(hot, prior, 0.0), axis=0, keepdims=True) for hot in hots]

    e_ref[...] = jnp.concatenate(picks, axis=0)
    w_ref[...] = jnp.concatenate([ROUTED_SCALE * gk / total for gk in gates], axis=0)
    pos_ref[...] = jnp.concatenate(pos, axis=0).astype(jnp.int32)
    hist_ref[0] = cnt[...]
    cnt[...] = cnt[...] + jnp.sum(chosen, axis=1, keepdims=True)
    cnt_ref[...] = cnt[...]


def _router_call(geom, h2, w_router, router_bias):
    tm = geom.c
    wr = jnp.zeros((D_MODEL, 128), F32).at[:, :N_EXPERTS].set(w_router)
    tok = pl.BlockSpec((TOP_K, tm), lambda i: (0, i))
    return pl.pallas_call(
        _router_kernel,
        grid=(geom.nt // tm,),
        in_specs=[pl.BlockSpec((tm, D_MODEL), lambda i: (i, 0)),
                  pl.BlockSpec((D_MODEL, 128), lambda i: (0, 0)),
                  pl.BlockSpec((N_EXPERTS, 1), lambda i: (0, 0))],
        out_specs=[tok, tok, tok, pl.BlockSpec((N_EXPERTS, 128), lambda i: (0, 0)),
                   pl.BlockSpec((1, N_EXPERTS, 128), lambda i: (i, 0, 0))],
        out_shape=[jax.ShapeDtypeStruct((TOP_K, geom.nt), jnp.int32),
                   jax.ShapeDtypeStruct((TOP_K, geom.nt), F32),
                   jax.ShapeDtypeStruct((TOP_K, geom.nt), jnp.int32),
                   jax.ShapeDtypeStruct((N_EXPERTS, 128), F32),
                   jax.ShapeDtypeStruct((geom.nt // tm, N_EXPERTS, 128), F32)],
        scratch_shapes=[pltpu.VMEM((N_EXPERTS, 128), F32)],
        compiler_params=_params(1, 32),
        name="moe_router",
    )(h2, wr, router_bias.reshape(N_EXPERTS, 1))


HALF = D_MODEL // 2


def _pack_bf16_pairs(v):
    lo = pltpu.bitcast(v[:, :HALF].astype(BF16).astype(F32), jnp.uint32)
    hi = pltpu.bitcast(v[:, HALF:].astype(BF16).astype(F32), jnp.uint32)
    return jnp.bitwise_or(jnp.right_shift(lo, jnp.uint32(16)), hi)


def _unpack_bf16_pairs(w):
    lo = pltpu.bitcast(jnp.left_shift(w, jnp.uint32(16)), F32)
    hi = pltpu.bitcast(jnp.bitwise_and(w, jnp.uint32(0xFFFF0000)), F32)
    return lo, hi


RUN_ALIGN = 8
SORTED_ROWS = 256 * TOP_K + N_EXPERTS * RUN_ALIGN
RUN_BITS = tuple(range(8, 2, -1))


def _for_each_run_piece(n_ref, src_ref, tile, visit):
    def per_expert(e, off):
        n = n_ref[tile * N_EXPERTS + e]
        src = src_ref[tile * N_EXPERTS + e]
        for lb in RUN_BITS:
            done = (n >> (lb + 1)) << (lb + 1)

            @pl.when((n & (1 << lb)) != 0)
            def _piece():
                visit(pl.multiple_of(off + done, RUN_ALIGN), pl.multiple_of(src + done, RUN_ALIGN), 1 << lb)

        return off + n

    lax.fori_loop(0, N_EXPERTS, per_expert, 0)


TOTAL_BITS = tuple(range(11, 2, -1))


def _wait_rows(total, wait_piece):
    for lb in TOTAL_BITS:
        @pl.when((total & (1 << lb)) != 0)
        def _amount():
            wait_piece(1 << lb)


def _dispatch_kernel(last_ref, n_ref, src_ref, tot_ref, sidx_ref, h_ref, xs_out, packed, zblk, sem, zsem):
    tm = h_ref.shape[0]
    i = pl.program_id(0)

    @pl.when(pl.program_id(0) == 0)
    def _zero_tail_blocks():
        zblk[...] = jnp.zeros(zblk.shape, zblk.dtype)

        def zero_copy(e):
            return pltpu.make_async_copy(zblk, xs_out.at[pl.ds(last_ref[e] * MOE_BLOCK, MOE_BLOCK)], zsem)

        def start(e, carry):
            zero_copy(e).start()
            return carry

        def wait(e, carry):
            zero_copy(e).wait()
            return carry

        lax.fori_loop(0, N_EXPERTS, start, 0)
        lax.fori_loop(0, N_EXPERTS, wait, 0)

    rows = lax.broadcasted_iota(jnp.int32, (SORTED_ROWS, tm), 0)
    pick = jnp.zeros((SORTED_ROWS, tm), F32)
    for k in range(TOP_K):
        pick = jnp.where(rows == sidx_ref[k:k + 1, :], 1.0, pick)
    sorted_rows = jnp.dot(pick.astype(BF16), h_ref[...].astype(BF16), preferred_element_type=F32)
    lo = pltpu.bitcast(sorted_rows[:, :HALF], jnp.uint32)
    hi = pltpu.bitcast(sorted_rows[:, HALF:], jnp.uint32)
    packed[...] = jnp.bitwise_or(jnp.right_shift(lo, jnp.uint32(16)), hi)

    def piece(sorted_row, slot_row, rows_):
        return pltpu.make_async_copy(packed.at[pl.ds(sorted_row, rows_)], xs_out.at[pl.ds(slot_row, rows_)], sem)

    _for_each_run_piece(n_ref, src_ref, i, lambda a, b, r: piece(a, b, r).start())
    _wait_rows(tot_ref[i], lambda r: piece(0, 0, r).wait())


def _dispatch_call(geom, last_block, run_rows, run_slot, tile_rows, sidx, h2, n_blocks):
    tm = geom.c
    assert tm * TOP_K + N_EXPERTS * RUN_ALIGN == SORTED_ROWS
    grid_spec = pltpu.PrefetchScalarGridSpec(
        num_scalar_prefetch=4,
        grid=(geom.nt // tm,),
        in_specs=[pl.BlockSpec((TOP_K, tm), lambda i, *_: (0, i)),
                  pl.BlockSpec((tm, D_MODEL), lambda i, *_: (i, 0))],
        out_specs=pl.BlockSpec(memory_space=pl.ANY),
        scratch_shapes=[pltpu.VMEM((SORTED_ROWS, HALF), jnp.uint32), pltpu.VMEM((MOE_BLOCK, HALF), jnp.uint32),
                        pltpu.SemaphoreType.DMA(()), pltpu.SemaphoreType.DMA(())],
    )
    return pl.pallas_call(
        _dispatch_kernel,
        grid_spec=grid_spec,
        out_shape=jax.ShapeDtypeStruct(((n_blocks + 1) * MOE_BLOCK, HALF), jnp.uint32),
        compiler_params=_params(1, 48),
        name="moe_dispatch",
    )(last_block, run_rows, run_slot, tile_rows, sidx, h2)


def _expert_kernel(be_ref, nu_ref, x_ref, wg_ref, wu_ref, wd_ref, o_ref, wgu_s, wd_s):
    i = pl.program_id(0)
    live = i < nu_ref[0]
    changed = jnp.logical_or(i == 0, be_ref[i] != be_ref[jnp.maximum(i - 1, 0)])

    @pl.when(jnp.logical_and(live, changed))
    def _load_expert():
        wgu_s[:, :D_EXPERT] = wg_ref[0, 0].astype(BF16)
        wgu_s[:, D_EXPERT:] = wu_ref[0, 0].astype(BF16)
        wd_s[...] = wd_ref[0, 0].astype(BF16)

    @pl.when(live)
    def _run():
        lo, hi = _unpack_bf16_pairs(x_ref[...])
        x = jnp.concatenate([lo, hi], axis=1).astype(BF16)
        hgu = jnp.dot(x, wgu_s[...], preferred_element_type=F32)
        hid = _silu(hgu[:, :D_EXPERT]) * hgu[:, D_EXPERT:]
        o_ref[...] = _pack_bf16_pairs(jnp.dot(hid.astype(BF16), wd_s[...], preferred_element_type=F32))


def _expert_call(layer, block_e, n_used, xs, w_gate, w_up, w_down):
    n_blocks = xs.shape[0] // MOE_BLOCK - 1
    live = lambda i, be, nu: jnp.minimum(i, nu[0] - 1)
    expert = lambda i, be, nu: (layer, be[live(i, be, nu)], 0, 0)
    grid_spec = pltpu.PrefetchScalarGridSpec(
        num_scalar_prefetch=2,
        grid=(n_blocks,),
        in_specs=[pl.BlockSpec((MOE_BLOCK, HALF), lambda i, be, nu: (live(i, be, nu), 0)),
                  pl.BlockSpec((1, 1, D_MODEL, D_EXPERT), expert),
                  pl.BlockSpec((1, 1, D_MODEL, D_EXPERT), expert),
                  pl.BlockSpec((1, 1, D_EXPERT, D_MODEL), expert)],
        out_specs=pl.BlockSpec((MOE_BLOCK, HALF), lambda i, be, nu: (live(i, be, nu), 0)),
        scratch_shapes=[pltpu.VMEM((D_MODEL, 2 * D_EXPERT), BF16), pltpu.VMEM((D_EXPERT, D_MODEL), BF16)],
    )
    return pl.pallas_call(
        _expert_kernel,
        grid_spec=grid_spec,
        out_shape=jax.ShapeDtypeStruct(xs.shape, jnp.uint32),
        compiler_params=_params(1, 32),
        name="moe_experts",
    )(block_e, n_used, xs, w_gate, w_up, w_down)


def _combine_kernel(n_ref, src_ref, tot_ref, sidx_ref, wt_ref, ys_hbm, h_ref, x_ref, g2_ref, wgu_ref, wd_ref,
                    lng_ref, lnb_ref, o_ref, buf_a, buf_b, sem, *, alpha):
    i = pl.program_id(0)
    n = pl.num_programs(0)
    tm = h_ref.shape[0]
    even = i % 2 == 0

    def piece(buf, slot, sorted_row, slot_row, rows_):
        return pltpu.make_async_copy(ys_hbm.at[pl.ds(slot_row, rows_)], buf.at[pl.ds(sorted_row, rows_)], sem.at[slot])

    def gather(tile, buf, slot):
        _for_each_run_piece(n_ref, src_ref, tile, lambda a, b, r: piece(buf, slot, a, b, r).start())

    def drain(tile, buf, slot):
        _wait_rows(tot_ref[tile], lambda r: piece(buf, slot, 0, 0, r).wait())

    def finish(buf):
        wt = wt_ref[...]
        cols = lax.broadcasted_iota(jnp.int32, (tm, SORTED_ROWS), 1)
        mix = jnp.zeros((tm, SORTED_ROWS), F32)
        for k in range(TOP_K):
            mix = jnp.where(cols == sidx_ref[:, k:k + 1], wt[:, k:k + 1], mix)
        mix = mix.astype(BF16)
        lo, hi = _unpack_bf16_pairs(buf[...])
        routed = jnp.concatenate([jnp.dot(mix, lo.astype(BF16), preferred_element_type=F32),
                                  jnp.dot(mix, hi.astype(BF16), preferred_element_type=F32)], axis=1)
        hgu = jnp.dot(h_ref[...].astype(BF16), wgu_ref[...], preferred_element_type=F32)
        hid = _silu(hgu[:, :D_SHARED]) * hgu[:, D_SHARED:]
        shared = jnp.dot(hid.astype(BF16), wd_ref[...], preferred_element_type=F32)
        o_ref[...] = _layer_norm(alpha * x_ref[...] + g2_ref[0] * (routed + shared), lng_ref[...], lnb_ref[...])

    @pl.when(i == 0)
    def _first():
        buf_a[...] = jnp.zeros(buf_a.shape, buf_a.dtype)
        buf_b[...] = jnp.zeros(buf_b.shape, buf_b.dtype)
        gather(i, buf_a, 0)

    @pl.when(jnp.logical_and(even, i + 1 < n))
    def _ahead_b():
        gather(i + 1, buf_b, 1)

    @pl.when(jnp.logical_and(jnp.logical_not(even), i + 1 < n))
    def _ahead_a():
        gather(i + 1, buf_a, 0)

    @pl.when(even)
    def _finish_a():
        drain(i, buf_a, 0)
        finish(buf_a)

    @pl.when(jnp.logical_not(even))
    def _finish_b():
        drain(i, buf_b, 1)
        finish(buf_b)


def _combine_call(geom, alpha, run_rows, run_slot, tile_rows, sidx_tok, w_tok, ys, h2, x1, mods, w_sh_gu, w_sh_down, ln_g, ln_b):
    tm = geom.c
    d = D_MODEL
    n = geom.nt // tm
    tile = pl.BlockSpec((tm, d), lambda i, *_: (i, 0))
    vsp = pl.BlockSpec((1, d), lambda i, *_: (0, 0))
    per_tok = pl.BlockSpec((tm, TOP_K), lambda i, *_: (i, 0))
    grid_spec = pltpu.PrefetchScalarGridSpec(
        num_scalar_prefetch=3,
        grid=(n,),
        in_specs=[per_tok, per_tok,
                  pl.BlockSpec(memory_space=pl.ANY),
                  tile, tile, _mod_spec(geom, tm, 5),
                  pl.BlockSpec((d, 2 * D_SHARED), lambda i, *_: (0, 0)),
                  pl.BlockSpec((D_SHARED, d), lambda i, *_: (0, 0)),
                  vsp, vsp],
        out_specs=tile,
        scratch_shapes=[pltpu.VMEM((SORTED_ROWS, HALF), jnp.uint32), pltpu.VMEM((SORTED_ROWS, HALF), jnp.uint32),
                        pltpu.SemaphoreType.DMA((2,))],
    )
    return pl.pallas_call(
        functools.partial(_combine_kernel, alpha=alpha),
        grid_spec=grid_spec,
        out_shape=jax.ShapeDtypeStruct((geom.nt, d), F32),
        compiler_params=_params(1, 56),
        name="moe_combine_ln2",
    )(run_rows, run_slot, tile_rows, sidx_tok, w_tok, ys, h2, x1, mods, w_sh_gu, w_sh_down,
      ln_g.reshape(1, d), ln_b.reshape(1, d))


def _rope_tables(t):
    rows = t // GRID_W
    row = jnp.repeat(jnp.arange(rows, dtype=F32), GRID_W)
    col = jnp.tile(jnp.arange(GRID_W, dtype=F32), rows)
    n_freq = ATT_DH // 4
    inv_freq = ROPE_THETA ** (-jnp.arange(n_freq, dtype=F32) / n_freq)
    ang = jnp.concatenate([row[:, None] * inv_freq, col[:, None] * inv_freq], axis=-1)
    cos, sin = jnp.cos(ang), jnp.sin(ang)
    cos64 = jnp.concatenate([cos, cos], axis=-1)
    sin64 = jnp.concatenate([-sin, sin], axis=-1)
    return cos64, sin64


def kernel(x, c, ctx, c_ctx, w_ada, b_ada, w_in, ret_decay_logit, att_q_norm, att_k_norm, conv_dw, conv_db, conv_ln_g, conv_ln_b, w_ret_o, w_att_o, w_conv_o, w_out, ln1_g, ln1_b, w_router, router_bias, w_exp_gate, w_exp_up, w_exp_down, w_sh_gate, w_sh_up, w_sh_down, ln2_g, ln2_b):
    b, t, d = x.shape
    n_ctx = ctx.shape[1]
    depth = w_ada.shape[0]
    assert d == D_MODEL and w_in.shape[-1] == D_IN
    geom = _Geom(b, t, n_ctx)
    alpha = float((2 * depth) ** 0.25)

    cos64, sin64 = _rope_tables(t)
    cos128 = jnp.concatenate([cos64, cos64], axis=-1)
    sin128 = jnp.concatenate([sin64, sin64], axis=-1)

    n_rows = -(-(b + 1) // 8) * 8
    cvecs = jnp.zeros((n_rows, d), F32).at[:b].set(c).at[b].set(c_ctx)
    mods_all = _mods_call(cvecs, w_ada, b_ada).reshape(depth, n_rows * 6, 1, d)

    n_tiles = geom.nt // geom.c
    n_blocks = -(-(geom.nt * TOP_K + n_tiles * N_EXPERTS * (RUN_ALIGN - 1)) // MOE_BLOCK) + N_EXPERTS

    xt = jnp.concatenate([x.reshape(geom.nl, d), ctx.reshape(geom.nc, d)], axis=0)
    for l in range(depth):
        mods = mods_all[l]
        w_in_l = _permute_columns(w_in[l]).astype(BF16)
        z = _inproj_call(geom, xt, mods, w_in_l)

        log_gamma = jax.nn.log_sigmoid(ret_decay_logit[l].astype(F32))
        ret = _retention_call(geom, z, log_gamma, cos128, sin128)
        att = _attention_call(geom, z, att_q_norm[l], att_k_norm[l], cos128, sin128)
        cv = _conv_call(geom, z, conv_dw[l], conv_db[l], conv_ln_g[l], conv_ln_b[l])
        x1, h2 = _mix_call(geom, alpha, ret, att, cv, z, xt, mods,
                           w_ret_o[l].astype(BF16), w_att_o[l].astype(BF16), w_conv_o[l].astype(BF16),
                           w_out[l].astype(BF16), ln1_g[l], ln1_b[l])

        top_e, gate_w, pos, counts, cnt_hist = _router_call(geom, h2, w_router[l], router_bias[l])
        before = cnt_hist[:, :, 0].astype(jnp.int32)
        total = counts[:, 0].astype(jnp.int32)
        tile_n = jnp.concatenate([before[1:], total[None, :]], axis=0) - before
        run_rows = (tile_n + RUN_ALIGN - 1) // RUN_ALIGN * RUN_ALIGN
        run_before = jnp.cumsum(run_rows, axis=0) - run_rows
        blocks_e = (jnp.sum(run_rows, axis=0) + MOE_BLOCK - 1) // MOE_BLOCK
        blocks_end = jnp.cumsum(blocks_e)
        start_row = (blocks_end - blocks_e) * MOE_BLOCK
        run_slot = start_row[None, :] + run_before
        run_sorted = jnp.cumsum(run_rows, axis=1) - run_rows
        onehot = top_e[:, :, None] == jnp.arange(N_EXPERTS, dtype=jnp.int32)[None, None, :]
        per_token = lambda table: jnp.sum(jnp.where(onehot, jnp.repeat(table, geom.c, axis=0)[None], 0), axis=-1)
        sidx = per_token(run_sorted) + pos - per_token(before)
        block_ids = jnp.arange(n_blocks, dtype=jnp.int32)
        block_e = jnp.minimum(jnp.sum((blocks_end[None, :] <= block_ids[:, None]).astype(jnp.int32), axis=1),
                              N_EXPERTS - 1)
        n_used = blocks_end[-1:].astype(jnp.int32)
        last_block = jnp.where(blocks_e > 0, blocks_end - 1, n_blocks).astype(jnp.int32)
        run_rows_flat, run_slot_flat = run_rows.reshape(-1), run_slot.reshape(-1).astype(jnp.int32)
        tile_rows = jnp.sum(run_rows, axis=1)

        xs = _dispatch_call(geom, last_block, run_rows_flat, run_slot_flat, tile_rows, sidx, h2, n_blocks)
        ys = _expert_call(l, block_e, n_used, xs, w_exp_gate, w_exp_up, w_exp_down)
        w_sh_gu = jnp.concatenate([w_sh_gate[l], w_sh_up[l]], axis=-1).astype(BF16)
        xt = _combine_call(geom, alpha, run_rows_flat, run_slot_flat, tile_rows, sidx.T, gate_w.T, ys, h2, x1, mods, w_sh_gu,
                           w_sh_down[l].astype(BF16), ln2_g[l], ln2_b[l])
    return xt[:geom.nl].reshape(b, t, d)
```

```python
import functools

import jax
import jax.numpy as jnp
from jax import lax
from jax.experimental import pallas as pl
from jax.experimental.pallas import tpu as pltpu

F32 = jnp.float32
BF16 = jnp.bfloat16
HIGHEST = lax.Precision.HIGHEST

D_MODEL = 1024
GRID_W = 64
EPS = 1e-6

RET_HEADS = 8
RET_DK = 64
RET_DV = 128
RET_CHUNK = 256
RET_W = RET_HEADS * RET_DV

ATT_HEADS = 16
ATT_KV_HEADS = 4
ATT_DH = 64
ATT_GROUP = ATT_HEADS // ATT_KV_HEADS
ATT_W = ATT_HEADS * ATT_DH
ROPE_THETA = 10000.0
ATT_KEY_BLOCK = 2048

CONV_CH = 1024
CONV_K = 31
CONV_HALO = 16

N_EXPERTS = 64
TOP_K = 8
N_GROUPS = 8
TOPK_GROUPS = 4
D_EXPERT = 256
D_SHARED = 256
ROUTED_SCALE = 2.5
MOE_BLOCK = 512

_ORIG = dict(rq=0, rk=512, rv=1024, rg=2048, aq=3072, ak=4096, av=4352, cu=4608, gt=6656)
D_IN = 9728
COL_CU = 0
COL_GT = 2048
COL_RG = 5120
COL_RV = 6144
COL_RQ = 7168
COL_RK = 7680
COL_ATT = 8192
ATT_SECTION = ATT_GROUP * ATT_DH + 2 * ATT_DH


def _column_ranges():
    rng = [(_ORIG["cu"], _ORIG["cu"] + 2 * CONV_CH),
           (_ORIG["gt"], _ORIG["gt"] + 3 * D_MODEL),
           (_ORIG["rg"], _ORIG["rg"] + RET_W),
           (_ORIG["rv"], _ORIG["rv"] + RET_W),
           (_ORIG["rq"], _ORIG["rq"] + RET_HEADS * RET_DK),
           (_ORIG["rk"], _ORIG["rk"] + RET_HEADS * RET_DK)]
    for g in range(ATT_KV_HEADS):
        rng.append((_ORIG["aq"] + g * ATT_GROUP * ATT_DH, _ORIG["aq"] + (g + 1) * ATT_GROUP * ATT_DH))
        rng.append((_ORIG["ak"] + g * ATT_DH, _ORIG["ak"] + (g + 1) * ATT_DH))
        rng.append((_ORIG["av"] + g * ATT_DH, _ORIG["av"] + (g + 1) * ATT_DH))
    cols = [c for a, b in rng for c in range(a, b)]
    assert sorted(cols) == list(range(D_IN))
    return rng


def _permute_columns(w):
    return jnp.concatenate([w[:, a:b] for a, b in _column_ranges()], axis=1)


def _params(n_axes, vmem_mib):
    return pltpu.CompilerParams(dimension_semantics=("arbitrary",) * n_axes,
                                vmem_limit_bytes=vmem_mib * 1024 * 1024)


def _silu(v):
    return v * jax.nn.sigmoid(v)


def _layer_norm(v, g, b):
    mu = jnp.mean(v, axis=-1, keepdims=True)
    d = v - mu
    var = jnp.mean(d * d, axis=-1, keepdims=True)
    return d * lax.rsqrt(var + EPS) * g + b


def _mods_kernel(c_ref, w_ref, b_ref, o_ref):
    s = _silu(c_ref[...])
    o_ref[0] = jnp.dot(s, w_ref[0], preferred_element_type=F32, precision=HIGHEST) + b_ref[0]


def _mods_call(cvecs, w_ada, b_ada):
    n_layers = w_ada.shape[0]
    rows, d = cvecs.shape
    return pl.pallas_call(
        _mods_kernel,
        grid=(n_layers, 6),
        in_specs=[pl.BlockSpec((rows, d), lambda l, j: (0, 0)),
                  pl.BlockSpec((1, d, d), lambda l, j: (l, 0, j)),
                  pl.BlockSpec((1, 1, d), lambda l, j: (l, 0, j))],
        out_specs=pl.BlockSpec((1, rows, d), lambda l, j: (l, 0, j)),
        out_shape=jax.ShapeDtypeStruct((n_layers, rows, 6 * d), F32),
        compiler_params=_params(2, 32),
        name="adaln_mods",
    )(cvecs, w_ada, b_ada.reshape(n_layers, 1, 6 * d))


class _Geom:
    def __init__(self, b, t, c):
        assert t % c == 0 and c % RET_CHUNK == 0 and c % CONV_HALO == 0 and t % ATT_KEY_BLOCK == 0
        self.b, self.t, self.c = b, t, c
        self.nl, self.nc = b * t, b * c
        self.nt = self.nl + self.nc
        self.lat_blocks = t // c
        self.nlb = self.nl // c
        self.p = t + c

    def row_block(self, bi, r):
        return jnp.where(r < self.lat_blocks, bi * self.lat_blocks + r, self.nlb + bi)

    def mod_row(self, i, tm):
        return jnp.where(i * tm < self.nl, (i * tm) // self.t, self.b)


def _mod_spec(geom, tm, which, grid_pos=0):
    d = D_MODEL
    if grid_pos == 0:
        return pl.BlockSpec((1, 1, d), lambda i, *_: (geom.mod_row(i, tm) * 6 + which, 0, 0))
    return pl.BlockSpec((1, 1, d), lambda j, i: (geom.mod_row(i, tm) * 6 + which, 0, 0))


def _inproj_kernel(x_ref, sh_ref, sc_ref, w_ref, o_ref):
    h = x_ref[...] * (1.0 + sc_ref[0]) + sh_ref[0]
    o_ref[...] = jnp.dot(h.astype(BF16), w_ref[...], preferred_element_type=F32).astype(o_ref.dtype)


def _inproj_call(geom, x, mods, w_in_bf16):
    tm = 512 if geom.nc % 512 == 0 and geom.t % 512 == 0 else geom.c
    tn = D_IN // 2
    return pl.pallas_call(
        _inproj_kernel,
        grid=(D_IN // tn, geom.nt // tm),
        in_specs=[pl.BlockSpec((tm, D_MODEL), lambda j, i: (i, 0)),
                  _mod_spec(geom, tm, 0, grid_pos=1),
                  _mod_spec(geom, tm, 1, grid_pos=1),
                  pl.BlockSpec((D_MODEL, tn), lambda j, i: (0, j))],
        out_specs=pl.BlockSpec((tm, tn), lambda j, i: (i, j)),
        out_shape=jax.ShapeDtypeStruct((geom.nt, D_IN), BF16),
        compiler_params=_params(2, 48),
        name="in_proj",
    )(x, mods, mods, w_in_bf16)


def _rot_half_128(v):
    lane = lax.broadcasted_iota(jnp.int32, v.shape, 1)
    return jnp.where((lane % 64) < 32, pltpu.roll(v, 96, 1), pltpu.roll(v, 32, 1))


def _ret_kernel(lg_ref, ql_ref, qc_ref, kl_ref, kc_ref, vl_ref, vc_ref, g_ref, cos_ref, sin_ref, o_ref,
                qs, kts, yf, yb, st, dm, qwb, kwb, gcs, *, t, c):
    ch = RET_CHUNK
    hp = pl.program_id(1)
    r = pl.program_id(2)
    lat_blocks = t // c
    n_lat, n_ctx = t // ch, c // ch

    @pl.when(r == 0)
    def _scan():
        ri = lax.broadcasted_iota(jnp.int32, (ch, ch), 0).astype(F32)
        ci = lax.broadcasted_iota(jnp.int32, (ch, ch), 1).astype(F32)
        rv = lax.broadcasted_iota(jnp.int32, (ch, RET_DV), 0).astype(F32)
        for d in range(2):
            for h in range(2):
                u = 2 * d + h
                lg = lg_ref[d, 2 * hp + h]
                rel = (ri - ci) if d == 0 else (ci - ri)
                dm[u] = jnp.where(rel >= 0.0, jnp.exp(lg * jnp.maximum(rel, 0.0)), 0.0)
                qwb[u] = jnp.exp(lg * ((rv + 1.0) if d == 0 else (float(ch) - rv)))
                kwb[u] = jnp.exp(lg * ((float(ch) - 1.0 - rv) if d == 0 else rv))
                gcs[u] = jnp.exp(jnp.full((RET_DK, RET_DV), lg * float(ch), F32))
                st[u] = jnp.zeros((RET_DK, RET_DV), F32)

        def stage(q, k, seq_rows):
            qs[0, seq_rows, :] = q[:, :RET_DK].astype(BF16)
            qs[1, seq_rows, :] = q[:, RET_DK:].astype(BF16)
            kt = k.T
            kts[0, :, seq_rows] = kt[:RET_DK].astype(BF16)
            kts[1, :, seq_rows] = kt[RET_DK:].astype(BF16)

        kscale = RET_DK ** -0.5
        for cc in range(n_ctx):
            rows = pl.ds(cc * ch, ch)
            stage(qc_ref[rows, :].astype(F32), kc_ref[rows, :].astype(F32) * kscale, rows)

        def stage_lat(cc, carry):
            rows = pl.ds(pl.multiple_of(cc * ch, ch), ch)
            cs, sn = cos_ref[rows, :], sin_ref[rows, :]
            q = ql_ref[rows, :].astype(F32)
            k = kl_ref[rows, :].astype(F32)
            q = q * cs + _rot_half_128(q) * sn
            k = (k * cs + _rot_half_128(k) * sn) * kscale
            stage(q, k, pl.ds(pl.multiple_of(c + cc * ch, ch), ch))
            return carry

        lax.fori_loop(0, n_lat, stage_lat, 0)

        def run_segment(v_ref, seq_off, n):
            def body(i, carry):
                for d, cc in ((0, i), (1, n - 1 - i)):
                    vrows = pl.ds(pl.multiple_of(cc * ch, ch), ch)
                    srows = pl.ds(pl.multiple_of(seq_off + cc * ch, ch), ch)
                    for h in range(2):
                        u = 2 * d + h
                        q = qs[h, srows, :]
                        kt = kts[h, :, srows]
                        v = v_ref[vrows, h * RET_DV:(h + 1) * RET_DV].astype(F32)
                        s = jnp.dot(q, kt, preferred_element_type=F32)
                        y = jnp.dot((s * dm[u]).astype(BF16), v.astype(BF16), preferred_element_type=F32)
                        state = st[u]
                        y = y + jnp.dot(q, state.astype(BF16), preferred_element_type=F32) * qwb[u]
                        dst = yf if d == 0 else yb
                        dst[srows, h * RET_DV:(h + 1) * RET_DV] = y
                        kv = jnp.dot(kt, (v * kwb[u]).astype(BF16), preferred_element_type=F32)
                        st[u] = gcs[u] * state + kv
                return carry

            lax.fori_loop(0, n, body, 0, unroll=2)

        run_segment(vc_ref, 0, n_ctx)
        run_segment(vl_ref, c, n_lat)

    def finish(srows):
        y = yf[srows, :] + yb[srows, :]
        for h in range(2):
            cols = slice(h * RET_DV, (h + 1) * RET_DV)
            yh = y[:, cols]
            mu = jnp.mean(yh, axis=-1, keepdims=True)
            dlt = yh - mu
            var = jnp.mean(dlt * dlt, axis=-1, keepdims=True)
            out = _silu(g_ref[:, cols].astype(F32)) * (dlt * lax.rsqrt(var + EPS))
            o_ref[:, cols] = out.astype(o_ref.dtype)

    @pl.when(r < lat_blocks)
    def _fin_lat():
        finish(pl.ds(pl.multiple_of(c + r * c, c), c))

    @pl.when(r == lat_blocks)
    def _fin_ctx():
        finish(pl.ds(0, c))


def _retention_call(geom, z, log_gamma, cos128, sin128):
    t, c, p = geom.t, geom.c, geom.p
    hpairs = RET_HEADS // 2
    qb, kb = COL_RQ // 128, COL_RK // 128
    vb, gb = COL_RV // 256, COL_RG // 256
    rb = geom.row_block
    in_specs = [
        pl.BlockSpec(memory_space=pltpu.SMEM),
        pl.BlockSpec((t, 128), lambda b, h, r: (b, qb + h)),
        pl.BlockSpec((c, 128), lambda b, h, r: (geom.nlb + b, qb + h)),
        pl.BlockSpec((t, 128), lambda b, h, r: (b, kb + h)),
        pl.BlockSpec((c, 128), lambda b, h, r: (geom.nlb + b, kb + h)),
        pl.BlockSpec((t, 256), lambda b, h, r: (b, vb + h)),
        pl.BlockSpec((c, 256), lambda b, h, r: (geom.nlb + b, vb + h)),
        pl.BlockSpec((c, 256), lambda b, h, r: (rb(b, r), gb + h)),
        pl.BlockSpec((t, 128), lambda b, h, r: (0, 0)),
        pl.BlockSpec((t, 128), lambda b, h, r: (0, 0)),
    ]
    scratch = [
        pltpu.VMEM((2, p, RET_DK), BF16),
        pltpu.VMEM((2, RET_DK, p), BF16),
        pltpu.VMEM((p, 2 * RET_DV), F32),
        pltpu.VMEM((p, 2 * RET_DV), F32),
        pltpu.VMEM((4, RET_DK, RET_DV), F32),
        pltpu.VMEM((4, RET_CHUNK, RET_CHUNK), F32),
        pltpu.VMEM((4, RET_CHUNK, RET_DV), F32),
        pltpu.VMEM((4, RET_CHUNK, RET_DV), F32),
        pltpu.VMEM((4, RET_DK, RET_DV), F32),
    ]
    return pl.pallas_call(
        functools.partial(_ret_kernel, t=t, c=c),
        grid=(geom.b, hpairs, geom.lat_blocks + 1),
        in_specs=in_specs,
        out_specs=pl.BlockSpec((c, 256), lambda b, h, r: (rb(b, r), h)),
        out_shape=jax.ShapeDtypeStruct((geom.nt, RET_W), BF16),
        scratch_shapes=scratch,
        compiler_params=_params(3, 56),
        name="retention",
    )(log_gamma, z, z, z, z, z, z, z, cos128, sin128)


def _rms_heads_128(v, g):
    li = lax.broadcasted_iota(jnp.int32, (128, 128), 0) // ATT_DH
    lj = lax.broadcasted_iota(jnp.int32, (128, 128), 1) // ATT_DH
    avg = jnp.where(li == lj, 1.0 / ATT_DH, 0.0).astype(BF16)
    sq = v * v
    hi = sq.astype(BF16)
    lo = (sq - hi.astype(F32)).astype(BF16)
    ms = jnp.dot(hi, avg, preferred_element_type=F32) + jnp.dot(lo, avg, preferred_element_type=F32)
    return v * lax.rsqrt(ms + EPS) * g


def _att_kernel(qa_ref, qb_ref, kvl_ref, kvc_ref, qn_ref, kn_ref, cos_ref, sin_ref, o_ref,
                kts, vs, m_s, acc_s, *, t, c):
    r = pl.program_id(2)
    lat_blocks = t // c
    dh = ATT_DH
    tk = ATT_KEY_BLOCK
    lane = lax.broadcasted_iota(jnp.int32, (c, 2 * dh), 1)

    def stage_tile(kv, dst, cs, sn):
        k = _rms_heads_128(kv, kn_ref[...])
        if cs is not None:
            k = k * cs + _rot_half_128(k) * sn
        kts[:, dst] = k.T[:dh].astype(BF16)
        vs[dst, :] = jnp.where(lane < dh, pltpu.roll(kv, dh, 1), 1.0).astype(BF16)

    @pl.when(r == 0)
    def _stage_kv():
        stage_tile(kvc_ref[...].astype(F32), pl.ds(0, c), None, None)

        def stage(i, carry):
            rows = pl.ds(pl.multiple_of(i * c, c), c)
            stage_tile(kvl_ref[rows, :].astype(F32), pl.ds(pl.multiple_of(c + i * c, c), c),
                       cos_ref[rows, :], sin_ref[rows, :])
            return carry

        lax.fori_loop(0, lat_blocks, stage, 0)

    is_ctx = r == lat_blocks
    rows = pl.ds(pl.multiple_of(jnp.minimum(r, lat_blocks - 1) * c, c), c)
    cs, sn = cos_ref[rows, :], sin_ref[rows, :]
    q_heads = []
    for src in (qa_ref, qb_ref):
        xn = _rms_heads_128(src[...].astype(F32), qn_ref[...])
        xr = jnp.where(is_ctx, xn, xn * cs + _rot_half_128(xn) * sn) * (dh ** -0.5)
        q_heads.append(xr[:, :dh].astype(BF16))
        q_heads.append(pltpu.roll(xr, dh, 1)[:, :dh].astype(BF16))
    q = jnp.concatenate(q_heads, axis=0)

    m_s[...] = jnp.full(m_s.shape, -jnp.inf, F32)
    acc_s[...] = jnp.zeros(acc_s.shape, F32)

    def flash_step(kt, v):
        n = kt.shape[1]
        s = jnp.dot(q, kt, preferred_element_type=F32)
        m_prev = m_s[...]
        m_next = jnp.maximum(m_prev, jnp.max(s, axis=1, keepdims=True))
        prob = jnp.exp(s - jnp.concatenate([m_next] * (n // 128), axis=1))
        acc_s[...] = acc_s[...] * jnp.exp(m_prev - m_next) + jnp.dot(prob.astype(BF16), v, preferred_element_type=F32)
        m_s[...] = m_next

    flash_step(kts[:, 0:c], vs[0:c, :])

    @pl.when(jnp.logical_not(is_ctx))
    def _latent_keys():
        def lat_step(j, carry):
            krows = pl.ds(pl.multiple_of(c + j * tk, 128), tk)
            flash_step(kts[:, krows], vs[krows, :])
            return carry

        lax.fori_loop(0, t // tk, lat_step, 0, unroll=2)

    outs = []
    for h in range(ATT_GROUP):
        acc = acc_s[h * c:(h + 1) * c, :]
        outs.append(acc * pltpu.roll(1.0 / acc, dh, 1))
    for pair in range(ATT_GROUP // 2):
        both = jnp.where(lane < dh, outs[2 * pair], pltpu.roll(outs[2 * pair + 1], dh, 1))
        o_ref[:, pair * 2 * dh:(pair + 1) * 2 * dh] = both.astype(o_ref.dtype)


def _attention_call(geom, z, q_norm, k_norm, cos128, sin128):
    t, c, p = geom.t, geom.c, geom.p
    ab = COL_ATT // 128
    sec = ATT_SECTION // 128
    rb = geom.row_block
    in_specs = [
        pl.BlockSpec((c, 128), lambda b, g, r: (rb(b, r), ab + sec * g)),
        pl.BlockSpec((c, 128), lambda b, g, r: (rb(b, r), ab + sec * g + 1)),
        pl.BlockSpec((t, 128), lambda b, g, r: (b, ab + sec * g + 2)),
        pl.BlockSpec((c, 128), lambda b, g, r: (geom.nlb + b, ab + sec * g + 2)),
        pl.BlockSpec((1, 128), lambda b, g, r: (0, 0)),
        pl.BlockSpec((1, 128), lambda b, g, r: (0, 0)),
        pl.BlockSpec((t, 128), lambda b, g, r: (0, 0)),
        pl.BlockSpec((t, 128), lambda b, g, r: (0, 0)),
    ]
    two_heads = lambda v: jnp.tile(v.reshape(1, ATT_DH), (1, 2))
    scratch = [
        pltpu.VMEM((ATT_DH, p), BF16),
        pltpu.VMEM((p, 2 * ATT_DH), BF16),
        pltpu.VMEM((ATT_GROUP * c, 128), F32),
        pltpu.VMEM((ATT_GROUP * c, 2 * ATT_DH), F32),
    ]
    return pl.pallas_call(
        functools.partial(_att_kernel, t=t, c=c),
        grid=(geom.b, ATT_KV_HEADS, geom.lat_blocks + 1),
        in_specs=in_specs,
        out_specs=pl.BlockSpec((c, ATT_GROUP * ATT_DH), lambda b, g, r: (rb(b, r), g)),
        out_shape=jax.ShapeDtypeStruct((geom.nt, ATT_W), BF16),
        scratch_shapes=scratch,
        compiler_params=_params(3, 48),
        name="attention",
    )(z, z, z, z, two_heads(q_norm), two_heads(k_norm), cos128, sin128)


def _conv_kernel(a_ref, g_ref, ap_ref, gp_ref, an_ref, gn_ref, w_ref, b_ref, lng_ref, lnb_ref, o_ref,
                 ext, ys, shifted, *, t, c):
    r = pl.program_id(1)
    lat_blocks = t // c
    halo = CONV_HALO
    has_prev = jnp.logical_and(r != 0, r != lat_blocks)
    has_next = jnp.logical_and(r != lat_blocks - 1, r != lat_blocks)
    glu = lambda a, g: a[...].astype(F32) * jax.nn.sigmoid(g[...].astype(F32))
    ext[halo:halo + c, :] = glu(a_ref, g_ref)
    ext[0:halo, :] = jnp.where(has_prev, glu(ap_ref, gp_ref), 0.0)
    ext[halo + c:, :] = jnp.where(has_next, glu(an_ref, gn_ref), 0.0)

    rt = 64
    first = halo - CONV_K // 2
    span = c + 2 * halo - 8
    for s in range(1, 8):
        shifted[s - 1, 0:span, :] = ext[s:s + span, :]

    def lane_block(cb, carry):
        lanes = pl.ds(pl.multiple_of(cb * 128, 128), 128)
        for ti in range(c // rt):
            acc = jnp.zeros((rt, 128), F32)
            for j in range(CONV_K):
                row, s = divmod(ti * rt + first + j, 8)
                src = ext if s == 0 else shifted.at[s - 1]
                acc = acc + w_ref[pl.ds(j, 1), lanes] * src[pl.ds(row * 8, rt), lanes]
            ys[pl.ds(ti * rt, rt), lanes] = acc
        return carry

    lax.fori_loop(0, CONV_CH // 128, lane_block, 0)
    y = ys[...] + b_ref[...]
    o_ref[...] = _silu(_layer_norm(y, lng_ref[...], lnb_ref[...])).astype(o_ref.dtype)


def _conv_call(geom, z, conv_dw, conv_db, ln_g, ln_b):
    t, c = geom.t, geom.c
    rb = geom.row_block
    hb = c // CONV_HALO
    last = geom.nt // CONV_HALO - 1
    prev = lambda b, r: jnp.maximum(rb(b, r) * hb - 1, 0)
    nxt = lambda b, r: jnp.minimum((rb(b, r) + 1) * hb, last)
    w = jnp.zeros((32, CONV_CH), F32).at[:CONV_K].set(conv_dw)
    vec = lambda v: v.reshape(1, CONV_CH)
    cst = pl.BlockSpec((1, CONV_CH), lambda b, r: (0, 0))
    in_specs = [
        pl.BlockSpec((c, CONV_CH), lambda b, r: (rb(b, r), 0)),
        pl.BlockSpec((c, CONV_CH), lambda b, r: (rb(b, r), 1)),
        pl.BlockSpec((CONV_HALO, CONV_CH), lambda b, r: (prev(b, r), 0)),
        pl.BlockSpec((CONV_HALO, CONV_CH), lambda b, r: (prev(b, r), 1)),
        pl.BlockSpec((CONV_HALO, CONV_CH), lambda b, r: (nxt(b, r), 0)),
        pl.BlockSpec((CONV_HALO, CONV_CH), lambda b, r: (nxt(b, r), 1)),
        pl.BlockSpec((32, CONV_CH), lambda b, r: (0, 0)),
        cst, cst, cst,
    ]
    return pl.pallas_call(
        functools.partial(_conv_kernel, t=t, c=c),
        grid=(geom.b, geom.lat_blocks + 1),
        in_specs=in_specs,
        out_specs=pl.BlockSpec((c, CONV_CH), lambda b, r: (rb(b, r), 0)),
        out_shape=jax.ShapeDtypeStruct((geom.nt, CONV_CH), BF16),
        scratch_shapes=[pltpu.VMEM((c + 2 * CONV_HALO, CONV_CH), F32), pltpu.VMEM((c, CONV_CH), F32),
                        pltpu.VMEM((7, c + 2 * CONV_HALO, CONV_CH), F32)],
        compiler_params=_params(2, 32),
        name="conformer_conv",
    )(z, z, z, z, z, z, w, vec(conv_db), vec(ln_g), vec(ln_b))


def _mix_kernel(ret_ref, att_ref, cv_ref, gr_ref, ga_ref, gc_ref, x_ref, g1_ref, sh2_ref, sc2_ref,
                wr_ref, wa_ref, wc_ref, wo_ref, lng_ref, lnb_ref, x1_ref, h2_ref, *, alpha):
    def proj(v_ref, w_ref):
        return jnp.dot(v_ref[...].astype(BF16), w_ref[...], preferred_element_type=F32)

    gate = lambda g_ref: jax.nn.sigmoid(g_ref[...].astype(F32))
    merged = (gate(gr_ref) * proj(ret_ref, wr_ref)
              + gate(ga_ref) * proj(att_ref, wa_ref)
              + gate(gc_ref) * proj(cv_ref, wc_ref))
    y = jnp.dot(merged.astype(BF16), wo_ref[...], preferred_element_type=F32)
    x1 = _layer_norm(alpha * x_ref[...] + g1_ref[0] * y, lng_ref[...], lnb_ref[...])
    x1_ref[...] = x1
    h2_ref[...] = x1 * (1.0 + sc2_ref[0]) + sh2_ref[0]


def _mix_call(geom, alpha, ret, att, cv, z, x, mods, w_ret_o, w_att_o, w_conv_o, w_out, ln_g, ln_b):
    tm = geom.c
    d = D_MODEL
    tile = pl.BlockSpec((tm, d), lambda i: (i, 0))
    gate = lambda k: pl.BlockSpec((tm, d), lambda i: (i, COL_GT // d + k))
    wsp = pl.BlockSpec((d, d), lambda i: (0, 0))
    vsp = pl.BlockSpec((1, d), lambda i: (0, 0))
    return pl.pallas_call(
        functools.partial(_mix_kernel, alpha=alpha),
        grid=(geom.nt // tm,),
        in_specs=[tile, tile, tile, gate(0), gate(1), gate(2), tile,
                  _mod_spec(geom, tm, 2), _mod_spec(geom, tm, 3), _mod_spec(geom, tm, 4),
                  wsp, wsp, wsp, wsp, vsp, vsp],
        out_specs=[tile, tile],
        out_shape=[jax.ShapeDtypeStruct((geom.nt, d), F32)] * 2,
        compiler_params=_params(1, 48),
        name="merge_ln1",
    )(ret, att, cv, z, z, z, x, mods, mods, mods, w_ret_o, w_att_o, w_conv_o, w_out,
      ln_g.reshape(1, d), ln_b.reshape(1, d))


def _router_kernel(h_ref, wr_ref, bias_ref, e_ref, w_ref, pos_ref, cnt_ref, hist_ref, cnt):
    i = pl.program_id(0)
    tm = h_ref.shape[0]
    ne, per = N_EXPERTS, N_EXPERTS // N_GROUPS
    neg = -jnp.inf

    @pl.when(i == 0)
    def _init():
        cnt[...] = jnp.zeros(cnt.shape, F32)

    logits = jnp.dot(h_ref[...], wr_ref[...], preferred_element_type=F32, precision=HIGHEST)
    scores = jax.nn.sigmoid(logits.T[:ne])
    sel = scores + bias_ref[...]

    member = lax.broadcasted_iota(jnp.int32, (per, tm), 0)
    grp_rows = []
    for g in range(N_GROUPS):
        blk = sel[g * per:(g + 1) * per]
        m1 = jnp.max(blk, axis=0, keepdims=True)
        first = jnp.min(jnp.where(blk == m1, member, per), axis=0, keepdims=True)
        m2 = jnp.max(jnp.where(member == first, neg, blk), axis=0, keepdims=True)
        grp_rows.append(m1 + m2)
    gs = jnp.concatenate(grp_rows, axis=0)

    gidx = lax.broadcasted_iota(jnp.int32, (N_GROUPS, tm), 0)
    rank = jnp.zeros((N_GROUPS, tm), jnp.int32)
    for g in range(N_GROUPS):
        row = gs[g:g + 1]
        ahead = jnp.logical_or(row > gs, jnp.logical_and(row == gs, g < gidx))
        rank = rank + ahead.astype(jnp.int32)
    keep = (rank < TOPK_GROUPS).astype(F32)
    keep_e = jnp.concatenate([jnp.broadcast_to(keep[g:g + 1], (per, tm)) for g in range(N_GROUPS)], axis=0)
    cand = jnp.where(keep_e > 0.5, sel, neg)

    eidx = lax.broadcasted_iota(jnp.int32, (ne, tm), 0)
    picks, gates, hots = [], [], []
    chosen = jnp.zeros((ne, tm), F32)
    for _ in range(TOP_K):
        m = jnp.max(cand, axis=0, keepdims=True)
        idx = jnp.min(jnp.where(cand == m, eidx, ne), axis=0, keepdims=True)
        hot = eidx == idx
        picks.append(idx)
        gates.append(jnp.sum(jnp.where(hot, scores, 0.0), axis=0, keepdims=True))
        hots.append(hot)
        chosen = jnp.where(hot, 1.0, chosen)
        cand = jnp.where(hot, neg, cand)
    total = gates[0]
    for gk in gates[1:]:
        total = total + gk

    ti = lax.broadcasted_iota(jnp.int32, (tm, tm), 0)
    tj = lax.broadcasted_iota(jnp.int32, (tm, tm), 1)
    before = jnp.where(ti < tj, 1.0, 0.0).astype(BF16)
    prior = jnp.dot(chosen.astype(BF16), before, preferred_element_type=F32) + cnt[...][:, :1]
    pos = [jnp.sum(jnp.where(hot, prior, 0.0), axis=0, keepdims=True) for hot in hots]

    e_ref[...] = jnp.concatenate(picks, axis=0)
    w_ref[...] = jnp.concatenate([ROUTED_SCALE * gk / total for gk in gates], axis=0)
    pos_ref[...] = jnp.concatenate(pos, axis=0).astype(jnp.int32)
    hist_ref[0] = cnt[...]
    cnt[...] = cnt[...] + jnp.sum(chosen, axis=1, keepdims=True)
    cnt_ref[...] = cnt[...]


def _router_call(geom, h2, w_router, router_bias):
    tm = geom.c
    wr = jnp.zeros((D_MODEL, 128), F32).at[:, :N_EXPERTS].set(w_router)
    tok = pl.BlockSpec((TOP_K, tm), lambda i: (0, i))
    return pl.pallas_call(
        _router_kernel,
        grid=(geom.nt // tm,),
        in_specs=[pl.BlockSpec((tm, D_MODEL), lambda i: (i, 0)),
                  pl.BlockSpec((D_MODEL, 128), lambda i: (0, 0)),
                  pl.BlockSpec((N_EXPERTS, 1), lambda i: (0, 0))],
        out_specs=[tok, tok, tok, pl.BlockSpec((N_EXPERTS, 128), lambda i: (0, 0)),
                   pl.BlockSpec((1, N_EXPERTS, 128), lambda i: (i, 0, 0))],
        out_shape=[jax.ShapeDtypeStruct((TOP_K, geom.nt), jnp.int32),
                   jax.ShapeDtypeStruct((TOP_K, geom.nt), F32),
                   jax.ShapeDtypeStruct((TOP_K, geom.nt), jnp.int32),
                   jax.ShapeDtypeStruct((N_EXPERTS, 128), F32),
                   jax.ShapeDtypeStruct((geom.nt // tm, N_EXPERTS, 128), F32)],
        scratch_shapes=[pltpu.VMEM((N_EXPERTS, 128), F32)],
        compiler_params=_params(1, 32),
        name="moe_router",
    )(h2, wr, router_bias.reshape(N_EXPERTS, 1))


HALF = D_MODEL // 2


def _pack_bf16_pairs(v):
    lo = pltpu.bitcast(v[:, :HALF].astype(BF16).astype(F32), jnp.uint32)
    hi = pltpu.bitcast(v[:, HALF:].astype(BF16).astype(F32), jnp.uint32)
    return jnp.bitwise_or(jnp.right_shift(lo, jnp.uint32(16)), hi)


def _unpack_bf16_pairs(w):
    lo = pltpu.bitcast(jnp.left_shift(w, jnp.uint32(16)), F32)
    hi = pltpu.bitcast(jnp.bitwise_and(w, jnp.uint32(0xFFFF0000)), F32)
    return lo, hi


RUN_ALIGN = 8
SORTED_ROWS = 256 * TOP_K + N_EXPERTS * RUN_ALIGN
RUN_BITS = tuple(range(8, 2, -1))


def _for_each_run_piece(n_ref, src_ref, tile, visit):
    def per_expert(e, off):
        n = n_ref[tile * N_EXPERTS + e]
        src = src_ref[tile * N_EXPERTS + e]

        def pieces(bits):
            for lb in bits:
                done = (n >> (lb + 1)) << (lb + 1)

                @pl.when((n & (1 << lb)) != 0)
                def _piece():
                    visit(pl.multiple_of(off + done, RUN_ALIGN), pl.multiple_of(src + done, RUN_ALIGN), 1 << lb)

        @pl.when(n >= (1 << RUN_BITS[1]))
        def _long_run():
            pieces(RUN_BITS[:2])

        pieces(RUN_BITS[2:])
        return off + n

    lax.fori_loop(0, N_EXPERTS, per_expert, 0)


TOTAL_BITS = tuple(range(11, 2, -1))


def _wait_rows(total, wait_piece):
    for lb in TOTAL_BITS:
        @pl.when((total & (1 << lb)) != 0)
        def _amount():
            wait_piece(1 << lb)


def _dispatch_kernel(last_ref, n_ref, src_ref, tot_ref, sidx_ref, h_ref, xs_out, packed, zblk, sem, zsem):
    tm = h_ref.shape[0]
    i = pl.program_id(0)

    @pl.when(pl.program_id(0) == 0)
    def _zero_tail_blocks():
        zblk[...] = jnp.zeros(zblk.shape, zblk.dtype)

        def zero_copy(e):
            return pltpu.make_async_copy(zblk, xs_out.at[pl.ds(last_ref[e] * MOE_BLOCK, MOE_BLOCK)], zsem)

        def start(e, carry):
            zero_copy(e).start()
            return carry

        def wait(e, carry):
            zero_copy(e).wait()
            return carry

        lax.fori_loop(0, N_EXPERTS, start, 0)
        lax.fori_loop(0, N_EXPERTS, wait, 0)

    rows = lax.broadcasted_iota(jnp.int32, (SORTED_ROWS, tm), 0)
    pick = jnp.zeros((SORTED_ROWS, tm), F32)
    for k in range(TOP_K):
        pick = jnp.where(rows == sidx_ref[k:k + 1, :], 1.0, pick)
    sorted_rows = jnp.dot(pick.astype(BF16), h_ref[...].astype(BF16), preferred_element_type=F32)
    lo = pltpu.bitcast(sorted_rows[:, :HALF], jnp.uint32)
    hi = pltpu.bitcast(sorted_rows[:, HALF:], jnp.uint32)
    packed[...] = jnp.bitwise_or(jnp.right_shift(lo, jnp.uint32(16)), hi)

    def piece(sorted_row, slot_row, rows_):
        return pltpu.make_async_copy(packed.at[pl.ds(sorted_row, rows_)], xs_out.at[pl.ds(slot_row, rows_)], sem)

    _for_each_run_piece(n_ref, src_ref, i, lambda a, b, r: piece(a, b, r).start())
    _wait_rows(tot_ref[i], lambda r: piece(0, 0, r).wait())


def _dispatch_call(geom, last_block, run_rows, run_slot, tile_rows, sidx, h2, n_blocks):
    tm = geom.c
    assert tm * TOP_K + N_EXPERTS * RUN_ALIGN == SORTED_ROWS
    grid_spec = pltpu.PrefetchScalarGridSpec(
        num_scalar_prefetch=4,
        grid=(geom.nt // tm,),
        in_specs=[pl.BlockSpec((TOP_K, tm), lambda i, *_: (0, i)),
                  pl.BlockSpec((tm, D_MODEL), lambda i, *_: (i, 0))],
        out_specs=pl.BlockSpec(memory_space=pl.ANY),
        scratch_shapes=[pltpu.VMEM((SORTED_ROWS, HALF), jnp.uint32), pltpu.VMEM((MOE_BLOCK, HALF), jnp.uint32),
                        pltpu.SemaphoreType.DMA(()), pltpu.SemaphoreType.DMA(())],
    )
    return pl.pallas_call(
        _dispatch_kernel,
        grid_spec=grid_spec,
        out_shape=jax.ShapeDtypeStruct(((n_blocks + 1) * MOE_BLOCK, HALF), jnp.uint32),
        compiler_params=_params(1, 48),
        name="moe_dispatch",
    )(last_block, run_rows, run_slot, tile_rows, sidx, h2)


def _expert_kernel(be_ref, nu_ref, x_ref, wg_ref, wu_ref, wd_ref, o_ref, wgu_s, wd_s):
    i = pl.program_id(0)
    live = i < nu_ref[0]
    changed = jnp.logical_or(i == 0, be_ref[i] != be_ref[jnp.maximum(i - 1, 0)])

    @pl.when(jnp.logical_and(live, changed))
    def _load_expert():
        wgu_s[:, :D_EXPERT] = wg_ref[0, 0].astype(BF16)
        wgu_s[:, D_EXPERT:] = wu_ref[0, 0].astype(BF16)
        wd_s[...] = wd_ref[0, 0].astype(BF16)

    @pl.when(live)
    def _run():
        lo, hi = _unpack_bf16_pairs(x_ref[...])
        x = jnp.concatenate([lo, hi], axis=1).astype(BF16)
        hgu = jnp.dot(x, wgu_s[...], preferred_element_type=F32)
        hid = _silu(hgu[:, :D_EXPERT]) * hgu[:, D_EXPERT:]
        o_ref[...] = _pack_bf16_pairs(jnp.dot(hid.astype(BF16), wd_s[...], preferred_element_type=F32))


def _expert_call(layer, block_e, n_used, xs, w_gate, w_up, w_down):
    n_blocks = xs.shape[0] // MOE_BLOCK - 1
    live = lambda i, be, nu: jnp.minimum(i, nu[0] - 1)
    expert = lambda i, be, nu: (layer, be[live(i, be, nu)], 0, 0)
    grid_spec = pltpu.PrefetchScalarGridSpec(
        num_scalar_prefetch=2,
        grid=(n_blocks,),
        in_specs=[pl.BlockSpec((MOE_BLOCK, HALF), lambda i, be, nu: (live(i, be, nu), 0)),
                  pl.BlockSpec((1, 1, D_MODEL, D_EXPERT), expert),
                  pl.BlockSpec((1, 1, D_MODEL, D_EXPERT), expert),
                  pl.BlockSpec((1, 1, D_EXPERT, D_MODEL), expert)],
        out_specs=pl.BlockSpec((MOE_BLOCK, HALF), lambda i, be, nu: (live(i, be, nu), 0)),
        scratch_shapes=[pltpu.VMEM((D_MODEL, 2 * D_EXPERT), BF16), pltpu.VMEM((D_EXPERT, D_MODEL), BF16)],
    )
    return pl.pallas_call(
        _expert_kernel,
        grid_spec=grid_spec,
        out_shape=jax.ShapeDtypeStruct(xs.shape, jnp.uint32),
        compiler_params=_params(1, 32),
        name="moe_experts",
    )(block_e, n_used, xs, w_gate, w_up, w_down)


def _combine_kernel(n_ref, src_ref, tot_ref, sidx_ref, wt_ref, ys_hbm, h_ref, x_ref, g2_ref, wgu_ref, wd_ref,
                    lng_ref, lnb_ref, o_ref, buf_a, buf_b, sem, *, alpha):
    i = pl.program_id(0)
    n = pl.num_programs(0)
    tm = h_ref.shape[0]
    even = i % 2 == 0

    def piece(buf, slot, sorted_row, slot_row, rows_):
        return pltpu.make_async_copy(ys_hbm.at[pl.ds(slot_row, rows_)], buf.at[pl.ds(sorted_row, rows_)], sem.at[slot])

    def gather(tile, buf, slot):
        _for_each_run_piece(n_ref, src_ref, tile, lambda a, b, r: piece(buf, slot, a, b, r).start())

    def drain(tile, buf, slot):
        _wait_rows(tot_ref[tile], lambda r: piece(buf, slot, 0, 0, r).wait())

    def finish(buf):
        wt = wt_ref[...]
        cols = lax.broadcasted_iota(jnp.int32, (tm, SORTED_ROWS), 1)
        mix = jnp.zeros((tm, SORTED_ROWS), F32)
        for k in range(TOP_K):
            mix = jnp.where(cols == sidx_ref[:, k:k + 1], wt[:, k:k + 1], mix)
        mix = mix.astype(BF16)
        lo, hi = _unpack_bf16_pairs(buf[...])
        routed = jnp.concatenate([jnp.dot(mix, lo.astype(BF16), preferred_element_type=F32),
                                  jnp.dot(mix, hi.astype(BF16), preferred_element_type=F32)], axis=1)
        hgu = jnp.dot(h_ref[...].astype(BF16), wgu_ref[...], preferred_element_type=F32)
        hid = _silu(hgu[:, :D_SHARED]) * hgu[:, D_SHARED:]
        shared = jnp.dot(hid.astype(BF16), wd_ref[...], preferred_element_type=F32)
        o_ref[...] = _layer_norm(alpha * x_ref[...] + g2_ref[0] * (routed + shared), lng_ref[...], lnb_ref[...])

    @pl.when(i == 0)
    def _first():
        buf_a[...] = jnp.zeros(buf_a.shape, buf_a.dtype)
        buf_b[...] = jnp.zeros(buf_b.shape, buf_b.dtype)
        gather(i, buf_a, 0)

    @pl.when(jnp.logical_and(even, i + 1 < n))
    def _ahead_b():
        gather(i + 1, buf_b, 1)

    @pl.when(jnp.logical_and(jnp.logical_not(even), i + 1 < n))
    def _ahead_a():
        gather(i + 1, buf_a, 0)

    @pl.when(even)
    def _finish_a():
        drain(i, buf_a, 0)
        finish(buf_a)

    @pl.when(jnp.logical_not(even))
    def _finish_b():
        drain(i, buf_b, 1)
        finish(buf_b)


def _combine_call(geom, alpha, run_rows, run_slot, tile_rows, sidx_tok, w_tok, ys, h2, x1, mods, w_sh_gu, w_sh_down, ln_g, ln_b):
    tm = geom.c
    d = D_MODEL
    n = geom.nt // tm
    tile = pl.BlockSpec((tm, d), lambda i, *_: (i, 0))
    vsp = pl.BlockSpec((1, d), lambda i, *_: (0, 0))
    per_tok = pl.BlockSpec((tm, TOP_K), lambda i, *_: (i, 0))
    grid_spec = pltpu.PrefetchScalarGridSpec(
        num_scalar_prefetch=3,
        grid=(n,),
        in_specs=[per_tok, per_tok,
                  pl.BlockSpec(memory_space=pl.ANY),
                  tile, tile, _mod_spec(geom, tm, 5),
                  pl.BlockSpec((d, 2 * D_SHARED), lambda i, *_: (0, 0)),
                  pl.BlockSpec((D_SHARED, d), lambda i, *_: (0, 0)),
                  vsp, vsp],
        out_specs=tile,
        scratch_shapes=[pltpu.VMEM((SORTED_ROWS, HALF), jnp.uint32), pltpu.VMEM((SORTED_ROWS, HALF), jnp.uint32),
                        pltpu.SemaphoreType.DMA((2,))],
    )
    return pl.pallas_call(
        functools.partial(_combine_kernel, alpha=alpha),
        grid_spec=grid_spec,
        out_shape=jax.ShapeDtypeStruct((geom.nt, d), F32),
        compiler_params=_params(1, 56),
        name="moe_combine_ln2",
    )(run_rows, run_slot, tile_rows, sidx_tok, w_tok, ys, h2, x1, mods, w_sh_gu, w_sh_down,
      ln_g.reshape(1, d), ln_b.reshape(1, d))


def _rope_tables(t):
    rows = t // GRID_W
    row = jnp.repeat(jnp.arange(rows, dtype=F32), GRID_W)
    col = jnp.tile(jnp.arange(GRID_W, dtype=F32), rows)
    n_freq = ATT_DH // 4
    inv_freq = ROPE_THETA ** (-jnp.arange(n_freq, dtype=F32) / n_freq)
    ang = jnp.concatenate([row[:, None] * inv_freq, col[:, None] * inv_freq], axis=-1)
    cos, sin = jnp.cos(ang), jnp.sin(ang)
    cos64 = jnp.concatenate([cos, cos], axis=-1)
    sin64 = jnp.concatenate([-sin, sin], axis=-1)
    return cos64, sin64


def kernel(x, c, ctx, c_ctx, w_ada, b_ada, w_in, ret_decay_logit, att_q_norm, att_k_norm, conv_dw, conv_db, conv_ln_g, conv_ln_b, w_ret_o, w_att_o, w_conv_o, w_out, ln1_g, ln1_b, w_router, router_bias, w_exp_gate, w_exp_up, w_exp_down, w_sh_gate, w_sh_up, w_sh_down, ln2_g, ln2_b):
    b, t, d = x.shape
    n_ctx = ctx.shape[1]
    depth = w_ada.shape[0]
    assert d == D_MODEL and w_in.shape[-1] == D_IN
    geom = _Geom(b, t, n_ctx)
    alpha = float((2 * depth) ** 0.25)

    cos64, sin64 = _rope_tables(t)
    cos128 = jnp.concatenate([cos64, cos64], axis=-1)
    sin128 = jnp.concatenate([sin64, sin64], axis=-1)

    n_rows = -(-(b + 1) // 8) * 8
    cvecs = jnp.zeros((n_rows, d), F32).at[:b].set(c).at[b].set(c_ctx)
    mods_all = _mods_call(cvecs, w_ada, b_ada).reshape(depth, n_rows * 6, 1, d)

    n_tiles = geom.nt // geom.c
    n_blocks = -(-(geom.nt * TOP_K + n_tiles * N_EXPERTS * (RUN_ALIGN - 1)) // MOE_BLOCK) + N_EXPERTS

    xt = jnp.concatenate([x.reshape(geom.nl, d), ctx.reshape(geom.nc, d)], axis=0)
    for l in range(depth):
        mods = mods_all[l]
        w_in_l = _permute_columns(w_in[l]).astype(BF16)
        z = _inproj_call(geom, xt, mods, w_in_l)

        log_gamma = jax.nn.log_sigmoid(ret_decay_logit[l].astype(F32))
        ret = _retention_call(geom, z, log_gamma, cos128, sin128)
        att = _attention_call(geom, z, att_q_norm[l], att_k_norm[l], cos128, sin128)
        cv = _conv_call(geom, z, conv_dw[l], conv_db[l], conv_ln_g[l], conv_ln_b[l])
        x1, h2 = _mix_call(geom, alpha, ret, att, cv, z, xt, mods,
                           w_ret_o[l].astype(BF16), w_att_o[l].astype(BF16), w_conv_o[l].astype(BF16),
                           w_out[l].astype(BF16), ln1_g[l], ln1_b[l])

        top_e, gate_w, pos, counts, cnt_hist = _router_call(geom, h2, w_router[l], router_bias[l])
        before = cnt_hist[:, :, 0].astype(jnp.int32)
        total = counts[:, 0].astype(jnp.int32)
        tile_n = jnp.concatenate([before[1:], total[None, :]], axis=0) - before
        run_rows = (tile_n + RUN_ALIGN - 1) // RUN_ALIGN * RUN_ALIGN
        run_before = jnp.cumsum(run_rows, axis=0) - run_rows
        blocks_e = (jnp.sum(run_rows, axis=0) + MOE_BLOCK - 1) // MOE_BLOCK
        blocks_end = jnp.cumsum(blocks_e)
        start_row = (blocks_end - blocks_e) * MOE_BLOCK
        run_slot = start_row[None, :] + run_before
        run_sorted = jnp.cumsum(run_rows, axis=1) - run_rows
        onehot = top_e[:, :, None] == jnp.arange(N_EXPERTS, dtype=jnp.int32)[None, None, :]
        per_token = lambda table: jnp.sum(jnp.where(onehot, jnp.repeat(table, geom.c, axis=0)[None], 0), axis=-1)
        sidx = per_token(run_sorted) + pos - per_token(before)
        block_ids = jnp.arange(n_blocks, dtype=jnp.int32)
        block_e = jnp.minimum(jnp.sum((blocks_end[None, :] <= block_ids[:, None]).astype(jnp.int32), axis=1),
                              N_EXPERTS - 1)
        n_used = blocks_end[-1:].astype(jnp.int32)
        last_block = jnp.where(blocks_e > 0, blocks_end - 1, n_blocks).astype(jnp.int32)
        run_rows_flat, run_slot_flat = run_rows.reshape(-1), run_slot.reshape(-1).astype(jnp.int32)
        tile_rows = jnp.sum(run_rows, axis=1)

        xs = _dispatch_call(geom, last_block, run_rows_flat, run_slot_flat, tile_rows, sidx, h2, n_blocks)
        ys = _expert_call(l, block_e, n_used, xs, w_exp_gate, w_exp_up, w_exp_down)
        w_sh_gu = jnp.concatenate([w_sh_gate[l], w_sh_up[l]], axis=-1).astype(BF16)
        xt = _combine_call(geom, alpha, run_rows_flat, run_slot_flat, tile_rows, sidx.T, gate_w.T, ys, h2, x1, mods, w_sh_gu,
                           w_sh_down[l].astype(BF16), ln2_g[l], ln2_b[l])
    return xt[:geom.nl].reshape(b, t, d)
```

```python
import functools

import jax
import jax.numpy as jnp
from jax import lax
from jax.experimental import pallas as pl
from jax.experimental.pallas import tpu as pltpu

F32 = jnp.float32
BF16 = jnp.bfloat16
HIGHEST = lax.Precision.HIGHEST

D_MODEL = 1024
GRID_W = 64
EPS = 1e-6

RET_HEADS = 8
RET_DK = 64
RET_DV = 128
RET_CHUNK = 256
RET_W = RET_HEADS * RET_DV

ATT_HEADS = 16
ATT_KV_HEADS = 4
ATT_DH = 64
ATT_GROUP = ATT_HEADS // ATT_KV_HEADS
ATT_W = ATT_HEADS * ATT_DH
ROPE_THETA = 10000.0
ATT_KEY_BLOCK = 2048

CONV_CH = 1024
CONV_K = 31
CONV_HALO = 16

N_EXPERTS = 64
TOP_K = 8
N_GROUPS = 8
TOPK_GROUPS = 4
D_EXPERT = 256
D_SHARED = 256
ROUTED_SCALE = 2.5
MOE_BLOCK = 512

_ORIG = dict(rq=0, rk=512, rv=1024, rg=2048, aq=3072, ak=4096, av=4352, cu=4608, gt=6656)
D_IN = 9728
COL_CU = 0
COL_GT = 2048
COL_RG = 5120
COL_RV = 6144
COL_RQ = 7168
COL_RK = 7680
COL_ATT = 8192
ATT_SECTION = ATT_GROUP * ATT_DH + 2 * ATT_DH


def _column_ranges():
    rng = [(_ORIG["cu"], _ORIG["cu"] + 2 * CONV_CH),
           (_ORIG["gt"], _ORIG["gt"] + 3 * D_MODEL),
           (_ORIG["rg"], _ORIG["rg"] + RET_W),
           (_ORIG["rv"], _ORIG["rv"] + RET_W),
           (_ORIG["rq"], _ORIG["rq"] + RET_HEADS * RET_DK),
           (_ORIG["rk"], _ORIG["rk"] + RET_HEADS * RET_DK)]
    for g in range(ATT_KV_HEADS):
        rng.append((_ORIG["aq"] + g * ATT_GROUP * ATT_DH, _ORIG["aq"] + (g + 1) * ATT_GROUP * ATT_DH))
        rng.append((_ORIG["ak"] + g * ATT_DH, _ORIG["ak"] + (g + 1) * ATT_DH))
        rng.append((_ORIG["av"] + g * ATT_DH, _ORIG["av"] + (g + 1) * ATT_DH))
    cols = [c for a, b in rng for c in range(a, b)]
    assert sorted(cols) == list(range(D_IN))
    return rng


def _permute_columns(w):
    return jnp.concatenate([w[:, a:b] for a, b in _column_ranges()], axis=1)


def _params(n_axes, vmem_mib):
    return pltpu.CompilerParams(dimension_semantics=("arbitrary",) * n_axes,
                                vmem_limit_bytes=vmem_mib * 1024 * 1024)


def _silu(v):
    return v * jax.nn.sigmoid(v)


def _layer_norm(v, g, b):
    mu = jnp.mean(v, axis=-1, keepdims=True)
    d = v - mu
    var = jnp.mean(d * d, axis=-1, keepdims=True)
    return d * lax.rsqrt(var + EPS) * g + b


def _mods_kernel(c_ref, w_ref, b_ref, o_ref):
    s = _silu(c_ref[...])
    o_ref[0] = jnp.dot(s, w_ref[0], preferred_element_type=F32, precision=HIGHEST) + b_ref[0]


def _mods_call(cvecs, w_ada, b_ada):
    n_layers = w_ada.shape[0]
    rows, d = cvecs.shape
    return pl.pallas_call(
        _mods_kernel,
        grid=(n_layers, 6),
        in_specs=[pl.BlockSpec((rows, d), lambda l, j: (0, 0)),
                  pl.BlockSpec((1, d, d), lambda l, j: (l, 0, j)),
                  pl.BlockSpec((1, 1, d), lambda l, j: (l, 0, j))],
        out_specs=pl.BlockSpec((1, rows, d), lambda l, j: (l, 0, j)),
        out_shape=jax.ShapeDtypeStruct((n_layers, rows, 6 * d), F32),
        compiler_params=_params(2, 32),
        name="adaln_mods",
    )(cvecs, w_ada, b_ada.reshape(n_layers, 1, 6 * d))


class _Geom:
    def __init__(self, b, t, c):
        assert t % c == 0 and c % RET_CHUNK == 0 and c % CONV_HALO == 0 and t % ATT_KEY_BLOCK == 0
        self.b, self.t, self.c = b, t, c
        self.nl, self.nc = b * t, b * c
        self.nt = self.nl + self.nc
        self.lat_blocks = t // c
        self.nlb = self.nl // c
        self.p = t + c

    def row_block(self, bi, r):
        return jnp.where(r < self.lat_blocks, bi * self.lat_blocks + r, self.nlb + bi)

    def mod_row(self, i, tm):
        return jnp.where(i * tm < self.nl, (i * tm) // self.t, self.b)


def _mod_spec(geom, tm, which, grid_pos=0):
    d = D_MODEL
    if grid_pos == 0:
        return pl.BlockSpec((1, 1, d), lambda i, *_: (geom.mod_row(i, tm) * 6 + which, 0, 0))
    return pl.BlockSpec((1, 1, d), lambda j, i: (geom.mod_row(i, tm) * 6 + which, 0, 0))


def _inproj_kernel(x_ref, sh_ref, sc_ref, w_ref, o_ref):
    h = x_ref[...] * (1.0 + sc_ref[0]) + sh_ref[0]
    o_ref[...] = jnp.dot(h.astype(BF16), w_ref[...], preferred_element_type=F32).astype(o_ref.dtype)


def _inproj_call(geom, x, mods, w_in_bf16):
    tm = 512 if geom.nc % 512 == 0 and geom.t % 512 == 0 else geom.c
    tn = D_IN // 2
    return pl.pallas_call(
        _inproj_kernel,
        grid=(D_IN // tn, geom.nt // tm),
        in_specs=[pl.BlockSpec((tm, D_MODEL), lambda j, i: (i, 0)),
                  _mod_spec(geom, tm, 0, grid_pos=1),
                  _mod_spec(geom, tm, 1, grid_pos=1),
                  pl.BlockSpec((D_MODEL, tn), lambda j, i: (0, j))],
        out_specs=pl.BlockSpec((tm, tn), lambda j, i: (i, j)),
        out_shape=jax.ShapeDtypeStruct((geom.nt, D_IN), BF16),
        compiler_params=_params(2, 48),
        name="in_proj",
    )(x, mods, mods, w_in_bf16)


def _rot_half_128(v):
    lane = lax.broadcasted_iota(jnp.int32, v.shape, 1)
    return jnp.where((lane % 64) < 32, pltpu.roll(v, 96, 1), pltpu.roll(v, 32, 1))


def _ret_kernel(lg_ref, ql_ref, qc_ref, kl_ref, kc_ref, vl_ref, vc_ref, g_ref, cos_ref, sin_ref, o_ref,
                qs, kts, yf, yb, st, dm, qwb, kwb, gcs, *, t, c):
    ch = RET_CHUNK
    hp = pl.program_id(1)
    r = pl.program_id(2)
    lat_blocks = t // c
    n_lat, n_ctx = t // ch, c // ch

    @pl.when(r == 0)
    def _scan():
        ri = lax.broadcasted_iota(jnp.int32, (ch, ch), 0).astype(F32)
        ci = lax.broadcasted_iota(jnp.int32, (ch, ch), 1).astype(F32)
        rv = lax.broadcasted_iota(jnp.int32, (ch, RET_DV), 0).astype(F32)
        for d in range(2):
            for h in range(2):
                u = 2 * d + h
                lg = lg_ref[d, 2 * hp + h]
                rel = (ri - ci) if d == 0 else (ci - ri)
                dm[u] = jnp.where(rel >= 0.0, jnp.exp(lg * jnp.maximum(rel, 0.0)), 0.0)
                qwb[u] = jnp.exp(lg * ((rv + 1.0) if d == 0 else (float(ch) - rv)))
                kwb[u] = jnp.exp(lg * ((float(ch) - 1.0 - rv) if d == 0 else rv))
                gcs[u] = jnp.exp(jnp.full((RET_DK, RET_DV), lg * float(ch), F32))
                st[u] = jnp.zeros((RET_DK, RET_DV), F32)

        def stage(q, k, seq_rows):
            qs[0, seq_rows, :] = q[:, :RET_DK].astype(BF16)
            qs[1, seq_rows, :] = q[:, RET_DK:].astype(BF16)
            kt = k.T
            kts[0, :, seq_rows] = kt[:RET_DK].astype(BF16)
            kts[1, :, seq_rows] = kt[RET_DK:].astype(BF16)

        kscale = RET_DK ** -0.5
        for cc in range(n_ctx):
            rows = pl.ds(cc * ch, ch)
            stage(qc_ref[rows, :].astype(F32), kc_ref[rows, :].astype(F32) * kscale, rows)

        def stage_lat(cc, carry):
            rows = pl.ds(pl.multiple_of(cc * ch, ch), ch)
            cs, sn = cos_ref[rows, :], sin_ref[rows, :]
            q = ql_ref[rows, :].astype(F32)
            k = kl_ref[rows, :].astype(F32)
            q = q * cs + _rot_half_128(q) * sn
            k = (k * cs + _rot_half_128(k) * sn) * kscale
            stage(q, k, pl.ds(pl.multiple_of(c + cc * ch, ch), ch))
            return carry

        lax.fori_loop(0, n_lat, stage_lat, 0)

        def run_segment(v_ref, seq_off, n):
            def body(i, carry):
                for d, cc in ((0, i), (1, n - 1 - i)):
                    vrows = pl.ds(pl.multiple_of(cc * ch, ch), ch)
                    srows = pl.ds(pl.multiple_of(seq_off + cc * ch, ch), ch)
                    for h in range(2):
                        u = 2 * d + h
                        q = qs[h, srows, :]
                        kt = kts[h, :, srows]
                        v = v_ref[vrows, h * RET_DV:(h + 1) * RET_DV].astype(F32)
                        s = jnp.dot(q, kt, preferred_element_type=F32)
                        y = jnp.dot((s * dm[u]).astype(BF16), v.astype(BF16), preferred_element_type=F32)
                        state = st[u]
                        y = y + jnp.dot(q, state.astype(BF16), preferred_element_type=F32) * qwb[u]
                        dst = yf if d == 0 else yb
                        dst[srows, h * RET_DV:(h + 1) * RET_DV] = y
                        kv = jnp.dot(kt, (v * kwb[u]).astype(BF16), preferred_element_type=F32)
                        st[u] = gcs[u] * state + kv
                return carry

            lax.fori_loop(0, n, body, 0, unroll=2)

        run_segment(vc_ref, 0, n_ctx)
        run_segment(vl_ref, c, n_lat)

    def finish(srows):
        y = yf[srows, :] + yb[srows, :]
        for h in range(2):
            cols = slice(h * RET_DV, (h + 1) * RET_DV)
            yh = y[:, cols]
            mu = jnp.mean(yh, axis=-1, keepdims=True)
            dlt = yh - mu
            var = jnp.mean(dlt * dlt, axis=-1, keepdims=True)
            out = _silu(g_ref[:, cols].astype(F32)) * (dlt * lax.rsqrt(var + EPS))
            o_ref[:, cols] = out.astype(o_ref.dtype)

    @pl.when(r < lat_blocks)
    def _fin_lat():
        finish(pl.ds(pl.multiple_of(c + r * c, c), c))

    @pl.when(r == lat_blocks)
    def _fin_ctx():
        finish(pl.ds(0, c))


def _retention_call(geom, z, log_gamma, cos128, sin128):
    t, c, p = geom.t, geom.c, geom.p
    hpairs = RET_HEADS // 2
    qb, kb = COL_RQ // 128, COL_RK // 128
    vb, gb = COL_RV // 256, COL_RG // 256
    rb = geom.row_block
    in_specs = [
        pl.BlockSpec(memory_space=pltpu.SMEM),
        pl.BlockSpec((t, 128), lambda b, h, r: (b, qb + h)),
        pl.BlockSpec((c, 128), lambda b, h, r: (geom.nlb + b, qb + h)),
        pl.BlockSpec((t, 128), lambda b, h, r: (b, kb + h)),
        pl.BlockSpec((c, 128), lambda b, h, r: (geom.nlb + b, kb + h)),
        pl.BlockSpec((t, 256), lambda b, h, r: (b, vb + h)),
        pl.BlockSpec((c, 256), lambda b, h, r: (geom.nlb + b, vb + h)),
        pl.BlockSpec((c, 256), lambda b, h, r: (rb(b, r), gb + h)),
        pl.BlockSpec((t, 128), lambda b, h, r: (0, 0)),
        pl.BlockSpec((t, 128), lambda b, h, r: (0, 0)),
    ]
    scratch = [
        pltpu.VMEM((2, p, RET_DK), BF16),
        pltpu.VMEM((2, RET_DK, p), BF16),
        pltpu.VMEM((p, 2 * RET_DV), F32),
        pltpu.VMEM((p, 2 * RET_DV), F32),
        pltpu.VMEM((4, RET_DK, RET_DV), F32),
        pltpu.VMEM((4, RET_CHUNK, RET_CHUNK), F32),
        pltpu.VMEM((4, RET_CHUNK, RET_DV), F32),
        pltpu.VMEM((4, RET_CHUNK, RET_DV), F32),
        pltpu.VMEM((4, RET_DK, RET_DV), F32),
    ]
    return pl.pallas_call(
        functools.partial(_ret_kernel, t=t, c=c),
        grid=(geom.b, hpairs, geom.lat_blocks + 1),
        in_specs=in_specs,
        out_specs=pl.BlockSpec((c, 256), lambda b, h, r: (rb(b, r), h)),
        out_shape=jax.ShapeDtypeStruct((geom.nt, RET_W), BF16),
        scratch_shapes=scratch,
        compiler_params=_params(3, 56),
        name="retention",
    )(log_gamma, z, z, z, z, z, z, z, cos128, sin128)


def _rms_heads_128(v, g):
    li = lax.broadcasted_iota(jnp.int32, (128, 128), 0) // ATT_DH
    lj = lax.broadcasted_iota(jnp.int32, (128, 128), 1) // ATT_DH
    avg = jnp.where(li == lj, 1.0 / ATT_DH, 0.0).astype(BF16)
    sq = v * v
    hi = sq.astype(BF16)
    lo = (sq - hi.astype(F32)).astype(BF16)
    ms = jnp.dot(hi, avg, preferred_element_type=F32) + jnp.dot(lo, avg, preferred_element_type=F32)
    return v * lax.rsqrt(ms + EPS) * g


def _att_kernel(qa_ref, qb_ref, kvl_ref, kvc_ref, qn_ref, kn_ref, cos_ref, sin_ref, o_ref,
                kts, vs, m_s, acc_s, *, t, c):
    r = pl.program_id(2)
    lat_blocks = t // c
    dh = ATT_DH
    tk = ATT_KEY_BLOCK
    lane = lax.broadcasted_iota(jnp.int32, (c, 2 * dh), 1)

    def stage_tile(kv, dst, cs, sn):
        k = _rms_heads_128(kv, kn_ref[...])
        if cs is not None:
            k = k * cs + _rot_half_128(k) * sn
        kts[:, dst] = k.T[:dh].astype(BF16)
        vs[dst, :] = jnp.where(lane < dh, pltpu.roll(kv, dh, 1), 1.0).astype(BF16)

    @pl.when(r == 0)
    def _stage_kv():
        stage_tile(kvc_ref[...].astype(F32), pl.ds(0, c), None, None)

        def stage(i, carry):
            rows = pl.ds(pl.multiple_of(i * c, c), c)
            stage_tile(kvl_ref[rows, :].astype(F32), pl.ds(pl.multiple_of(c + i * c, c), c),
                       cos_ref[rows, :], sin_ref[rows, :])
            return carry

        lax.fori_loop(0, lat_blocks, stage, 0)

    is_ctx = r == lat_blocks
    rows = pl.ds(pl.multiple_of(jnp.minimum(r, lat_blocks - 1) * c, c), c)
    cs, sn = cos_ref[rows, :], sin_ref[rows, :]
    q_heads = []
    for src in (qa_ref, qb_ref):
        xn = _rms_heads_128(src[...].astype(F32), qn_ref[...])
        xr = jnp.where(is_ctx, xn, xn * cs + _rot_half_128(xn) * sn) * (dh ** -0.5)
        q_heads.append(xr[:, :dh].astype(BF16))
        q_heads.append(pltpu.roll(xr, dh, 1)[:, :dh].astype(BF16))
    q = jnp.concatenate(q_heads, axis=0)

    m_s[...] = jnp.full(m_s.shape, -jnp.inf, F32)
    acc_s[...] = jnp.zeros(acc_s.shape, F32)

    def flash_step(kt, v):
        n = kt.shape[1]
        s = jnp.dot(q, kt, preferred_element_type=F32)
        m_prev = m_s[...]
        m_next = jnp.maximum(m_prev, jnp.max(s, axis=1, keepdims=True))
        prob = jnp.exp(s - jnp.concatenate([m_next] * (n // 128), axis=1))
        acc_s[...] = acc_s[...] * jnp.exp(m_prev - m_next) + jnp.dot(prob.astype(BF16), v, preferred_element_type=F32)
        m_s[...] = m_next

    flash_step(kts[:, 0:c], vs[0:c, :])

    @pl.when(jnp.logical_not(is_ctx))
    def _latent_keys():
        def lat_step(j, carry):
            krows = pl.ds(pl.multiple_of(c + j * tk, 128), tk)
            flash_step(kts[:, krows], vs[krows, :])
            return carry

        lax.fori_loop(0, t // tk, lat_step, 0, unroll=2)

    outs = []
    for h in range(ATT_GROUP):
        acc = acc_s[h * c:(h + 1) * c, :]
        outs.append(acc * pltpu.roll(1.0 / acc, dh, 1))
    for pair in range(ATT_GROUP // 2):
        both = jnp.where(lane < dh, outs[2 * pair], pltpu.roll(outs[2 * pair + 1], dh, 1))
        o_ref[:, pair * 2 * dh:(pair + 1) * 2 * dh] = both.astype(o_ref.dtype)


def _attention_call(geom, z, q_norm, k_norm, cos128, sin128):
    t, c, p = geom.t, geom.c, geom.p
    ab = COL_ATT // 128
    sec = ATT_SECTION // 128
    rb = geom.row_block
    in_specs = [
        pl.BlockSpec((c, 128), lambda b, g, r: (rb(b, r), ab + sec * g)),
        pl.BlockSpec((c, 128), lambda b, g, r: (rb(b, r), ab + sec * g + 1)),
        pl.BlockSpec((t, 128), lambda b, g, r: (b, ab + sec * g + 2)),
        pl.BlockSpec((c, 128), lambda b, g, r: (geom.nlb + b, ab + sec * g + 2)),
        pl.BlockSpec((1, 128), lambda b, g, r: (0, 0)),
        pl.BlockSpec((1, 128), lambda b, g, r: (0, 0)),
        pl.BlockSpec((t, 128), lambda b, g, r: (0, 0)),
        pl.BlockSpec((t, 128), lambda b, g, r: (0, 0)),
    ]
    two_heads = lambda v: jnp.tile(v.reshape(1, ATT_DH), (1, 2))
    scratch = [
        pltpu.VMEM((ATT_DH, p), BF16),
        pltpu.VMEM((p, 2 * ATT_DH), BF16),
        pltpu.VMEM((ATT_GROUP * c, 128), F32),
        pltpu.VMEM((ATT_GROUP * c, 2 * ATT_DH), F32),
    ]
    return pl.pallas_call(
        functools.partial(_att_kernel, t=t, c=c),
        grid=(geom.b, ATT_KV_HEADS, geom.lat_blocks + 1),
        in_specs=in_specs,
        out_specs=pl.BlockSpec((c, ATT_GROUP * ATT_DH), lambda b, g, r: (rb(b, r), g)),
        out_shape=jax.ShapeDtypeStruct((geom.nt, ATT_W), BF16),
        scratch_shapes=scratch,
        compiler_params=_params(3, 48),
        name="attention",
    )(z, z, z, z, two_heads(q_norm), two_heads(k_norm), cos128, sin128)


def _conv_kernel(a_ref, g_ref, ap_ref, gp_ref, an_ref, gn_ref, w_ref, b_ref, lng_ref, lnb_ref, o_ref,
                 ext, ys, shifted, *, t, c):
    r = pl.program_id(1)
    lat_blocks = t // c
    halo = CONV_HALO
    has_prev = jnp.logical_and(r != 0, r != lat_blocks)
    has_next = jnp.logical_and(r != lat_blocks - 1, r != lat_blocks)
    glu = lambda a, g: a[...].astype(F32) * jax.nn.sigmoid(g[...].astype(F32))
    ext[halo:halo + c, :] = glu(a_ref, g_ref)
    ext[0:halo, :] = jnp.where(has_prev, glu(ap_ref, gp_ref), 0.0)
    ext[halo + c:, :] = jnp.where(has_next, glu(an_ref, gn_ref), 0.0)

    rt = 64
    first = halo - CONV_K // 2
    span = c + 2 * halo - 8
    for s in range(1, 8):
        shifted[s - 1, 0:span, :] = ext[s:s + span, :]

    def lane_block(cb, carry):
        lanes = pl.ds(pl.multiple_of(cb * 128, 128), 128)
        for ti in range(c // rt):
            acc = jnp.zeros((rt, 128), F32)
            for j in range(CONV_K):
                row, s = divmod(ti * rt + first + j, 8)
                src = ext if s == 0 else shifted.at[s - 1]
                acc = acc + w_ref[pl.ds(j, 1), lanes] * src[pl.ds(row * 8, rt), lanes]
            ys[pl.ds(ti * rt, rt), lanes] = acc
        return carry

    lax.fori_loop(0, CONV_CH // 128, lane_block, 0)
    y = ys[...] + b_ref[...]
    o_ref[...] = _silu(_layer_norm(y, lng_ref[...], lnb_ref[...])).astype(o_ref.dtype)


def _conv_call(geom, z, conv_dw, conv_db, ln_g, ln_b):
    t, c = geom.t, geom.c
    rb = geom.row_block
    hb = c // CONV_HALO
    last = geom.nt // CONV_HALO - 1
    prev = lambda b, r: jnp.maximum(rb(b, r) * hb - 1, 0)
    nxt = lambda b, r: jnp.minimum((rb(b, r) + 1) * hb, last)
    w = jnp.zeros((32, CONV_CH), F32).at[:CONV_K].set(conv_dw)
    vec = lambda v: v.reshape(1, CONV_CH)
    cst = pl.BlockSpec((1, CONV_CH), lambda b, r: (0, 0))
    in_specs = [
        pl.BlockSpec((c, CONV_CH), lambda b, r: (rb(b, r), 0)),
        pl.BlockSpec((c, CONV_CH), lambda b, r: (rb(b, r), 1)),
        pl.BlockSpec((CONV_HALO, CONV_CH), lambda b, r: (prev(b, r), 0)),
        pl.BlockSpec((CONV_HALO, CONV_CH), lambda b, r: (prev(b, r), 1)),
        pl.BlockSpec((CONV_HALO, CONV_CH), lambda b, r: (nxt(b, r), 0)),
        pl.BlockSpec((CONV_HALO, CONV_CH), lambda b, r: (nxt(b, r), 1)),
        pl.BlockSpec((32, CONV_CH), lambda b, r: (0, 0)),
        cst, cst, cst,
    ]
    return pl.pallas_call(
        functools.partial(_conv_kernel, t=t, c=c),
        grid=(geom.b, geom.lat_blocks + 1),
        in_specs=in_specs,
        out_specs=pl.BlockSpec((c, CONV_CH), lambda b, r: (rb(b, r), 0)),
        out_shape=jax.ShapeDtypeStruct((geom.nt, CONV_CH), BF16),
        scratch_shapes=[pltpu.VMEM((c + 2 * CONV_HALO, CONV_CH), F32), pltpu.VMEM((c, CONV_CH), F32),
                        pltpu.VMEM((7, c + 2 * CONV_HALO, CONV_CH), F32)],
        compiler_params=_params(2, 32),
        name="conformer_conv",
    )(z, z, z, z, z, z, w, vec(conv_db), vec(ln_g), vec(ln_b))


def _mix_kernel(ret_ref, att_ref, cv_ref, gr_ref, ga_ref, gc_ref, x_ref, g1_ref, sh2_ref, sc2_ref,
                wr_ref, wa_ref, wc_ref, wo_ref, lng_ref, lnb_ref, x1_ref, h2_ref, *, alpha):
    def proj(v_ref, w_ref):
        return jnp.dot(v_ref[...].astype(BF16), w_ref[...], preferred_element_type=F32)

    gate = lambda g_ref: jax.nn.sigmoid(g_ref[...].astype(F32))
    merged = (gate(gr_ref) * proj(ret_ref, wr_ref)
              + gate(ga_ref) * proj(att_ref, wa_ref)
              + gate(gc_ref) * proj(cv_ref, wc_ref))
    y = jnp.dot(merged.astype(BF16), wo_ref[...], preferred_element_type=F32)
    x1 = _layer_norm(alpha * x_ref[...] + g1_ref[0] * y, lng_ref[...], lnb_ref[...])
    x1_ref[...] = x1
    h2_ref[...] = x1 * (1.0 + sc2_ref[0]) + sh2_ref[0]


def _mix_call(geom, alpha, ret, att, cv, z, x, mods, w_ret_o, w_att_o, w_conv_o, w_out, ln_g, ln_b):
    tm = geom.c
    d = D_MODEL
    tile = pl.BlockSpec((tm, d), lambda i: (i, 0))
    gate = lambda k: pl.BlockSpec((tm, d), lambda i: (i, COL_GT // d + k))
    wsp = pl.BlockSpec((d, d), lambda i: (0, 0))
    vsp = pl.BlockSpec((1, d), lambda i: (0, 0))
    return pl.pallas_call(
        functools.partial(_mix_kernel, alpha=alpha),
        grid=(geom.nt // tm,),
        in_specs=[tile, tile, tile, gate(0), gate(1), gate(2), tile,
                  _mod_spec(geom, tm, 2), _mod_spec(geom, tm, 3), _mod_spec(geom, tm, 4),
                  wsp, wsp, wsp, wsp, vsp, vsp],
        out_specs=[tile, tile],
        out_shape=[jax.ShapeDtypeStruct((geom.nt, d), F32)] * 2,
        compiler_params=_params(1, 48),
        name="merge_ln1",
    )(ret, att, cv, z, z, z, x, mods, mods, mods, w_ret_o, w_att_o, w_conv_o, w_out,
      ln_g.reshape(1, d), ln_b.reshape(1, d))


def _router_kernel(h_ref, wr_ref, bias_ref, e_ref, w_ref, pos_ref, cnt_ref, hist_ref, cnt):
    i = pl.program_id(0)
    tm = h_ref.shape[0]
    ne, per = N_EXPERTS, N_EXPERTS // N_GROUPS
    neg = -jnp.inf

    @pl.when(i == 0)
    def _init():
        cnt[...] = jnp.zeros(cnt.shape, F32)

    logits = jnp.dot(h_ref[...], wr_ref[...], preferred_element_type=F32, precision=HIGHEST)
    scores = jax.nn.sigmoid(logits.T[:ne])
    sel = scores + bias_ref[...]

    member = lax.broadcasted_iota(jnp.int32, (per, tm), 0)
    grp_rows = []
    for g in range(N_GROUPS):
        blk = sel[g * per:(g + 1) * per]
        m1 = jnp.max(blk, axis=0, keepdims=True)
        first = jnp.min(jnp.where(blk == m1, member, per), axis=0, keepdims=True)
        m2 = jnp.max(jnp.where(member == first, neg, blk), axis=0, keepdims=True)
        grp_rows.append(m1 + m2)
    gs = jnp.concatenate(grp_rows, axis=0)

    gidx = lax.broadcasted_iota(jnp.int32, (N_GROUPS, tm), 0)
    rank = jnp.zeros((N_GROUPS, tm), jnp.int32)
    for g in range(N_GROUPS):
        row = gs[g:g + 1]
        ahead = jnp.logical_or(row > gs, jnp.logical_and(row == gs, g < gidx))
        rank = rank + ahead.astype(jnp.int32)
    keep = (rank < TOPK_GROUPS).astype(F32)
    keep_e = jnp.concatenate([jnp.broadcast_to(keep[g:g + 1], (per, tm)) for g in range(N_GROUPS)], axis=0)
    cand = jnp.where(keep_e > 0.5, sel, neg)

    eidx = lax.broadcasted_iota(jnp.int32, (ne, tm), 0)
    picks, gates, hots = [], [], []
    chosen = jnp.zeros((ne, tm), F32)
    for _ in range(TOP_K):
        m = jnp.max(cand, axis=0, keepdims=True)
        idx = jnp.min(jnp.where(cand == m, eidx, ne), axis=0, keepdims=True)
        hot = eidx == idx
        picks.append(idx)
        gates.append(jnp.sum(jnp.where(hot, scores, 0.0), axis=0, keepdims=True))
        hots.append(hot)
        chosen = jnp.where(hot, 1.0, chosen)
        cand = jnp.where(hot, neg, cand)
    total = gates[0]
    for gk in gates[1:]:
        total = total + gk

    ti = lax.broadcasted_iota(jnp.int32, (tm, tm), 0)
    tj = lax.broadcasted_iota(jnp.int32, (tm, tm), 1)
    before = jnp.where(ti < tj, 1.0, 0.0).astype(BF16)
    prior = jnp.dot(chosen.astype(BF16), before, preferred_element_type=F32) + cnt[...][:, :1]
    pos = [jnp.sum(jnp.where(hot, prior, 0.0), axis=0, keepdims=True) for hot in hots]

    e_ref[...] = jnp.concatenate(picks, axis=0)
    w_ref[...] = jnp.concatenate([ROUTED_SCALE * gk / total for gk in gates], axis=0)
    pos_ref[...] = jnp.concatenate(pos, axis=0).astype(jnp.int32)
    hist_ref[0] = cnt[...]
    cnt[...] = cnt[...] + jnp.sum(chosen, axis=1, keepdims=True)
    cnt_ref[...] = cnt[...]


def _router_call(geom, h2, w_router, router_bias):
    tm = geom.c
    wr = jnp.zeros((D_MODEL, 128), F32).at[:, :N_EXPERTS].set(w_router)
    tok = pl.BlockSpec((TOP_K, tm), lambda i: (0, i))
    return pl.pallas_call(
        _router_kernel,
        grid=(geom.nt // tm,),
        in_specs=[pl.BlockSpec((tm, D_MODEL), lambda i: (i, 0)),
                  pl.BlockSpec((D_MODEL, 128), lambda i: (0, 0)),
                  pl.BlockSpec((N_EXPERTS, 1), lambda i: (0, 0))],
        out_specs=[tok, tok, tok, pl.BlockSpec((N_EXPERTS, 128), lambda i: (0, 0)),
                   pl.BlockSpec((1, N_EXPERTS, 128), lambda i: (i, 0, 0))],
        out_shape=[jax.ShapeDtypeStruct((TOP_K, geom.nt), jnp.int32),
                   jax.ShapeDtypeStruct((TOP_K, geom.nt), F32),
                   jax.ShapeDtypeStruct((TOP_K, geom.nt), jnp.int32),
                   jax.ShapeDtypeStruct((N_EXPERTS, 128), F32),
                   jax.ShapeDtypeStruct((geom.nt // tm, N_EXPERTS, 128), F32)],
        scratch_shapes=[pltpu.VMEM((N_EXPERTS, 128), F32)],
        compiler_params=_params(1, 32),
        name="moe_router",
    )(h2, wr, router_bias.reshape(N_EXPERTS, 1))


HALF = D_MODEL // 2


def _pack_bf16_pairs(v):
    lo = pltpu.bitcast(v[:, :HALF].astype(BF16).astype(F32), jnp.uint32)
    hi = pltpu.bitcast(v[:, HALF:].astype(BF16).astype(F32), jnp.uint32)
    return jnp.bitwise_or(jnp.right_shift(lo, jnp.uint32(16)), hi)


def _unpack_bf16_pairs(w):
    lo = pltpu.bitcast(jnp.left_shift(w, jnp.uint32(16)), F32)
    hi = pltpu.bitcast(jnp.bitwise_and(w, jnp.uint32(0xFFFF0000)), F32)
    return lo, hi


RUN_ALIGN = 8
SORTED_ROWS = 256 * TOP_K + N_EXPERTS * RUN_ALIGN
RUN_BITS = tuple(range(8, 2, -1))


def _for_each_run_piece(n_ref, src_ref, tile, visit):
    def per_expert(e, off):
        n = n_ref[tile * N_EXPERTS + e]
        src = src_ref[tile * N_EXPERTS + e]

        for lb in RUN_BITS:
            done = (n >> (lb + 1)) << (lb + 1)

            @pl.when((n & (1 << lb)) != 0)
            def _piece():
                visit(pl.multiple_of(off + done, RUN_ALIGN), pl.multiple_of(src + done, RUN_ALIGN), 1 << lb)

        return off + n

    lax.fori_loop(0, N_EXPERTS, per_expert, 0)


TOTAL_BITS = tuple(range(11, 2, -1))


def _wait_rows(total, wait_piece):
    for lb in TOTAL_BITS:
        @pl.when((total & (1 << lb)) != 0)
        def _amount():
            wait_piece(1 << lb)


def _dispatch_kernel(last_ref, n_ref, src_ref, tot_ref, sidx_ref, h_ref, xs_out, packed, zblk, sem, zsem):
    tm = h_ref.shape[0]
    i = pl.program_id(0)

    @pl.when(pl.program_id(0) == 0)
    def _zero_tail_blocks():
        zblk[...] = jnp.zeros(zblk.shape, zblk.dtype)

        def zero_copy(e):
            return pltpu.make_async_copy(zblk, xs_out.at[pl.ds(last_ref[e] * MOE_BLOCK, MOE_BLOCK)], zsem)

        def start(e, carry):
            zero_copy(e).start()
            return carry

        def wait(e, carry):
            zero_copy(e).wait()
            return carry

        lax.fori_loop(0, N_EXPERTS, start, 0)
        lax.fori_loop(0, N_EXPERTS, wait, 0)

    rows = lax.broadcasted_iota(jnp.int32, (SORTED_ROWS, tm), 0).astype(jnp.int16)
    sidx16 = sidx_ref[...].astype(jnp.int16)
    pick = jnp.zeros((SORTED_ROWS, tm), BF16)
    for k in range(TOP_K):
        pick = jnp.where(rows == sidx16[k:k + 1, :], jnp.ones((), BF16), pick)
    sorted_rows = jnp.dot(pick, h_ref[...].astype(BF16), preferred_element_type=F32)
    lo = pltpu.bitcast(sorted_rows[:, :HALF], jnp.uint32)
    hi = pltpu.bitcast(sorted_rows[:, HALF:], jnp.uint32)
    packed[...] = jnp.bitwise_or(jnp.right_shift(lo, jnp.uint32(16)), hi)

    def piece(sorted_row, slot_row, rows_):
        return pltpu.make_async_copy(packed.at[pl.ds(sorted_row, rows_)], xs_out.at[pl.ds(slot_row, rows_)], sem)

    _for_each_run_piece(n_ref, src_ref, i, lambda a, b, r: piece(a, b, r).start())
    _wait_rows(tot_ref[i], lambda r: piece(0, 0, r).wait())


def _dispatch_call(geom, last_block, run_rows, run_slot, tile_rows, sidx, h2, n_blocks):
    tm = geom.c
    assert tm * TOP_K + N_EXPERTS * RUN_ALIGN == SORTED_ROWS
    grid_spec = pltpu.PrefetchScalarGridSpec(
        num_scalar_prefetch=4,
        grid=(geom.nt // tm,),
        in_specs=[pl.BlockSpec((TOP_K, tm), lambda i, *_: (0, i)),
                  pl.BlockSpec((tm, D_MODEL), lambda i, *_: (i, 0))],
        out_specs=pl.BlockSpec(memory_space=pl.ANY),
        scratch_shapes=[pltpu.VMEM((SORTED_ROWS, HALF), jnp.uint32), pltpu.VMEM((MOE_BLOCK, HALF), jnp.uint32),
                        pltpu.SemaphoreType.DMA(()), pltpu.SemaphoreType.DMA(())],
    )
    return pl.pallas_call(
        _dispatch_kernel,
        grid_spec=grid_spec,
        out_shape=jax.ShapeDtypeStruct(((n_blocks + 1) * MOE_BLOCK, HALF), jnp.uint32),
        compiler_params=_params(1, 48),
        name="moe_dispatch",
    )(last_block, run_rows, run_slot, tile_rows, sidx, h2)


def _expert_kernel(be_ref, nu_ref, x_ref, wg_ref, wu_ref, wd_ref, o_ref, wgu_s, wd_s):
    i = pl.program_id(0)
    live = i < nu_ref[0]
    changed = jnp.logical_or(i == 0, be_ref[i] != be_ref[jnp.maximum(i - 1, 0)])

    @pl.when(jnp.logical_and(live, changed))
    def _load_expert():
        wgu_s[:, :D_EXPERT] = wg_ref[0, 0].astype(BF16)
        wgu_s[:, D_EXPERT:] = wu_ref[0, 0].astype(BF16)
        wd_s[...] = wd_ref[0, 0].astype(BF16)

    @pl.when(live)
    def _run():
        lo, hi = _unpack_bf16_pairs(x_ref[...])
        x = jnp.concatenate([lo, hi], axis=1).astype(BF16)
        hgu = jnp.dot(x, wgu_s[...], preferred_element_type=F32)
        hid = _silu(hgu[:, :D_EXPERT]) * hgu[:, D_EXPERT:]
        o_ref[...] = _pack_bf16_pairs(jnp.dot(hid.astype(BF16), wd_s[...], preferred_element_type=F32))


def _expert_call(layer, block_e, n_used, xs, w_gate, w_up, w_down):
    n_blocks = xs.shape[0] // MOE_BLOCK - 1
    live = lambda i, be, nu: jnp.minimum(i, nu[0] - 1)
    expert = lambda i, be, nu: (layer, be[live(i, be, nu)], 0, 0)
    grid_spec = pltpu.PrefetchScalarGridSpec(
        num_scalar_prefetch=2,
        grid=(n_blocks,),
        in_specs=[pl.BlockSpec((MOE_BLOCK, HALF), lambda i, be, nu: (live(i, be, nu), 0)),
                  pl.BlockSpec((1, 1, D_MODEL, D_EXPERT), expert),
                  pl.BlockSpec((1, 1, D_MODEL, D_EXPERT), expert),
                  pl.BlockSpec((1, 1, D_EXPERT, D_MODEL), expert)],
        out_specs=pl.BlockSpec((MOE_BLOCK, HALF), lambda i, be, nu: (live(i, be, nu), 0)),
        scratch_shapes=[pltpu.VMEM((D_MODEL, 2 * D_EXPERT), BF16), pltpu.VMEM((D_EXPERT, D_MODEL), BF16)],
    )
    return pl.pallas_call(
        _expert_kernel,
        grid_spec=grid_spec,
        out_shape=jax.ShapeDtypeStruct(xs.shape, jnp.uint32),
        compiler_params=_params(1, 32),
        name="moe_experts",
    )(block_e, n_used, xs, w_gate, w_up, w_down)


def _combine_kernel(n_ref, src_ref, tot_ref, sidx_ref, wt_ref, ys_hbm, h_ref, x_ref, g2_ref, wgu_ref, wd_ref,
                    lng_ref, lnb_ref, o_ref, buf_a, buf_b, sem, *, alpha):
    i = pl.program_id(0)
    n = pl.num_programs(0)
    tm = h_ref.shape[0]
    even = i % 2 == 0

    def piece(buf, slot, sorted_row, slot_row, rows_):
        return pltpu.make_async_copy(ys_hbm.at[pl.ds(slot_row, rows_)], buf.at[pl.ds(sorted_row, rows_)], sem.at[slot])

    def gather(tile, buf, slot):
        _for_each_run_piece(n_ref, src_ref, tile, lambda a, b, r: piece(buf, slot, a, b, r).start())

    def drain(tile, buf, slot):
        _wait_rows(tot_ref[tile], lambda r: piece(buf, slot, 0, 0, r).wait())

    def finish(buf):
        wt = wt_ref[...].astype(BF16)
        sidx16 = sidx_ref[...].astype(jnp.int16)
        cols = lax.broadcasted_iota(jnp.int32, (tm, SORTED_ROWS), 1).astype(jnp.int16)
        mix = jnp.zeros((tm, SORTED_ROWS), BF16)
        for k in range(TOP_K):
            mix = jnp.where(cols == sidx16[:, k:k + 1], wt[:, k:k + 1], mix)
        lo, hi = _unpack_bf16_pairs(buf[...])
        routed = jnp.concatenate([jnp.dot(mix, lo.astype(BF16), preferred_element_type=F32),
                                  jnp.dot(mix, hi.astype(BF16), preferred_element_type=F32)], axis=1)
        hgu = jnp.dot(h_ref[...].astype(BF16), wgu_ref[...], preferred_element_type=F32)
        hid = _silu(hgu[:, :D_SHARED]) * hgu[:, D_SHARED:]
        shared = jnp.dot(hid.astype(BF16), wd_ref[...], preferred_element_type=F32)
        o_ref[...] = _layer_norm(alpha * x_ref[...] + g2_ref[0] * (routed + shared), lng_ref[...], lnb_ref[...])

    @pl.when(i == 0)
    def _first():
        buf_a[...] = jnp.zeros(buf_a.shape, buf_a.dtype)
        buf_b[...] = jnp.zeros(buf_b.shape, buf_b.dtype)
        gather(i, buf_a, 0)

    @pl.when(jnp.logical_and(even, i + 1 < n))
    def _ahead_b():
        gather(i + 1, buf_b, 1)

    @pl.when(jnp.logical_and(jnp.logical_not(even), i + 1 < n))
    def _ahead_a():
        gather(i + 1, buf_a, 0)

    @pl.when(even)
    def _finish_a():
        drain(i, buf_a, 0)
        finish(buf_a)

    @pl.when(jnp.logical_not(even))
    def _finish_b():
        drain(i, buf_b, 1)
        finish(buf_b)


def _combine_call(geom, alpha, run_rows, run_slot, tile_rows, sidx_tok, w_tok, ys, h2, x1, mods, w_sh_gu, w_sh_down, ln_g, ln_b):
    tm = geom.c
    d = D_MODEL
    n = geom.nt // tm
    tile = pl.BlockSpec((tm, d), lambda i, *_: (i, 0))
    vsp = pl.BlockSpec((1, d), lambda i, *_: (0, 0))
    per_tok = pl.BlockSpec((tm, TOP_K), lambda i, *_: (i, 0))
    grid_spec = pltpu.PrefetchScalarGridSpec(
        num_scalar_prefetch=3,
        grid=(n,),
        in_specs=[per_tok, per_tok,
                  pl.BlockSpec(memory_space=pl.ANY),
                  tile, tile, _mod_spec(geom, tm, 5),
                  pl.BlockSpec((d, 2 * D_SHARED), lambda i, *_: (0, 0)),
                  pl.BlockSpec((D_SHARED, d), lambda i, *_: (0, 0)),
                  vsp, vsp],
        out_specs=tile,
        scratch_shapes=[pltpu.VMEM((SORTED_ROWS, HALF), jnp.uint32), pltpu.VMEM((SORTED_ROWS, HALF), jnp.uint32),
                        pltpu.SemaphoreType.DMA((2,))],
    )
    return pl.pallas_call(
        functools.partial(_combine_kernel, alpha=alpha),
        grid_spec=grid_spec,
        out_shape=jax.ShapeDtypeStruct((geom.nt, d), F32),
        compiler_params=_params(1, 56),
        name="moe_combine_ln2",
    )(run_rows, run_slot, tile_rows, sidx_tok, w_tok, ys, h2, x1, mods, w_sh_gu, w_sh_down,
      ln_g.reshape(1, d), ln_b.reshape(1, d))


def _rope_tables(t):
    rows = t // GRID_W
    row = jnp.repeat(jnp.arange(rows, dtype=F32), GRID_W)
    col = jnp.tile(jnp.arange(GRID_W, dtype=F32), rows)
    n_freq = ATT_DH // 4
    inv_freq = ROPE_THETA ** (-jnp.arange(n_freq, dtype=F32) / n_freq)
    ang = jnp.concatenate([row[:, None] * inv_freq, col[:, None] * inv_freq], axis=-1)
    cos, sin = jnp.cos(ang), jnp.sin(ang)
    cos64 = jnp.concatenate([cos, cos], axis=-1)
    sin64 = jnp.concatenate([-sin, sin], axis=-1)
    return cos64, sin64


def kernel(x, c, ctx, c_ctx, w_ada, b_ada, w_in, ret_decay_logit, att_q_norm, att_k_norm, conv_dw, conv_db, conv_ln_g, conv_ln_b, w_ret_o, w_att_o, w_conv_o, w_out, ln1_g, ln1_b, w_router, router_bias, w_exp_gate, w_exp_up, w_exp_down, w_sh_gate, w_sh_up, w_sh_down, ln2_g, ln2_b):
    b, t, d = x.shape
    n_ctx = ctx.shape[1]
    depth = w_ada.shape[0]
    assert d == D_MODEL and w_in.shape[-1] == D_IN
    geom = _Geom(b, t, n_ctx)
    alpha = float((2 * depth) ** 0.25)

    cos64, sin64 = _rope_tables(t)
    cos128 = jnp.concatenate([cos64, cos64], axis=-1)
    sin128 = jnp.concatenate([sin64, sin64], axis=-1)

    n_rows = -(-(b + 1) // 8) * 8
    cvecs = jnp.zeros((n_rows, d), F32).at[:b].set(c).at[b].set(c_ctx)
    mods_all = _mods_call(cvecs, w_ada, b_ada).reshape(depth, n_rows * 6, 1, d)

    n_tiles = geom.nt // geom.c
    n_blocks = -(-(geom.nt * TOP_K + n_tiles * N_EXPERTS * (RUN_ALIGN - 1)) // MOE_BLOCK) + N_EXPERTS

    xt = jnp.concatenate([x.reshape(geom.nl, d), ctx.reshape(geom.nc, d)], axis=0)
    for l in range(depth):
        mods = mods_all[l]
        w_in_l = _permute_columns(w_in[l]).astype(BF16)
        z = _inproj_call(geom, xt, mods, w_in_l)

        log_gamma = jax.nn.log_sigmoid(ret_decay_logit[l].astype(F32))
        ret = _retention_call(geom, z, log_gamma, cos128, sin128)
        att = _attention_call(geom, z, att_q_norm[l], att_k_norm[l], cos128, sin128)
        cv = _conv_call(geom, z, conv_dw[l], conv_db[l], conv_ln_g[l], conv_ln_b[l])
        x1, h2 = _mix_call(geom, alpha, ret, att, cv, z, xt, mods,
                           w_ret_o[l].astype(BF16), w_att_o[l].astype(BF16), w_conv_o[l].astype(BF16),
                           w_out[l].astype(BF16), ln1_g[l], ln1_b[l])

        top_e, gate_w, pos, counts, cnt_hist = _router_call(geom, h2, w_router[l], router_bias[l])
        before = cnt_hist[:, :, 0].astype(jnp.int32)
        total = counts[:, 0].astype(jnp.int32)
        tile_n = jnp.concatenate([before[1:], total[None, :]], axis=0) - before
        run_rows = (tile_n + RUN_ALIGN - 1) // RUN_ALIGN * RUN_ALIGN
        run_before = jnp.cumsum(run_rows, axis=0) - run_rows
        blocks_e = (jnp.sum(run_rows, axis=0) + MOE_BLOCK - 1) // MOE_BLOCK
        blocks_end = jnp.cumsum(blocks_e)
        start_row = (blocks_end - blocks_e) * MOE_BLOCK
        run_slot = start_row[None, :] + run_before
        run_sorted = jnp.cumsum(run_rows, axis=1) - run_rows
        onehot = top_e[:, :, None] == jnp.arange(N_EXPERTS, dtype=jnp.int32)[None, None, :]
        per_token = lambda table: jnp.sum(jnp.where(onehot, jnp.repeat(table, geom.c, axis=0)[None], 0), axis=-1)
        sidx = per_token(run_sorted) + pos - per_token(before)
        block_ids = jnp.arange(n_blocks, dtype=jnp.int32)
        block_e = jnp.minimum(jnp.sum((blocks_end[None, :] <= block_ids[:, None]).astype(jnp.int32), axis=1),
                              N_EXPERTS - 1)
        n_used = blocks_end[-1:].astype(jnp.int32)
        last_block = jnp.where(blocks_e > 0, blocks_end - 1, n_blocks).astype(jnp.int32)
        run_rows_flat, run_slot_flat = run_rows.reshape(-1), run_slot.reshape(-1).astype(jnp.int32)
        tile_rows = jnp.sum(run_rows, axis=1)

        xs = _dispatch_call(geom, last_block, run_rows_flat, run_slot_flat, tile_rows, sidx, h2, n_blocks)
        ys = _expert_call(l, block_e, n_used, xs, w_exp_gate, w_exp_up, w_exp_down)
        w_sh_gu = jnp.concatenate([w_sh_gate[l], w_sh_up[l]], axis=-1).astype(BF16)
        xt = _combine_call(geom, alpha, run_rows_flat, run_slot_flat, tile_rows, sidx.T, gate_w.T, ys, h2, x1, mods, w_sh_gu,
                           w_sh_down[l].astype(BF16), ln2_g[l], ln2_b[l])
    return xt[:geom.nl].reshape(b, t, d)
```

```python
import functools

import jax
import jax.numpy as jnp
from jax import lax
from jax.experimental import pallas as pl
from jax.experimental.pallas import tpu as pltpu

F32 = jnp.float32
BF16 = jnp.bfloat16
HIGHEST = lax.Precision.HIGHEST

D_MODEL = 1024
GRID_W = 64
EPS = 1e-6

RET_HEADS = 8
RET_DK = 64
RET_DV = 128
RET_CHUNK = 256
RET_W = RET_HEADS * RET_DV

ATT_HEADS = 16
ATT_KV_HEADS = 4
ATT_DH = 64
ATT_GROUP = ATT_HEADS // ATT_KV_HEADS
ATT_W = ATT_HEADS * ATT_DH
ROPE_THETA = 10000.0
ATT_KEY_BLOCK = 2048

CONV_CH = 1024
CONV_K = 31
CONV_HALO = 16

N_EXPERTS = 64
TOP_K = 8
N_GROUPS = 8
TOPK_GROUPS = 4
D_EXPERT = 256
D_SHARED = 256
ROUTED_SCALE = 2.5
MOE_BLOCK = 512

_ORIG = dict(rq=0, rk=512, rv=1024, rg=2048, aq=3072, ak=4096, av=4352, cu=4608, gt=6656)
D_IN = 9728
COL_CU = 0
COL_GT = 2048
COL_RG = 5120
COL_RV = 6144
COL_RQ = 7168
COL_RK = 7680
COL_ATT = 8192
ATT_SECTION = ATT_GROUP * ATT_DH + 2 * ATT_DH


def _column_ranges():
    rng = [(_ORIG["cu"], _ORIG["cu"] + 2 * CONV_CH),
           (_ORIG["gt"], _ORIG["gt"] + 3 * D_MODEL),
           (_ORIG["rg"], _ORIG["rg"] + RET_W),
           (_ORIG["rv"], _ORIG["rv"] + RET_W),
           (_ORIG["rq"], _ORIG["rq"] + RET_HEADS * RET_DK),
           (_ORIG["rk"], _ORIG["rk"] + RET_HEADS * RET_DK)]
    for g in range(ATT_KV_HEADS):
        rng.append((_ORIG["aq"] + g * ATT_GROUP * ATT_DH, _ORIG["aq"] + (g + 1) * ATT_GROUP * ATT_DH))
        rng.append((_ORIG["ak"] + g * ATT_DH, _ORIG["ak"] + (g + 1) * ATT_DH))
        rng.append((_ORIG["av"] + g * ATT_DH, _ORIG["av"] + (g + 1) * ATT_DH))
    cols = [c for a, b in rng for c in range(a, b)]
    assert sorted(cols) == list(range(D_IN))
    return rng


def _permute_columns(w):
    return jnp.concatenate([w[:, a:b] for a, b in _column_ranges()], axis=1)


def _params(n_axes, vmem_mib):
    return pltpu.CompilerParams(dimension_semantics=("arbitrary",) * n_axes,
                                vmem_limit_bytes=vmem_mib * 1024 * 1024)


def _silu(v):
    return v * jax.nn.sigmoid(v)


def _layer_norm(v, g, b):
    mu = jnp.mean(v, axis=-1, keepdims=True)
    d = v - mu
    var = jnp.mean(d * d, axis=-1, keepdims=True)
    return d * lax.rsqrt(var + EPS) * g + b


def _mods_kernel(c_ref, w_ref, b_ref, o_ref):
    s = _silu(c_ref[...])
    o_ref[0] = jnp.dot(s, w_ref[0], preferred_element_type=F32, precision=HIGHEST) + b_ref[0]


def _mods_call(cvecs, w_ada, b_ada):
    n_layers = w_ada.shape[0]
    rows, d = cvecs.shape
    return pl.pallas_call(
        _mods_kernel,
        grid=(n_layers, 6),
        in_specs=[pl.BlockSpec((rows, d), lambda l, j: (0, 0)),
                  pl.BlockSpec((1, d, d), lambda l, j: (l, 0, j)),
                  pl.BlockSpec((1, 1, d), lambda l, j: (l, 0, j))],
        out_specs=pl.BlockSpec((1, rows, d), lambda l, j: (l, 0, j)),
        out_shape=jax.ShapeDtypeStruct((n_layers, rows, 6 * d), F32),
        compiler_params=_params(2, 32),
        name="adaln_mods",
    )(cvecs, w_ada, b_ada.reshape(n_layers, 1, 6 * d))


class _Geom:
    def __init__(self, b, t, c):
        assert t % c == 0 and c % RET_CHUNK == 0 and c % CONV_HALO == 0 and t % ATT_KEY_BLOCK == 0
        self.b, self.t, self.c = b, t, c
        self.nl, self.nc = b * t, b * c
        self.nt = self.nl + self.nc
        self.lat_blocks = t // c
        self.nlb = self.nl // c
        self.p = t + c

    def row_block(self, bi, r):
        return jnp.where(r < self.lat_blocks, bi * self.lat_blocks + r, self.nlb + bi)

    def mod_row(self, i, tm):
        return jnp.where(i * tm < self.nl, (i * tm) // self.t, self.b)


def _mod_spec(geom, tm, which, grid_pos=0):
    d = D_MODEL
    if grid_pos == 0:
        return pl.BlockSpec((1, 1, d), lambda i, *_: (geom.mod_row(i, tm) * 6 + which, 0, 0))
    return pl.BlockSpec((1, 1, d), lambda j, i: (geom.mod_row(i, tm) * 6 + which, 0, 0))


def _inproj_kernel(x_ref, sh_ref, sc_ref, w_ref, o_ref):
    h = x_ref[...] * (1.0 + sc_ref[0]) + sh_ref[0]
    o_ref[...] = jnp.dot(h.astype(BF16), w_ref[...], preferred_element_type=F32).astype(o_ref.dtype)


def _inproj_call(geom, x, mods, w_in_bf16):
    tm = 512 if geom.nc % 512 == 0 and geom.t % 512 == 0 else geom.c
    tn = D_IN // 2
    return pl.pallas_call(
        _inproj_kernel,
        grid=(D_IN // tn, geom.nt // tm),
        in_specs=[pl.BlockSpec((tm, D_MODEL), lambda j, i: (i, 0)),
                  _mod_spec(geom, tm, 0, grid_pos=1),
                  _mod_spec(geom, tm, 1, grid_pos=1),
                  pl.BlockSpec((D_MODEL, tn), lambda j, i: (0, j))],
        out_specs=pl.BlockSpec((tm, tn), lambda j, i: (i, j)),
        out_shape=jax.ShapeDtypeStruct((geom.nt, D_IN), BF16),
        compiler_params=_params(2, 48),
        name="in_proj",
    )(x, mods, mods, w_in_bf16)


def _rot_half_128(v):
    lane = lax.broadcasted_iota(jnp.int32, v.shape, 1)
    return jnp.where((lane % 64) < 32, pltpu.roll(v, 96, 1), pltpu.roll(v, 32, 1))


def _ret_kernel(lg_ref, ql_ref, qc_ref, kl_ref, kc_ref, vl_ref, vc_ref, gl_ref, gc_ref, cos_ref, sin_ref,
                ol_ref, oc_ref, qs, kts, yf, yb, st, dm, qwb, kwb, gcs, *, t, c):
    ch = RET_CHUNK
    hp = pl.program_id(1)
    lat_blocks = t // c
    n_lat, n_ctx = t // ch, c // ch

    def scan():
        ri = lax.broadcasted_iota(jnp.int32, (ch, ch), 0).astype(F32)
        ci = lax.broadcasted_iota(jnp.int32, (ch, ch), 1).astype(F32)
        rv = lax.broadcasted_iota(jnp.int32, (ch, RET_DV), 0).astype(F32)
        for d in range(2):
            for h in range(2):
                u = 2 * d + h
                lg = lg_ref[d, 2 * hp + h]
                rel = (ri - ci) if d == 0 else (ci - ri)
                dm[u] = jnp.where(rel >= 0.0, jnp.exp(lg * jnp.maximum(rel, 0.0)), 0.0)
                qwb[u] = jnp.exp(lg * ((rv + 1.0) if d == 0 else (float(ch) - rv)))
                kwb[u] = jnp.exp(lg * ((float(ch) - 1.0 - rv) if d == 0 else rv))
                gcs[u] = jnp.exp(jnp.full((RET_DK, RET_DV), lg * float(ch), F32))
                st[u] = jnp.zeros((RET_DK, RET_DV), F32)

        def stage(q, k, seq_rows):
            qs[0, seq_rows, :] = q[:, :RET_DK].astype(BF16)
            qs[1, seq_rows, :] = q[:, RET_DK:].astype(BF16)
            kt = k.T
            kts[0, :, seq_rows] = kt[:RET_DK].astype(BF16)
            kts[1, :, seq_rows] = kt[RET_DK:].astype(BF16)

        kscale = RET_DK ** -0.5
        for cc in range(n_ctx):
            rows = pl.ds(cc * ch, ch)
            stage(qc_ref[rows, :].astype(F32), kc_ref[rows, :].astype(F32) * kscale, rows)

        def stage_lat(cc, carry):
            rows = pl.ds(pl.multiple_of(cc * ch, ch), ch)
            cs, sn = cos_ref[rows, :], sin_ref[rows, :]
            q = ql_ref[rows, :].astype(F32)
            k = kl_ref[rows, :].astype(F32)
            q = q * cs + _rot_half_128(q) * sn
            k = (k * cs + _rot_half_128(k) * sn) * kscale
            stage(q, k, pl.ds(pl.multiple_of(c + cc * ch, ch), ch))
            return carry

        lax.fori_loop(0, n_lat, stage_lat, 0)

        def run_segment(v_ref, seq_off, n):
            def body(i, carry):
                for d, cc in ((0, i), (1, n - 1 - i)):
                    vrows = pl.ds(pl.multiple_of(cc * ch, ch), ch)
                    srows = pl.ds(pl.multiple_of(seq_off + cc * ch, ch), ch)
                    for h in range(2):
                        u = 2 * d + h
                        q = qs[h, srows, :]
                        kt = kts[h, :, srows]
                        v = v_ref[vrows, h * RET_DV:(h + 1) * RET_DV].astype(F32)
                        s = jnp.dot(q, kt, preferred_element_type=F32)
                        y = jnp.dot((s * dm[u]).astype(BF16), v.astype(BF16), preferred_element_type=F32)
                        state = st[u]
                        y = y + jnp.dot(q, state.astype(BF16), preferred_element_type=F32) * qwb[u]
                        dst = yf if d == 0 else yb
                        dst[srows, h * RET_DV:(h + 1) * RET_DV] = y
                        kv = jnp.dot(kt, (v * kwb[u]).astype(BF16), preferred_element_type=F32)
                        st[u] = gcs[u] * state + kv
                return carry

            lax.fori_loop(0, n, body, 0, unroll=2)

        run_segment(vc_ref, 0, n_ctx)
        run_segment(vl_ref, c, n_lat)

    def finish(srows, g_ref, o_ref, rows):
        y = yf[srows, :] + yb[srows, :]
        for h in range(2):
            cols = slice(h * RET_DV, (h + 1) * RET_DV)
            yh = y[:, cols]
            mu = jnp.mean(yh, axis=-1, keepdims=True)
            dlt = yh - mu
            var = jnp.mean(dlt * dlt, axis=-1, keepdims=True)
            out = _silu(g_ref[rows, cols].astype(F32)) * (dlt * lax.rsqrt(var + EPS))
            o_ref[rows, cols] = out.astype(o_ref.dtype)

    scan()
    finish(pl.ds(0, c), gc_ref, oc_ref, pl.ds(0, c))

    def fin_lat(i, carry):
        rows = pl.ds(pl.multiple_of(i * c, c), c)
        finish(pl.ds(pl.multiple_of(c + i * c, c), c), gl_ref, ol_ref, rows)
        return carry

    lax.fori_loop(0, lat_blocks, fin_lat, 0)


def _retention_call(geom, z, log_gamma, cos128, sin128):
    t, c, p = geom.t, geom.c, geom.p
    hpairs = RET_HEADS // 2
    qb, kb = COL_RQ // 128, COL_RK // 128
    vb, gb = COL_RV // 256, COL_RG // 256
    in_specs = [
        pl.BlockSpec(memory_space=pltpu.SMEM),
        pl.BlockSpec((t, 128), lambda b, h: (b, qb + h)),
        pl.BlockSpec((c, 128), lambda b, h: (geom.nlb + b, qb + h)),
        pl.BlockSpec((t, 128), lambda b, h: (b, kb + h)),
        pl.BlockSpec((c, 128), lambda b, h: (geom.nlb + b, kb + h)),
        pl.BlockSpec((t, 256), lambda b, h: (b, vb + h)),
        pl.BlockSpec((c, 256), lambda b, h: (geom.nlb + b, vb + h)),
        pl.BlockSpec((t, 256), lambda b, h: (b, gb + h)),
        pl.BlockSpec((c, 256), lambda b, h: (geom.nlb + b, gb + h)),
        pl.BlockSpec((t, 128), lambda b, h: (0, 0)),
        pl.BlockSpec((t, 128), lambda b, h: (0, 0)),
    ]
    scratch = [
        pltpu.VMEM((2, p, RET_DK), BF16),
        pltpu.VMEM((2, RET_DK, p), BF16),
        pltpu.VMEM((p, 2 * RET_DV), F32),
        pltpu.VMEM((p, 2 * RET_DV), F32),
        pltpu.VMEM((4, RET_DK, RET_DV), F32),
        pltpu.VMEM((4, RET_CHUNK, RET_CHUNK), F32),
        pltpu.VMEM((4, RET_CHUNK, RET_DV), F32),
        pltpu.VMEM((4, RET_CHUNK, RET_DV), F32),
        pltpu.VMEM((4, RET_DK, RET_DV), F32),
    ]
    return pl.pallas_call(
        functools.partial(_ret_kernel, t=t, c=c),
        grid=(geom.b, hpairs),
        in_specs=in_specs,
        out_specs=[pl.BlockSpec((t, 256), lambda b, h: (b, h)), pl.BlockSpec((c, 256), lambda b, h: (b, h))],
        out_shape=[jax.ShapeDtypeStruct((geom.nl, RET_W), BF16),
                   jax.ShapeDtypeStruct((geom.nc, RET_W), BF16)],
        scratch_shapes=scratch,
        compiler_params=_params(2, 56),
        name="retention",
    )(log_gamma, z, z, z, z, z, z, z, z, cos128, sin128)


def _rms_heads_128(v, g):
    li = lax.broadcasted_iota(jnp.int32, (128, 128), 0) // ATT_DH
    lj = lax.broadcasted_iota(jnp.int32, (128, 128), 1) // ATT_DH
    avg = jnp.where(li == lj, 1.0 / ATT_DH, 0.0).astype(BF16)
    sq = v * v
    hi = sq.astype(BF16)
    lo = (sq - hi.astype(F32)).astype(BF16)
    ms = jnp.dot(hi, avg, preferred_element_type=F32) + jnp.dot(lo, avg, preferred_element_type=F32)
    return v * lax.rsqrt(ms + EPS) * g


def _att_kernel(qa_ref, qb_ref, kvl_ref, kvc_ref, qn_ref, kn_ref, cos_ref, sin_ref, o_ref,
                kts, vs, m_s, acc_s, *, t, c):
    r = pl.program_id(2)
    lat_blocks = t // c
    dh = ATT_DH
    tk = ATT_KEY_BLOCK
    lane = lax.broadcasted_iota(jnp.int32, (c, 2 * dh), 1)

    def stage_tile(kv, dst, cs, sn):
        k = _rms_heads_128(kv, kn_ref[...])
        if cs is not None:
            k = k * cs + _rot_half_128(k) * sn
        kts[:, dst] = k.T[:dh].astype(BF16)
        vs[dst, :] = jnp.where(lane < dh, pltpu.roll(kv, dh, 1), 1.0).astype(BF16)

    @pl.when(r == 0)
    def _stage_kv():
        stage_tile(kvc_ref[...].astype(F32), pl.ds(0, c), None, None)

        def stage(i, carry):
            rows = pl.ds(pl.multiple_of(i * c, c), c)
            stage_tile(kvl_ref[rows, :].astype(F32), pl.ds(pl.multiple_of(c + i * c, c), c),
                       cos_ref[rows, :], sin_ref[rows, :])
            return carry

        lax.fori_loop(0, lat_blocks, stage, 0)

    is_ctx = r == lat_blocks
    rows = pl.ds(pl.multiple_of(jnp.minimum(r, lat_blocks - 1) * c, c), c)
    cs, sn = cos_ref[rows, :], sin_ref[rows, :]
    q_heads = []
    for src in (qa_ref, qb_ref):
        xn = _rms_heads_128(src[...].astype(F32), qn_ref[...])
        xr = jnp.where(is_ctx, xn, xn * cs + _rot_half_128(xn) * sn) * (dh ** -0.5)
        q_heads.append(xr[:, :dh].astype(BF16))
        q_heads.append(pltpu.roll(xr, dh, 1)[:, :dh].astype(BF16))
    q = jnp.concatenate(q_heads, axis=0)

    m_s[...] = jnp.full(m_s.shape, -jnp.inf, F32)
    acc_s[...] = jnp.zeros(acc_s.shape, F32)

    def flash_step(kt, v):
        n = kt.shape[1]
        s = jnp.dot(q, kt, preferred_element_type=F32)
        m_prev = m_s[...]
        m_next = jnp.maximum(m_prev, jnp.max(s, axis=1, keepdims=True))
        prob = jnp.exp(s - jnp.concatenate([m_next] * (n // 128), axis=1))
        acc_s[...] = acc_s[...] * jnp.exp(m_prev - m_next) + jnp.dot(prob.astype(BF16), v, preferred_element_type=F32)
        m_s[...] = m_next

    flash_step(kts[:, 0:c], vs[0:c, :])

    @pl.when(jnp.logical_not(is_ctx))
    def _latent_keys():
        def lat_step(j, carry):
            krows = pl.ds(pl.multiple_of(c + j * tk, 128), tk)
            flash_step(kts[:, krows], vs[krows, :])
            return carry

        lax.fori_loop(0, t // tk, lat_step, 0, unroll=2)

    outs = []
    for h in range(ATT_GROUP):
        acc = acc_s[h * c:(h + 1) * c, :]
        outs.append(acc * pltpu.roll(1.0 / acc, dh, 1))
    for pair in range(ATT_GROUP // 2):
        both = jnp.where(lane < dh, outs[2 * pair], pltpu.roll(outs[2 * pair + 1], dh, 1))
        o_ref[:, pair * 2 * dh:(pair + 1) * 2 * dh] = both.astype(o_ref.dtype)


def _attention_call(geom, z, q_norm, k_norm, cos128, sin128):
    t, c, p = geom.t, geom.c, geom.p
    ab = COL_ATT // 128
    sec = ATT_SECTION // 128
    rb = geom.row_block
    in_specs = [
        pl.BlockSpec((c, 128), lambda b, g, r: (rb(b, r), ab + sec * g)),
        pl.BlockSpec((c, 128), lambda b, g, r: (rb(b, r), ab + sec * g + 1)),
        pl.BlockSpec((t, 128), lambda b, g, r: (b, ab + sec * g + 2)),
        pl.BlockSpec((c, 128), lambda b, g, r: (geom.nlb + b, ab + sec * g + 2)),
        pl.BlockSpec((1, 128), lambda b, g, r: (0, 0)),
        pl.BlockSpec((1, 128), lambda b, g, r: (0, 0)),
        pl.BlockSpec((t, 128), lambda b, g, r: (0, 0)),
        pl.BlockSpec((t, 128), lambda b, g, r: (0, 0)),
    ]
    two_heads = lambda v: jnp.tile(v.reshape(1, ATT_DH), (1, 2))
    scratch = [
        pltpu.VMEM((ATT_DH, p), BF16),
        pltpu.VMEM((p, 2 * ATT_DH), BF16),
        pltpu.VMEM((ATT_GROUP * c, 128), F32),
        pltpu.VMEM((ATT_GROUP * c, 2 * ATT_DH), F32),
    ]
    return pl.pallas_call(
        functools.partial(_att_kernel, t=t, c=c),
        grid=(geom.b, ATT_KV_HEADS, geom.lat_blocks + 1),
        in_specs=in_specs,
        out_specs=pl.BlockSpec((c, ATT_GROUP * ATT_DH), lambda b, g, r: (rb(b, r), g)),
        out_shape=jax.ShapeDtypeStruct((geom.nt, ATT_W), BF16),
        scratch_shapes=scratch,
        compiler_params=_params(3, 48),
        name="attention",
    )(z, z, z, z, two_heads(q_norm), two_heads(k_norm), cos128, sin128)


def _conv_kernel(a_ref, g_ref, ap_ref, gp_ref, an_ref, gn_ref, w_ref, b_ref, lng_ref, lnb_ref, o_ref,
                 ext, ys, shifted, *, t, c):
    r = pl.program_id(1)
    lat_blocks = t // c
    halo = CONV_HALO
    has_prev = jnp.logical_and(r != 0, r != lat_blocks)
    has_next = jnp.logical_and(r != lat_blocks - 1, r != lat_blocks)
    glu = lambda a, g: a[...].astype(F32) * jax.nn.sigmoid(g[...].astype(F32))
    ext[halo:halo + c, :] = glu(a_ref, g_ref)
    ext[0:halo, :] = jnp.where(has_prev, glu(ap_ref, gp_ref), 0.0)
    ext[halo + c:, :] = jnp.where(has_next, glu(an_ref, gn_ref), 0.0)

    rt = 64
    first = halo - CONV_K // 2
    span = c + 2 * halo - 8
    for s in range(1, 8):
        shifted[s - 1, 0:span, :] = ext[s:s + span, :]

    def lane_block(cb, carry):
        lanes = pl.ds(pl.multiple_of(cb * 128, 128), 128)
        for ti in range(c // rt):
            acc = jnp.zeros((rt, 128), F32)
            for j in range(CONV_K):
                row, s = divmod(ti * rt + first + j, 8)
                src = ext if s == 0 else shifted.at[s - 1]
                acc = acc + w_ref[pl.ds(j, 1), lanes] * src[pl.ds(row * 8, rt), lanes]
            ys[pl.ds(ti * rt, rt), lanes] = acc
        return carry

    lax.fori_loop(0, CONV_CH // 128, lane_block, 0)
    y = ys[...] + b_ref[...]
    o_ref[...] = _silu(_layer_norm(y, lng_ref[...], lnb_ref[...])).astype(o_ref.dtype)


def _conv_call(geom, z, conv_dw, conv_db, ln_g, ln_b):
    t, c = geom.t, geom.c
    rb = geom.row_block
    hb = c // CONV_HALO
    last = geom.nt // CONV_HALO - 1
    prev = lambda b, r: jnp.maximum(rb(b, r) * hb - 1, 0)
    nxt = lambda b, r: jnp.minimum((rb(b, r) + 1) * hb, last)
    w = jnp.zeros((32, CONV_CH), F32).at[:CONV_K].set(conv_dw)
    vec = lambda v: v.reshape(1, CONV_CH)
    cst = pl.BlockSpec((1, CONV_CH), lambda b, r: (0, 0))
    in_specs = [
        pl.BlockSpec((c, CONV_CH), lambda b, r: (rb(b, r), 0)),
        pl.BlockSpec((c, CONV_CH), lambda b, r: (rb(b, r), 1)),
        pl.BlockSpec((CONV_HALO, CONV_CH), lambda b, r: (prev(b, r), 0)),
        pl.BlockSpec((CONV_HALO, CONV_CH), lambda b, r: (prev(b, r), 1)),
        pl.BlockSpec((CONV_HALO, CONV_CH), lambda b, r: (nxt(b, r), 0)),
        pl.BlockSpec((CONV_HALO, CONV_CH), lambda b, r: (nxt(b, r), 1)),
        pl.BlockSpec((32, CONV_CH), lambda b, r: (0, 0)),
        cst, cst, cst,
    ]
    return pl.pallas_call(
        functools.partial(_conv_kernel, t=t, c=c),
        grid=(geom.b, geom.lat_blocks + 1),
        in_specs=in_specs,
        out_specs=pl.BlockSpec((c, CONV_CH), lambda b, r: (rb(b, r), 0)),
        out_shape=jax.ShapeDtypeStruct((geom.nt, CONV_CH), BF16),
        scratch_shapes=[pltpu.VMEM((c + 2 * CONV_HALO, CONV_CH), F32), pltpu.VMEM((c, CONV_CH), F32),
                        pltpu.VMEM((7, c + 2 * CONV_HALO, CONV_CH), F32)],
        compiler_params=_params(2, 32),
        name="conformer_conv",
    )(z, z, z, z, z, z, w, vec(conv_db), vec(ln_g), vec(ln_b))


def _mix_kernel(retl_ref, retc_ref, att_ref, cv_ref, gr_ref, ga_ref, gc_ref, x_ref, g1_ref, sh2_ref, sc2_ref,
                wr_ref, wa_ref, wc_ref, wo_ref, lng_ref, lnb_ref, x1_ref, h2_ref, *, alpha, n_lat_tiles):
    def proj(v, w_ref):
        return jnp.dot(v.astype(BF16), w_ref[...], preferred_element_type=F32)

    gate = lambda g_ref: jax.nn.sigmoid(g_ref[...].astype(F32))
    ret = jnp.where(pl.program_id(0) < n_lat_tiles, retl_ref[...], retc_ref[...])
    merged = (gate(gr_ref) * proj(ret, wr_ref)
              + gate(ga_ref) * proj(att_ref[...], wa_ref)
              + gate(gc_ref) * proj(cv_ref[...], wc_ref))
    y = jnp.dot(merged.astype(BF16), wo_ref[...], preferred_element_type=F32)
    x1 = _layer_norm(alpha * x_ref[...] + g1_ref[0] * y, lng_ref[...], lnb_ref[...])
    x1_ref[...] = x1
    h2_ref[...] = x1 * (1.0 + sc2_ref[0]) + sh2_ref[0]


def _mix_call(geom, alpha, ret_l, ret_c, att, cv, z, x, mods, w_ret_o, w_att_o, w_conv_o, w_out, ln_g, ln_b):
    tm = geom.c
    d = D_MODEL
    n_lat = geom.nl // tm
    tile = pl.BlockSpec((tm, d), lambda i: (i, 0))
    gate = lambda k: pl.BlockSpec((tm, d), lambda i: (i, COL_GT // d + k))
    wsp = pl.BlockSpec((d, d), lambda i: (0, 0))
    vsp = pl.BlockSpec((1, d), lambda i: (0, 0))
    return pl.pallas_call(
        functools.partial(_mix_kernel, alpha=alpha, n_lat_tiles=n_lat),
        grid=(geom.nt // tm,),
        in_specs=[pl.BlockSpec((tm, d), lambda i: (jnp.minimum(i, n_lat - 1), 0)),
                  pl.BlockSpec((tm, d), lambda i: (jnp.maximum(i - n_lat, 0), 0)),
                  tile, tile, gate(0), gate(1), gate(2), tile,
                  _mod_spec(geom, tm, 2), _mod_spec(geom, tm, 3), _mod_spec(geom, tm, 4),
                  wsp, wsp, wsp, wsp, vsp, vsp],
        out_specs=[tile, tile],
        out_shape=[jax.ShapeDtypeStruct((geom.nt, d), F32)] * 2,
        compiler_params=_params(1, 48),
        name="merge_ln1",
    )(ret_l, ret_c, att, cv, z, z, z, x, mods, mods, mods, w_ret_o, w_att_o, w_conv_o, w_out,
      ln_g.reshape(1, d), ln_b.reshape(1, d))


def _router_kernel(h_ref, wr_ref, bias_ref, e_ref, w_ref, pos_ref, cnt_ref, hist_ref, cnt):
    i = pl.program_id(0)
    tm = h_ref.shape[0]
    ne, per = N_EXPERTS, N_EXPERTS // N_GROUPS
    neg = -jnp.inf

    @pl.when(i == 0)
    def _init():
        cnt[...] = jnp.zeros(cnt.shape, F32)

    logits = jnp.dot(h_ref[...], wr_ref[...], preferred_element_type=F32, precision=HIGHEST)
    scores = jax.nn.sigmoid(logits.T[:ne])
    sel = scores + bias_ref[...]

    member = lax.broadcasted_iota(jnp.int32, (per, tm), 0)
    grp_rows = []
    for g in range(N_GROUPS):
        blk = sel[g * per:(g + 1) * per]
        m1 = jnp.max(blk, axis=0, keepdims=True)
        first = jnp.min(jnp.where(blk == m1, member, per), axis=0, keepdims=True)
        m2 = jnp.max(jnp.where(member == first, neg, blk), axis=0, keepdims=True)
        grp_rows.append(m1 + m2)
    gs = jnp.concatenate(grp_rows, axis=0)

    gidx = lax.broadcasted_iota(jnp.int32, (N_GROUPS, tm), 0)
    rank = jnp.zeros((N_GROUPS, tm), jnp.int32)
    for g in range(N_GROUPS):
        row = gs[g:g + 1]
        ahead = jnp.logical_or(row > gs, jnp.logical_and(row == gs, g < gidx))
        rank = rank + ahead.astype(jnp.int32)
    keep = (rank < TOPK_GROUPS).astype(F32)
    keep_e = jnp.concatenate([jnp.broadcast_to(keep[g:g + 1], (per, tm)) for g in range(N_GROUPS)], axis=0)
    cand = jnp.where(keep_e > 0.5, sel, neg)

    eidx = lax.broadcasted_iota(jnp.int32, (ne, tm), 0)
    picks, gates, hots = [], [], []
    chosen = jnp.zeros((ne, tm), F32)
    for _ in range(TOP_K):
        m = jnp.max(cand, axis=0, keepdims=True)
        idx = jnp.min(jnp.where(cand == m, eidx, ne), axis=0, keepdims=True)
        hot = eidx == idx
        picks.append(idx)
        gates.append(jnp.sum(jnp.where(hot, scores, 0.0), axis=0, keepdims=True))
        hots.append(hot)
        chosen = jnp.where(hot, 1.0, chosen)
        cand = jnp.where(hot, neg, cand)
    total = gates[0]
    for gk in gates[1:]:
        total = total + gk

    ti = lax.broadcasted_iota(jnp.int32, (tm, tm), 0)
    tj = lax.broadcasted_iota(jnp.int32, (tm, tm), 1)
    before = jnp.where(ti < tj, 1.0, 0.0).astype(BF16)
    prior = jnp.dot(chosen.astype(BF16), before, preferred_element_type=F32) + cnt[...][:, :1]
    pos = [jnp.sum(jnp.where(hot, prior, 0.0), axis=0, keepdims=True) for hot in hots]

    e_ref[...] = jnp.concatenate(picks, axis=0)
    w_ref[...] = jnp.concatenate([ROUTED_SCALE * gk / total for gk in gates], axis=0)
    pos_ref[...] = jnp.concatenate(pos, axis=0).astype(jnp.int32)
    hist_ref[0] = cnt[...]
    cnt[...] = cnt[...] + jnp.sum(chosen, axis=1, keepdims=True)
    cnt_ref[...] = cnt[...]


def _router_call(geom, h2, w_router, router_bias):
    tm = geom.c
    wr = jnp.zeros((D_MODEL, 128), F32).at[:, :N_EXPERTS].set(w_router)
    tok = pl.BlockSpec((TOP_K, tm), lambda i: (0, i))
    return pl.pallas_call(
        _router_kernel,
        grid=(geom.nt // tm,),
        in_specs=[pl.BlockSpec((tm, D_MODEL), lambda i: (i, 0)),
                  pl.BlockSpec((D_MODEL, 128), lambda i: (0, 0)),
                  pl.BlockSpec((N_EXPERTS, 1), lambda i: (0, 0))],
        out_specs=[tok, tok, tok, pl.BlockSpec((N_EXPERTS, 128), lambda i: (0, 0)),
                   pl.BlockSpec((1, N_EXPERTS, 128), lambda i: (i, 0, 0))],
        out_shape=[jax.ShapeDtypeStruct((TOP_K, geom.nt), jnp.int32),
                   jax.ShapeDtypeStruct((TOP_K, geom.nt), F32),
                   jax.ShapeDtypeStruct((TOP_K, geom.nt), jnp.int32),
                   jax.ShapeDtypeStruct((N_EXPERTS, 128), F32),
                   jax.ShapeDtypeStruct((geom.nt // tm, N_EXPERTS, 128), F32)],
        scratch_shapes=[pltpu.VMEM((N_EXPERTS, 128), F32)],
        compiler_params=_params(1, 32),
        name="moe_router",
    )(h2, wr, router_bias.reshape(N_EXPERTS, 1))


HALF = D_MODEL // 2


def _pack_bf16_pairs(v):
    lo = pltpu.bitcast(v[:, :HALF].astype(BF16).astype(F32), jnp.uint32)
    hi = pltpu.bitcast(v[:, HALF:].astype(BF16).astype(F32), jnp.uint32)
    return jnp.bitwise_or(jnp.right_shift(lo, jnp.uint32(16)), hi)


def _unpack_bf16_pairs(w):
    lo = pltpu.bitcast(jnp.left_shift(w, jnp.uint32(16)), F32)
    hi = pltpu.bitcast(jnp.bitwise_and(w, jnp.uint32(0xFFFF0000)), F32)
    return lo, hi


RUN_ALIGN = 8
SORTED_ROWS = 256 * TOP_K + N_EXPERTS * RUN_ALIGN
RUN_BITS = tuple(range(8, 2, -1))


def _for_each_run_piece(n_ref, src_ref, tile, visit):
    def per_expert(e, off):
        n = n_ref[tile * N_EXPERTS + e]
        src = src_ref[tile * N_EXPERTS + e]

        for lb in RUN_BITS:
            done = (n >> (lb + 1)) << (lb + 1)

            @pl.when((n & (1 << lb)) != 0)
            def _piece():
                visit(pl.multiple_of(off + done, RUN_ALIGN), pl.multiple_of(src + done, RUN_ALIGN), 1 << lb)

        return off + n

    lax.fori_loop(0, N_EXPERTS, per_expert, 0)


TOTAL_BITS = tuple(range(11, 2, -1))


def _wait_rows(total, wait_piece):
    for lb in TOTAL_BITS:
        @pl.when((total & (1 << lb)) != 0)
        def _amount():
            wait_piece(1 << lb)


def _dispatch_kernel(last_ref, n_ref, src_ref, tot_ref, sidx_ref, h_ref, xs_out, packed, zblk, sem, zsem):
    tm = h_ref.shape[0]
    i = pl.program_id(0)

    @pl.when(pl.program_id(0) == 0)
    def _zero_tail_blocks():
        zblk[...] = jnp.zeros(zblk.shape, zblk.dtype)

        def zero_copy(e):
            return pltpu.make_async_copy(zblk, xs_out.at[pl.ds(last_ref[e] * MOE_BLOCK, MOE_BLOCK)], zsem)

        def start(e, carry):
            zero_copy(e).start()
            return carry

        def wait(e, carry):
            zero_copy(e).wait()
            return carry

        lax.fori_loop(0, N_EXPERTS, start, 0)
        lax.fori_loop(0, N_EXPERTS, wait, 0)

    rows = lax.broadcasted_iota(jnp.int32, (SORTED_ROWS, tm), 0).astype(jnp.int16)
    sidx16 = sidx_ref[...].astype(jnp.int16)
    pick = jnp.zeros((SORTED_ROWS, tm), BF16)
    for k in range(TOP_K):
        pick = jnp.where(rows == sidx16[k:k + 1, :], jnp.ones((), BF16), pick)
    sorted_rows = jnp.dot(pick, h_ref[...].astype(BF16), preferred_element_type=F32)
    lo = pltpu.bitcast(sorted_rows[:, :HALF], jnp.uint32)
    hi = pltpu.bitcast(sorted_rows[:, HALF:], jnp.uint32)
    packed[...] = jnp.bitwise_or(jnp.right_shift(lo, jnp.uint32(16)), hi)

    def piece(sorted_row, slot_row, rows_):
        return pltpu.make_async_copy(packed.at[pl.ds(sorted_row, rows_)], xs_out.at[pl.ds(slot_row, rows_)], sem)

    _for_each_run_piece(n_ref, src_ref, i, lambda a, b, r: piece(a, b, r).start())
    _wait_rows(tot_ref[i], lambda r: piece(0, 0, r).wait())


def _dispatch_call(geom, last_block, run_rows, run_slot, tile_rows, sidx, h2, n_blocks):
    tm = geom.c
    assert tm * TOP_K + N_EXPERTS * RUN_ALIGN == SORTED_ROWS
    grid_spec = pltpu.PrefetchScalarGridSpec(
        num_scalar_prefetch=4,
        grid=(geom.nt // tm,),
        in_specs=[pl.BlockSpec((TOP_K, tm), lambda i, *_: (0, i)),
                  pl.BlockSpec((tm, D_MODEL), lambda i, *_: (i, 0))],
        out_specs=pl.BlockSpec(memory_space=pl.ANY),
        scratch_shapes=[pltpu.VMEM((SORTED_ROWS, HALF), jnp.uint32), pltpu.VMEM((MOE_BLOCK, HALF), jnp.uint32),
                        pltpu.SemaphoreType.DMA(()), pltpu.SemaphoreType.DMA(())],
    )
    return pl.pallas_call(
        _dispatch_kernel,
        grid_spec=grid_spec,
        out_shape=jax.ShapeDtypeStruct(((n_blocks + 1) * MOE_BLOCK, HALF), jnp.uint32),
        compiler_params=_params(1, 48),
        name="moe_dispatch",
    )(last_block, run_rows, run_slot, tile_rows, sidx, h2)


def _expert_kernel(be_ref, nu_ref, x_ref, wg_ref, wu_ref, wd_ref, o_ref, wgu_s, wd_s):
    i = pl.program_id(0)
    live = i < nu_ref[0]
    changed = jnp.logical_or(i == 0, be_ref[i] != be_ref[jnp.maximum(i - 1, 0)])

    @pl.when(jnp.logical_and(live, changed))
    def _load_expert():
        wgu_s[:, :D_EXPERT] = wg_ref[0, 0].astype(BF16)
        wgu_s[:, D_EXPERT:] = wu_ref[0, 0].astype(BF16)
        wd_s[...] = wd_ref[0, 0].astype(BF16)

    @pl.when(live)
    def _run():
        lo, hi = _unpack_bf16_pairs(x_ref[...])
        x = jnp.concatenate([lo, hi], axis=1).astype(BF16)
        hgu = jnp.dot(x, wgu_s[...], preferred_element_type=F32)
        hid = _silu(hgu[:, :D_EXPERT]) * hgu[:, D_EXPERT:]
        o_ref[...] = _pack_bf16_pairs(jnp.dot(hid.astype(BF16), wd_s[...], preferred_element_type=F32))


def _expert_call(layer, block_e, n_used, xs, w_gate, w_up, w_down):
    n_blocks = xs.shape[0] // MOE_BLOCK - 1
    live = lambda i, be, nu: jnp.minimum(i, nu[0] - 1)
    expert = lambda i, be, nu: (layer, be[live(i, be, nu)], 0, 0)
    grid_spec = pltpu.PrefetchScalarGridSpec(
        num_scalar_prefetch=2,
        grid=(n_blocks,),
        in_specs=[pl.BlockSpec((MOE_BLOCK, HALF), lambda i, be, nu: (live(i, be, nu), 0)),
                  pl.BlockSpec((1, 1, D_MODEL, D_EXPERT), expert),
                  pl.BlockSpec((1, 1, D_MODEL, D_EXPERT), expert),
                  pl.BlockSpec((1, 1, D_EXPERT, D_MODEL), expert)],
        out_specs=pl.BlockSpec((MOE_BLOCK, HALF), lambda i, be, nu: (live(i, be, nu), 0)),
        scratch_shapes=[pltpu.VMEM((D_MODEL, 2 * D_EXPERT), BF16), pltpu.VMEM((D_EXPERT, D_MODEL), BF16)],
    )
    return pl.pallas_call(
        _expert_kernel,
        grid_spec=grid_spec,
        out_shape=jax.ShapeDtypeStruct(xs.shape, jnp.uint32),
        compiler_params=_params(1, 32),
        name="moe_experts",
    )(block_e, n_used, xs, w_gate, w_up, w_down)


def _combine_kernel(n_ref, src_ref, tot_ref, sidx_ref, wt_ref, ys_hbm, h_ref, x_ref, g2_ref, wgu_ref, wd_ref,
                    lng_ref, lnb_ref, o_ref, buf_a, buf_b, sem, *, alpha):
    i = pl.program_id(0)
    n = pl.num_programs(0)
    tm = h_ref.shape[0]
    even = i % 2 == 0

    def piece(buf, slot, sorted_row, slot_row, rows_):
        return pltpu.make_async_copy(ys_hbm.at[pl.ds(slot_row, rows_)], buf.at[pl.ds(sorted_row, rows_)], sem.at[slot])

    def gather(tile, buf, slot):
        _for_each_run_piece(n_ref, src_ref, tile, lambda a, b, r: piece(buf, slot, a, b, r).start())

    def drain(tile, buf, slot):
        _wait_rows(tot_ref[tile], lambda r: piece(buf, slot, 0, 0, r).wait())

    def finish(buf):
        wt = wt_ref[...].astype(BF16)
        sidx16 = sidx_ref[...].astype(jnp.int16)
        cols = lax.broadcasted_iota(jnp.int32, (tm, SORTED_ROWS), 1).astype(jnp.int16)
        mix = jnp.zeros((tm, SORTED_ROWS), BF16)
        for k in range(TOP_K):
            mix = jnp.where(cols == sidx16[:, k:k + 1], wt[:, k:k + 1], mix)
        lo, hi = _unpack_bf16_pairs(buf[...])
        routed = jnp.concatenate([jnp.dot(mix, lo.astype(BF16), preferred_element_type=F32),
                                  jnp.dot(mix, hi.astype(BF16), preferred_element_type=F32)], axis=1)
        hgu = jnp.dot(h_ref[...].astype(BF16), wgu_ref[...], preferred_element_type=F32)
        hid = _silu(hgu[:, :D_SHARED]) * hgu[:, D_SHARED:]
        shared = jnp.dot(hid.astype(BF16), wd_ref[...], preferred_element_type=F32)
        o_ref[...] = _layer_norm(alpha * x_ref[...] + g2_ref[0] * (routed + shared), lng_ref[...], lnb_ref[...])

    @pl.when(i == 0)
    def _first():
        buf_a[...] = jnp.zeros(buf_a.shape, buf_a.dtype)
        buf_b[...] = jnp.zeros(buf_b.shape, buf_b.dtype)
        gather(i, buf_a, 0)

    @pl.when(jnp.logical_and(even, i + 1 < n))
    def _ahead_b():
        gather(i + 1, buf_b, 1)

    @pl.when(jnp.logical_and(jnp.logical_not(even), i + 1 < n))
    def _ahead_a():
        gather(i + 1, buf_a, 0)

    @pl.when(even)
    def _finish_a():
        drain(i, buf_a, 0)
        finish(buf_a)

    @pl.when(jnp.logical_not(even))
    def _finish_b():
        drain(i, buf_b, 1)
        finish(buf_b)


def _combine_call(geom, alpha, run_rows, run_slot, tile_rows, sidx_tok, w_tok, ys, h2, x1, mods, w_sh_gu, w_sh_down, ln_g, ln_b):
    tm = geom.c
    d = D_MODEL
    n = geom.nt // tm
    tile = pl.BlockSpec((tm, d), lambda i, *_: (i, 0))
    vsp = pl.BlockSpec((1, d), lambda i, *_: (0, 0))
    per_tok = pl.BlockSpec((tm, TOP_K), lambda i, *_: (i, 0))
    grid_spec = pltpu.PrefetchScalarGridSpec(
        num_scalar_prefetch=3,
        grid=(n,),
        in_specs=[per_tok, per_tok,
                  pl.BlockSpec(memory_space=pl.ANY),
                  tile, tile, _mod_spec(geom, tm, 5),
                  pl.BlockSpec((d, 2 * D_SHARED), lambda i, *_: (0, 0)),
                  pl.BlockSpec((D_SHARED, d), lambda i, *_: (0, 0)),
                  vsp, vsp],
        out_specs=tile,
        scratch_shapes=[pltpu.VMEM((SORTED_ROWS, HALF), jnp.uint32), pltpu.VMEM((SORTED_ROWS, HALF), jnp.uint32),
                        pltpu.SemaphoreType.DMA((2,))],
    )
    return pl.pallas_call(
        functools.partial(_combine_kernel, alpha=alpha),
        grid_spec=grid_spec,
        out_shape=jax.ShapeDtypeStruct((geom.nt, d), F32),
        compiler_params=_params(1, 56),
        name="moe_combine_ln2",
    )(run_rows, run_slot, tile_rows, sidx_tok, w_tok, ys, h2, x1, mods, w_sh_gu, w_sh_down,
      ln_g.reshape(1, d), ln_b.reshape(1, d))


def _rope_tables(t):
    rows = t // GRID_W
    row = jnp.repeat(jnp.arange(rows, dtype=F32), GRID_W)
    col = jnp.tile(jnp.arange(GRID_W, dtype=F32), rows)
    n_freq = ATT_DH // 4
    inv_freq = ROPE_THETA ** (-jnp.arange(n_freq, dtype=F32) / n_freq)
    ang = jnp.concatenate([row[:, None] * inv_freq, col[:, None] * inv_freq], axis=-1)
    cos, sin = jnp.cos(ang), jnp.sin(ang)
    cos64 = jnp.concatenate([cos, cos], axis=-1)
    sin64 = jnp.concatenate([-sin, sin], axis=-1)
    return cos64, sin64


def kernel(x, c, ctx, c_ctx, w_ada, b_ada, w_in, ret_decay_logit, att_q_norm, att_k_norm, conv_dw, conv_db, conv_ln_g, conv_ln_b, w_ret_o, w_att_o, w_conv_o, w_out, ln1_g, ln1_b, w_router, router_bias, w_exp_gate, w_exp_up, w_exp_down, w_sh_gate, w_sh_up, w_sh_down, ln2_g, ln2_b):
    b, t, d = x.shape
    n_ctx = ctx.shape[1]
    depth = w_ada.shape[0]
    assert d == D_MODEL and w_in.shape[-1] == D_IN
    geom = _Geom(b, t, n_ctx)
    alpha = float((2 * depth) ** 0.25)

    cos64, sin64 = _rope_tables(t)
    cos128 = jnp.concatenate([cos64, cos64], axis=-1)
    sin128 = jnp.concatenate([sin64, sin64], axis=-1)

    n_rows = -(-(b + 1) // 8) * 8
    cvecs = jnp.zeros((n_rows, d), F32).at[:b].set(c).at[b].set(c_ctx)
    mods_all = _mods_call(cvecs, w_ada, b_ada).reshape(depth, n_rows * 6, 1, d)

    n_tiles = geom.nt // geom.c
    n_blocks = -(-(geom.nt * TOP_K + n_tiles * N_EXPERTS * (RUN_ALIGN - 1)) // MOE_BLOCK) + N_EXPERTS

    xt = jnp.concatenate([x.reshape(geom.nl, d), ctx.reshape(geom.nc, d)], axis=0)
    for l in range(depth):
        mods = mods_all[l]
        w_in_l = _permute_columns(w_in[l]).astype(BF16)
        z = _inproj_call(geom, xt, mods, w_in_l)

        log_gamma = jax.nn.log_sigmoid(ret_decay_logit[l].astype(F32))
        ret_l, ret_c = _retention_call(geom, z, log_gamma, cos128, sin128)
        att = _attention_call(geom, z, att_q_norm[l], att_k_norm[l], cos128, sin128)
        cv = _conv_call(geom, z, conv_dw[l], conv_db[l], conv_ln_g[l], conv_ln_b[l])
        x1, h2 = _mix_call(geom, alpha, ret_l, ret_c, att, cv, z, xt, mods,
                           w_ret_o[l].astype(BF16), w_att_o[l].astype(BF16), w_conv_o[l].astype(BF16),
                           w_out[l].astype(BF16), ln1_g[l], ln1_b[l])

        top_e, gate_w, pos, counts, cnt_hist = _router_call(geom, h2, w_router[l], router_bias[l])
        before = cnt_hist[:, :, 0].astype(jnp.int32)
        total = counts[:, 0].astype(jnp.int32)
        tile_n = jnp.concatenate([before[1:], total[None, :]], axis=0) - before
        run_rows = (tile_n + RUN_ALIGN - 1) // RUN_ALIGN * RUN_ALIGN
        run_before = jnp.cumsum(run_rows, axis=0) - run_rows
        blocks_e = (jnp.sum(run_rows, axis=0) + MOE_BLOCK - 1) // MOE_BLOCK
        blocks_end = jnp.cumsum(blocks_e)
        start_row = (blocks_end - blocks_e) * MOE_BLOCK
        run_slot = start_row[None, :] + run_before
        run_sorted = jnp.cumsum(run_rows, axis=1) - run_rows
        onehot = top_e[:, :, None] == jnp.arange(N_EXPERTS, dtype=jnp.int32)[None, None, :]
        per_token = lambda table: jnp.sum(jnp.where(onehot, jnp.repeat(table, geom.c, axis=0)[None], 0), axis=-1)
        sidx = per_token(run_sorted) + pos - per_token(before)
        block_ids = jnp.arange(n_blocks, dtype=jnp.int32)
        block_e = jnp.minimum(jnp.sum((blocks_end[None, :] <= block_ids[:, None]).astype(jnp.int32), axis=1),
                              N_EXPERTS - 1)
        n_used = blocks_end[-1:].astype(jnp.int32)
        last_block = jnp.where(blocks_e > 0, blocks_end - 1, n_blocks).astype(jnp.int32)
        run_rows_flat, run_slot_flat = run_rows.reshape(-1), run_slot.reshape(-1).astype(jnp.int32)
        tile_rows = jnp.sum(run_rows, axis=1)

        xs = _dispatch_call(geom, last_block, run_rows_flat, run_slot_flat, tile_rows, sidx, h2, n_blocks)
        ys = _expert_call(l, block_e, n_used, xs, w_exp_gate, w_exp_up, w_exp_down)
        w_sh_gu = jnp.concatenate([w_sh_gate[l], w_sh_up[l]], axis=-1).astype(BF16)
        xt = _combine_call(geom, alpha, run_rows_flat, run_slot_flat, tile_rows, sidx.T, gate_w.T, ys, h2, x1, mods, w_sh_gu,
                           w_sh_down[l].astype(BF16), ln2_g[l], ln2_b[l])
    return xt[:geom.nl].reshape(b, t, d)
```

```python
import functools

import jax
import jax.numpy as jnp
from jax import lax
from jax.experimental import pallas as pl
from jax.experimental.pallas import tpu as pltpu

F32 = jnp.float32
BF16 = jnp.bfloat16
HIGHEST = lax.Precision.HIGHEST

D_MODEL = 1024
GRID_W = 64
EPS = 1e-6

RET_HEADS = 8
RET_DK = 64
RET_DV = 128
RET_CHUNK = 256
RET_W = RET_HEADS * RET_DV

ATT_HEADS = 16
ATT_KV_HEADS = 4
ATT_DH = 64
ATT_GROUP = ATT_HEADS // ATT_KV_HEADS
ATT_W = ATT_HEADS * ATT_DH
ROPE_THETA = 10000.0
ATT_KEY_BLOCK = 2048

CONV_CH = 1024
CONV_K = 31
CONV_HALO = 16

N_EXPERTS = 64
TOP_K = 8
N_GROUPS = 8
TOPK_GROUPS = 4
D_EXPERT = 256
D_SHARED = 256
ROUTED_SCALE = 2.5
MOE_BLOCK = 512

_ORIG = dict(rq=0, rk=512, rv=1024, rg=2048, aq=3072, ak=4096, av=4352, cu=4608, gt=6656)
D_IN = 9728
COL_CU = 0
COL_GT = 2048
COL_RG = 5120
COL_RV = 6144
COL_RQ = 7168
COL_RK = 7680
COL_ATT = 8192
ATT_SECTION = ATT_GROUP * ATT_DH + 2 * ATT_DH


def _column_ranges():
    rng = [(_ORIG["cu"], _ORIG["cu"] + 2 * CONV_CH),
           (_ORIG["gt"], _ORIG["gt"] + 3 * D_MODEL),
           (_ORIG["rg"], _ORIG["rg"] + RET_W),
           (_ORIG["rv"], _ORIG["rv"] + RET_W),
           (_ORIG["rq"], _ORIG["rq"] + RET_HEADS * RET_DK),
           (_ORIG["rk"], _ORIG["rk"] + RET_HEADS * RET_DK)]
    for g in range(ATT_KV_HEADS):
        rng.append((_ORIG["aq"] + g * ATT_GROUP * ATT_DH, _ORIG["aq"] + (g + 1) * ATT_GROUP * ATT_DH))
        rng.append((_ORIG["ak"] + g * ATT_DH, _ORIG["ak"] + (g + 1) * ATT_DH))
        rng.append((_ORIG["av"] + g * ATT_DH, _ORIG["av"] + (g + 1) * ATT_DH))
    cols = [c for a, b in rng for c in range(a, b)]
    assert sorted(cols) == list(range(D_IN))
    return rng


def _permute_columns(w):
    return jnp.concatenate([w[:, a:b] for a, b in _column_ranges()], axis=1)


def _params(n_axes, vmem_mib):
    return pltpu.CompilerParams(dimension_semantics=("arbitrary",) * n_axes,
                                vmem_limit_bytes=vmem_mib * 1024 * 1024)


def _silu(v):
    return v * jax.nn.sigmoid(v)


def _layer_norm(v, g, b):
    mu = jnp.mean(v, axis=-1, keepdims=True)
    d = v - mu
    var = jnp.mean(d * d, axis=-1, keepdims=True)
    return d * lax.rsqrt(var + EPS) * g + b


def _mods_kernel(c_ref, w_ref, b_ref, o_ref):
    s = _silu(c_ref[...])
    o_ref[0] = jnp.dot(s, w_ref[0], preferred_element_type=F32, precision=HIGHEST) + b_ref[0]


def _mods_call(cvecs, w_ada, b_ada):
    n_layers = w_ada.shape[0]
    rows, d = cvecs.shape
    return pl.pallas_call(
        _mods_kernel,
        grid=(n_layers, 6),
        in_specs=[pl.BlockSpec((rows, d), lambda l, j: (0, 0)),
                  pl.BlockSpec((1, d, d), lambda l, j: (l, 0, j)),
                  pl.BlockSpec((1, 1, d), lambda l, j: (l, 0, j))],
        out_specs=pl.BlockSpec((1, rows, d), lambda l, j: (l, 0, j)),
        out_shape=jax.ShapeDtypeStruct((n_layers, rows, 6 * d), F32),
        compiler_params=_params(2, 32),
        name="adaln_mods",
    )(cvecs, w_ada, b_ada.reshape(n_layers, 1, 6 * d))


class _Geom:
    def __init__(self, b, t, c):
        assert t % c == 0 and c % RET_CHUNK == 0 and c % CONV_HALO == 0 and t % ATT_KEY_BLOCK == 0
        self.b, self.t, self.c = b, t, c
        self.nl, self.nc = b * t, b * c
        self.nt = self.nl + self.nc
        self.lat_blocks = t // c
        self.nlb = self.nl // c
        self.p = t + c

    def row_block(self, bi, r):
        return jnp.where(r < self.lat_blocks, bi * self.lat_blocks + r, self.nlb + bi)

    def mod_row(self, i, tm):
        return jnp.where(i * tm < self.nl, (i * tm) // self.t, self.b)


def _mod_spec(geom, tm, which, grid_pos=0):
    d = D_MODEL
    if grid_pos == 0:
        return pl.BlockSpec((1, 1, d), lambda i, *_: (geom.mod_row(i, tm) * 6 + which, 0, 0))
    return pl.BlockSpec((1, 1, d), lambda j, i: (geom.mod_row(i, tm) * 6 + which, 0, 0))


def _inproj_kernel(x_ref, sh_ref, sc_ref, w_ref, o_ref):
    h = x_ref[...] * (1.0 + sc_ref[0]) + sh_ref[0]
    o_ref[...] = jnp.dot(h.astype(BF16), w_ref[...], preferred_element_type=F32).astype(o_ref.dtype)


def _inproj_call(geom, x, mods, w_in_bf16):
    tm = 512 if geom.nc % 512 == 0 and geom.t % 512 == 0 else geom.c
    tn = D_IN // 2
    return pl.pallas_call(
        _inproj_kernel,
        grid=(D_IN // tn, geom.nt // tm),
        in_specs=[pl.BlockSpec((tm, D_MODEL), lambda j, i: (i, 0)),
                  _mod_spec(geom, tm, 0, grid_pos=1),
                  _mod_spec(geom, tm, 1, grid_pos=1),
                  pl.BlockSpec((D_MODEL, tn), lambda j, i: (0, j))],
        out_specs=pl.BlockSpec((tm, tn), lambda j, i: (i, j)),
        out_shape=jax.ShapeDtypeStruct((geom.nt, D_IN), BF16),
        compiler_params=_params(2, 48),
        name="in_proj",
    )(x, mods, mods, w_in_bf16)


def _rot_half_128(v):
    lane = lax.broadcasted_iota(jnp.int32, v.shape, 1)
    return jnp.where((lane % 64) < 32, pltpu.roll(v, 96, 1), pltpu.roll(v, 32, 1))


def _ret_kernel(lg_ref, ql_ref, qc_ref, kl_ref, kc_ref, vl_ref, vc_ref, gl_ref, gc_ref, cos_ref, sin_ref,
                ol_ref, oc_ref, qs, kts, yf, yb, st, dm, qwb, kwb, gcs, *, t, c):
    ch = RET_CHUNK
    hp = pl.program_id(1)
    lat_blocks = t // c
    n_lat, n_ctx = t // ch, c // ch

    def scan():
        ri = lax.broadcasted_iota(jnp.int32, (ch, ch), 0).astype(F32)
        ci = lax.broadcasted_iota(jnp.int32, (ch, ch), 1).astype(F32)
        rv = lax.broadcasted_iota(jnp.int32, (ch, RET_DV), 0).astype(F32)
        for d in range(2):
            for h in range(2):
                u = 2 * d + h
                lg = lg_ref[d, 2 * hp + h]
                rel = (ri - ci) if d == 0 else (ci - ri)
                dm[u] = jnp.where(rel >= 0.0, jnp.exp(lg * jnp.maximum(rel, 0.0)), 0.0)
                qwb[u] = jnp.exp(lg * ((rv + 1.0) if d == 0 else (float(ch) - rv)))
                kwb[u] = jnp.exp(lg * ((float(ch) - 1.0 - rv) if d == 0 else rv))
                gcs[u] = jnp.exp(jnp.full((RET_DK, RET_DV), lg * float(ch), F32))
                st[u] = jnp.zeros((RET_DK, RET_DV), F32)

        def stage(q, k, seq_rows):
            qs[0, seq_rows, :] = q[:, :RET_DK].astype(BF16)
            qs[1, seq_rows, :] = q[:, RET_DK:].astype(BF16)
            kt = k.T
            kts[0, :, seq_rows] = kt[:RET_DK].astype(BF16)
            kts[1, :, seq_rows] = kt[RET_DK:].astype(BF16)

        kscale = RET_DK ** -0.5
        for cc in range(n_ctx):
            rows = pl.ds(cc * ch, ch)
            stage(qc_ref[rows, :].astype(F32), kc_ref[rows, :].astype(F32) * kscale, rows)

        def stage_lat(cc, carry):
            rows = pl.ds(pl.multiple_of(cc * ch, ch), ch)
            cs, sn = cos_ref[rows, :], sin_ref[rows, :]
            q = ql_ref[rows, :].astype(F32)
            k = kl_ref[rows, :].astype(F32)
            q = q * cs + _rot_half_128(q) * sn
            k = (k * cs + _rot_half_128(k) * sn) * kscale
            stage(q, k, pl.ds(pl.multiple_of(c + cc * ch, ch), ch))
            return carry

        lax.fori_loop(0, n_lat, stage_lat, 0)

        def run_segment(v_ref, seq_off, n):
            def body(i, carry):
                for d, cc in ((0, i), (1, n - 1 - i)):
                    vrows = pl.ds(pl.multiple_of(cc * ch, ch), ch)
                    srows = pl.ds(pl.multiple_of(seq_off + cc * ch, ch), ch)
                    for h in range(2):
                        u = 2 * d + h
                        q = qs[h, srows, :]
                        kt = kts[h, :, srows]
                        v = v_ref[vrows, h * RET_DV:(h + 1) * RET_DV].astype(F32)
                        s = jnp.dot(q, kt, preferred_element_type=F32)
                        y = jnp.dot((s * dm[u]).astype(BF16), v.astype(BF16), preferred_element_type=F32)
                        state = st[u]
                        y = y + jnp.dot(q, state.astype(BF16), preferred_element_type=F32) * qwb[u]
                        dst = yf if d == 0 else yb
                        dst[srows, h * RET_DV:(h + 1) * RET_DV] = y
                        kv = jnp.dot(kt, (v * kwb[u]).astype(BF16), preferred_element_type=F32)
                        st[u] = gcs[u] * state + kv
                return carry

            lax.fori_loop(0, n, body, 0, unroll=2)

        run_segment(vc_ref, 0, n_ctx)
        run_segment(vl_ref, c, n_lat)

    def finish(srows, g_ref, o_ref, rows):
        y = yf[srows, :] + yb[srows, :]
        for h in range(2):
            cols = slice(h * RET_DV, (h + 1) * RET_DV)
            yh = y[:, cols]
            mu = jnp.mean(yh, axis=-1, keepdims=True)
            dlt = yh - mu
            var = jnp.mean(dlt * dlt, axis=-1, keepdims=True)
            out = _silu(g_ref[rows, cols].astype(F32)) * (dlt * lax.rsqrt(var + EPS))
            o_ref[rows, cols] = out.astype(o_ref.dtype)

    scan()
    finish(pl.ds(0, c), gc_ref, oc_ref, pl.ds(0, c))

    def fin_lat(i, carry):
        rows = pl.ds(pl.multiple_of(i * c, c), c)
        finish(pl.ds(pl.multiple_of(c + i * c, c), c), gl_ref, ol_ref, rows)
        return carry

    lax.fori_loop(0, lat_blocks, fin_lat, 0)


def _retention_call(geom, z, log_gamma, cos128, sin128):
    t, c, p = geom.t, geom.c, geom.p
    hpairs = RET_HEADS // 2
    qb, kb = COL_RQ // 128, COL_RK // 128
    vb, gb = COL_RV // 256, COL_RG // 256
    in_specs = [
        pl.BlockSpec(memory_space=pltpu.SMEM),
        pl.BlockSpec((t, 128), lambda b, h: (b, qb + h)),
        pl.BlockSpec((c, 128), lambda b, h: (geom.nlb + b, qb + h)),
        pl.BlockSpec((t, 128), lambda b, h: (b, kb + h)),
        pl.BlockSpec((c, 128), lambda b, h: (geom.nlb + b, kb + h)),
        pl.BlockSpec((t, 256), lambda b, h: (b, vb + h)),
        pl.BlockSpec((c, 256), lambda b, h: (geom.nlb + b, vb + h)),
        pl.BlockSpec((t, 256), lambda b, h: (b, gb + h)),
        pl.BlockSpec((c, 256), lambda b, h: (geom.nlb + b, gb + h)),
        pl.BlockSpec((t, 128), lambda b, h: (0, 0)),
        pl.BlockSpec((t, 128), lambda b, h: (0, 0)),
    ]
    scratch = [
        pltpu.VMEM((2, p, RET_DK), BF16),
        pltpu.VMEM((2, RET_DK, p), BF16),
        pltpu.VMEM((p, 2 * RET_DV), F32),
        pltpu.VMEM((p, 2 * RET_DV), F32),
        pltpu.VMEM((4, RET_DK, RET_DV), F32),
        pltpu.VMEM((4, RET_CHUNK, RET_CHUNK), F32),
        pltpu.VMEM((4, RET_CHUNK, RET_DV), F32),
        pltpu.VMEM((4, RET_CHUNK, RET_DV), F32),
        pltpu.VMEM((4, RET_DK, RET_DV), F32),
    ]
    return pl.pallas_call(
        functools.partial(_ret_kernel, t=t, c=c),
        grid=(geom.b, hpairs),
        in_specs=in_specs,
        out_specs=[pl.BlockSpec((t, 256), lambda b, h: (b, h)), pl.BlockSpec((c, 256), lambda b, h: (b, h))],
        out_shape=[jax.ShapeDtypeStruct((geom.nl, RET_W), BF16),
                   jax.ShapeDtypeStruct((geom.nc, RET_W), BF16)],
        scratch_shapes=scratch,
        compiler_params=_params(2, 56),
        name="retention",
    )(log_gamma, z, z, z, z, z, z, z, z, cos128, sin128)


def _rms_heads_128(v, g):
    li = lax.broadcasted_iota(jnp.int32, (128, 128), 0) // ATT_DH
    lj = lax.broadcasted_iota(jnp.int32, (128, 128), 1) // ATT_DH
    avg = jnp.where(li == lj, 1.0 / ATT_DH, 0.0).astype(BF16)
    sq = v * v
    hi = sq.astype(BF16)
    lo = (sq - hi.astype(F32)).astype(BF16)
    ms = jnp.dot(hi, avg, preferred_element_type=F32) + jnp.dot(lo, avg, preferred_element_type=F32)
    return v * lax.rsqrt(ms + EPS) * g


def _att_kernel(qal_ref, qbl_ref, qac_ref, qbc_ref, kvl_ref, kvc_ref, qn_ref, kn_ref, cos_ref, sin_ref,
                ol_ref, oc_ref, kts, vs, m_s, acc_s, *, t, c):
    lat_blocks = t // c
    dh = ATT_DH
    tk = ATT_KEY_BLOCK
    lane = lax.broadcasted_iota(jnp.int32, (c, 2 * dh), 1)

    def stage_tile(kv, dst, cs, sn):
        k = _rms_heads_128(kv, kn_ref[...])
        if cs is not None:
            k = k * cs + _rot_half_128(k) * sn
        kts[:, dst] = k.T[:dh].astype(BF16)
        vs[dst, :] = jnp.where(lane < dh, pltpu.roll(kv, dh, 1), 1.0).astype(BF16)

    stage_tile(kvc_ref[...].astype(F32), pl.ds(0, c), None, None)

    def stage(i, carry):
        rows = pl.ds(pl.multiple_of(i * c, c), c)
        stage_tile(kvl_ref[rows, :].astype(F32), pl.ds(pl.multiple_of(c + i * c, c), c),
                   cos_ref[rows, :], sin_ref[rows, :])
        return carry

    lax.fori_loop(0, lat_blocks, stage, 0)

    def attend(q_tiles, rope, o_ref, rows):
        q_heads = []
        for x in q_tiles:
            xn = _rms_heads_128(x, qn_ref[...])
            if rope is not None:
                xn = xn * rope[0] + _rot_half_128(xn) * rope[1]
            xr = xn * (dh ** -0.5)
            q_heads.append(xr[:, :dh].astype(BF16))
            q_heads.append(pltpu.roll(xr, dh, 1)[:, :dh].astype(BF16))
        q = jnp.concatenate(q_heads, axis=0)

        m_s[...] = jnp.full(m_s.shape, -jnp.inf, F32)
        acc_s[...] = jnp.zeros(acc_s.shape, F32)

        def flash_step(kt, v):
            n = kt.shape[1]
            s = jnp.dot(q, kt, preferred_element_type=F32)
            m_prev = m_s[...]
            m_next = jnp.maximum(m_prev, jnp.max(s, axis=1, keepdims=True))
            prob = jnp.exp(s - jnp.concatenate([m_next] * (n // 128), axis=1))
            acc_s[...] = (acc_s[...] * jnp.exp(m_prev - m_next)
                          + jnp.dot(prob.astype(BF16), v, preferred_element_type=F32))
            m_s[...] = m_next

        flash_step(kts[:, 0:c], vs[0:c, :])
        if rope is not None:
            for j in range(t // tk):
                flash_step(kts[:, c + j * tk:c + (j + 1) * tk], vs[c + j * tk:c + (j + 1) * tk, :])

        outs = []
        for h in range(ATT_GROUP):
            acc = acc_s[h * c:(h + 1) * c, :]
            outs.append(acc * pltpu.roll(1.0 / acc, dh, 1))
        for pair in range(ATT_GROUP // 2):
            both = jnp.where(lane < dh, outs[2 * pair], pltpu.roll(outs[2 * pair + 1], dh, 1))
            o_ref[rows, pair * 2 * dh:(pair + 1) * 2 * dh] = both.astype(o_ref.dtype)

    attend((qac_ref[...].astype(F32), qbc_ref[...].astype(F32)), None, oc_ref, pl.ds(0, c))

    def lat_block(i, carry):
        rows = pl.ds(pl.multiple_of(i * c, c), c)
        attend((qal_ref[rows, :].astype(F32), qbl_ref[rows, :].astype(F32)),
               (cos_ref[rows, :], sin_ref[rows, :]), ol_ref, rows)
        return carry

    lax.fori_loop(0, lat_blocks, lat_block, 0)


def _attention_call(geom, z, q_norm, k_norm, cos128, sin128):
    t, c, p = geom.t, geom.c, geom.p
    ab = COL_ATT // 128
    sec = ATT_SECTION // 128
    lat = lambda col: pl.BlockSpec((t, 128), lambda b, g: (b, ab + sec * g + col))
    ctx = lambda col: pl.BlockSpec((c, 128), lambda b, g: (geom.nlb + b, ab + sec * g + col))
    cst = lambda rows: pl.BlockSpec((rows, 128), lambda b, g: (0, 0))
    in_specs = [lat(0), lat(1), ctx(0), ctx(1), lat(2), ctx(2), cst(1), cst(1), cst(t), cst(t)]
    two_heads = lambda v: jnp.tile(v.reshape(1, ATT_DH), (1, 2))
    scratch = [
        pltpu.VMEM((ATT_DH, p), BF16),
        pltpu.VMEM((p, 2 * ATT_DH), BF16),
        pltpu.VMEM((ATT_GROUP * c, 128), F32),
        pltpu.VMEM((ATT_GROUP * c, 2 * ATT_DH), F32),
    ]
    width = ATT_GROUP * ATT_DH
    return pl.pallas_call(
        functools.partial(_att_kernel, t=t, c=c),
        grid=(geom.b, ATT_KV_HEADS),
        in_specs=in_specs,
        out_specs=[pl.BlockSpec((t, width), lambda b, g: (b, g)), pl.BlockSpec((c, width), lambda b, g: (b, g))],
        out_shape=[jax.ShapeDtypeStruct((geom.nl, ATT_W), BF16),
                   jax.ShapeDtypeStruct((geom.nc, ATT_W), BF16)],
        scratch_shapes=scratch,
        compiler_params=_params(2, 48),
        name="attention",
    )(z, z, z, z, z, z, two_heads(q_norm), two_heads(k_norm), cos128, sin128)


def _conv_kernel(a_ref, g_ref, ap_ref, gp_ref, an_ref, gn_ref, w_ref, b_ref, lng_ref, lnb_ref, o_ref,
                 ext, ys, shifted, *, t, c):
    r = pl.program_id(1)
    lat_blocks = t // c
    halo = CONV_HALO
    has_prev = jnp.logical_and(r != 0, r != lat_blocks)
    has_next = jnp.logical_and(r != lat_blocks - 1, r != lat_blocks)
    glu = lambda a, g: a[...].astype(F32) * jax.nn.sigmoid(g[...].astype(F32))
    ext[halo:halo + c, :] = glu(a_ref, g_ref)
    ext[0:halo, :] = jnp.where(has_prev, glu(ap_ref, gp_ref), 0.0)
    ext[halo + c:, :] = jnp.where(has_next, glu(an_ref, gn_ref), 0.0)

    rt = 64
    first = halo - CONV_K // 2
    span = c + 2 * halo - 8
    for s in range(1, 8):
        shifted[s - 1, 0:span, :] = ext[s:s + span, :]

    def lane_block(cb, carry):
        lanes = pl.ds(pl.multiple_of(cb * 128, 128), 128)
        for ti in range(c // rt):
            acc = jnp.zeros((rt, 128), F32)
            for j in range(CONV_K):
                row, s = divmod(ti * rt + first + j, 8)
                src = ext if s == 0 else shifted.at[s - 1]
                acc = acc + w_ref[pl.ds(j, 1), lanes] * src[pl.ds(row * 8, rt), lanes]
            ys[pl.ds(ti * rt, rt), lanes] = acc
        return carry

    lax.fori_loop(0, CONV_CH // 128, lane_block, 0)
    y = ys[...] + b_ref[...]
    o_ref[...] = _silu(_layer_norm(y, lng_ref[...], lnb_ref[...])).astype(o_ref.dtype)


def _conv_call(geom, z, conv_dw, conv_db, ln_g, ln_b):
    t, c = geom.t, geom.c
    rb = geom.row_block
    hb = c // CONV_HALO
    last = geom.nt // CONV_HALO - 1
    prev = lambda b, r: jnp.maximum(rb(b, r) * hb - 1, 0)
    nxt = lambda b, r: jnp.minimum((rb(b, r) + 1) * hb, last)
    w = jnp.zeros((32, CONV_CH), F32).at[:CONV_K].set(conv_dw)
    vec = lambda v: v.reshape(1, CONV_CH)
    cst = pl.BlockSpec((1, CONV_CH), lambda b, r: (0, 0))
    in_specs = [
        pl.BlockSpec((c, CONV_CH), lambda b, r: (rb(b, r), 0)),
        pl.BlockSpec((c, CONV_CH), lambda b, r: (rb(b, r), 1)),
        pl.BlockSpec((CONV_HALO, CONV_CH), lambda b, r: (prev(b, r), 0)),
        pl.BlockSpec((CONV_HALO, CONV_CH), lambda b, r: (prev(b, r), 1)),
        pl.BlockSpec((CONV_HALO, CONV_CH), lambda b, r: (nxt(b, r), 0)),
        pl.BlockSpec((CONV_HALO, CONV_CH), lambda b, r: (nxt(b, r), 1)),
        pl.BlockSpec((32, CONV_CH), lambda b, r: (0, 0)),
        cst, cst, cst,
    ]
    return pl.pallas_call(
        functools.partial(_conv_kernel, t=t, c=c),
        grid=(geom.b, geom.lat_blocks + 1),
        in_specs=in_specs,
        out_specs=pl.BlockSpec((c, CONV_CH), lambda b, r: (rb(b, r), 0)),
        out_shape=jax.ShapeDtypeStruct((geom.nt, CONV_CH), BF16),
        scratch_shapes=[pltpu.VMEM((c + 2 * CONV_HALO, CONV_CH), F32), pltpu.VMEM((c, CONV_CH), F32),
                        pltpu.VMEM((7, c + 2 * CONV_HALO, CONV_CH), F32)],
        compiler_params=_params(2, 32),
        name="conformer_conv",
    )(z, z, z, z, z, z, w, vec(conv_db), vec(ln_g), vec(ln_b))


def _mix_kernel(retl_ref, retc_ref, attl_ref, attc_ref, cv_ref, gr_ref, ga_ref, gc_ref, x_ref, g1_ref, sh2_ref, sc2_ref,
                wr_ref, wa_ref, wc_ref, wo_ref, lng_ref, lnb_ref, x1_ref, h2_ref, *, alpha, n_lat_tiles):
    def proj(v, w_ref):
        return jnp.dot(v.astype(BF16), w_ref[...], preferred_element_type=F32)

    gate = lambda g_ref: jax.nn.sigmoid(g_ref[...].astype(F32))
    is_lat = pl.program_id(0) < n_lat_tiles
    ret = jnp.where(is_lat, retl_ref[...], retc_ref[...])
    att = jnp.where(is_lat, attl_ref[...], attc_ref[...])
    merged = (gate(gr_ref) * proj(ret, wr_ref)
              + gate(ga_ref) * proj(att, wa_ref)
              + gate(gc_ref) * proj(cv_ref[...], wc_ref))
    y = jnp.dot(merged.astype(BF16), wo_ref[...], preferred_element_type=F32)
    x1 = _layer_norm(alpha * x_ref[...] + g1_ref[0] * y, lng_ref[...], lnb_ref[...])
    x1_ref[...] = x1
    h2_ref[...] = x1 * (1.0 + sc2_ref[0]) + sh2_ref[0]


def _mix_call(geom, alpha, ret_l, ret_c, att_l, att_c, cv, z, x, mods, w_ret_o, w_att_o, w_conv_o, w_out, ln_g, ln_b):
    tm = geom.c
    d = D_MODEL
    n_lat = geom.nl // tm
    tile = pl.BlockSpec((tm, d), lambda i: (i, 0))
    lat_tile = pl.BlockSpec((tm, d), lambda i: (jnp.minimum(i, n_lat - 1), 0))
    ctx_tile = pl.BlockSpec((tm, d), lambda i: (jnp.maximum(i - n_lat, 0), 0))
    gate = lambda k: pl.BlockSpec((tm, d), lambda i: (i, COL_GT // d + k))
    wsp = pl.BlockSpec((d, d), lambda i: (0, 0))
    vsp = pl.BlockSpec((1, d), lambda i: (0, 0))
    return pl.pallas_call(
        functools.partial(_mix_kernel, alpha=alpha, n_lat_tiles=n_lat),
        grid=(geom.nt // tm,),
        in_specs=[lat_tile, ctx_tile, lat_tile, ctx_tile, tile, gate(0), gate(1), gate(2), tile,
                  _mod_spec(geom, tm, 2), _mod_spec(geom, tm, 3), _mod_spec(geom, tm, 4),
                  wsp, wsp, wsp, wsp, vsp, vsp],
        out_specs=[tile, tile],
        out_shape=[jax.ShapeDtypeStruct((geom.nt, d), F32)] * 2,
        compiler_params=_params(1, 48),
        name="merge_ln1",
    )(ret_l, ret_c, att_l, att_c, cv, z, z, z, x, mods, mods, mods, w_ret_o, w_att_o, w_conv_o, w_out,
      ln_g.reshape(1, d), ln_b.reshape(1, d))


def _router_kernel(h_ref, wr_ref, bias_ref, e_ref, w_ref, pos_ref, cnt_ref, hist_ref, cnt):
    i = pl.program_id(0)
    tm = h_ref.shape[0]
    ne, per = N_EXPERTS, N_EXPERTS // N_GROUPS
    neg = -jnp.inf

    @pl.when(i == 0)
    def _init():
        cnt[...] = jnp.zeros(cnt.shape, F32)

    logits = jnp.dot(h_ref[...], wr_ref[...], preferred_element_type=F32, precision=HIGHEST)
    scores = jax.nn.sigmoid(logits.T[:ne])
    sel = scores + bias_ref[...]

    member = lax.broadcasted_iota(jnp.int32, (per, tm), 0)
    grp_rows = []
    for g in range(N_GROUPS):
        blk = sel[g * per:(g + 1) * per]
        m1 = jnp.max(blk, axis=0, keepdims=True)
        first = jnp.min(jnp.where(blk == m1, member, per), axis=0, keepdims=True)
        m2 = jnp.max(jnp.where(member == first, neg, blk), axis=0, keepdims=True)
        grp_rows.append(m1 + m2)
    gs = jnp.concatenate(grp_rows, axis=0)

    gidx = lax.broadcasted_iota(jnp.int32, (N_GROUPS, tm), 0)
    rank = jnp.zeros((N_GROUPS, tm), jnp.int32)
    for g in range(N_GROUPS):
        row = gs[g:g + 1]
        ahead = jnp.logical_or(row > gs, jnp.logical_and(row == gs, g < gidx))
        rank = rank + ahead.astype(jnp.int32)
    keep = (rank < TOPK_GROUPS).astype(F32)
    keep_e = jnp.concatenate([jnp.broadcast_to(keep[g:g + 1], (per, tm)) for g in range(N_GROUPS)], axis=0)
    cand = jnp.where(keep_e > 0.5, sel, neg)

    eidx = lax.broadcasted_iota(jnp.int32, (ne, tm), 0)
    picks, gates, hots = [], [], []
    chosen = jnp.zeros((ne, tm), F32)
    for _ in range(TOP_K):
        m = jnp.max(cand, axis=0, keepdims=True)
        idx = jnp.min(jnp.where(cand == m, eidx, ne), axis=0, keepdims=True)
        hot = eidx == idx
        picks.append(idx)
        gates.append(jnp.sum(jnp.where(hot, scores, 0.0), axis=0, keepdims=True))
        hots.append(hot)
        chosen = jnp.where(hot, 1.0, chosen)
        cand = jnp.where(hot, neg, cand)
    total = gates[0]
    for gk in gates[1:]:
        total = total + gk

    ti = lax.broadcasted_iota(jnp.int32, (tm, tm), 0)
    tj = lax.broadcasted_iota(jnp.int32, (tm, tm), 1)
    before = jnp.where(ti < tj, 1.0, 0.0).astype(BF16)
    prior = jnp.dot(chosen.astype(BF16), before, preferred_element_type=F32) + cnt[...][:, :1]
    pos = [jnp.sum(jnp.where(hot, prior, 0.0), axis=0, keepdims=True) for hot in hots]

    e_ref[...] = jnp.concatenate(picks, axis=0)
    w_ref[...] = jnp.concatenate([ROUTED_SCALE * gk / total for gk in gates], axis=0)
    pos_ref[...] = jnp.concatenate(pos, axis=0).astype(jnp.int32)
    hist_ref[0] = cnt[...]
    cnt[...] = cnt[...] + jnp.sum(chosen, axis=1, keepdims=True)
    cnt_ref[...] = cnt[...]


def _router_call(geom, h2, w_router, router_bias):
    tm = geom.c
    wr = jnp.zeros((D_MODEL, 128), F32).at[:, :N_EXPERTS].set(w_router)
    tok = pl.BlockSpec((TOP_K, tm), lambda i: (0, i))
    return pl.pallas_call(
        _router_kernel,
        grid=(geom.nt // tm,),
        in_specs=[pl.BlockSpec((tm, D_MODEL), lambda i: (i, 0)),
                  pl.BlockSpec((D_MODEL, 128), lambda i: (0, 0)),
                  pl.BlockSpec((N_EXPERTS, 1), lambda i: (0, 0))],
        out_specs=[tok, tok, tok, pl.BlockSpec((N_EXPERTS, 128), lambda i: (0, 0)),
                   pl.BlockSpec((1, N_EXPERTS, 128), lambda i: (i, 0, 0))],
        out_shape=[jax.ShapeDtypeStruct((TOP_K, geom.nt), jnp.int32),
                   jax.ShapeDtypeStruct((TOP_K, geom.nt), F32),
                   jax.ShapeDtypeStruct((TOP_K, geom.nt), jnp.int32),
                   jax.ShapeDtypeStruct((N_EXPERTS, 128), F32),
                   jax.ShapeDtypeStruct((geom.nt // tm, N_EXPERTS, 128), F32)],
        scratch_shapes=[pltpu.VMEM((N_EXPERTS, 128), F32)],
        compiler_params=_params(1, 32),
        name="moe_router",
    )(h2, wr, router_bias.reshape(N_EXPERTS, 1))


HALF = D_MODEL // 2


def _pack_bf16_pairs(v):
    lo = pltpu.bitcast(v[:, :HALF].astype(BF16).astype(F32), jnp.uint32)
    hi = pltpu.bitcast(v[:, HALF:].astype(BF16).astype(F32), jnp.uint32)
    return jnp.bitwise_or(jnp.right_shift(lo, jnp.uint32(16)), hi)


def _unpack_bf16_pairs(w):
    lo = pltpu.bitcast(jnp.left_shift(w, jnp.uint32(16)), F32)
    hi = pltpu.bitcast(jnp.bitwise_and(w, jnp.uint32(0xFFFF0000)), F32)
    return lo, hi


RUN_ALIGN = 8
SORTED_ROWS = 256 * TOP_K + N_EXPERTS * RUN_ALIGN
RUN_BITS = tuple(range(8, 2, -1))


def _for_each_run_piece(n_ref, src_ref, tile, visit):
    def per_expert(e, off):
        n = n_ref[tile * N_EXPERTS + e]
        src = src_ref[tile * N_EXPERTS + e]

        for lb in RUN_BITS:
            done = (n >> (lb + 1)) << (lb + 1)

            @pl.when((n & (1 << lb)) != 0)
            def _piece():
                visit(pl.multiple_of(off + done, RUN_ALIGN), pl.multiple_of(src + done, RUN_ALIGN), 1 << lb)

        return off + n

    lax.fori_loop(0, N_EXPERTS, per_expert, 0)


TOTAL_BITS = tuple(range(11, 2, -1))


def _wait_rows(total, wait_piece):
    for lb in TOTAL_BITS:
        @pl.when((total & (1 << lb)) != 0)
        def _amount():
            wait_piece(1 << lb)


def _dispatch_kernel(last_ref, n_ref, src_ref, tot_ref, sidx_ref, h_ref, xs_out, packed, zblk, sem, zsem):
    tm = h_ref.shape[0]
    i = pl.program_id(0)

    @pl.when(pl.program_id(0) == 0)
    def _zero_tail_blocks():
        zblk[...] = jnp.zeros(zblk.shape, zblk.dtype)

        def zero_copy(e):
            return pltpu.make_async_copy(zblk, xs_out.at[pl.ds(last_ref[e] * MOE_BLOCK, MOE_BLOCK)], zsem)

        def start(e, carry):
            zero_copy(e).start()
            return carry

        def wait(e, carry):
            zero_copy(e).wait()
            return carry

        lax.fori_loop(0, N_EXPERTS, start, 0)
        lax.fori_loop(0, N_EXPERTS, wait, 0)

    rows = lax.broadcasted_iota(jnp.int32, (SORTED_ROWS, tm), 0).astype(jnp.int16)
    sidx16 = sidx_ref[...].astype(jnp.int16)
    pick = jnp.zeros((SORTED_ROWS, tm), BF16)
    for k in range(TOP_K):
        pick = jnp.where(rows == sidx16[k:k + 1, :], jnp.ones((), BF16), pick)
    sorted_rows = jnp.dot(pick, h_ref[...].astype(BF16), preferred_element_type=F32)
    lo = pltpu.bitcast(sorted_rows[:, :HALF], jnp.uint32)
    hi = pltpu.bitcast(sorted_rows[:, HALF:], jnp.uint32)
    packed[...] = jnp.bitwise_or(jnp.right_shift(lo, jnp.uint32(16)), hi)

    def piece(sorted_row, slot_row, rows_):
        return pltpu.make_async_copy(packed.at[pl.ds(sorted_row, rows_)], xs_out.at[pl.ds(slot_row, rows_)], sem)

    _for_each_run_piece(n_ref, src_ref, i, lambda a, b, r: piece(a, b, r).start())
    _wait_rows(tot_ref[i], lambda r: piece(0, 0, r).wait())


def _dispatch_call(geom, last_block, run_rows, run_slot, tile_rows, sidx, h2, n_blocks):
    tm = geom.c
    assert tm * TOP_K + N_EXPERTS * RUN_ALIGN == SORTED_ROWS
    grid_spec = pltpu.PrefetchScalarGridSpec(
        num_scalar_prefetch=4,
        grid=(geom.nt // tm,),
        in_specs=[pl.BlockSpec((TOP_K, tm), lambda i, *_: (0, i)),
                  pl.BlockSpec((tm, D_MODEL), lambda i, *_: (i, 0))],
        out_specs=pl.BlockSpec(memory_space=pl.ANY),
        scratch_shapes=[pltpu.VMEM((SORTED_ROWS, HALF), jnp.uint32), pltpu.VMEM((MOE_BLOCK, HALF), jnp.uint32),
                        pltpu.SemaphoreType.DMA(()), pltpu.SemaphoreType.DMA(())],
    )
    return pl.pallas_call(
        _dispatch_kernel,
        grid_spec=grid_spec,
        out_shape=jax.ShapeDtypeStruct(((n_blocks + 1) * MOE_BLOCK, HALF), jnp.uint32),
        compiler_params=_params(1, 48),
        name="moe_dispatch",
    )(last_block, run_rows, run_slot, tile_rows, sidx, h2)


def _expert_kernel(be_ref, nu_ref, x_ref, wg_ref, wu_ref, wd_ref, o_ref, wgu_s, wd_s):
    i = pl.program_id(0)
    live = i < nu_ref[0]
    changed = jnp.logical_or(i == 0, be_ref[i] != be_ref[jnp.maximum(i - 1, 0)])

    @pl.when(jnp.logical_and(live, changed))
    def _load_expert():
        wgu_s[:, :D_EXPERT] = wg_ref[0, 0].astype(BF16)
        wgu_s[:, D_EXPERT:] = wu_ref[0, 0].astype(BF16)
        wd_s[...] = wd_ref[0, 0].astype(BF16)

    @pl.when(live)
    def _run():
        lo, hi = _unpack_bf16_pairs(x_ref[...])
        x = jnp.concatenate([lo, hi], axis=1).astype(BF16)
        hgu = jnp.dot(x, wgu_s[...], preferred_element_type=F32)
        hid = _silu(hgu[:, :D_EXPERT]) * hgu[:, D_EXPERT:]
        o_ref[...] = _pack_bf16_pairs(jnp.dot(hid.astype(BF16), wd_s[...], preferred_element_type=F32))


def _expert_call(layer, block_e, n_used, xs, w_gate, w_up, w_down):
    n_blocks = xs.shape[0] // MOE_BLOCK - 1
    live = lambda i, be, nu: jnp.minimum(i, nu[0] - 1)
    expert = lambda i, be, nu: (layer, be[live(i, be, nu)], 0, 0)
    grid_spec = pltpu.PrefetchScalarGridSpec(
        num_scalar_prefetch=2,
        grid=(n_blocks,),
        in_specs=[pl.BlockSpec((MOE_BLOCK, HALF), lambda i, be, nu: (live(i, be, nu), 0)),
                  pl.BlockSpec((1, 1, D_MODEL, D_EXPERT), expert),
                  pl.BlockSpec((1, 1, D_MODEL, D_EXPERT), expert),
                  pl.BlockSpec((1, 1, D_EXPERT, D_MODEL), expert)],
        out_specs=pl.BlockSpec((MOE_BLOCK, HALF), lambda i, be, nu: (live(i, be, nu), 0)),
        scratch_shapes=[pltpu.VMEM((D_MODEL, 2 * D_EXPERT), BF16), pltpu.VMEM((D_EXPERT, D_MODEL), BF16)],
    )
    return pl.pallas_call(
        _expert_kernel,
        grid_spec=grid_spec,
        out_shape=jax.ShapeDtypeStruct(xs.shape, jnp.uint32),
        compiler_params=_params(1, 32),
        name="moe_experts",
    )(block_e, n_used, xs, w_gate, w_up, w_down)


def _combine_kernel(n_ref, src_ref, tot_ref, sidx_ref, wt_ref, ys_hbm, h_ref, x_ref, g2_ref, wgu_ref, wd_ref,
                    lng_ref, lnb_ref, o_ref, buf_a, buf_b, sem, *, alpha):
    i = pl.program_id(0)
    n = pl.num_programs(0)
    tm = h_ref.shape[0]
    even = i % 2 == 0

    def piece(buf, slot, sorted_row, slot_row, rows_):
        return pltpu.make_async_copy(ys_hbm.at[pl.ds(slot_row, rows_)], buf.at[pl.ds(sorted_row, rows_)], sem.at[slot])

    def gather(tile, buf, slot):
        _for_each_run_piece(n_ref, src_ref, tile, lambda a, b, r: piece(buf, slot, a, b, r).start())

    def drain(tile, buf, slot):
        _wait_rows(tot_ref[tile], lambda r: piece(buf, slot, 0, 0, r).wait())

    def finish(buf):
        wt = wt_ref[...].astype(BF16)
        sidx16 = sidx_ref[...].astype(jnp.int16)
        cols = lax.broadcasted_iota(jnp.int32, (tm, SORTED_ROWS), 1).astype(jnp.int16)
        mix = jnp.zeros((tm, SORTED_ROWS), BF16)
        for k in range(TOP_K):
            mix = jnp.where(cols == sidx16[:, k:k + 1], wt[:, k:k + 1], mix)
        lo, hi = _unpack_bf16_pairs(buf[...])
        routed = jnp.concatenate([jnp.dot(mix, lo.astype(BF16), preferred_element_type=F32),
                                  jnp.dot(mix, hi.astype(BF16), preferred_element_type=F32)], axis=1)
        hgu = jnp.dot(h_ref[...].astype(BF16), wgu_ref[...], preferred_element_type=F32)
        hid = _silu(hgu[:, :D_SHARED]) * hgu[:, D_SHARED:]
        shared = jnp.dot(hid.astype(BF16), wd_ref[...], preferred_element_type=F32)
        o_ref[...] = _layer_norm(alpha * x_ref[...] + g2_ref[0] * (routed + shared), lng_ref[...], lnb_ref[...])

    @pl.when(i == 0)
    def _first():
        buf_a[...] = jnp.zeros(buf_a.shape, buf_a.dtype)
        buf_b[...] = jnp.zeros(buf_b.shape, buf_b.dtype)
        gather(i, buf_a, 0)

    @pl.when(jnp.logical_and(even, i + 1 < n))
    def _ahead_b():
        gather(i + 1, buf_b, 1)

    @pl.when(jnp.logical_and(jnp.logical_not(even), i + 1 < n))
    def _ahead_a():
        gather(i + 1, buf_a, 0)

    @pl.when(even)
    def _finish_a():
        drain(i, buf_a, 0)
        finish(buf_a)

    @pl.when(jnp.logical_not(even))
    def _finish_b():
        drain(i, buf_b, 1)
        finish(buf_b)


def _combine_call(geom, alpha, run_rows, run_slot, tile_rows, sidx_tok, w_tok, ys, h2, x1, mods, w_sh_gu, w_sh_down, ln_g, ln_b):
    tm = geom.c
    d = D_MODEL
    n = geom.nt // tm
    tile = pl.BlockSpec((tm, d), lambda i, *_: (i, 0))
    vsp = pl.BlockSpec((1, d), lambda i, *_: (0, 0))
    per_tok = pl.BlockSpec((tm, TOP_K), lambda i, *_: (i, 0))
    grid_spec = pltpu.PrefetchScalarGridSpec(
        num_scalar_prefetch=3,
        grid=(n,),
        in_specs=[per_tok, per_tok,
                  pl.BlockSpec(memory_space=pl.ANY),
                  tile, tile, _mod_spec(geom, tm, 5),
                  pl.BlockSpec((d, 2 * D_SHARED), lambda i, *_: (0, 0)),
                  pl.BlockSpec((D_SHARED, d), lambda i, *_: (0, 0)),
                  vsp, vsp],
        out_specs=tile,
        scratch_shapes=[pltpu.VMEM((SORTED_ROWS, HALF), jnp.uint32), pltpu.VMEM((SORTED_ROWS, HALF), jnp.uint32),
                        pltpu.SemaphoreType.DMA((2,))],
    )
    return pl.pallas_call(
        functools.partial(_combine_kernel, alpha=alpha),
        grid_spec=grid_spec,
        out_shape=jax.ShapeDtypeStruct((geom.nt, d), F32),
        compiler_params=_params(1, 56),
        name="moe_combine_ln2",
    )(run_rows, run_slot, tile_rows, sidx_tok, w_tok, ys, h2, x1, mods, w_sh_gu, w_sh_down,
      ln_g.reshape(1, d), ln_b.reshape(1, d))


def _rope_tables(t):
    rows = t // GRID_W
    row = jnp.repeat(jnp.arange(rows, dtype=F32), GRID_W)
    col = jnp.tile(jnp.arange(GRID_W, dtype=F32), rows)
    n_freq = ATT_DH // 4
    inv_freq = ROPE_THETA ** (-jnp.arange(n_freq, dtype=F32) / n_freq)
    ang = jnp.concatenate([row[:, None] * inv_freq, col[:, None] * inv_freq], axis=-1)
    cos, sin = jnp.cos(ang), jnp.sin(ang)
    cos64 = jnp.concatenate([cos, cos], axis=-1)
    sin64 = jnp.concatenate([-sin, sin], axis=-1)
    return cos64, sin64


def kernel(x, c, ctx, c_ctx, w_ada, b_ada, w_in, ret_decay_logit, att_q_norm, att_k_norm, conv_dw, conv_db, conv_ln_g, conv_ln_b, w_ret_o, w_att_o, w_conv_o, w_out, ln1_g, ln1_b, w_router, router_bias, w_exp_gate, w_exp_up, w_exp_down, w_sh_gate, w_sh_up, w_sh_down, ln2_g, ln2_b):
    b, t, d = x.shape
    n_ctx = ctx.shape[1]
    depth = w_ada.shape[0]
    assert d == D_MODEL and w_in.shape[-1] == D_IN
    geom = _Geom(b, t, n_ctx)
    alpha = float((2 * depth) ** 0.25)

    cos64, sin64 = _rope_tables(t)
    cos128 = jnp.concatenate([cos64, cos64], axis=-1)
    sin128 = jnp.concatenate([sin64, sin64], axis=-1)

    n_rows = -(-(b + 1) // 8) * 8
    cvecs = jnp.zeros((n_rows, d), F32).at[:b].set(c).at[b].set(c_ctx)
    mods_all = _mods_call(cvecs, w_ada, b_ada).reshape(depth, n_rows * 6, 1, d)

    n_tiles = geom.nt // geom.c
    n_blocks = -(-(geom.nt * TOP_K + n_tiles * N_EXPERTS * (RUN_ALIGN - 1)) // MOE_BLOCK) + N_EXPERTS

    xt = jnp.concatenate([x.reshape(geom.nl, d), ctx.reshape(geom.nc, d)], axis=0)
    for l in range(depth):
        mods = mods_all[l]
        w_in_l = _permute_columns(w_in[l]).astype(BF16)
        z = _inproj_call(geom, xt, mods, w_in_l)

        log_gamma = jax.nn.log_sigmoid(ret_decay_logit[l].astype(F32))
        ret_l, ret_c = _retention_call(geom, z, log_gamma, cos128, sin128)
        att_l, att_c = _attention_call(geom, z, att_q_norm[l], att_k_norm[l], cos128, sin128)
        cv = _conv_call(geom, z, conv_dw[l], conv_db[l], conv_ln_g[l], conv_ln_b[l])
        x1, h2 = _mix_call(geom, alpha, ret_l, ret_c, att_l, att_c, cv, z, xt, mods,
                           w_ret_o[l].astype(BF16), w_att_o[l].astype(BF16), w_conv_o[l].astype(BF16),
                           w_out[l].astype(BF16), ln1_g[l], ln1_b[l])

        top_e, gate_w, pos, counts, cnt_hist = _router_call(geom, h2, w_router[l], router_bias[l])
        before = cnt_hist[:, :, 0].astype(jnp.int32)
        total = counts[:, 0].astype(jnp.int32)
        tile_n = jnp.concatenate([before[1:], total[None, :]], axis=0) - before
        run_rows = (tile_n + RUN_ALIGN - 1) // RUN_ALIGN * RUN_ALIGN
        run_before = jnp.cumsum(run_rows, axis=0) - run_rows
        blocks_e = (jnp.sum(run_rows, axis=0) + MOE_BLOCK - 1) // MOE_BLOCK
        blocks_end = jnp.cumsum(blocks_e)
        start_row = (blocks_end - blocks_e) * MOE_BLOCK
        run_slot = start_row[None, :] + run_before
        run_sorted = jnp.cumsum(run_rows, axis=1) - run_rows
        onehot = top_e[:, :, None] == jnp.arange(N_EXPERTS, dtype=jnp.int32)[None, None, :]
        per_token = lambda table: jnp.sum(jnp.where(onehot, jnp.repeat(table, geom.c, axis=0)[None], 0), axis=-1)
        sidx = per_token(run_sorted) + pos - per_token(before)
        block_ids = jnp.arange(n_blocks, dtype=jnp.int32)
        block_e = jnp.minimum(jnp.sum((blocks_end[None, :] <= block_ids[:, None]).astype(jnp.int32), axis=1),
                              N_EXPERTS - 1)
        n_used = blocks_end[-1:].astype(jnp.int32)
        last_block = jnp.where(blocks_e > 0, blocks_end - 1, n_blocks).astype(jnp.int32)
        run_rows_flat, run_slot_flat = run_rows.reshape(-1), run_slot.reshape(-1).astype(jnp.int32)
        tile_rows = jnp.sum(run_rows, axis=1)

        xs = _dispatch_call(geom, last_block, run_rows_flat, run_slot_flat, tile_rows, sidx, h2, n_blocks)
        ys = _expert_call(l, block_e, n_used, xs, w_exp_gate, w_exp_up, w_exp_down)
        w_sh_gu = jnp.concatenate([w_sh_gate[l], w_sh_up[l]], axis=-1).astype(BF16)
        xt = _combine_call(geom, alpha, run_rows_flat, run_slot_flat, tile_rows, sidx.T, gate_w.T, ys, h2, x1, mods, w_sh_gu,
                           w_sh_down[l].astype(BF16), ln2_g[l], ln2_b[l])
    return xt[:geom.nl].reshape(b, t, d)
```

```python
import functools

import jax
import jax.numpy as jnp
from jax import lax
from jax.experimental import pallas as pl
from jax.experimental.pallas import tpu as pltpu

F32 = jnp.float32
BF16 = jnp.bfloat16
HIGHEST = lax.Precision.HIGHEST

D_MODEL = 1024
GRID_W = 64
EPS = 1e-6

RET_HEADS = 8
RET_DK = 64
RET_DV = 128
RET_CHUNK = 256
RET_W = RET_HEADS * RET_DV

ATT_HEADS = 16
ATT_KV_HEADS = 4
ATT_DH = 64
ATT_GROUP = ATT_HEADS // ATT_KV_HEADS
ATT_W = ATT_HEADS * ATT_DH
ROPE_THETA = 10000.0
ATT_KEY_BLOCK = 2048

CONV_CH = 1024
CONV_K = 31
CONV_HALO = 16

N_EXPERTS = 64
TOP_K = 8
N_GROUPS = 8
TOPK_GROUPS = 4
D_EXPERT = 256
D_SHARED = 256
ROUTED_SCALE = 2.5
MOE_BLOCK = 1024

_ORIG = dict(rq=0, rk=512, rv=1024, rg=2048, aq=3072, ak=4096, av=4352, cu=4608, gt=6656)
D_IN = 9728
COL_CU = 0
COL_GT = 2048
COL_RG = 5120
COL_RV = 6144
COL_RQ = 7168
COL_RK = 7680
COL_ATT = 8192
ATT_SECTION = ATT_GROUP * ATT_DH + 2 * ATT_DH


def _column_ranges():
    rng = [(_ORIG["cu"], _ORIG["cu"] + 2 * CONV_CH),
           (_ORIG["gt"], _ORIG["gt"] + 3 * D_MODEL),
           (_ORIG["rg"], _ORIG["rg"] + RET_W),
           (_ORIG["rv"], _ORIG["rv"] + RET_W),
           (_ORIG["rq"], _ORIG["rq"] + RET_HEADS * RET_DK),
           (_ORIG["rk"], _ORIG["rk"] + RET_HEADS * RET_DK)]
    for g in range(ATT_KV_HEADS):
        rng.append((_ORIG["aq"] + g * ATT_GROUP * ATT_DH, _ORIG["aq"] + (g + 1) * ATT_GROUP * ATT_DH))
        rng.append((_ORIG["ak"] + g * ATT_DH, _ORIG["ak"] + (g + 1) * ATT_DH))
        rng.append((_ORIG["av"] + g * ATT_DH, _ORIG["av"] + (g + 1) * ATT_DH))
    cols = [c for a, b in rng for c in range(a, b)]
    assert sorted(cols) == list(range(D_IN))
    return rng


def _permute_columns(w):
    return jnp.concatenate([w[:, a:b] for a, b in _column_ranges()], axis=1)


def _params(n_axes, vmem_mib):
    return pltpu.CompilerParams(dimension_semantics=("arbitrary",) * n_axes,
                                vmem_limit_bytes=vmem_mib * 1024 * 1024)


def _silu(v):
    return v * jax.nn.sigmoid(v)


def _layer_norm(v, g, b):
    mu = jnp.mean(v, axis=-1, keepdims=True)
    d = v - mu
    var = jnp.mean(d * d, axis=-1, keepdims=True)
    return d * lax.rsqrt(var + EPS) * g + b


def _mods_kernel(c_ref, w_ref, b_ref, o_ref):
    s = _silu(c_ref[...])
    o_ref[0] = jnp.dot(s, w_ref[0], preferred_element_type=F32, precision=HIGHEST) + b_ref[0]


def _mods_call(cvecs, w_ada, b_ada):
    n_layers = w_ada.shape[0]
    rows, d = cvecs.shape
    return pl.pallas_call(
        _mods_kernel,
        grid=(n_layers, 6),
        in_specs=[pl.BlockSpec((rows, d), lambda l, j: (0, 0)),
                  pl.BlockSpec((1, d, d), lambda l, j: (l, 0, j)),
                  pl.BlockSpec((1, 1, d), lambda l, j: (l, 0, j))],
        out_specs=pl.BlockSpec((1, rows, d), lambda l, j: (l, 0, j)),
        out_shape=jax.ShapeDtypeStruct((n_layers, rows, 6 * d), F32),
        compiler_params=_params(2, 32),
        name="adaln_mods",
    )(cvecs, w_ada, b_ada.reshape(n_layers, 1, 6 * d))


class _Geom:
    def __init__(self, b, t, c):
        assert t % c == 0 and c % RET_CHUNK == 0 and c % CONV_HALO == 0 and t % ATT_KEY_BLOCK == 0
        self.b, self.t, self.c = b, t, c
        self.nl, self.nc = b * t, b * c
        self.nt = self.nl + self.nc
        self.lat_blocks = t // c
        self.nlb = self.nl // c
        self.p = t + c

    def row_block(self, bi, r):
        return jnp.where(r < self.lat_blocks, bi * self.lat_blocks + r, self.nlb + bi)

    def mod_row(self, i, tm):
        return jnp.where(i * tm < self.nl, (i * tm) // self.t, self.b)


def _mod_spec(geom, tm, which, grid_pos=0):
    d = D_MODEL
    if grid_pos == 0:
        return pl.BlockSpec((1, 1, d), lambda i, *_: (geom.mod_row(i, tm) * 6 + which, 0, 0))
    return pl.BlockSpec((1, 1, d), lambda j, i: (geom.mod_row(i, tm) * 6 + which, 0, 0))


def _inproj_kernel(x_ref, sh_ref, sc_ref, w_ref, o_ref):
    h = x_ref[...] * (1.0 + sc_ref[0]) + sh_ref[0]
    o_ref[...] = jnp.dot(h.astype(BF16), w_ref[...], preferred_element_type=F32).astype(o_ref.dtype)


def _inproj_call(geom, x, mods, w_in_bf16):
    tm = 512 if geom.nc % 512 == 0 and geom.t % 512 == 0 else geom.c
    tn = D_IN // 2
    return pl.pallas_call(
        _inproj_kernel,
        grid=(D_IN // tn, geom.nt // tm),
        in_specs=[pl.BlockSpec((tm, D_MODEL), lambda j, i: (i, 0)),
                  _mod_spec(geom, tm, 0, grid_pos=1),
                  _mod_spec(geom, tm, 1, grid_pos=1),
                  pl.BlockSpec((D_MODEL, tn), lambda j, i: (0, j))],
        out_specs=pl.BlockSpec((tm, tn), lambda j, i: (i, j)),
        out_shape=jax.ShapeDtypeStruct((geom.nt, D_IN), BF16),
        compiler_params=_params(2, 48),
        name="in_proj",
    )(x, mods, mods, w_in_bf16)


def _rot_half_128(v):
    lane = lax.broadcasted_iota(jnp.int32, v.shape, 1)
    return jnp.where((lane % 64) < 32, pltpu.roll(v, 96, 1), pltpu.roll(v, 32, 1))


def _ret_kernel(lg_ref, ql_ref, qc_ref, kl_ref, kc_ref, vl_ref, vc_ref, gl_ref, gc_ref, cos_ref, sin_ref,
                ol_ref, oc_ref, qs, kts, yf, yb, st, dm, qwb, kwb, gcs, *, t, c):
    ch = RET_CHUNK
    hp = pl.program_id(1)
    lat_blocks = t // c
    n_lat, n_ctx = t // ch, c // ch

    def scan():
        ri = lax.broadcasted_iota(jnp.int32, (ch, ch), 0).astype(F32)
        ci = lax.broadcasted_iota(jnp.int32, (ch, ch), 1).astype(F32)
        rv = lax.broadcasted_iota(jnp.int32, (ch, RET_DV), 0).astype(F32)
        for d in range(2):
            for h in range(2):
                u = 2 * d + h
                lg = lg_ref[d, 2 * hp + h]
                rel = (ri - ci) if d == 0 else (ci - ri)
                dm[u] = jnp.where(rel >= 0.0, jnp.exp(lg * jnp.maximum(rel, 0.0)), 0.0)
                qwb[u] = jnp.exp(lg * ((rv + 1.0) if d == 0 else (float(ch) - rv)))
                kwb[u] = jnp.exp(lg * ((float(ch) - 1.0 - rv) if d == 0 else rv))
                gcs[u] = jnp.exp(jnp.full((RET_DK, RET_DV), lg * float(ch), F32))
                st[u] = jnp.zeros((RET_DK, RET_DV), F32)

        def stage(q, k, seq_rows):
            qs[0, seq_rows, :] = q[:, :RET_DK].astype(BF16)
            qs[1, seq_rows, :] = q[:, RET_DK:].astype(BF16)
            kt = k.T
            kts[0, :, seq_rows] = kt[:RET_DK].astype(BF16)
            kts[1, :, seq_rows] = kt[RET_DK:].astype(BF16)

        kscale = RET_DK ** -0.5
        for cc in range(n_ctx):
            rows = pl.ds(cc * ch, ch)
            stage(qc_ref[rows, :].astype(F32), kc_ref[rows, :].astype(F32) * kscale, rows)

        def stage_lat(cc, carry):
            rows = pl.ds(pl.multiple_of(cc * ch, ch), ch)
            cs, sn = cos_ref[rows, :], sin_ref[rows, :]
            q = ql_ref[rows, :].astype(F32)
            k = kl_ref[rows, :].astype(F32)
            q = q * cs + _rot_half_128(q) * sn
            k = (k * cs + _rot_half_128(k) * sn) * kscale
            stage(q, k, pl.ds(pl.multiple_of(c + cc * ch, ch), ch))
            return carry

        lax.fori_loop(0, n_lat, stage_lat, 0)

        def run_segment(v_ref, seq_off, n):
            def body(i, carry):
                for d, cc in ((0, i), (1, n - 1 - i)):
                    vrows = pl.ds(pl.multiple_of(cc * ch, ch), ch)
                    srows = pl.ds(pl.multiple_of(seq_off + cc * ch, ch), ch)
                    for h in range(2):
                        u = 2 * d + h
                        q = qs[h, srows, :]
                        kt = kts[h, :, srows]
                        v = v_ref[vrows, h * RET_DV:(h + 1) * RET_DV].astype(F32)
                        s = jnp.dot(q, kt, preferred_element_type=F32)
                        y = jnp.dot((s * dm[u]).astype(BF16), v.astype(BF16), preferred_element_type=F32)
                        state = st[u]
                        y = y + jnp.dot(q, state.astype(BF16), preferred_element_type=F32) * qwb[u]
                        dst = yf if d == 0 else yb
                        dst[srows, h * RET_DV:(h + 1) * RET_DV] = y
                        kv = jnp.dot(kt, (v * kwb[u]).astype(BF16), preferred_element_type=F32)
                        st[u] = gcs[u] * state + kv
                return carry

            lax.fori_loop(0, n, body, 0, unroll=2)

        run_segment(vc_ref, 0, n_ctx)
        run_segment(vl_ref, c, n_lat)

    def finish(srows, g_ref, o_ref, rows):
        y = yf[srows, :] + yb[srows, :]
        for h in range(2):
            cols = slice(h * RET_DV, (h + 1) * RET_DV)
            yh = y[:, cols]
            mu = jnp.mean(yh, axis=-1, keepdims=True)
            dlt = yh - mu
            var = jnp.mean(dlt * dlt, axis=-1, keepdims=True)
            out = _silu(g_ref[rows, cols].astype(F32)) * (dlt * lax.rsqrt(var + EPS))
            o_ref[rows, cols] = out.astype(o_ref.dtype)

    scan()
    finish(pl.ds(0, c), gc_ref, oc_ref, pl.ds(0, c))

    def fin_lat(i, carry):
        rows = pl.ds(pl.multiple_of(i * c, c), c)
        finish(pl.ds(pl.multiple_of(c + i * c, c), c), gl_ref, ol_ref, rows)
        return carry

    lax.fori_loop(0, lat_blocks, fin_lat, 0)


def _retention_call(geom, z, log_gamma, cos128, sin128):
    t, c, p = geom.t, geom.c, geom.p
    hpairs = RET_HEADS // 2
    qb, kb = COL_RQ // 128, COL_RK // 128
    vb, gb = COL_RV // 256, COL_RG // 256
    in_specs = [
        pl.BlockSpec(memory_space=pltpu.SMEM),
        pl.BlockSpec((t, 128), lambda b, h: (b, qb + h)),
        pl.BlockSpec((c, 128), lambda b, h: (geom.nlb + b, qb + h)),
        pl.BlockSpec((t, 128), lambda b, h: (b, kb + h)),
        pl.BlockSpec((c, 128), lambda b, h: (geom.nlb + b, kb + h)),
        pl.BlockSpec((t, 256), lambda b, h: (b, vb + h)),
        pl.BlockSpec((c, 256), lambda b, h: (geom.nlb + b, vb + h)),
        pl.BlockSpec((t, 256), lambda b, h: (b, gb + h)),
        pl.BlockSpec((c, 256), lambda b, h: (geom.nlb + b, gb + h)),
        pl.BlockSpec((t, 128), lambda b, h: (0, 0)),
        pl.BlockSpec((t, 128), lambda b, h: (0, 0)),
    ]
    scratch = [
        pltpu.VMEM((2, p, RET_DK), BF16),
        pltpu.VMEM((2, RET_DK, p), BF16),
        pltpu.VMEM((p, 2 * RET_DV), F32),
        pltpu.VMEM((p, 2 * RET_DV), F32),
        pltpu.VMEM((4, RET_DK, RET_DV), F32),
        pltpu.VMEM((4, RET_CHUNK, RET_CHUNK), F32),
        pltpu.VMEM((4, RET_CHUNK, RET_DV), F32),
        pltpu.VMEM((4, RET_CHUNK, RET_DV), F32),
        pltpu.VMEM((4, RET_DK, RET_DV), F32),
    ]
    return pl.pallas_call(
        functools.partial(_ret_kernel, t=t, c=c),
        grid=(geom.b, hpairs),
        in_specs=in_specs,
        out_specs=[pl.BlockSpec((t, 256), lambda b, h: (b, h)), pl.BlockSpec((c, 256), lambda b, h: (b, h))],
        out_shape=[jax.ShapeDtypeStruct((geom.nl, RET_W), BF16),
                   jax.ShapeDtypeStruct((geom.nc, RET_W), BF16)],
        scratch_shapes=scratch,
        compiler_params=_params(2, 56),
        name="retention",
    )(log_gamma, z, z, z, z, z, z, z, z, cos128, sin128)


def _rms_heads_128(v, g):
    li = lax.broadcasted_iota(jnp.int32, (128, 128), 0) // ATT_DH
    lj = lax.broadcasted_iota(jnp.int32, (128, 128), 1) // ATT_DH
    avg = jnp.where(li == lj, 1.0 / ATT_DH, 0.0).astype(BF16)
    sq = v * v
    hi = sq.astype(BF16)
    lo = (sq - hi.astype(F32)).astype(BF16)
    ms = jnp.dot(hi, avg, preferred_element_type=F32) + jnp.dot(lo, avg, preferred_element_type=F32)
    return v * lax.rsqrt(ms + EPS) * g


def _att_kernel(qal_ref, qbl_ref, qac_ref, qbc_ref, kvl_ref, kvc_ref, qn_ref, kn_ref, cos_ref, sin_ref,
                ol_ref, oc_ref, kts, vs, m_all, acc_all, *, t, c):
    lat_blocks = t // c
    assert lat_blocks % 2 == 0
    dh = ATT_DH
    tk = ATT_KEY_BLOCK
    lane = lax.broadcasted_iota(jnp.int32, (c, 2 * dh), 1)

    def stage_tile(kv, dst, cs, sn):
        k = _rms_heads_128(kv, kn_ref[...])
        if cs is not None:
            k = k * cs + _rot_half_128(k) * sn
        kts[:, dst] = k.T[:dh].astype(BF16)
        vs[dst, :] = jnp.where(lane < dh, pltpu.roll(kv, dh, 1), 1.0).astype(BF16)

    stage_tile(kvc_ref[...].astype(F32), pl.ds(0, c), None, None)

    def stage(i, carry):
        rows = pl.ds(pl.multiple_of(i * c, c), c)
        stage_tile(kvl_ref[rows, :].astype(F32), pl.ds(pl.multiple_of(c + i * c, c), c),
                   cos_ref[rows, :], sin_ref[rows, :])
        return carry

    lax.fori_loop(0, lat_blocks, stage, 0)

    def attend(q_tiles, rope, o_ref, rows, m_s, acc_s):
        q_heads = []
        for x in q_tiles:
            xn = _rms_heads_128(x, qn_ref[...])
            if rope is not None:
                xn = xn * rope[0] + _rot_half_128(xn) * rope[1]
            xr = xn * (dh ** -0.5)
            q_heads.append(xr[:, :dh].astype(BF16))
            q_heads.append(pltpu.roll(xr, dh, 1)[:, :dh].astype(BF16))
        q = jnp.concatenate(q_heads, axis=0)

        m_s[...] = jnp.full(m_s.shape, -jnp.inf, F32)
        acc_s[...] = jnp.zeros(acc_s.shape, F32)

        def flash_step(kt, v):
            n = kt.shape[1]
            s = jnp.dot(q, kt, preferred_element_type=F32)
            m_prev = m_s[...]
            m_next = jnp.maximum(m_prev, jnp.max(s, axis=1, keepdims=True))
            prob = jnp.exp(s - jnp.concatenate([m_next] * (n // 128), axis=1))
            acc_s[...] = (acc_s[...] * jnp.exp(m_prev - m_next)
                          + jnp.dot(prob.astype(BF16), v, preferred_element_type=F32))
            m_s[...] = m_next

        flash_step(kts[:, 0:c], vs[0:c, :])
        if rope is not None:
            for j in range(t // tk):
                flash_step(kts[:, c + j * tk:c + (j + 1) * tk], vs[c + j * tk:c + (j + 1) * tk, :])

        outs = []
        for h in range(ATT_GROUP):
            acc = acc_s[h * c:(h + 1) * c, :]
            outs.append(acc * pltpu.roll(1.0 / acc, dh, 1))
        for pair in range(ATT_GROUP // 2):
            both = jnp.where(lane < dh, outs[2 * pair], pltpu.roll(outs[2 * pair + 1], dh, 1))
            o_ref[rows, pair * 2 * dh:(pair + 1) * 2 * dh] = both.astype(o_ref.dtype)

    attend((qac_ref[...].astype(F32), qbc_ref[...].astype(F32)), None, oc_ref, pl.ds(0, c), m_all.at[0], acc_all.at[0])

    def lat_pair(i, carry):
        for half in range(2):
            rows = pl.ds(pl.multiple_of((2 * i + half) * c, c), c)
            attend((qal_ref[rows, :].astype(F32), qbl_ref[rows, :].astype(F32)),
                   (cos_ref[rows, :], sin_ref[rows, :]), ol_ref, rows, m_all.at[half], acc_all.at[half])
        return carry

    lax.fori_loop(0, lat_blocks // 2, lat_pair, 0)


def _attention_call(geom, z, q_norm, k_norm, cos128, sin128):
    t, c, p = geom.t, geom.c, geom.p
    ab = COL_ATT // 128
    sec = ATT_SECTION // 128
    lat = lambda col: pl.BlockSpec((t, 128), lambda b, g: (b, ab + sec * g + col))
    ctx = lambda col: pl.BlockSpec((c, 128), lambda b, g: (geom.nlb + b, ab + sec * g + col))
    cst = lambda rows: pl.BlockSpec((rows, 128), lambda b, g: (0, 0))
    in_specs = [lat(0), lat(1), ctx(0), ctx(1), lat(2), ctx(2), cst(1), cst(1), cst(t), cst(t)]
    two_heads = lambda v: jnp.tile(v.reshape(1, ATT_DH), (1, 2))
    scratch = [
        pltpu.VMEM((ATT_DH, p), BF16),
        pltpu.VMEM((p, 2 * ATT_DH), BF16),
        pltpu.VMEM((2, ATT_GROUP * c, 128), F32),
        pltpu.VMEM((2, ATT_GROUP * c, 2 * ATT_DH), F32),
    ]
    width = ATT_GROUP * ATT_DH
    return pl.pallas_call(
        functools.partial(_att_kernel, t=t, c=c),
        grid=(geom.b, ATT_KV_HEADS),
        in_specs=in_specs,
        out_specs=[pl.BlockSpec((t, width), lambda b, g: (b, g)), pl.BlockSpec((c, width), lambda b, g: (b, g))],
        out_shape=[jax.ShapeDtypeStruct((geom.nl, ATT_W), BF16),
                   jax.ShapeDtypeStruct((geom.nc, ATT_W), BF16)],
        scratch_shapes=scratch,
        compiler_params=_params(2, 48),
        name="attention",
    )(z, z, z, z, z, z, two_heads(q_norm), two_heads(k_norm), cos128, sin128)


def _conv_kernel(a_ref, g_ref, ap_ref, gp_ref, an_ref, gn_ref, w_ref, b_ref, lng_ref, lnb_ref, o_ref,
                 ext, ys, shifted, *, t, c):
    r = pl.program_id(1)
    lat_blocks = t // c
    halo = CONV_HALO
    has_prev = jnp.logical_and(r != 0, r != lat_blocks)
    has_next = jnp.logical_and(r != lat_blocks - 1, r != lat_blocks)
    glu = lambda a, g: a[...].astype(F32) * jax.nn.sigmoid(g[...].astype(F32))
    ext[halo:halo + c, :] = glu(a_ref, g_ref)
    ext[0:halo, :] = jnp.where(has_prev, glu(ap_ref, gp_ref), 0.0)
    ext[halo + c:, :] = jnp.where(has_next, glu(an_ref, gn_ref), 0.0)

    rt = 64
    first = halo - CONV_K // 2
    span = c + 2 * halo - 8
    for s in range(1, 8):
        shifted[s - 1, 0:span, :] = ext[s:s + span, :]

    def lane_block(cb, carry):
        lanes = pl.ds(pl.multiple_of(cb * 128, 128), 128)
        for ti in range(c // rt):
            acc = jnp.zeros((rt, 128), F32)
            for j in range(CONV_K):
                row, s = divmod(ti * rt + first + j, 8)
                src = ext if s == 0 else shifted.at[s - 1]
                acc = acc + w_ref[pl.ds(j, 1), lanes] * src[pl.ds(row * 8, rt), lanes]
            ys[pl.ds(ti * rt, rt), lanes] = acc
        return carry

    lax.fori_loop(0, CONV_CH // 128, lane_block, 0)
    y = ys[...] + b_ref[...]
    o_ref[...] = _silu(_layer_norm(y, lng_ref[...], lnb_ref[...])).astype(o_ref.dtype)


def _conv_call(geom, z, conv_dw, conv_db, ln_g, ln_b):
    t, c = geom.t, geom.c
    rb = geom.row_block
    hb = c // CONV_HALO
    last = geom.nt // CONV_HALO - 1
    prev = lambda b, r: jnp.maximum(rb(b, r) * hb - 1, 0)
    nxt = lambda b, r: jnp.minimum((rb(b, r) + 1) * hb, last)
    w = jnp.zeros((32, CONV_CH), F32).at[:CONV_K].set(conv_dw)
    vec = lambda v: v.reshape(1, CONV_CH)
    cst = pl.BlockSpec((1, CONV_CH), lambda b, r: (0, 0))
    in_specs = [
        pl.BlockSpec((c, CONV_CH), lambda b, r: (rb(b, r), 0)),
        pl.BlockSpec((c, CONV_CH), lambda b, r: (rb(b, r), 1)),
        pl.BlockSpec((CONV_HALO, CONV_CH), lambda b, r: (prev(b, r), 0)),
        pl.BlockSpec((CONV_HALO, CONV_CH), lambda b, r: (prev(b, r), 1)),
        pl.BlockSpec((CONV_HALO, CONV_CH), lambda b, r: (nxt(b, r), 0)),
        pl.BlockSpec((CONV_HALO, CONV_CH), lambda b, r: (nxt(b, r), 1)),
        pl.BlockSpec((32, CONV_CH), lambda b, r: (0, 0)),
        cst, cst, cst,
    ]
    return pl.pallas_call(
        functools.partial(_conv_kernel, t=t, c=c),
        grid=(geom.b, geom.lat_blocks + 1),
        in_specs=in_specs,
        out_specs=pl.BlockSpec((c, CONV_CH), lambda b, r: (rb(b, r), 0)),
        out_shape=jax.ShapeDtypeStruct((geom.nt, CONV_CH), BF16),
        scratch_shapes=[pltpu.VMEM((c + 2 * CONV_HALO, CONV_CH), F32), pltpu.VMEM((c, CONV_CH), F32),
                        pltpu.VMEM((7, c + 2 * CONV_HALO, CONV_CH), F32)],
        compiler_params=_params(2, 32),
        name="conformer_conv",
    )(z, z, z, z, z, z, w, vec(conv_db), vec(ln_g), vec(ln_b))


def _mix_kernel(retl_ref, retc_ref, attl_ref, attc_ref, cv_ref, gr_ref, ga_ref, gc_ref, x_ref, g1_ref, sh2_ref, sc2_ref,
                wr_ref, wa_ref, wc_ref, wo_ref, lng_ref, lnb_ref, x1_ref, h2_ref, *, alpha, n_lat_tiles):
    def proj(v, w_ref):
        return jnp.dot(v.astype(BF16), w_ref[...], preferred_element_type=F32)

    gate = lambda g_ref: jax.nn.sigmoid(g_ref[...].astype(F32))
    is_lat = pl.program_id(0) < n_lat_tiles
    ret = jnp.where(is_lat, retl_ref[...], retc_ref[...])
    att = jnp.where(is_lat, attl_ref[...], attc_ref[...])
    merged = (gate(gr_ref) * proj(ret, wr_ref)
              + gate(ga_ref) * proj(att, wa_ref)
              + gate(gc_ref) * proj(cv_ref[...], wc_ref))
    y = jnp.dot(merged.astype(BF16), wo_ref[...], preferred_element_type=F32)
    x1 = _layer_norm(alpha * x_ref[...] + g1_ref[0] * y, lng_ref[...], lnb_ref[...])
    x1_ref[...] = x1
    h2_ref[...] = x1 * (1.0 + sc2_ref[0]) + sh2_ref[0]


def _mix_call(geom, alpha, ret_l, ret_c, att_l, att_c, cv, z, x, mods, w_ret_o, w_att_o, w_conv_o, w_out, ln_g, ln_b):
    tm = geom.c
    d = D_MODEL
    n_lat = geom.nl // tm
    tile = pl.BlockSpec((tm, d), lambda i: (i, 0))
    lat_tile = pl.BlockSpec((tm, d), lambda i: (jnp.minimum(i, n_lat - 1), 0))
    ctx_tile = pl.BlockSpec((tm, d), lambda i: (jnp.maximum(i - n_lat, 0), 0))
    gate = lambda k: pl.BlockSpec((tm, d), lambda i: (i, COL_GT // d + k))
    wsp = pl.BlockSpec((d, d), lambda i: (0, 0))
    vsp = pl.BlockSpec((1, d), lambda i: (0, 0))
    return pl.pallas_call(
        functools.partial(_mix_kernel, alpha=alpha, n_lat_tiles=n_lat),
        grid=(geom.nt // tm,),
        in_specs=[lat_tile, ctx_tile, lat_tile, ctx_tile, tile, gate(0), gate(1), gate(2), tile,
                  _mod_spec(geom, tm, 2), _mod_spec(geom, tm, 3), _mod_spec(geom, tm, 4),
                  wsp, wsp, wsp, wsp, vsp, vsp],
        out_specs=[tile, tile],
        out_shape=[jax.ShapeDtypeStruct((geom.nt, d), F32)] * 2,
        compiler_params=_params(1, 48),
        name="merge_ln1",
    )(ret_l, ret_c, att_l, att_c, cv, z, z, z, x, mods, mods, mods, w_ret_o, w_att_o, w_conv_o, w_out,
      ln_g.reshape(1, d), ln_b.reshape(1, d))


def _router_kernel(h_ref, wr_ref, bias_ref, e_ref, w_ref, pos_ref, cnt_ref, hist_ref, cnt):
    i = pl.program_id(0)
    tm = h_ref.shape[0]
    ne, per = N_EXPERTS, N_EXPERTS // N_GROUPS
    neg = -jnp.inf

    @pl.when(i == 0)
    def _init():
        cnt[...] = jnp.zeros(cnt.shape, F32)

    logits = jnp.dot(h_ref[...], wr_ref[...], preferred_element_type=F32, precision=HIGHEST)
    scores = jax.nn.sigmoid(logits.T[:ne])
    sel = scores + bias_ref[...]

    member = lax.broadcasted_iota(jnp.int32, (per, tm), 0)
    grp_rows = []
    for g in range(N_GROUPS):
        blk = sel[g * per:(g + 1) * per]
        m1 = jnp.max(blk, axis=0, keepdims=True)
        first = jnp.min(jnp.where(blk == m1, member, per), axis=0, keepdims=True)
        m2 = jnp.max(jnp.where(member == first, neg, blk), axis=0, keepdims=True)
        grp_rows.append(m1 + m2)
    gs = jnp.concatenate(grp_rows, axis=0)

    gidx = lax.broadcasted_iota(jnp.int32, (N_GROUPS, tm), 0)
    rank = jnp.zeros((N_GROUPS, tm), jnp.int32)
    for g in range(N_GROUPS):
        row = gs[g:g + 1]
        ahead = jnp.logical_or(row > gs, jnp.logical_and(row == gs, g < gidx))
        rank = rank + ahead.astype(jnp.int32)
    keep = (rank < TOPK_GROUPS).astype(F32)
    keep_e = jnp.concatenate([jnp.broadcast_to(keep[g:g + 1], (per, tm)) for g in range(N_GROUPS)], axis=0)
    cand = jnp.where(keep_e > 0.5, sel, neg)

    eidx = lax.broadcasted_iota(jnp.int32, (ne, tm), 0)
    picks, gates, hots = [], [], []
    chosen = jnp.zeros((ne, tm), F32)
    for _ in range(TOP_K):
        m = jnp.max(cand, axis=0, keepdims=True)
        idx = jnp.min(jnp.where(cand == m, eidx, ne), axis=0, keepdims=True)
        hot = eidx == idx
        picks.append(idx)
        gates.append(jnp.sum(jnp.where(hot, scores, 0.0), axis=0, keepdims=True))
        hots.append(hot)
        chosen = jnp.where(hot, 1.0, chosen)
        cand = jnp.where(hot, neg, cand)
    total = gates[0]
    for gk in gates[1:]:
        total = total + gk

    ti = lax.broadcasted_iota(jnp.int32, (tm, tm), 0)
    tj = lax.broadcasted_iota(jnp.int32, (tm, tm), 1)
    before = jnp.where(ti < tj, 1.0, 0.0).astype(BF16)
    prior = jnp.dot(chosen.astype(BF16), before, preferred_element_type=F32) + cnt[...][:, :1]
    pos = [jnp.sum(jnp.where(hot, prior, 0.0), axis=0, keepdims=True) for hot in hots]

    e_ref[...] = jnp.concatenate(picks, axis=0)
    w_ref[...] = jnp.concatenate([ROUTED_SCALE * gk / total for gk in gates], axis=0)
    pos_ref[...] = jnp.concatenate(pos, axis=0).astype(jnp.int32)
    hist_ref[0] = cnt[...]
    cnt[...] = cnt[...] + jnp.sum(chosen, axis=1, keepdims=True)
    cnt_ref[...] = cnt[...]


def _router_call(geom, h2, w_router, router_bias):
    tm = geom.c
    wr = jnp.zeros((D_MODEL, 128), F32).at[:, :N_EXPERTS].set(w_router)
    tok = pl.BlockSpec((TOP_K, tm), lambda i: (0, i))
    return pl.pallas_call(
        _router_kernel,
        grid=(geom.nt // tm,),
        in_specs=[pl.BlockSpec((tm, D_MODEL), lambda i: (i, 0)),
                  pl.BlockSpec((D_MODEL, 128), lambda i: (0, 0)),
                  pl.BlockSpec((N_EXPERTS, 1), lambda i: (0, 0))],
        out_specs=[tok, tok, tok, pl.BlockSpec((N_EXPERTS, 128), lambda i: (0, 0)),
                   pl.BlockSpec((1, N_EXPERTS, 128), lambda i: (i, 0, 0))],
        out_shape=[jax.ShapeDtypeStruct((TOP_K, geom.nt), jnp.int32),
                   jax.ShapeDtypeStruct((TOP_K, geom.nt), F32),
                   jax.ShapeDtypeStruct((TOP_K, geom.nt), jnp.int32),
                   jax.ShapeDtypeStruct((N_EXPERTS, 128), F32),
                   jax.ShapeDtypeStruct((geom.nt // tm, N_EXPERTS, 128), F32)],
        scratch_shapes=[pltpu.VMEM((N_EXPERTS, 128), F32)],
        compiler_params=_params(1, 32),
        name="moe_router",
    )(h2, wr, router_bias.reshape(N_EXPERTS, 1))


HALF = D_MODEL // 2


def _pack_bf16_pairs(v):
    lo = pltpu.bitcast(v[:, :HALF].astype(BF16).astype(F32), jnp.uint32)
    hi = pltpu.bitcast(v[:, HALF:].astype(BF16).astype(F32), jnp.uint32)
    return jnp.bitwise_or(jnp.right_shift(lo, jnp.uint32(16)), hi)


def _unpack_bf16_pairs(w):
    lo = pltpu.bitcast(jnp.left_shift(w, jnp.uint32(16)), F32)
    hi = pltpu.bitcast(jnp.bitwise_and(w, jnp.uint32(0xFFFF0000)), F32)
    return lo, hi


RUN_ALIGN = 8
SORTED_ROWS = 256 * TOP_K + N_EXPERTS * RUN_ALIGN
RUN_BITS = tuple(range(8, 2, -1))


def _for_each_run_piece(n_ref, src_ref, tile, visit):
    def per_expert(e, off):
        n = n_ref[tile * N_EXPERTS + e]
        src = src_ref[tile * N_EXPERTS + e]

        for lb in RUN_BITS:
            done = (n >> (lb + 1)) << (lb + 1)

            @pl.when((n & (1 << lb)) != 0)
            def _piece():
                visit(pl.multiple_of(off + done, RUN_ALIGN), pl.multiple_of(src + done, RUN_ALIGN), 1 << lb)

        return off + n

    lax.fori_loop(0, N_EXPERTS, per_expert, 0)


TOTAL_BITS = tuple(range(11, 2, -1))


def _wait_rows(total, wait_piece):
    for lb in TOTAL_BITS:
        @pl.when((total & (1 << lb)) != 0)
        def _amount():
            wait_piece(1 << lb)


def _dispatch_kernel(last_ref, n_ref, src_ref, tot_ref, sidx_ref, h_ref, xs_out, packed, zblk, sem, zsem):
    tm = h_ref.shape[0]
    i = pl.program_id(0)

    @pl.when(pl.program_id(0) == 0)
    def _zero_tail_blocks():
        zblk[...] = jnp.zeros(zblk.shape, zblk.dtype)

        def zero_copy(e):
            return pltpu.make_async_copy(zblk, xs_out.at[pl.ds(last_ref[e] * MOE_BLOCK, MOE_BLOCK)], zsem)

        def start(e, carry):
            zero_copy(e).start()
            return carry

        def wait(e, carry):
            zero_copy(e).wait()
            return carry

        lax.fori_loop(0, N_EXPERTS, start, 0)
        lax.fori_loop(0, N_EXPERTS, wait, 0)

    rows = lax.broadcasted_iota(jnp.int32, (SORTED_ROWS, tm), 0).astype(jnp.int16)
    sidx16 = sidx_ref[...].astype(jnp.int16)
    pick = jnp.zeros((SORTED_ROWS, tm), BF16)
    for k in range(TOP_K):
        pick = jnp.where(rows == sidx16[k:k + 1, :], jnp.ones((), BF16), pick)
    sorted_rows = jnp.dot(pick, h_ref[...].astype(BF16), preferred_element_type=F32)
    lo = pltpu.bitcast(sorted_rows[:, :HALF], jnp.uint32)
    hi = pltpu.bitcast(sorted_rows[:, HALF:], jnp.uint32)
    packed[...] = jnp.bitwise_or(jnp.right_shift(lo, jnp.uint32(16)), hi)

    def piece(sorted_row, slot_row, rows_):
        return pltpu.make_async_copy(packed.at[pl.ds(sorted_row, rows_)], xs_out.at[pl.ds(slot_row, rows_)], sem)

    _for_each_run_piece(n_ref, src_ref, i, lambda a, b, r: piece(a, b, r).start())
    _wait_rows(tot_ref[i], lambda r: piece(0, 0, r).wait())


def _dispatch_call(geom, last_block, run_rows, run_slot, tile_rows, sidx, h2, n_blocks):
    tm = geom.c
    assert tm * TOP_K + N_EXPERTS * RUN_ALIGN == SORTED_ROWS
    grid_spec = pltpu.PrefetchScalarGridSpec(
        num_scalar_prefetch=4,
        grid=(geom.nt // tm,),
        in_specs=[pl.BlockSpec((TOP_K, tm), lambda i, *_: (0, i)),
                  pl.BlockSpec((tm, D_MODEL), lambda i, *_: (i, 0))],
        out_specs=pl.BlockSpec(memory_space=pl.ANY),
        scratch_shapes=[pltpu.VMEM((SORTED_ROWS, HALF), jnp.uint32), pltpu.VMEM((MOE_BLOCK, HALF), jnp.uint32),
                        pltpu.SemaphoreType.DMA(()), pltpu.SemaphoreType.DMA(())],
    )
    return pl.pallas_call(
        _dispatch_kernel,
        grid_spec=grid_spec,
        out_shape=jax.ShapeDtypeStruct(((n_blocks + 1) * MOE_BLOCK, HALF), jnp.uint32),
        compiler_params=_params(1, 48),
        name="moe_dispatch",
    )(last_block, run_rows, run_slot, tile_rows, sidx, h2)


def _expert_kernel(be_ref, nu_ref, x_ref, wg_ref, wu_ref, wd_ref, o_ref, wgu_s, wd_s):
    i = pl.program_id(0)
    live = i < nu_ref[0]
    changed = jnp.logical_or(i == 0, be_ref[i] != be_ref[jnp.maximum(i - 1, 0)])

    @pl.when(jnp.logical_and(live, changed))
    def _load_expert():
        wgu_s[:, :D_EXPERT] = wg_ref[0, 0].astype(BF16)
        wgu_s[:, D_EXPERT:] = wu_ref[0, 0].astype(BF16)
        wd_s[...] = wd_ref[0, 0].astype(BF16)

    @pl.when(live)
    def _run():
        lo, hi = _unpack_bf16_pairs(x_ref[...])
        x = jnp.concatenate([lo, hi], axis=1).astype(BF16)
        hgu = jnp.dot(x, wgu_s[...], preferred_element_type=F32)
        hid = _silu(hgu[:, :D_EXPERT]) * hgu[:, D_EXPERT:]
        o_ref[...] = _pack_bf16_pairs(jnp.dot(hid.astype(BF16), wd_s[...], preferred_element_type=F32))


def _expert_call(layer, block_e, n_used, xs, w_gate, w_up, w_down):
    n_blocks = xs.shape[0] // MOE_BLOCK - 1
    live = lambda i, be, nu: jnp.minimum(i, nu[0] - 1)
    expert = lambda i, be, nu: (layer, be[live(i, be, nu)], 0, 0)
    grid_spec = pltpu.PrefetchScalarGridSpec(
        num_scalar_prefetch=2,
        grid=(n_blocks,),
        in_specs=[pl.BlockSpec((MOE_BLOCK, HALF), lambda i, be, nu: (live(i, be, nu), 0)),
                  pl.BlockSpec((1, 1, D_MODEL, D_EXPERT), expert),
                  pl.BlockSpec((1, 1, D_MODEL, D_EXPERT), expert),
                  pl.BlockSpec((1, 1, D_EXPERT, D_MODEL), expert)],
        out_specs=pl.BlockSpec((MOE_BLOCK, HALF), lambda i, be, nu: (live(i, be, nu), 0)),
        scratch_shapes=[pltpu.VMEM((D_MODEL, 2 * D_EXPERT), BF16), pltpu.VMEM((D_EXPERT, D_MODEL), BF16)],
    )
    return pl.pallas_call(
        _expert_kernel,
        grid_spec=grid_spec,
        out_shape=jax.ShapeDtypeStruct(xs.shape, jnp.uint32),
        compiler_params=_params(1, 32),
        name="moe_experts",
    )(block_e, n_used, xs, w_gate, w_up, w_down)


def _combine_kernel(n_ref, src_ref, tot_ref, sidx_ref, wt_ref, ys_hbm, h_ref, x_ref, g2_ref, wgu_ref, wd_ref,
                    lng_ref, lnb_ref, o_ref, buf_a, buf_b, sem, *, alpha):
    i = pl.program_id(0)
    n = pl.num_programs(0)
    tm = h_ref.shape[0]
    even = i % 2 == 0

    def piece(buf, slot, sorted_row, slot_row, rows_):
        return pltpu.make_async_copy(ys_hbm.at[pl.ds(slot_row, rows_)], buf.at[pl.ds(sorted_row, rows_)], sem.at[slot])

    def gather(tile, buf, slot):
        _for_each_run_piece(n_ref, src_ref, tile, lambda a, b, r: piece(buf, slot, a, b, r).start())

    def drain(tile, buf, slot):
        _wait_rows(tot_ref[tile], lambda r: piece(buf, slot, 0, 0, r).wait())

    def finish(buf):
        wt = wt_ref[...].astype(BF16)
        sidx16 = sidx_ref[...].astype(jnp.int16)
        cols = lax.broadcasted_iota(jnp.int32, (tm, SORTED_ROWS), 1).astype(jnp.int16)
        mix = jnp.zeros((tm, SORTED_ROWS), BF16)
        for k in range(TOP_K):
            mix = jnp.where(cols == sidx16[:, k:k + 1], wt[:, k:k + 1], mix)
        lo, hi = _unpack_bf16_pairs(buf[...])
        routed = jnp.concatenate([jnp.dot(mix, lo.astype(BF16), preferred_element_type=F32),
                                  jnp.dot(mix, hi.astype(BF16), preferred_element_type=F32)], axis=1)
        hgu = jnp.dot(h_ref[...].astype(BF16), wgu_ref[...], preferred_element_type=F32)
        hid = _silu(hgu[:, :D_SHARED]) * hgu[:, D_SHARED:]
        shared = jnp.dot(hid.astype(BF16), wd_ref[...], preferred_element_type=F32)
        o_ref[...] = _layer_norm(alpha * x_ref[...] + g2_ref[0] * (routed + shared), lng_ref[...], lnb_ref[...])

    @pl.when(i == 0)
    def _first():
        buf_a[...] = jnp.zeros(buf_a.shape, buf_a.dtype)
        buf_b[...] = jnp.zeros(buf_b.shape, buf_b.dtype)
        gather(i, buf_a, 0)

    @pl.when(jnp.logical_and(even, i + 1 < n))
    def _ahead_b():
        gather(i + 1, buf_b, 1)

    @pl.when(jnp.logical_and(jnp.logical_not(even), i + 1 < n))
    def _ahead_a():
        gather(i + 1, buf_a, 0)

    @pl.when(even)
    def _finish_a():
        drain(i, buf_a, 0)
        finish(buf_a)

    @pl.when(jnp.logical_not(even))
    def _finish_b():
        drain(i, buf_b, 1)
        finish(buf_b)


def _combine_call(geom, alpha, run_rows, run_slot, tile_rows, sidx_tok, w_tok, ys, h2, x1, mods, w_sh_gu, w_sh_down, ln_g, ln_b):
    tm = geom.c
    d = D_MODEL
    n = geom.nt // tm
    tile = pl.BlockSpec((tm, d), lambda i, *_: (i, 0))
    vsp = pl.BlockSpec((1, d), lambda i, *_: (0, 0))
    per_tok = pl.BlockSpec((tm, TOP_K), lambda i, *_: (i, 0))
    grid_spec = pltpu.PrefetchScalarGridSpec(
        num_scalar_prefetch=3,
        grid=(n,),
        in_specs=[per_tok, per_tok,
                  pl.BlockSpec(memory_space=pl.ANY),
                  tile, tile, _mod_spec(geom, tm, 5),
                  pl.BlockSpec((d, 2 * D_SHARED), lambda i, *_: (0, 0)),
                  pl.BlockSpec((D_SHARED, d), lambda i, *_: (0, 0)),
                  vsp, vsp],
        out_specs=tile,
        scratch_shapes=[pltpu.VMEM((SORTED_ROWS, HALF), jnp.uint32), pltpu.VMEM((SORTED_ROWS, HALF), jnp.uint32),
                        pltpu.SemaphoreType.DMA((2,))],
    )
    return pl.pallas_call(
        functools.partial(_combine_kernel, alpha=alpha),
        grid_spec=grid_spec,
        out_shape=jax.ShapeDtypeStruct((geom.nt, d), F32),
        compiler_params=_params(1, 56),
        name="moe_combine_ln2",
    )(run_rows, run_slot, tile_rows, sidx_tok, w_tok, ys, h2, x1, mods, w_sh_gu, w_sh_down,
      ln_g.reshape(1, d), ln_b.reshape(1, d))


def _rope_tables(t):
    rows = t // GRID_W
    row = jnp.repeat(jnp.arange(rows, dtype=F32), GRID_W)
    col = jnp.tile(jnp.arange(GRID_W, dtype=F32), rows)
    n_freq = ATT_DH // 4
    inv_freq = ROPE_THETA ** (-jnp.arange(n_freq, dtype=F32) / n_freq)
    ang = jnp.concatenate([row[:, None] * inv_freq, col[:, None] * inv_freq], axis=-1)
    cos, sin = jnp.cos(ang), jnp.sin(ang)
    cos64 = jnp.concatenate([cos, cos], axis=-1)
    sin64 = jnp.concatenate([-sin, sin], axis=-1)
    return cos64, sin64


def kernel(x, c, ctx, c_ctx, w_ada, b_ada, w_in, ret_decay_logit, att_q_norm, att_k_norm, conv_dw, conv_db, conv_ln_g, conv_ln_b, w_ret_o, w_att_o, w_conv_o, w_out, ln1_g, ln1_b, w_router, router_bias, w_exp_gate, w_exp_up, w_exp_down, w_sh_gate, w_sh_up, w_sh_down, ln2_g, ln2_b):
    b, t, d = x.shape
    n_ctx = ctx.shape[1]
    depth = w_ada.shape[0]
    assert d == D_MODEL and w_in.shape[-1] == D_IN
    geom = _Geom(b, t, n_ctx)
    alpha = float((2 * depth) ** 0.25)

    cos64, sin64 = _rope_tables(t)
    cos128 = jnp.concatenate([cos64, cos64], axis=-1)
    sin128 = jnp.concatenate([sin64, sin64], axis=-1)

    n_rows = -(-(b + 1) // 8) * 8
    cvecs = jnp.zeros((n_rows, d), F32).at[:b].set(c).at[b].set(c_ctx)
    mods_all = _mods_call(cvecs, w_ada, b_ada).reshape(depth, n_rows * 6, 1, d)

    n_tiles = geom.nt // geom.c
    n_blocks = -(-(geom.nt * TOP_K + n_tiles * N_EXPERTS * (RUN_ALIGN - 1)) // MOE_BLOCK) + N_EXPERTS

    xt = jnp.concatenate([x.reshape(geom.nl, d), ctx.reshape(geom.nc, d)], axis=0)
    for l in range(depth):
        mods = mods_all[l]
        w_in_l = _permute_columns(w_in[l]).astype(BF16)
        z = _inproj_call(geom, xt, mods, w_in_l)

        log_gamma = jax.nn.log_sigmoid(ret_decay_logit[l].astype(F32))
        ret_l, ret_c = _retention_call(geom, z, log_gamma, cos128, sin128)
        att_l, att_c = _attention_call(geom, z, att_q_norm[l], att_k_norm[l], cos128, sin128)
        cv = _conv_call(geom, z, conv_dw[l], conv_db[l], conv_ln_g[l], conv_ln_b[l])
        x1, h2 = _mix_call(geom, alpha, ret_l, ret_c, att_l, att_c, cv, z, xt, mods,
                           w_ret_o[l].astype(BF16), w_att_o[l].astype(BF16), w_conv_o[l].astype(BF16),
                           w_out[l].astype(BF16), ln1_g[l], ln1_b[l])

        top_e, gate_w, pos, counts, cnt_hist = _router_call(geom, h2, w_router[l], router_bias[l])
        before = cnt_hist[:, :, 0].astype(jnp.int32)
        total = counts[:, 0].astype(jnp.int32)
        tile_n = jnp.concatenate([before[1:], total[None, :]], axis=0) - before
        run_rows = (tile_n + RUN_ALIGN - 1) // RUN_ALIGN * RUN_ALIGN
        run_before = jnp.cumsum(run_rows, axis=0) - run_rows
        blocks_e = (jnp.sum(run_rows, axis=0) + MOE_BLOCK - 1) // MOE_BLOCK
        blocks_end = jnp.cumsum(blocks_e)
        start_row = (blocks_end - blocks_e) * MOE_BLOCK
        run_slot = start_row[None, :] + run_before
        run_sorted = jnp.cumsum(run_rows, axis=1) - run_rows
        onehot = top_e[:, :, None] == jnp.arange(N_EXPERTS, dtype=jnp.int32)[None, None, :]
        per_token = lambda table: jnp.sum(jnp.where(onehot, jnp.repeat(table, geom.c, axis=0)[None], 0), axis=-1)
        sidx = per_token(run_sorted) + pos - per_token(before)
        block_ids = jnp.arange(n_blocks, dtype=jnp.int32)
        block_e = jnp.minimum(jnp.sum((blocks_end[None, :] <= block_ids[:, None]).astype(jnp.int32), axis=1),
                              N_EXPERTS - 1)
        n_used = blocks_end[-1:].astype(jnp.int32)
        last_block = jnp.where(blocks_e > 0, blocks_end - 1, n_blocks).astype(jnp.int32)
        run_rows_flat, run_slot_flat = run_rows.reshape(-1), run_slot.reshape(-1).astype(jnp.int32)
        tile_rows = jnp.sum(run_rows, axis=1)

        xs = _dispatch_call(geom, last_block, run_rows_flat, run_slot_flat, tile_rows, sidx, h2, n_blocks)
        ys = _expert_call(l, block_e, n_used, xs, w_exp_gate, w_exp_up, w_exp_down)
        w_sh_gu = jnp.concatenate([w_sh_gate[l], w_sh_up[l]], axis=-1).astype(BF16)
        xt = _combine_call(geom, alpha, run_rows_flat, run_slot_flat, tile_rows, sidx.T, gate_w.T, ys, h2, x1, mods, w_sh_gu,
                           w_sh_down[l].astype(BF16), ln2_g[l], ln2_b[l])
    return xt[:geom.nl].reshape(b, t, d)
```

```python
import functools

import jax
import jax.numpy as jnp
from jax import lax
from jax.experimental import pallas as pl
from jax.experimental.pallas import tpu as pltpu

F32 = jnp.float32
BF16 = jnp.bfloat16
HIGHEST = lax.Precision.HIGHEST

D_MODEL = 1024
GRID_W = 64
EPS = 1e-6

RET_HEADS = 8
RET_DK = 64
RET_DV = 128
RET_CHUNK = 256
RET_W = RET_HEADS * RET_DV

ATT_HEADS = 16
ATT_KV_HEADS = 4
ATT_DH = 64
ATT_GROUP = ATT_HEADS // ATT_KV_HEADS
ATT_W = ATT_HEADS * ATT_DH
ROPE_THETA = 10000.0
ATT_KEY_BLOCK = 2048
ATT_Q_UNROLL = 2

CONV_CH = 1024
CONV_K = 31
CONV_HALO = 16

N_EXPERTS = 64
TOP_K = 8
N_GROUPS = 8
TOPK_GROUPS = 4
D_EXPERT = 256
D_SHARED = 256
ROUTED_SCALE = 2.5
MOE_BLOCK = 1024

_ORIG = dict(rq=0, rk=512, rv=1024, rg=2048, aq=3072, ak=4096, av=4352, cu=4608, gt=6656)
D_IN = 9728
COL_CU = 0
COL_GT = 2048
COL_RG = 5120
COL_RV = 6144
COL_RQ = 7168
COL_RK = 7680
COL_ATT = 8192
ATT_SECTION = ATT_GROUP * ATT_DH + 2 * ATT_DH


def _column_ranges():
    rng = [(_ORIG["cu"], _ORIG["cu"] + 2 * CONV_CH),
           (_ORIG["gt"], _ORIG["gt"] + 3 * D_MODEL),
           (_ORIG["rg"], _ORIG["rg"] + RET_W),
           (_ORIG["rv"], _ORIG["rv"] + RET_W),
           (_ORIG["rq"], _ORIG["rq"] + RET_HEADS * RET_DK),
           (_ORIG["rk"], _ORIG["rk"] + RET_HEADS * RET_DK)]
    for g in range(ATT_KV_HEADS):
        rng.append((_ORIG["aq"] + g * ATT_GROUP * ATT_DH, _ORIG["aq"] + (g + 1) * ATT_GROUP * ATT_DH))
        rng.append((_ORIG["ak"] + g * ATT_DH, _ORIG["ak"] + (g + 1) * ATT_DH))
        rng.append((_ORIG["av"] + g * ATT_DH, _ORIG["av"] + (g + 1) * ATT_DH))
    cols = [c for a, b in rng for c in range(a, b)]
    assert sorted(cols) == list(range(D_IN))
    return rng


def _permute_columns(w):
    return jnp.concatenate([w[:, a:b] for a, b in _column_ranges()], axis=1)


def _params(n_axes, vmem_mib):
    return pltpu.CompilerParams(dimension_semantics=("arbitrary",) * n_axes,
                                vmem_limit_bytes=vmem_mib * 1024 * 1024)


def _silu(v):
    return v * jax.nn.sigmoid(v)


def _layer_norm(v, g, b):
    mu = jnp.mean(v, axis=-1, keepdims=True)
    d = v - mu
    var = jnp.mean(d * d, axis=-1, keepdims=True)
    return d * lax.rsqrt(var + EPS) * g + b


def _mods_kernel(c_ref, w_ref, b_ref, o_ref):
    s = _silu(c_ref[...])
    o_ref[0] = jnp.dot(s, w_ref[0], preferred_element_type=F32, precision=HIGHEST) + b_ref[0]


def _mods_call(cvecs, w_ada, b_ada):
    n_layers = w_ada.shape[0]
    rows, d = cvecs.shape
    return pl.pallas_call(
        _mods_kernel,
        grid=(n_layers, 6),
        in_specs=[pl.BlockSpec((rows, d), lambda l, j: (0, 0)),
                  pl.BlockSpec((1, d, d), lambda l, j: (l, 0, j)),
                  pl.BlockSpec((1, 1, d), lambda l, j: (l, 0, j))],
        out_specs=pl.BlockSpec((1, rows, d), lambda l, j: (l, 0, j)),
        out_shape=jax.ShapeDtypeStruct((n_layers, rows, 6 * d), F32),
        compiler_params=_params(2, 32),
        name="adaln_mods",
    )(cvecs, w_ada, b_ada.reshape(n_layers, 1, 6 * d))


class _Geom:
    def __init__(self, b, t, c):
        assert t % c == 0 and c % RET_CHUNK == 0 and c % CONV_HALO == 0 and t % ATT_KEY_BLOCK == 0
        self.b, self.t, self.c = b, t, c
        self.nl, self.nc = b * t, b * c
        self.nt = self.nl + self.nc
        self.lat_blocks = t // c
        self.nlb = self.nl // c
        self.p = t + c

    def row_block(self, bi, r):
        return jnp.where(r < self.lat_blocks, bi * self.lat_blocks + r, self.nlb + bi)

    def mod_row(self, i, tm):
        return jnp.where(i * tm < self.nl, (i * tm) // self.t, self.b)


def _mod_spec(geom, tm, which, grid_pos=0):
    d = D_MODEL
    if grid_pos == 0:
        return pl.BlockSpec((1, 1, d), lambda i, *_: (geom.mod_row(i, tm) * 6 + which, 0, 0))
    return pl.BlockSpec((1, 1, d), lambda j, i: (geom.mod_row(i, tm) * 6 + which, 0, 0))


def _inproj_kernel(x_ref, sh_ref, sc_ref, w_ref, o_ref):
    h = x_ref[...] * (1.0 + sc_ref[0]) + sh_ref[0]
    o_ref[...] = jnp.dot(h.astype(BF16), w_ref[...], preferred_element_type=F32).astype(o_ref.dtype)


def _inproj_call(geom, x, mods, w_in_bf16):
    tm = 512 if geom.nc % 512 == 0 and geom.t % 512 == 0 else geom.c
    tn = D_IN // 2
    return pl.pallas_call(
        _inproj_kernel,
        grid=(D_IN // tn, geom.nt // tm),
        in_specs=[pl.BlockSpec((tm, D_MODEL), lambda j, i: (i, 0)),
                  _mod_spec(geom, tm, 0, grid_pos=1),
                  _mod_spec(geom, tm, 1, grid_pos=1),
                  pl.BlockSpec((D_MODEL, tn), lambda j, i: (0, j))],
        out_specs=pl.BlockSpec((tm, tn), lambda j, i: (i, j)),
        out_shape=jax.ShapeDtypeStruct((geom.nt, D_IN), BF16),
        compiler_params=_params(2, 48),
        name="in_proj",
    )(x, mods, mods, w_in_bf16)


def _rot_half_128(v):
    lane = lax.broadcasted_iota(jnp.int32, v.shape, 1)
    return jnp.where((lane % 64) < 32, pltpu.roll(v, 96, 1), pltpu.roll(v, 32, 1))


def _ret_kernel(lg_ref, ql_ref, qc_ref, kl_ref, kc_ref, vl_ref, vc_ref, gl_ref, gc_ref, cos_ref, sin_ref,
                ol_ref, oc_ref, qs, kts, yf, yb, st, dm, qwb, kwb, gcs, *, t, c):
    ch = RET_CHUNK
    hp = pl.program_id(1)
    lat_blocks = t // c
    n_lat, n_ctx = t // ch, c // ch

    def scan():
        ri = lax.broadcasted_iota(jnp.int32, (ch, ch), 0).astype(F32)
        ci = lax.broadcasted_iota(jnp.int32, (ch, ch), 1).astype(F32)
        rv = lax.broadcasted_iota(jnp.int32, (ch, RET_DV), 0).astype(F32)
        for d in range(2):
            for h in range(2):
                u = 2 * d + h
                lg = lg_ref[d, 2 * hp + h]
                rel = (ri - ci) if d == 0 else (ci - ri)
                dm[u] = jnp.where(rel >= 0.0, jnp.exp(lg * jnp.maximum(rel, 0.0)), 0.0)
                qwb[u] = jnp.exp(lg * ((rv + 1.0) if d == 0 else (float(ch) - rv)))
                kwb[u] = jnp.exp(lg * ((float(ch) - 1.0 - rv) if d == 0 else rv))
                gcs[u] = jnp.exp(jnp.full((RET_DK, RET_DV), lg * float(ch), F32))
                st[u] = jnp.zeros((RET_DK, RET_DV), F32)

        def stage(q, k, seq_rows):
            qs[0, seq_rows, :] = q[:, :RET_DK].astype(BF16)
            qs[1, seq_rows, :] = q[:, RET_DK:].astype(BF16)
            kt = k.T
            kts[0, :, seq_rows] = kt[:RET_DK].astype(BF16)
            kts[1, :, seq_rows] = kt[RET_DK:].astype(BF16)

        kscale = RET_DK ** -0.5
        for cc in range(n_ctx):
            rows = pl.ds(cc * ch, ch)
            stage(qc_ref[rows, :].astype(F32), kc_ref[rows, :].astype(F32) * kscale, rows)

        def stage_lat(cc, carry):
            rows = pl.ds(pl.multiple_of(cc * ch, ch), ch)
            cs, sn = cos_ref[rows, :], sin_ref[rows, :]
            q = ql_ref[rows, :].astype(F32)
            k = kl_ref[rows, :].astype(F32)
            q = q * cs + _rot_half_128(q) * sn
            k = (k * cs + _rot_half_128(k) * sn) * kscale
            stage(q, k, pl.ds(pl.multiple_of(c + cc * ch, ch), ch))
            return carry

        lax.fori_loop(0, n_lat, stage_lat, 0)

        def run_segment(v_ref, seq_off, n):
            def body(i, carry):
                for d, cc in ((0, i), (1, n - 1 - i)):
                    vrows = pl.ds(pl.multiple_of(cc * ch, ch), ch)
                    srows = pl.ds(pl.multiple_of(seq_off + cc * ch, ch), ch)
                    for h in range(2):
                        u = 2 * d + h
                        q = qs[h, srows, :]
                        kt = kts[h, :, srows]
                        v = v_ref[vrows, h * RET_DV:(h + 1) * RET_DV].astype(F32)
                        s = jnp.dot(q, kt, preferred_element_type=F32)
                        y = jnp.dot((s * dm[u]).astype(BF16), v.astype(BF16), preferred_element_type=F32)
                        state = st[u]
                        y = y + jnp.dot(q, state.astype(BF16), preferred_element_type=F32) * qwb[u]
                        dst = yf if d == 0 else yb
                        dst[srows, h * RET_DV:(h + 1) * RET_DV] = y
                        kv = jnp.dot(kt, (v * kwb[u]).astype(BF16), preferred_element_type=F32)
                        st[u] = gcs[u] * state + kv
                return carry

            lax.fori_loop(0, n, body, 0, unroll=2)

        run_segment(vc_ref, 0, n_ctx)
        run_segment(vl_ref, c, n_lat)

    def finish(srows, g_ref, o_ref, rows):
        y = yf[srows, :] + yb[srows, :]
        for h in range(2):
            cols = slice(h * RET_DV, (h + 1) * RET_DV)
            yh = y[:, cols]
            mu = jnp.mean(yh, axis=-1, keepdims=True)
            dlt = yh - mu
            var = jnp.mean(dlt * dlt, axis=-1, keepdims=True)
            out = _silu(g_ref[rows, cols].astype(F32)) * (dlt * lax.rsqrt(var + EPS))
            o_ref[rows, cols] = out.astype(o_ref.dtype)

    scan()
    finish(pl.ds(0, c), gc_ref, oc_ref, pl.ds(0, c))

    def fin_lat(i, carry):
        rows = pl.ds(pl.multiple_of(i * c, c), c)
        finish(pl.ds(pl.multiple_of(c + i * c, c), c), gl_ref, ol_ref, rows)
        return carry

    lax.fori_loop(0, lat_blocks, fin_lat, 0)


def _retention_call(geom, z, log_gamma, cos128, sin128):
    t, c, p = geom.t, geom.c, geom.p
    hpairs = RET_HEADS // 2
    qb, kb = COL_RQ // 128, COL_RK // 128
    vb, gb = COL_RV // 256, COL_RG // 256
    in_specs = [
        pl.BlockSpec(memory_space=pltpu.SMEM),
        pl.BlockSpec((t, 128), lambda b, h: (b, qb + h)),
        pl.BlockSpec((c, 128), lambda b, h: (geom.nlb + b, qb + h)),
        pl.BlockSpec((t, 128), lambda b, h: (b, kb + h)),
        pl.BlockSpec((c, 128), lambda b, h: (geom.nlb + b, kb + h)),
        pl.BlockSpec((t, 256), lambda b, h: (b, vb + h)),
        pl.BlockSpec((c, 256), lambda b, h: (geom.nlb + b, vb + h)),
        pl.BlockSpec((t, 256), lambda b, h: (b, gb + h)),
        pl.BlockSpec((c, 256), lambda b, h: (geom.nlb + b, gb + h)),
        pl.BlockSpec((t, 128), lambda b, h: (0, 0)),
        pl.BlockSpec((t, 128), lambda b, h: (0, 0)),
    ]
    scratch = [
        pltpu.VMEM((2, p, RET_DK), BF16),
        pltpu.VMEM((2, RET_DK, p), BF16),
        pltpu.VMEM((p, 2 * RET_DV), F32),
        pltpu.VMEM((p, 2 * RET_DV), F32),
        pltpu.VMEM((4, RET_DK, RET_DV), F32),
        pltpu.VMEM((4, RET_CHUNK, RET_CHUNK), F32),
        pltpu.VMEM((4, RET_CHUNK, RET_DV), F32),
        pltpu.VMEM((4, RET_CHUNK, RET_DV), F32),
        pltpu.VMEM((4, RET_DK, RET_DV), F32),
    ]
    return pl.pallas_call(
        functools.partial(_ret_kernel, t=t, c=c),
        grid=(geom.b, hpairs),
        in_specs=in_specs,
        out_specs=[pl.BlockSpec((t, 256), lambda b, h: (b, h)), pl.BlockSpec((c, 256), lambda b, h: (b, h))],
        out_shape=[jax.ShapeDtypeStruct((geom.nl, RET_W), BF16),
                   jax.ShapeDtypeStruct((geom.nc, RET_W), BF16)],
        scratch_shapes=scratch,
        compiler_params=_params(2, 56),
        name="retention",
    )(log_gamma, z, z, z, z, z, z, z, z, cos128, sin128)


def _rms_heads_128(v, g):
    li = lax.broadcasted_iota(jnp.int32, (128, 128), 0) // ATT_DH
    lj = lax.broadcasted_iota(jnp.int32, (128, 128), 1) // ATT_DH
    avg = jnp.where(li == lj, 1.0 / ATT_DH, 0.0).astype(BF16)
    sq = v * v
    hi = sq.astype(BF16)
    lo = (sq - hi.astype(F32)).astype(BF16)
    ms = jnp.dot(hi, avg, preferred_element_type=F32) + jnp.dot(lo, avg, preferred_element_type=F32)
    return v * lax.rsqrt(ms + EPS) * g


def _att_kernel(qal_ref, qbl_ref, qac_ref, qbc_ref, kvl_ref, kvc_ref, qn_ref, kn_ref, cos_ref, sin_ref,
                ol_ref, oc_ref, kts, vs, m_all, acc_all, *, t, c):
    lat_blocks = t // c
    assert lat_blocks % ATT_Q_UNROLL == 0
    dh = ATT_DH
    tk = ATT_KEY_BLOCK
    lane = lax.broadcasted_iota(jnp.int32, (c, 2 * dh), 1)

    def stage_tile(kv, dst, cs, sn):
        k = _rms_heads_128(kv, kn_ref[...])
        if cs is not None:
            k = k * cs + _rot_half_128(k) * sn
        kts[:, dst] = k.T[:dh].astype(BF16)
        vs[dst, :] = jnp.where(lane < dh, pltpu.roll(kv, dh, 1), 1.0).astype(BF16)

    stage_tile(kvc_ref[...].astype(F32), pl.ds(0, c), None, None)

    def stage(i, carry):
        rows = pl.ds(pl.multiple_of(i * c, c), c)
        stage_tile(kvl_ref[rows, :].astype(F32), pl.ds(pl.multiple_of(c + i * c, c), c),
                   cos_ref[rows, :], sin_ref[rows, :])
        return carry

    lax.fori_loop(0, lat_blocks, stage, 0)

    def attend(q_tiles, rope, o_ref, rows, m_s, acc_s):
        q_heads = []
        for x in q_tiles:
            xn = _rms_heads_128(x, qn_ref[...])
            if rope is not None:
                xn = xn * rope[0] + _rot_half_128(xn) * rope[1]
            xr = xn * (dh ** -0.5)
            q_heads.append(xr[:, :dh].astype(BF16))
            q_heads.append(pltpu.roll(xr, dh, 1)[:, :dh].astype(BF16))
        q = jnp.concatenate(q_heads, axis=0)

        m_s[...] = jnp.full(m_s.shape, -jnp.inf, F32)
        acc_s[...] = jnp.zeros(acc_s.shape, F32)

        def flash_step(kt, v):
            n = kt.shape[1]
            s = jnp.dot(q, kt, preferred_element_type=F32)
            m_prev = m_s[...]
            m_next = jnp.maximum(m_prev, jnp.max(s, axis=1, keepdims=True))
            prob = jnp.exp(s - jnp.concatenate([m_next] * (n // 128), axis=1))
            acc_s[...] = (acc_s[...] * jnp.exp(m_prev - m_next)
                          + jnp.dot(prob.astype(BF16), v, preferred_element_type=F32))
            m_s[...] = m_next

        flash_step(kts[:, 0:c], vs[0:c, :])
        if rope is not None:
            for j in range(t // tk):
                flash_step(kts[:, c + j * tk:c + (j + 1) * tk], vs[c + j * tk:c + (j + 1) * tk, :])

        outs = []
        for h in range(ATT_GROUP):
            acc = acc_s[h * c:(h + 1) * c, :]
            outs.append(acc * pltpu.roll(1.0 / acc, dh, 1))
        for pair in range(ATT_GROUP // 2):
            both = jnp.where(lane < dh, outs[2 * pair], pltpu.roll(outs[2 * pair + 1], dh, 1))
            o_ref[rows, pair * 2 * dh:(pair + 1) * 2 * dh] = both.astype(o_ref.dtype)

    attend((qac_ref[...].astype(F32), qbc_ref[...].astype(F32)), None, oc_ref, pl.ds(0, c), m_all.at[0], acc_all.at[0])

    def lat_pair(i, carry):
        for half in range(ATT_Q_UNROLL):
            rows = pl.ds(pl.multiple_of((ATT_Q_UNROLL * i + half) * c, c), c)
            attend((qal_ref[rows, :].astype(F32), qbl_ref[rows, :].astype(F32)),
                   (cos_ref[rows, :], sin_ref[rows, :]), ol_ref, rows, m_all.at[half], acc_all.at[half])
        return carry

    lax.fori_loop(0, lat_blocks // ATT_Q_UNROLL, lat_pair, 0)


def _attention_call(geom, z, q_norm, k_norm, cos128, sin128):
    t, c, p = geom.t, geom.c, geom.p
    ab = COL_ATT // 128
    sec = ATT_SECTION // 128
    lat = lambda col: pl.BlockSpec((t, 128), lambda b, g: (b, ab + sec * g + col))
    ctx = lambda col: pl.BlockSpec((c, 128), lambda b, g: (geom.nlb + b, ab + sec * g + col))
    cst = lambda rows: pl.BlockSpec((rows, 128), lambda b, g: (0, 0))
    in_specs = [lat(0), lat(1), ctx(0), ctx(1), lat(2), ctx(2), cst(1), cst(1), cst(t), cst(t)]
    two_heads = lambda v: jnp.tile(v.reshape(1, ATT_DH), (1, 2))
    scratch = [
        pltpu.VMEM((ATT_DH, p), BF16),
        pltpu.VMEM((p, 2 * ATT_DH), BF16),
        pltpu.VMEM((ATT_Q_UNROLL, ATT_GROUP * c, 128), F32),
        pltpu.VMEM((ATT_Q_UNROLL, ATT_GROUP * c, 2 * ATT_DH), F32),
    ]
    width = ATT_GROUP * ATT_DH
    return pl.pallas_call(
        functools.partial(_att_kernel, t=t, c=c),
        grid=(geom.b, ATT_KV_HEADS),
        in_specs=in_specs,
        out_specs=[pl.BlockSpec((t, width), lambda b, g: (b, g)), pl.BlockSpec((c, width), lambda b, g: (b, g))],
        out_shape=[jax.ShapeDtypeStruct((geom.nl, ATT_W), BF16),
                   jax.ShapeDtypeStruct((geom.nc, ATT_W), BF16)],
        scratch_shapes=scratch,
        compiler_params=_params(2, 48),
        name="attention",
    )(z, z, z, z, z, z, two_heads(q_norm), two_heads(k_norm), cos128, sin128)


def _conv_kernel(a_ref, g_ref, ap_ref, gp_ref, an_ref, gn_ref, w_ref, b_ref, lng_ref, lnb_ref, o_ref,
                 ext, ys, shifted, *, t, c):
    r = pl.program_id(1)
    lat_blocks = t // c
    halo = CONV_HALO
    has_prev = jnp.logical_and(r != 0, r != lat_blocks)
    has_next = jnp.logical_and(r != lat_blocks - 1, r != lat_blocks)
    glu = lambda a, g: a[...].astype(F32) * jax.nn.sigmoid(g[...].astype(F32))
    ext[halo:halo + c, :] = glu(a_ref, g_ref)
    ext[0:halo, :] = jnp.where(has_prev, glu(ap_ref, gp_ref), 0.0)
    ext[halo + c:, :] = jnp.where(has_next, glu(an_ref, gn_ref), 0.0)

    rt = 64
    first = halo - CONV_K // 2
    span = c + 2 * halo - 8
    for s in range(1, 8):
        shifted[s - 1, 0:span, :] = ext[s:s + span, :]

    def lane_block(cb, carry):
        lanes = pl.ds(pl.multiple_of(cb * 128, 128), 128)
        for ti in range(c // rt):
            acc = jnp.zeros((rt, 128), F32)
            for j in range(CONV_K):
                row, s = divmod(ti * rt + first + j, 8)
                src = ext if s == 0 else shifted.at[s - 1]
                acc = acc + w_ref[pl.ds(j, 1), lanes] * src[pl.ds(row * 8, rt), lanes]
            ys[pl.ds(ti * rt, rt), lanes] = acc
        return carry

    lax.fori_loop(0, CONV_CH // 128, lane_block, 0)
    y = ys[...] + b_ref[...]
    o_ref[...] = _silu(_layer_norm(y, lng_ref[...], lnb_ref[...])).astype(o_ref.dtype)


def _conv_call(geom, z, conv_dw, conv_db, ln_g, ln_b):
    t, c = geom.t, geom.c
    rb = geom.row_block
    hb = c // CONV_HALO
    last = geom.nt // CONV_HALO - 1
    prev = lambda b, r: jnp.maximum(rb(b, r) * hb - 1, 0)
    nxt = lambda b, r: jnp.minimum((rb(b, r) + 1) * hb, last)
    w = jnp.zeros((32, CONV_CH), F32).at[:CONV_K].set(conv_dw)
    vec = lambda v: v.reshape(1, CONV_CH)
    cst = pl.BlockSpec((1, CONV_CH), lambda b, r: (0, 0))
    in_specs = [
        pl.BlockSpec((c, CONV_CH), lambda b, r: (rb(b, r), 0)),
        pl.BlockSpec((c, CONV_CH), lambda b, r: (rb(b, r), 1)),
        pl.BlockSpec((CONV_HALO, CONV_CH), lambda b, r: (prev(b, r), 0)),
        pl.BlockSpec((CONV_HALO, CONV_CH), lambda b, r: (prev(b, r), 1)),
        pl.BlockSpec((CONV_HALO, CONV_CH), lambda b, r: (nxt(b, r), 0)),
        pl.BlockSpec((CONV_HALO, CONV_CH), lambda b, r: (nxt(b, r), 1)),
        pl.BlockSpec((32, CONV_CH), lambda b, r: (0, 0)),
        cst, cst, cst,
    ]
    return pl.pallas_call(
        functools.partial(_conv_kernel, t=t, c=c),
        grid=(geom.b, geom.lat_blocks + 1),
        in_specs=in_specs,
        out_specs=pl.BlockSpec((c, CONV_CH), lambda b, r: (rb(b, r), 0)),
        out_shape=jax.ShapeDtypeStruct((geom.nt, CONV_CH), BF16),
        scratch_shapes=[pltpu.VMEM((c + 2 * CONV_HALO, CONV_CH), F32), pltpu.VMEM((c, CONV_CH), F32),
                        pltpu.VMEM((7, c + 2 * CONV_HALO, CONV_CH), F32)],
        compiler_params=_params(2, 32),
        name="conformer_conv",
    )(z, z, z, z, z, z, w, vec(conv_db), vec(ln_g), vec(ln_b))


def _mix_kernel(retl_ref, retc_ref, attl_ref, attc_ref, cv_ref, gr_ref, ga_ref, gc_ref, x_ref, g1_ref, sh2_ref, sc2_ref,
                wr_ref, wa_ref, wc_ref, wo_ref, lng_ref, lnb_ref, x1_ref, h2_ref, *, alpha, n_lat_tiles):
    def proj(v, w_ref):
        return jnp.dot(v.astype(BF16), w_ref[...], preferred_element_type=F32)

    gate = lambda g_ref: jax.nn.sigmoid(g_ref[...].astype(F32))
    is_lat = pl.program_id(0) < n_lat_tiles
    ret = jnp.where(is_lat, retl_ref[...], retc_ref[...])
    att = jnp.where(is_lat, attl_ref[...], attc_ref[...])
    merged = (gate(gr_ref) * proj(ret, wr_ref)
              + gate(ga_ref) * proj(att, wa_ref)
              + gate(gc_ref) * proj(cv_ref[...], wc_ref))
    y = jnp.dot(merged.astype(BF16), wo_ref[...], preferred_element_type=F32)
    x1 = _layer_norm(alpha * x_ref[...] + g1_ref[0] * y, lng_ref[...], lnb_ref[...])
    x1_ref[...] = x1
    h2_ref[...] = x1 * (1.0 + sc2_ref[0]) + sh2_ref[0]


def _mix_call(geom, alpha, ret_l, ret_c, att_l, att_c, cv, z, x, mods, w_ret_o, w_att_o, w_conv_o, w_out, ln_g, ln_b):
    tm = 512 if geom.nc % 512 == 0 and geom.t % 512 == 0 else geom.c
    d = D_MODEL
    n_lat = geom.nl // tm
    tile = pl.BlockSpec((tm, d), lambda i: (i, 0))
    lat_tile = pl.BlockSpec((tm, d), lambda i: (jnp.minimum(i, n_lat - 1), 0))
    ctx_tile = pl.BlockSpec((tm, d), lambda i: (jnp.maximum(i - n_lat, 0), 0))
    gate = lambda k: pl.BlockSpec((tm, d), lambda i: (i, COL_GT // d + k))
    wsp = pl.BlockSpec((d, d), lambda i: (0, 0))
    vsp = pl.BlockSpec((1, d), lambda i: (0, 0))
    return pl.pallas_call(
        functools.partial(_mix_kernel, alpha=alpha, n_lat_tiles=n_lat),
        grid=(geom.nt // tm,),
        in_specs=[lat_tile, ctx_tile, lat_tile, ctx_tile, tile, gate(0), gate(1), gate(2), tile,
                  _mod_spec(geom, tm, 2), _mod_spec(geom, tm, 3), _mod_spec(geom, tm, 4),
                  wsp, wsp, wsp, wsp, vsp, vsp],
        out_specs=[tile, tile],
        out_shape=[jax.ShapeDtypeStruct((geom.nt, d), F32)] * 2,
        compiler_params=_params(1, 56),
        name="merge_ln1",
    )(ret_l, ret_c, att_l, att_c, cv, z, z, z, x, mods, mods, mods, w_ret_o, w_att_o, w_conv_o, w_out,
      ln_g.reshape(1, d), ln_b.reshape(1, d))


def _router_kernel(h_ref, wr_ref, bias_ref, e_ref, w_ref, pos_ref, cnt_ref, hist_ref, cnt):
    i = pl.program_id(0)
    tm = h_ref.shape[0]
    ne, per = N_EXPERTS, N_EXPERTS // N_GROUPS
    neg = -jnp.inf

    @pl.when(i == 0)
    def _init():
        cnt[...] = jnp.zeros(cnt.shape, F32)

    logits = jnp.dot(h_ref[...], wr_ref[...], preferred_element_type=F32, precision=HIGHEST)
    scores = jax.nn.sigmoid(logits.T[:ne])
    sel = scores + bias_ref[...]

    member = lax.broadcasted_iota(jnp.int32, (per, tm), 0)
    grp_rows = []
    for g in range(N_GROUPS):
        blk = sel[g * per:(g + 1) * per]
        m1 = jnp.max(blk, axis=0, keepdims=True)
        first = jnp.min(jnp.where(blk == m1, member, per), axis=0, keepdims=True)
        m2 = jnp.max(jnp.where(member == first, neg, blk), axis=0, keepdims=True)
        grp_rows.append(m1 + m2)
    gs = jnp.concatenate(grp_rows, axis=0)

    gidx = lax.broadcasted_iota(jnp.int32, (N_GROUPS, tm), 0)
    rank = jnp.zeros((N_GROUPS, tm), jnp.int32)
    for g in range(N_GROUPS):
        row = gs[g:g + 1]
        ahead = jnp.logical_or(row > gs, jnp.logical_and(row == gs, g < gidx))
        rank = rank + ahead.astype(jnp.int32)
    keep = (rank < TOPK_GROUPS).astype(F32)
    keep_e = jnp.concatenate([jnp.broadcast_to(keep[g:g + 1], (per, tm)) for g in range(N_GROUPS)], axis=0)
    cand = jnp.where(keep_e > 0.5, sel, neg)

    eidx = lax.broadcasted_iota(jnp.int32, (ne, tm), 0)
    picks, gates, hots = [], [], []
    chosen = jnp.zeros((ne, tm), F32)
    for _ in range(TOP_K):
        m = jnp.max(cand, axis=0, keepdims=True)
        idx = jnp.min(jnp.where(cand == m, eidx, ne), axis=0, keepdims=True)
        hot = eidx == idx
        picks.append(idx)
        gates.append(jnp.sum(jnp.where(hot, scores, 0.0), axis=0, keepdims=True))
        hots.append(hot)
        chosen = jnp.where(hot, 1.0, chosen)
        cand = jnp.where(hot, neg, cand)
    total = gates[0]
    for gk in gates[1:]:
        total = total + gk

    ti = lax.broadcasted_iota(jnp.int32, (tm, tm), 0)
    tj = lax.broadcasted_iota(jnp.int32, (tm, tm), 1)
    before = jnp.where(ti < tj, 1.0, 0.0).astype(BF16)
    prior = jnp.dot(chosen.astype(BF16), before, preferred_element_type=F32) + cnt[...][:, :1]
    pos = [jnp.sum(jnp.where(hot, prior, 0.0), axis=0, keepdims=True) for hot in hots]

    e_ref[...] = jnp.concatenate(picks, axis=0)
    w_ref[...] = jnp.concatenate([ROUTED_SCALE * gk / total for gk in gates], axis=0)
    pos_ref[...] = jnp.concatenate(pos, axis=0).astype(jnp.int32)
    hist_ref[0] = cnt[...]
    cnt[...] = cnt[...] + jnp.sum(chosen, axis=1, keepdims=True)
    cnt_ref[...] = cnt[...]


def _router_call(geom, h2, w_router, router_bias):
    tm = geom.c
    wr = jnp.zeros((D_MODEL, 128), F32).at[:, :N_EXPERTS].set(w_router)
    tok = pl.BlockSpec((TOP_K, tm), lambda i: (0, i))
    return pl.pallas_call(
        _router_kernel,
        grid=(geom.nt // tm,),
        in_specs=[pl.BlockSpec((tm, D_MODEL), lambda i: (i, 0)),
                  pl.BlockSpec((D_MODEL, 128), lambda i: (0, 0)),
                  pl.BlockSpec((N_EXPERTS, 1), lambda i: (0, 0))],
        out_specs=[tok, tok, tok, pl.BlockSpec((N_EXPERTS, 128), lambda i: (0, 0)),
                   pl.BlockSpec((1, N_EXPERTS, 128), lambda i: (i, 0, 0))],
        out_shape=[jax.ShapeDtypeStruct((TOP_K, geom.nt), jnp.int32),
                   jax.ShapeDtypeStruct((TOP_K, geom.nt), F32),
                   jax.ShapeDtypeStruct((TOP_K, geom.nt), jnp.int32),
                   jax.ShapeDtypeStruct((N_EXPERTS, 128), F32),
                   jax.ShapeDtypeStruct((geom.nt // tm, N_EXPERTS, 128), F32)],
        scratch_shapes=[pltpu.VMEM((N_EXPERTS, 128), F32)],
        compiler_params=_params(1, 32),
        name="moe_router",
    )(h2, wr, router_bias.reshape(N_EXPERTS, 1))


HALF = D_MODEL // 2


def _pack_bf16_pairs(v):
    lo = pltpu.bitcast(v[:, :HALF].astype(BF16).astype(F32), jnp.uint32)
    hi = pltpu.bitcast(v[:, HALF:].astype(BF16).astype(F32), jnp.uint32)
    return jnp.bitwise_or(jnp.right_shift(lo, jnp.uint32(16)), hi)


def _unpack_bf16_pairs(w):
    lo = pltpu.bitcast(jnp.left_shift(w, jnp.uint32(16)), F32)
    hi = pltpu.bitcast(jnp.bitwise_and(w, jnp.uint32(0xFFFF0000)), F32)
    return lo, hi


RUN_ALIGN = 8
SORTED_ROWS = 256 * TOP_K + N_EXPERTS * RUN_ALIGN
RUN_BITS = tuple(range(8, 2, -1))


def _for_each_run_piece(n_ref, src_ref, tile, visit):
    def per_expert(e, off):
        n = n_ref[tile * N_EXPERTS + e]
        src = src_ref[tile * N_EXPERTS + e]

        for lb in RUN_BITS:
            done = (n >> (lb + 1)) << (lb + 1)

            @pl.when((n & (1 << lb)) != 0)
            def _piece():
                visit(pl.multiple_of(off + done, RUN_ALIGN), pl.multiple_of(src + done, RUN_ALIGN), 1 << lb)

        return off + n

    lax.fori_loop(0, N_EXPERTS, per_expert, 0)


TOTAL_BITS = tuple(range(11, 2, -1))


def _wait_rows(total, wait_piece):
    for lb in TOTAL_BITS:
        @pl.when((total & (1 << lb)) != 0)
        def _amount():
            wait_piece(1 << lb)


def _dispatch_kernel(last_ref, n_ref, src_ref, tot_ref, sidx_ref, h_ref, xs_out, packed, zblk, sem, zsem):
    tm = h_ref.shape[0]
    i = pl.program_id(0)

    @pl.when(pl.program_id(0) == 0)
    def _zero_tail_blocks():
        zblk[...] = jnp.zeros(zblk.shape, zblk.dtype)

        def zero_copy(e):
            return pltpu.make_async_copy(zblk, xs_out.at[pl.ds(last_ref[e] * MOE_BLOCK, MOE_BLOCK)], zsem)

        def start(e, carry):
            zero_copy(e).start()
            return carry

        def wait(e, carry):
            zero_copy(e).wait()
            return carry

        lax.fori_loop(0, N_EXPERTS, start, 0)
        lax.fori_loop(0, N_EXPERTS, wait, 0)

    rows = lax.broadcasted_iota(jnp.int32, (SORTED_ROWS, tm), 0).astype(jnp.int16)
    sidx16 = sidx_ref[...].astype(jnp.int16)
    pick = jnp.zeros((SORTED_ROWS, tm), BF16)
    for k in range(TOP_K):
        pick = jnp.where(rows == sidx16[k:k + 1, :], jnp.ones((), BF16), pick)
    sorted_rows = jnp.dot(pick, h_ref[...].astype(BF16), preferred_element_type=F32)
    lo = pltpu.bitcast(sorted_rows[:, :HALF], jnp.uint32)
    hi = pltpu.bitcast(sorted_rows[:, HALF:], jnp.uint32)
    packed[...] = jnp.bitwise_or(jnp.right_shift(lo, jnp.uint32(16)), hi)

    def piece(sorted_row, slot_row, rows_):
        return pltpu.make_async_copy(packed.at[pl.ds(sorted_row, rows_)], xs_out.at[pl.ds(slot_row, rows_)], sem)

    _for_each_run_piece(n_ref, src_ref, i, lambda a, b, r: piece(a, b, r).start())
    _wait_rows(tot_ref[i], lambda r: piece(0, 0, r).wait())


def _dispatch_call(geom, last_block, run_rows, run_slot, tile_rows, sidx, h2, n_blocks):
    tm = geom.c
    assert tm * TOP_K + N_EXPERTS * RUN_ALIGN == SORTED_ROWS
    grid_spec = pltpu.PrefetchScalarGridSpec(
        num_scalar_prefetch=4,
        grid=(geom.nt // tm,),
        in_specs=[pl.BlockSpec((TOP_K, tm), lambda i, *_: (0, i)),
                  pl.BlockSpec((tm, D_MODEL), lambda i, *_: (i, 0))],
        out_specs=pl.BlockSpec(memory_space=pl.ANY),
        scratch_shapes=[pltpu.VMEM((SORTED_ROWS, HALF), jnp.uint32), pltpu.VMEM((MOE_BLOCK, HALF), jnp.uint32),
                        pltpu.SemaphoreType.DMA(()), pltpu.SemaphoreType.DMA(())],
    )
    return pl.pallas_call(
        _dispatch_kernel,
        grid_spec=grid_spec,
        out_shape=jax.ShapeDtypeStruct(((n_blocks + 1) * MOE_BLOCK, HALF), jnp.uint32),
        compiler_params=_params(1, 48),
        name="moe_dispatch",
    )(last_block, run_rows, run_slot, tile_rows, sidx, h2)


def _expert_kernel(be_ref, nu_ref, x_ref, wg_ref, wu_ref, wd_ref, o_ref, wgu_s, wd_s):
    i = pl.program_id(0)
    live = i < nu_ref[0]
    changed = jnp.logical_or(i == 0, be_ref[i] != be_ref[jnp.maximum(i - 1, 0)])

    @pl.when(jnp.logical_and(live, changed))
    def _load_expert():
        wgu_s[:, :D_EXPERT] = wg_ref[0, 0].astype(BF16)
        wgu_s[:, D_EXPERT:] = wu_ref[0, 0].astype(BF16)
        wd_s[...] = wd_ref[0, 0].astype(BF16)

    @pl.when(live)
    def _run():
        lo, hi = _unpack_bf16_pairs(x_ref[...])
        x = jnp.concatenate([lo, hi], axis=1).astype(BF16)
        hgu = jnp.dot(x, wgu_s[...], preferred_element_type=F32)
        hid = _silu(hgu[:, :D_EXPERT]) * hgu[:, D_EXPERT:]
        o_ref[...] = _pack_bf16_pairs(jnp.dot(hid.astype(BF16), wd_s[...], preferred_element_type=F32))


def _expert_call(layer, block_e, n_used, xs, w_gate, w_up, w_down):
    n_blocks = xs.shape[0] // MOE_BLOCK - 1
    live = lambda i, be, nu: jnp.minimum(i, nu[0] - 1)
    expert = lambda i, be, nu: (layer, be[live(i, be, nu)], 0, 0)
    grid_spec = pltpu.PrefetchScalarGridSpec(
        num_scalar_prefetch=2,
        grid=(n_blocks,),
        in_specs=[pl.BlockSpec((MOE_BLOCK, HALF), lambda i, be, nu: (live(i, be, nu), 0)),
                  pl.BlockSpec((1, 1, D_MODEL, D_EXPERT), expert),
                  pl.BlockSpec((1, 1, D_MODEL, D_EXPERT), expert),
                  pl.BlockSpec((1, 1, D_EXPERT, D_MODEL), expert)],
        out_specs=pl.BlockSpec((MOE_BLOCK, HALF), lambda i, be, nu: (live(i, be, nu), 0)),
        scratch_shapes=[pltpu.VMEM((D_MODEL, 2 * D_EXPERT), BF16), pltpu.VMEM((D_EXPERT, D_MODEL), BF16)],
    )
    return pl.pallas_call(
        _expert_kernel,
        grid_spec=grid_spec,
        out_shape=jax.ShapeDtypeStruct(xs.shape, jnp.uint32),
        compiler_params=_params(1, 32),
        name="moe_experts",
    )(block_e, n_used, xs, w_gate, w_up, w_down)


def _combine_kernel(n_ref, src_ref, tot_ref, sidx_ref, wt_ref, ys_hbm, h_ref, x_ref, g2_ref, wgu_ref, wd_ref,
                    lng_ref, lnb_ref, o_ref, buf_a, buf_b, sem, *, alpha):
    i = pl.program_id(0)
    n = pl.num_programs(0)
    tm = h_ref.shape[0]
    even = i % 2 == 0

    def piece(buf, slot, sorted_row, slot_row, rows_):
        return pltpu.make_async_copy(ys_hbm.at[pl.ds(slot_row, rows_)], buf.at[pl.ds(sorted_row, rows_)], sem.at[slot])

    def gather(tile, buf, slot):
        _for_each_run_piece(n_ref, src_ref, tile, lambda a, b, r: piece(buf, slot, a, b, r).start())

    def drain(tile, buf, slot):
        _wait_rows(tot_ref[tile], lambda r: piece(buf, slot, 0, 0, r).wait())

    def finish(buf):
        wt = wt_ref[...].astype(BF16)
        sidx16 = sidx_ref[...].astype(jnp.int16)
        cols = lax.broadcasted_iota(jnp.int32, (tm, SORTED_ROWS), 1).astype(jnp.int16)
        mix = jnp.zeros((tm, SORTED_ROWS), BF16)
        for k in range(TOP_K):
            mix = jnp.where(cols == sidx16[:, k:k + 1], wt[:, k:k + 1], mix)
        lo, hi = _unpack_bf16_pairs(buf[...])
        routed = jnp.concatenate([jnp.dot(mix, lo.astype(BF16), preferred_element_type=F32),
                                  jnp.dot(mix, hi.astype(BF16), preferred_element_type=F32)], axis=1)
        hgu = jnp.dot(h_ref[...].astype(BF16), wgu_ref[...], preferred_element_type=F32)
        hid = _silu(hgu[:, :D_SHARED]) * hgu[:, D_SHARED:]
        shared = jnp.dot(hid.astype(BF16), wd_ref[...], preferred_element_type=F32)
        o_ref[...] = _layer_norm(alpha * x_ref[...] + g2_ref[0] * (routed + shared), lng_ref[...], lnb_ref[...])

    @pl.when(i == 0)
    def _first():
        buf_a[...] = jnp.zeros(buf_a.shape, buf_a.dtype)
        buf_b[...] = jnp.zeros(buf_b.shape, buf_b.dtype)
        gather(i, buf_a, 0)

    @pl.when(jnp.logical_and(even, i + 1 < n))
    def _ahead_b():
        gather(i + 1, buf_b, 1)

    @pl.when(jnp.logical_and(jnp.logical_not(even), i + 1 < n))
    def _ahead_a():
        gather(i + 1, buf_a, 0)

    @pl.when(even)
    def _finish_a():
        drain(i, buf_a, 0)
        finish(buf_a)

    @pl.when(jnp.logical_not(even))
    def _finish_b():
        drain(i, buf_b, 1)
        finish(buf_b)


def _combine_call(geom, alpha, run_rows, run_slot, tile_rows, sidx_tok, w_tok, ys, h2, x1, mods, w_sh_gu, w_sh_down, ln_g, ln_b):
    tm = geom.c
    d = D_MODEL
    n = geom.nt // tm
    tile = pl.BlockSpec((tm, d), lambda i, *_: (i, 0))
    vsp = pl.BlockSpec((1, d), lambda i, *_: (0, 0))
    per_tok = pl.BlockSpec((tm, TOP_K), lambda i, *_: (i, 0))
    grid_spec = pltpu.PrefetchScalarGridSpec(
        num_scalar_prefetch=3,
        grid=(n,),
        in_specs=[per_tok, per_tok,
                  pl.BlockSpec(memory_space=pl.ANY),
                  tile, tile, _mod_spec(geom, tm, 5),
                  pl.BlockSpec((d, 2 * D_SHARED), lambda i, *_: (0, 0)),
                  pl.BlockSpec((D_SHARED, d), lambda i, *_: (0, 0)),
                  vsp, vsp],
        out_specs=tile,
        scratch_shapes=[pltpu.VMEM((SORTED_ROWS, HALF), jnp.uint32), pltpu.VMEM((SORTED_ROWS, HALF), jnp.uint32),
                        pltpu.SemaphoreType.DMA((2,))],
    )
    return pl.pallas_call(
        functools.partial(_combine_kernel, alpha=alpha),
        grid_spec=grid_spec,
        out_shape=jax.ShapeDtypeStruct((geom.nt, d), F32),
        compiler_params=_params(1, 56),
        name="moe_combine_ln2",
    )(run_rows, run_slot, tile_rows, sidx_tok, w_tok, ys, h2, x1, mods, w_sh_gu, w_sh_down,
      ln_g.reshape(1, d), ln_b.reshape(1, d))


def _rope_tables(t):
    rows = t // GRID_W
    row = jnp.repeat(jnp.arange(rows, dtype=F32), GRID_W)
    col = jnp.tile(jnp.arange(GRID_W, dtype=F32), rows)
    n_freq = ATT_DH // 4
    inv_freq = ROPE_THETA ** (-jnp.arange(n_freq, dtype=F32) / n_freq)
    ang = jnp.concatenate([row[:, None] * inv_freq, col[:, None] * inv_freq], axis=-1)
    cos, sin = jnp.cos(ang), jnp.sin(ang)
    cos64 = jnp.concatenate([cos, cos], axis=-1)
    sin64 = jnp.concatenate([-sin, sin], axis=-1)
    return cos64, sin64


def kernel(x, c, ctx, c_ctx, w_ada, b_ada, w_in, ret_decay_logit, att_q_norm, att_k_norm, conv_dw, conv_db, conv_ln_g, conv_ln_b, w_ret_o, w_att_o, w_conv_o, w_out, ln1_g, ln1_b, w_router, router_bias, w_exp_gate, w_exp_up, w_exp_down, w_sh_gate, w_sh_up, w_sh_down, ln2_g, ln2_b):
    b, t, d = x.shape
    n_ctx = ctx.shape[1]
    depth = w_ada.shape[0]
    assert d == D_MODEL and w_in.shape[-1] == D_IN
    geom = _Geom(b, t, n_ctx)
    alpha = float((2 * depth) ** 0.25)

    cos64, sin64 = _rope_tables(t)
    cos128 = jnp.concatenate([cos64, cos64], axis=-1)
    sin128 = jnp.concatenate([sin64, sin64], axis=-1)

    n_rows = -(-(b + 1) // 8) * 8
    cvecs = jnp.zeros((n_rows, d), F32).at[:b].set(c).at[b].set(c_ctx)
    mods_all = _mods_call(cvecs, w_ada, b_ada).reshape(depth, n_rows * 6, 1, d)

    n_tiles = geom.nt // geom.c
    n_blocks = -(-(geom.nt * TOP_K + n_tiles * N_EXPERTS * (RUN_ALIGN - 1)) // MOE_BLOCK) + N_EXPERTS

    xt = jnp.concatenate([x.reshape(geom.nl, d), ctx.reshape(geom.nc, d)], axis=0)
    for l in range(depth):
        mods = mods_all[l]
        w_in_l = _permute_columns(w_in[l]).astype(BF16)
        z = _inproj_call(geom, xt, mods, w_in_l)

        log_gamma = jax.nn.log_sigmoid(ret_decay_logit[l].astype(F32))
        ret_l, ret_c = _retention_call(geom, z, log_gamma, cos128, sin128)
        att_l, att_c = _attention_call(geom, z, att_q_norm[l], att_k_norm[l], cos128, sin128)
        cv = _conv_call(geom, z, conv_dw[l], conv_db[l], conv_ln_g[l], conv_ln_b[l])
        x1, h2 = _mix_call(geom, alpha, ret_l, ret_c, att_l, att_c, cv, z, xt, mods,
                           w_ret_o[l].astype(BF16), w_att_o[l].astype(BF16), w_conv_o[l].astype(BF16),
                           w_out[l].astype(BF16), ln1_g[l], ln1_b[l])

        top_e, gate_w, pos, counts, cnt_hist = _router_call(geom, h2, w_router[l], router_bias[l])
        before = cnt_hist[:, :, 0].astype(jnp.int32)
        total = counts[:, 0].astype(jnp.int32)
        tile_n = jnp.concatenate([before[1:], total[None, :]], axis=0) - before
        run_rows = (tile_n + RUN_ALIGN - 1) // RUN_ALIGN * RUN_ALIGN
        run_before = jnp.cumsum(run_rows, axis=0) - run_rows
        blocks_e = (jnp.sum(run_rows, axis=0) + MOE_BLOCK - 1) // MOE_BLOCK
        blocks_end = jnp.cumsum(blocks_e)
        start_row = (blocks_end - blocks_e) * MOE_BLOCK
        run_slot = start_row[None, :] + run_before
        run_sorted = jnp.cumsum(run_rows, axis=1) - run_rows
        onehot = top_e[:, :, None] == jnp.arange(N_EXPERTS, dtype=jnp.int32)[None, None, :]
        per_token = lambda table: jnp.sum(jnp.where(onehot, jnp.repeat(table, geom.c, axis=0)[None], 0), axis=-1)
        sidx = per_token(run_sorted) + pos - per_token(before)
        block_ids = jnp.arange(n_blocks, dtype=jnp.int32)
        block_e = jnp.minimum(jnp.sum((blocks_end[None, :] <= block_ids[:, None]).astype(jnp.int32), axis=1),
                              N_EXPERTS - 1)
        n_used = blocks_end[-1:].astype(jnp.int32)
        last_block = jnp.where(blocks_e > 0, blocks_end - 1, n_blocks).astype(jnp.int32)
        run_rows_flat, run_slot_flat = run_rows.reshape(-1), run_slot.reshape(-1).astype(jnp.int32)
        tile_rows = jnp.sum(run_rows, axis=1)

        xs = _dispatch_call(geom, last_block, run_rows_flat, run_slot_flat, tile_rows, sidx, h2, n_blocks)
        ys = _expert_call(l, block_e, n_used, xs, w_exp_gate, w_exp_up, w_exp_down)
        w_sh_gu = jnp.concatenate([w_sh_gate[l], w_sh_up[l]], axis=-1).astype(BF16)
        xt = _combine_call(geom, alpha, run_rows_flat, run_slot_flat, tile_rows, sidx.T, gate_w.T, ys, h2, x1, mods, w_sh_gu,
                           w_sh_down[l].astype(BF16), ln2_g[l], ln2_b[l])
    return xt[:geom.nl].reshape(b, t, d)
```
